```python
import jax, jax.numpy as jnp
from jax import lax
import numpy as np

D_MODEL = 1024
BATCH = 1
SEQ = 16384
DEPTH = 2
DEC_BATCH = 16
DEC_SEQ = 16
PAST_LEN = 1024

CHUNK = 64
N_HEADS = 16
HEAD_DIM = 64
N_KV_HEADS = 4
Q_PER_KV = N_HEADS // N_KV_HEADS
ATTN_DIM = N_HEADS * HEAD_DIM
KV_DIM = N_KV_HEADS * HEAD_DIM
IDX_HEADS = 8
IDX_DIM = 64
TOPK_MAX = 256
QBLK = 128
ROPE_THETA = 10000.0
CONV_DIM = D_MODEL
CONV_WIDTH = 31
D_FF = 2816
N_EXPERTS = 8
TOP_K_EXPERTS = 2
D_FF_EXPERT = 3584
MOE_BLK = 128
PLE_DIM = 256
EPS = 1e-6
N_DENSE = (DEPTH + 1) // 2
N_MOE = DEPTH // 2
SPLITS = (ATTN_DIM, KV_DIM, KV_DIM, IDX_HEADS * IDX_DIM, IDX_DIM, IDX_HEADS, 2 * CONV_DIM, D_MODEL, D_MODEL)
IN_DIM = sum(SPLITS)

kernel_name = 'streaming_dsa_conformer_hybrid'


def rms_norm(x, g):
    x32 = x.astype(jnp.float32)
    y = x32 * lax.rsqrt(jnp.mean(x32 * x32, axis=-1, keepdims=True) + EPS)
    return (y * g.astype(jnp.float32)).astype(x.dtype)


def layer_norm(x, g, b):
    x32 = x.astype(jnp.float32)
    mu = jnp.mean(x32, axis=-1, keepdims=True)
    xc = x32 - mu
    y = xc * lax.rsqrt(jnp.mean(xc * xc, axis=-1, keepdims=True) + EPS)
    return (y * g.astype(jnp.float32) + b.astype(jnp.float32)).astype(x.dtype)


def rope(x, pos):
    half = x.shape[-1] // 2
    inv = ROPE_THETA ** (-jnp.arange(half, dtype=jnp.float32) / half)
    ang = pos.astype(jnp.float32)[:, None] * inv[None, :]
    cos = jnp.cos(ang)[:, None, :]
    sin = jnp.sin(ang)[:, None, :]
    x32 = x.astype(jnp.float32)
    x1, x2 = x32[..., :half], x32[..., half:]
    return jnp.concatenate([x1 * cos - x2 * sin, x2 * cos + x1 * sin], axis=-1).astype(x.dtype)


def split_columns(proj):
    outs = []
    off = 0
    for w in SPLITS:
        outs.append(proj[..., off:off + w])
        off += w
    return outs


def n_selected(num_keys):
    return min(TOPK_MAX, num_keys // 4)


def indexed_attention(q, qi, wi, q_pos, k, v, ki, k_pos, n_sel):
    bsz, nq = q.shape[0], q.shape[1]
    rel = jax.nn.relu(jnp.einsum('bqhd,bsd->bqhs', qi, ki).astype(jnp.float32) * IDX_DIM ** -0.5)
    score = jnp.einsum('bqhs,bqh->bqs', rel, wi.astype(jnp.float32) * IDX_HEADS ** -0.5)
    admissible = (k_pos[None, :] // CHUNK) <= (q_pos[:, None] // CHUNK)
    score = jnp.where(admissible[None], score, -jnp.inf)
    top_score, sel = lax.top_k(score, n_sel)
    valid = top_score > -jnp.inf
    gather = jax.vmap(lambda rows, idx: rows[idx])
    k_sel = gather(k, sel)
    v_sel = gather(v, sel)
    qg = q.reshape(bsz, nq, N_KV_HEADS, Q_PER_KV, HEAD_DIM)
    logits = jnp.einsum('bqgrd,bqkgd->bqgrk', qg, k_sel).astype(jnp.float32) * HEAD_DIM ** -0.5
    logits = jnp.where(valid[:, :, None, None, :], logits, -jnp.inf)
    probs = jax.nn.softmax(logits, axis=-1).astype(v.dtype)
    out = jnp.einsum('bqgrk,bqkgd->bqgrd', probs, v_sel)
    return out.reshape(bsz, nq, ATTN_DIM)


def prompt_attention(q, qi, wi, k, v, ki):
    bsz, seq = q.shape[0], q.shape[1]
    nb = seq // QBLK
    pos = jnp.arange(seq, dtype=jnp.int32)
    n_sel = n_selected(seq)

    def to_blocks(a):
        return jnp.swapaxes(a.reshape((bsz, nb, QBLK) + a.shape[2:]), 0, 1)

    def one_block(args):
        qb, qib, wib, pb = args
        return indexed_attention(qb, qib, wib, pb, k, v, ki, pos, n_sel)

    out = lax.map(one_block, (to_blocks(q), to_blocks(qi), to_blocks(wi), pos.reshape(nb, QBLK)))
    return jnp.swapaxes(out, 0, 1).reshape(bsz, seq, ATTN_DIM)


def conv_module(glu_in, ctx, w_dw, b_dw, ln_g, ln_b, w_o):
    a, g = jnp.split(glu_in, 2, axis=-1)
    y = a * jax.nn.sigmoid(g)
    padded = jnp.concatenate([ctx, y], axis=1)
    conv = lax.conv_general_dilated(padded, w_dw[:, None, :], window_strides=(1,), padding='VALID',
                                    dimension_numbers=('NWC', 'WIO', 'NWC'),
                                    feature_group_count=CONV_DIM) + b_dw
    z = jax.nn.silu(layer_norm(conv, ln_g, ln_b))
    return z @ w_o, padded[:, -(CONV_WIDTH - 1):]


def swiglu(u, wg, wu, wd):
    return (jax.nn.silu(u @ wg) * (u @ wu)) @ wd


def moe_swiglu(u, w_router, w_gate, w_up, w_down):
    bsz, length, dm = u.shape
    xf = u.reshape(bsz * length, dm)
    n_tok = bsz * length
    logits = (xf @ w_router).astype(jnp.float32)
    top_logit, top_e = lax.top_k(logits, TOP_K_EXPERTS)
    gates = jax.nn.softmax(top_logit, axis=-1)
    n_asg = n_tok * TOP_K_EXPERTS
    e_flat = top_e.reshape(n_asg)
    order = jnp.argsort(e_flat)
    e_sorted = e_flat[order]
    tok_sorted = (order // TOP_K_EXPERTS).astype(jnp.int32)
    gate_sorted = gates.reshape(n_asg)[order]
    counts = jnp.zeros((N_EXPERTS,), jnp.int32).at[e_flat].add(1)
    padded = (counts + MOE_BLK - 1) // MOE_BLK * MOE_BLK
    pad_end = jnp.cumsum(padded)
    pad_start = pad_end - padded
    grp_start = jnp.cumsum(counts) - counts
    dest = pad_start[e_sorted] + (jnp.arange(n_asg, dtype=jnp.int32) - grp_start[e_sorted])
    n_blocks = -(-(n_asg + N_EXPERTS * (MOE_BLK - 1)) // MOE_BLK)
    n_rows = n_blocks * MOE_BLK
    row_tok = jnp.zeros((n_rows,), jnp.int32).at[dest].set(tok_sorted)
    row_gate = jnp.zeros((n_rows,), jnp.float32).at[dest].set(gate_sorted)
    blk_start = jnp.arange(n_blocks, dtype=jnp.int32) * MOE_BLK
    blk_expert = jnp.minimum(jnp.searchsorted(pad_end, blk_start, side='right'), N_EXPERTS - 1)

    def run_block(args):
        toks, e = args
        xb = xf[toks]
        hb = jax.nn.silu(xb @ w_gate[e]) * (xb @ w_up[e])
        return hb @ w_down[e]

    rows = lax.map(run_block, (row_tok.reshape(n_blocks, MOE_BLK), blk_expert))
    rows = rows.reshape(n_rows, dm) * row_gate[:, None].astype(rows.dtype)
    y = jnp.zeros_like(xf).at[row_tok].add(rows)
    return y.reshape(bsz, length, dm)


def setup_inputs(seed: int = 0) -> dict:
    key = jax.random.key(seed)
    ks = iter(jax.random.split(key, 40))

    def nrm(shape, scale):
        return scale * jax.random.normal(next(ks), shape, jnp.float32)

    def gain(shape):
        return 1.0 + 0.1 * jax.random.normal(next(ks), shape, jnp.float32)

    return {
        'x_prompt': nrm((BATCH, SEQ, D_MODEL), 1.0),
        'x_sample': nrm((DEC_BATCH, DEC_SEQ, D_MODEL), 1.0),
        'cache_k': nrm((DEPTH, DEC_BATCH, PAST_LEN, N_KV_HEADS, HEAD_DIM), 1.0),
        'cache_v': nrm((DEPTH, DEC_BATCH, PAST_LEN, N_KV_HEADS, HEAD_DIM), 1.0),
        'cache_kidx': nrm((DEPTH, DEC_BATCH, PAST_LEN, IDX_DIM), 1.0),
        'state_conv': nrm((DEPTH, DEC_BATCH, CONV_WIDTH - 1, CONV_DIM), 0.5),
        'p_prompt': nrm((DEPTH, BATCH, SEQ, PLE_DIM), 1.0),
        'p_sample': nrm((DEPTH, DEC_BATCH, DEC_SEQ, PLE_DIM), 1.0),
        'w_in': nrm((DEPTH, D_MODEL, IN_DIM), D_MODEL ** -0.5),
        'q_norm_g': gain((DEPTH, HEAD_DIM)),
        'k_norm_g': gain((DEPTH, HEAD_DIM)),
        'w_dw': nrm((DEPTH, CONV_WIDTH, CONV_DIM), CONV_WIDTH ** -0.5),
        'b_dw': nrm((DEPTH, CONV_DIM), 0.02),
        'conv_ln_g': gain((DEPTH, CONV_DIM)),
        'conv_ln_b': nrm((DEPTH, CONV_DIM), 0.02),
        'w_o_attn': nrm((DEPTH, ATTN_DIM, D_MODEL), ATTN_DIM ** -0.5),
        'w_o_conv': nrm((DEPTH, CONV_DIM, D_MODEL), CONV_DIM ** -0.5),
        'w_out': nrm((DEPTH, D_MODEL, D_MODEL), D_MODEL ** -0.5),
        'g_mix': gain((DEPTH, D_MODEL)),
        'g_ffn': gain((DEPTH, D_MODEL)),
        'ffn_w_gate': nrm((N_DENSE, D_MODEL, D_FF), D_MODEL ** -0.5),
        'ffn_w_up': nrm((N_DENSE, D_MODEL, D_FF), D_MODEL ** -0.5),
        'ffn_w_down': nrm((N_DENSE, D_FF, D_MODEL), D_FF ** -0.5),
        'moe_router': nrm((N_MOE, D_MODEL, N_EXPERTS), D_MODEL ** -0.5),
        'moe_w_gate': nrm((N_MOE, N_EXPERTS, D_MODEL, D_FF_EXPERT), D_MODEL ** -0.5),
        'moe_w_up': nrm((N_MOE, N_EXPERTS, D_MODEL, D_FF_EXPERT), D_MODEL ** -0.5),
        'moe_w_down': nrm((N_MOE, N_EXPERTS, D_FF_EXPERT, D_MODEL), D_FF_EXPERT ** -0.5),
        'g_ple': gain((DEPTH, D_MODEL)),
        'w_ple_gate': nrm((DEPTH, D_MODEL, D_MODEL), D_MODEL ** -0.5),
        'w_ple_proj': nrm((DEPTH, PLE_DIM, D_MODEL), PLE_DIM ** -0.5),
    }


def reference(x_prompt, x_sample, cache_k, cache_v, cache_kidx, state_conv, p_prompt, p_sample,
              w_in, q_norm_g, k_norm_g, w_dw, b_dw, conv_ln_g, conv_ln_b, w_o_attn, w_o_conv, w_out,
              g_mix, g_ffn, ffn_w_gate, ffn_w_up, ffn_w_down, moe_router, moe_w_gate, moe_w_up,
              moe_w_down, g_ple, w_ple_gate, w_ple_proj):

    def layer_stack(x, p, with_cache):
        bsz, length, _ = x.shape
        past = PAST_LEN if with_cache else 0
        pos = past + jnp.arange(length, dtype=jnp.int32)
        h = x
        ks, vs, kis, convs = [], [], [], []
        for i in range(DEPTH):
            u = rms_norm(h, g_mix[i])
            q, k, v, qi, ki, wi, glu, gate_a, gate_b = split_columns(u @ w_in[i])
            q = rope(rms_norm(q.reshape(bsz, length, N_HEADS, HEAD_DIM), q_norm_g[i]), pos)
            k = rope(rms_norm(k.reshape(bsz, length, N_KV_HEADS, HEAD_DIM), k_norm_g[i]), pos)
            v = v.reshape(bsz, length, N_KV_HEADS, HEAD_DIM)
            qi = rope(qi.reshape(bsz, length, IDX_HEADS, IDX_DIM), pos)
            ki = rope(ki[:, :, None, :], pos)[:, :, 0, :]
            if with_cache:
                k_all = jnp.concatenate([cache_k[i], k], axis=1)
                v_all = jnp.concatenate([cache_v[i], v], axis=1)
                ki_all = jnp.concatenate([cache_kidx[i], ki], axis=1)
                key_pos = jnp.arange(PAST_LEN + length, dtype=jnp.int32)
                attn = indexed_attention(q, qi, wi, pos, k_all, v_all, ki_all, key_pos,
                                         n_selected(PAST_LEN + length))
                conv_ctx = state_conv[i]
            else:
                attn = prompt_attention(q, qi, wi, k, v, ki)
                conv_ctx = jnp.zeros((bsz, CONV_WIDTH - 1, CONV_DIM), glu.dtype)
            conv_out, conv_state = conv_module(glu, conv_ctx, w_dw[i], b_dw[i], conv_ln_g[i],
                                               conv_ln_b[i], w_o_conv[i])
            merged = jax.nn.sigmoid(gate_a) * (attn @ w_o_attn[i]) + jax.nn.sigmoid(gate_b) * conv_out
            h = h + merged @ w_out[i]
            u = rms_norm(h, g_ffn[i])
            j = i // 2
            if i % 2 == 0:
                h = h + swiglu(u, ffn_w_gate[j], ffn_w_up[j], ffn_w_down[j])
            else:
                h = h + moe_swiglu(u, moe_router[j], moe_w_gate[j], moe_w_up[j], moe_w_down[j])
            u = rms_norm(h, g_ple[i])
            h = h + jax.nn.sigmoid(u @ w_ple_gate[i]) * (p[i] @ w_ple_proj[i])
            ks.append(k)
            vs.append(v)
            kis.append(ki)
            convs.append(conv_state)
        return h, jnp.stack(ks), jnp.stack(vs), jnp.stack(kis), jnp.stack(convs)

    y_prompt, k_prompt, v_prompt, kidx_prompt, conv_prompt = layer_stack(x_prompt, p_prompt, False)
    y_sample, k_sample, v_sample, kidx_sample, conv_sample = layer_stack(x_sample, p_sample, True)
    return (y_prompt, y_sample, k_prompt, v_prompt, kidx_prompt, conv_prompt,
            k_sample, v_sample, kidx_sample, conv_sample)
```

```python
import functools

import jax
import jax.numpy as jnp
from jax import lax
from jax.experimental import pallas as pl
from jax.experimental.pallas import tpu as pltpu

F32 = jnp.float32
BF16 = jnp.bfloat16
I32 = jnp.int32

CHUNK = 64
CHUNK_SHIFT = CHUNK.bit_length() - 1
assert 1 << CHUNK_SHIFT == CHUNK
N_HEADS = 16
HEAD_DIM = 64
N_KV_HEADS = 4
Q_PER_KV = N_HEADS // N_KV_HEADS
IDX_HEADS = 8
IDX_DIM = 64
TOPK_MAX = 256
ROPE_THETA = 10000.0
CONV_WIDTH = 31
TOP_K_EXPERTS = 2
EPS = 1e-6

LANES = 128
CONV_HALO = 32
MASKED_BIAS = -1e30
INT_MIN = -2 ** 31
KEY_NEG_INF = -2139095041
VMEM_LIMIT = 52 * 1024 * 1024


def _params(semantics):
    return pltpu.CompilerParams(dimension_semantics=semantics, vmem_limit_bytes=VMEM_LIMIT)


def _pick_tile(n, pref):
    if n <= pref:
        return n
    best = None
    for t in range(LANES, pref + 1, LANES):
        if n % t == 0:
            best = t
    assert best is not None, (n, pref)
    return best


def _fused_mm_kernel(*refs, n_lhs, has_gain, pair_lhs, n_tiled, n_rows, n_consts, n_out,
                     epilogue):
    it = iter(refs)
    lhs_refs = [next(it) for _ in range(n_lhs)]
    gain_refs = [next(it) if has_gain[a] else None for a in range(n_lhs)]
    rhs_refs = [next(it) for _ in range(len(pair_lhs))]
    tiled_refs = [next(it) for _ in range(n_tiled)]
    row_refs = [next(it) for _ in range(n_rows)]
    const_refs = [next(it) for _ in range(n_consts)]
    out_refs = [next(it) for _ in range(n_out)]
    stage_refs = [next(it) for _ in range(n_lhs)]

    @pl.when(pl.program_id(1) == 0)
    def _stage():
        for a in range(n_lhs):
            x = lhs_refs[a][...].astype(F32)
            if has_gain[a]:
                x = x * lax.rsqrt(jnp.mean(x * x, axis=-1, keepdims=True) + EPS)
                x = x * gain_refs[a][...]
            stage_refs[a][...] = x.astype(BF16)

    accs = [jnp.dot(stage_refs[pair_lhs[r]][...], rhs_refs[r][...], preferred_element_type=F32)
            for r in range(len(pair_lhs))]
    outs = epilogue(accs, [t[...] for t in tiled_refs], [t[...] for t in row_refs],
                    [t[...] for t in const_refs])
    for o_ref, o in zip(out_refs, outs):
        o_ref[...] = o.astype(o_ref.dtype)


def fused_mm(lhs, rhs, pair_lhs, epilogue, out_dtypes, *, gains=None, tiled=(), rows=(),
             consts=(), tm=512, tn=512):
    m = lhs[0].shape[0]
    n = rhs[0].shape[1]
    tm = _pick_tile(m, tm) if m % LANES == 0 else m
    tn = _pick_tile(n, tn)
    n_lhs = len(lhs)
    gains = list(gains) if gains is not None else [None] * n_lhs
    has_gain = tuple(g is not None for g in gains)

    in_specs, args = [], []
    for x in lhs:
        in_specs.append(pl.BlockSpec((tm, x.shape[1]), lambda i, j: (i, 0)))
        args.append(x)
    for g in gains:
        if g is not None:
            g2 = g.reshape(1, -1).astype(F32)
            in_specs.append(pl.BlockSpec(g2.shape, lambda i, j: (0, 0)))
            args.append(g2)
    for w in rhs:
        in_specs.append(pl.BlockSpec((w.shape[0], tn), lambda i, j: (0, j)))
        args.append(w)
    for t in tiled:
        in_specs.append(pl.BlockSpec((tm, tn), lambda i, j: (i, j)))
        args.append(t)
    for t in rows:
        in_specs.append(pl.BlockSpec((tm, t.shape[1]), lambda i, j: (i, 0)))
        args.append(t)
    for c in consts:
        in_specs.append(pl.BlockSpec(c.shape, lambda i, j: (0, 0)))
        args.append(c)

    kernel = functools.partial(
        _fused_mm_kernel, n_lhs=n_lhs, has_gain=has_gain, pair_lhs=tuple(pair_lhs),
        n_tiled=len(tiled), n_rows=len(rows), n_consts=len(consts), n_out=len(out_dtypes),
        epilogue=epilogue)
    outs = pl.pallas_call(
        kernel,
        grid=(m // tm, n // tn),
        in_specs=in_specs,
        out_specs=[pl.BlockSpec((tm, tn), lambda i, j: (i, j)) for _ in out_dtypes],
        out_shape=[jax.ShapeDtypeStruct((m, n), dt) for dt in out_dtypes],
        scratch_shapes=[pltpu.VMEM((tm, x.shape[1]), BF16) for x in lhs],
        compiler_params=_params(("parallel", "arbitrary")),
    )(*args)
    return outs


def _head_rms_norm(x, block_ones, gain):
    x2 = x * x
    hi = x2.astype(BF16)
    lo = (x2 - hi.astype(F32)).astype(BF16)
    ss = (jnp.dot(hi, block_ones, preferred_element_type=F32)
          + jnp.dot(lo, block_ones, preferred_element_type=F32))
    return x * lax.rsqrt(ss * (1.0 / HEAD_DIM) + EPS) * gain


def _rope(x, cos, sin_signed):
    n = x.shape[1]
    lane = lax.broadcasted_iota(I32, x.shape, 1)
    first_half = (lane & (HEAD_DIM // 2)) == 0
    partner = jnp.where(first_half, pltpu.roll(x, n - HEAD_DIM // 2, 1),
                        pltpu.roll(x, HEAD_DIM // 2, 1))
    return x * cos + partner * sin_signed


def _sigmoid(x):
    return 1.0 / (1.0 + jnp.exp(-x))


def _silu(x):
    return x * _sigmoid(x)


def _rope_tables(pos, width, rotary_width=None):
    half = HEAD_DIM // 2
    inv = ROPE_THETA ** (-jnp.arange(half, dtype=F32) / half)
    ang = pos.astype(F32)[:, None] * inv[None, :]
    cos = jnp.cos(ang)
    sin = jnp.sin(ang)
    cos64 = jnp.concatenate([cos, cos], axis=-1)
    sin64 = jnp.concatenate([-sin, sin], axis=-1)
    rotary_width = width if rotary_width is None else rotary_width
    reps = rotary_width // HEAD_DIM
    cos_t = jnp.tile(cos64, (1, reps))
    sin_t = jnp.tile(sin64, (1, reps))
    if rotary_width < width:
        pad = width - rotary_width
        cos_t = jnp.concatenate([cos_t, jnp.ones((pos.shape[0], pad), F32)], axis=-1)
        sin_t = jnp.concatenate([sin_t, jnp.zeros((pos.shape[0], pad), F32)], axis=-1)
    return cos_t, sin_t


def _block_ones(width):
    head = jnp.arange(width, dtype=I32) // HEAD_DIM
    return (head[:, None] == head[None, :]).astype(BF16)


def _float_key(x):
    b = lax.bitcast_convert_type(x, I32)
    return jnp.where(b >= 0, b, b ^ jnp.int32(0x7FFFFFFF))


def _select_kernel(qi_ref, ki_ref, wi_ref, tri_ref, bias_ref, keys_ref, *, tq, tk, nkb, causal,
                   q_pos0, n_valid, n_sel):
    qb = pl.program_id(1)
    if causal:
        n_act = jnp.minimum(nkb, ((qb + 1) * tq + tk - 1) // tk)
    else:
        n_act = nkb
    q_pos = q_pos0 + qb * tq + lax.broadcasted_iota(I32, (tq, tk), 0)
    q_chunk = lax.shift_right_logical(q_pos, CHUNK_SHIFT)
    w = wi_ref[0] * (IDX_HEADS ** -0.5)
    qi = qi_ref[0]

    def score_block(kb, carry):
        start = pl.multiple_of(kb * tk, tk)
        ki = ki_ref[0, pl.ds(start, tk), :]
        s = jnp.zeros((tq, tk), F32)
        for h in range(IDX_HEADS):
            r = lax.dot_general(qi[:, h * IDX_DIM:(h + 1) * IDX_DIM], ki,
                                (((1,), (1,)), ((), ())), preferred_element_type=F32)
            s = s + jnp.maximum(r * (IDX_DIM ** -0.5), 0.0) * w[:, h:h + 1]
        k_pos = start + lax.broadcasted_iota(I32, (tq, tk), 1)
        admissible = (lax.shift_right_logical(k_pos, CHUNK_SHIFT) <= q_chunk) & (k_pos < n_valid)
        s = jnp.where(admissible, s, -jnp.inf)
        keys_ref[kb] = _float_key(s)
        return carry

    lax.fori_loop(0, n_act, score_block, 0)

    def count(pred):
        def body(kb, acc):
            hit = jnp.where(pred(keys_ref[kb]), 1.0, 0.0)
            for c in range(tk // LANES):
                acc = acc + hit[:, c * LANES:(c + 1) * LANES]
            return acc
        acc = lax.fori_loop(0, n_act, body, jnp.zeros((tq, LANES), F32))
        return jnp.sum(acc, axis=1, keepdims=True)

    def bit_step(i, prefix):
        bit = jnp.left_shift(jnp.int32(1), 31 - i)
        cand = prefix | bit
        cand_signed = cand ^ jnp.int32(INT_MIN)
        cnt = count(lambda k: k >= cand_signed)
        return jnp.where(cnt >= n_sel, cand, prefix)

    prefix = lax.fori_loop(0, 32, bit_step, jnp.zeros((tq, 1), I32))
    tau = prefix ^ jnp.int32(INT_MIN)
    need = n_sel - count(lambda k: k > tau)

    def emit_block(kb, seen):
        key = keys_ref[kb]
        eq = key == tau
        rank = seen + jnp.dot(jnp.where(eq, 1.0, 0.0).astype(BF16), tri_ref[...],
                              preferred_element_type=F32)
        sel = ((key > tau) | (eq & (rank <= need))) & (key > KEY_NEG_INF)
        bias_ref[0, kb] = jnp.where(sel, 0.0, MASKED_BIAS).astype(bias_ref.dtype)
        return rank[:, tk - 1:tk]

    lax.fori_loop(0, n_act, emit_block, jnp.zeros((tq, 1), F32))

    def fill_block(kb, carry):
        bias_ref[0, kb] = jnp.full((tq, tk), MASKED_BIAS, bias_ref.dtype)
        return carry

    lax.fori_loop(n_act, nkb, fill_block, 0)


def select_bias(qi, ki, wi, *, tq, tk, causal, q_pos0, n_valid, n_sel):
    b, t_q, _ = qi.shape
    n_k = ki.shape[1]
    nkb = n_k // tk
    idx = jnp.arange(tk, dtype=I32)
    tri = (idx[:, None] <= idx[None, :]).astype(BF16)
    kernel = functools.partial(_select_kernel, tq=tq, tk=tk, nkb=nkb, causal=causal,
                               q_pos0=q_pos0, n_valid=n_valid, n_sel=n_sel)
    return pl.pallas_call(
        kernel,
        grid=(b, t_q // tq),
        in_specs=[
            pl.BlockSpec((1, tq, qi.shape[2]), lambda bi, qb: (bi, qb, 0)),
            pl.BlockSpec((1, n_k, ki.shape[2]), lambda bi, qb: (bi, 0, 0)),
            pl.BlockSpec((1, tq, wi.shape[2]), lambda bi, qb: (bi, qb, 0)),
            pl.BlockSpec((tk, tk), lambda bi, qb: (0, 0)),
        ],
        out_specs=pl.BlockSpec((1, nkb, tq, tk), lambda bi, qb: (bi, 0, qb, 0)),
        out_shape=jax.ShapeDtypeStruct((b, nkb, t_q, tk), BF16),
        scratch_shapes=[pltpu.VMEM((nkb, tq, tk), I32)],
        compiler_params=_params(("parallel", "parallel")),
    )(qi, ki, wi, tri)


def _attn_kernel(q_ref, k_ref, v_ref, bias_ref, o_ref, m_ref, l_ref, acc_ref, *, tq, tk, nkb,
                 causal):
    qb = pl.program_id(1)
    kb = pl.program_id(2)
    last = jnp.minimum(nkb - 1, ((qb + 1) * tq - 1) // tk) if causal else nkb - 1

    @pl.when(kb == 0)
    def _init():
        m_ref[...] = jnp.full(m_ref.shape, MASKED_BIAS, F32)
        l_ref[...] = jnp.zeros(l_ref.shape, F32)
        acc_ref[...] = jnp.zeros(acc_ref.shape, F32)

    @pl.when(kb <= last)
    def _step():
        bias = bias_ref[0, 0].astype(F32)
        for h in range(N_HEADS):
            g = h // Q_PER_KV
            hs = slice(h * HEAD_DIM, (h + 1) * HEAD_DIM)
            gs = slice(g * HEAD_DIM, (g + 1) * HEAD_DIM)
            s = lax.dot_general(q_ref[0, :, hs], k_ref[0, :, gs], (((1,), (1,)), ((), ())),
                                preferred_element_type=F32)
            s = s * (HEAD_DIM ** -0.5) + bias
            m_prev = m_ref[h]
            m_new = jnp.maximum(m_prev, jnp.max(s, axis=1, keepdims=True))
            alpha = jnp.exp(m_prev - m_new)
            p = jnp.exp(s - m_new)
            l_ref[h] = alpha * l_ref[h] + jnp.sum(p, axis=1, keepdims=True)
            acc_ref[:, hs] = alpha * acc_ref[:, hs] + jnp.dot(
                p.astype(BF16), v_ref[0, :, gs], preferred_element_type=F32)
            m_ref[h] = m_new

    @pl.when(kb == nkb - 1)
    def _finish():
        for h in range(N_HEADS):
            hs = slice(h * HEAD_DIM, (h + 1) * HEAD_DIM)
            o_ref[0, :, hs] = (acc_ref[:, hs] / l_ref[h]).astype(o_ref.dtype)


def masked_attention(q, k, v, bias, *, tq, causal):
    b, t_q, d_q = q.shape
    nkb, tk = bias.shape[1], bias.shape[3]

    def kv_block(qb, kb):
        return jnp.minimum(kb, ((qb + 1) * tq - 1) // tk) if causal else kb

    kernel = functools.partial(_attn_kernel, tq=tq, tk=tk, nkb=nkb, causal=causal)
    return pl.pallas_call(
        kernel,
        grid=(b, t_q // tq, nkb),
        in_specs=[
            pl.BlockSpec((1, tq, d_q), lambda bi, qb, kb: (bi, qb, 0)),
            pl.BlockSpec((1, tk, k.shape[2]), lambda bi, qb, kb: (bi, kv_block(qb, kb), 0)),
            pl.BlockSpec((1, tk, v.shape[2]), lambda bi, qb, kb: (bi, kv_block(qb, kb), 0)),
            pl.BlockSpec((1, 1, tq, tk), lambda bi, qb, kb: (bi, kv_block(qb, kb), qb, 0)),
        ],
        out_specs=pl.BlockSpec((1, tq, d_q), lambda bi, qb, kb: (bi, qb, 0)),
        out_shape=jax.ShapeDtypeStruct((b, t_q, d_q), BF16),
        scratch_shapes=[pltpu.VMEM((N_HEADS, tq, 1), F32), pltpu.VMEM((N_HEADS, tq, 1), F32),
                        pltpu.VMEM((tq, d_q), F32)],
        compiler_params=_params(("parallel", "parallel", "arbitrary")),
    )(q, k, v, bias)


def _conv_kernel(cur_ref, nxt_ref, w_ref, b_ref, g_ref, beta_ref, o_ref, buf_ref, conv_ref, *,
                 tm, cw):
    buf_ref[0:tm, :] = cur_ref[0]
    buf_ref[tm:tm + CONV_HALO, :] = nxt_ref[0]
    lead = CONV_HALO - (CONV_WIDTH - 1)
    c_total = buf_ref.shape[1]
    rw = min(tm, 128)
    for c in range(c_total // cw):
        cs = slice(c * cw, (c + 1) * cw)
        for r in range(tm // rw):
            acc = jnp.zeros((rw, cw), F32) + b_ref[:, cs]
            for j in range(CONV_WIDTH):
                r0 = lead + j + r * rw
                acc = acc + w_ref[j:j + 1, cs] * buf_ref[r0:r0 + rw, cs]
            conv_ref[r * rw:(r + 1) * rw, cs] = acc
    x = conv_ref[...]
    mu = jnp.mean(x, axis=-1, keepdims=True)
    xc = x - mu
    y = xc * lax.rsqrt(jnp.mean(xc * xc, axis=-1, keepdims=True) + EPS)
    y = y * g_ref[...] + beta_ref[...]
    o_ref[0] = _silu(y).astype(o_ref.dtype)


def conv_ln_silu(hist, w_dw, b_dw, ln_g, ln_b, *, tm):
    b, lp, c = hist.shape
    l = lp - CONV_HALO
    kernel = functools.partial(_conv_kernel, tm=tm, cw=LANES)
    vec = lambda a: a.reshape(1, c).astype(F32)
    return pl.pallas_call(
        kernel,
        grid=(b, l // tm),
        in_specs=[
            pl.BlockSpec((1, tm, c), lambda bi, i: (bi, i, 0)),
            pl.BlockSpec((1, CONV_HALO, c), lambda bi, i: (bi, (i + 1) * (tm // CONV_HALO), 0)),
            pl.BlockSpec((CONV_WIDTH, c), lambda bi, i: (0, 0)),
            pl.BlockSpec((1, c), lambda bi, i: (0, 0)),
            pl.BlockSpec((1, c), lambda bi, i: (0, 0)),
            pl.BlockSpec((1, c), lambda bi, i: (0, 0)),
        ],
        out_specs=pl.BlockSpec((1, tm, c), lambda bi, i: (bi, i, 0)),
        out_shape=jax.ShapeDtypeStruct((b, l, c), BF16),
        scratch_shapes=[pltpu.VMEM((tm + CONV_HALO, c), F32), pltpu.VMEM((tm, c), F32)],
        compiler_params=_params(("parallel", "parallel")),
    )(hist, hist, w_dw.astype(F32), vec(b_dw), vec(ln_g), vec(ln_b))


def _router_kernel(h_ref, g_ref, wr_ref, u_ref, gate_ref, *, n_experts):
    x = h_ref[...]
    x = x * lax.rsqrt(jnp.mean(x * x, axis=-1, keepdims=True) + EPS) * g_ref[...]
    u = x.astype(BF16)
    u_ref[...] = u
    logits = jnp.dot(u, wr_ref[...], preferred_element_type=F32)
    lane = lax.broadcasted_iota(I32, logits.shape, 1).astype(F32)
    logits = jnp.where(lane < n_experts, logits, -jnp.inf)
    top1 = jnp.max(logits, axis=1, keepdims=True)
    idx1 = jnp.min(jnp.where(logits == top1, lane, float(LANES)), axis=1, keepdims=True)
    rest = jnp.where(lane == idx1, -jnp.inf, logits)
    top2 = jnp.max(rest, axis=1, keepdims=True)
    idx2 = jnp.min(jnp.where(rest == top2, lane, float(LANES)), axis=1, keepdims=True)
    e2 = jnp.exp(top2 - top1)
    denom = 1.0 + e2
    gate_ref[...] = jnp.where(lane == idx1, 1.0 / denom, jnp.where(lane == idx2, e2 / denom, 0.0))


def moe_router(h, gain, w_router_padded, n_experts, *, tm):
    m, d = h.shape
    kernel = functools.partial(_router_kernel, n_experts=n_experts)
    return pl.pallas_call(
        kernel,
        grid=(m // tm,),
        in_specs=[pl.BlockSpec((tm, d), lambda i: (i, 0)),
                  pl.BlockSpec((1, d), lambda i: (0, 0)),
                  pl.BlockSpec((d, LANES), lambda i: (0, 0))],
        out_specs=[pl.BlockSpec((tm, d), lambda i: (i, 0)),
                   pl.BlockSpec((tm, LANES), lambda i: (i, 0))],
        out_shape=[jax.ShapeDtypeStruct((m, d), BF16), jax.ShapeDtypeStruct((m, LANES), F32)],
        compiler_params=_params(("parallel",)),
    )(h, gain.reshape(1, d).astype(F32), w_router_padded)


def _expert_up_kernel(u_ref, wg_ref, wu_ref, o_ref):
    u = u_ref[...]
    a = jnp.dot(u, wg_ref[0], preferred_element_type=F32)
    b = jnp.dot(u, wu_ref[0], preferred_element_type=F32)
    o_ref[0] = (_silu(a) * b).astype(o_ref.dtype)


def expert_up(u, w_gate, w_up, *, tm, tn):
    m, d = u.shape
    e, _, f = w_gate.shape
    return pl.pallas_call(
        _expert_up_kernel,
        grid=(e, m // tm, f // tn),
        in_specs=[pl.BlockSpec((tm, d), lambda ei, i, j: (i, 0)),
                  pl.BlockSpec((1, d, tn), lambda ei, i, j: (ei, 0, j)),
                  pl.BlockSpec((1, d, tn), lambda ei, i, j: (ei, 0, j))],
        out_specs=pl.BlockSpec((1, tm, tn), lambda ei, i, j: (ei, i, j)),
        out_shape=jax.ShapeDtypeStruct((e, m, f), BF16),
        compiler_params=_params(("parallel", "parallel", "parallel")),
    )(u, w_gate, w_up)


def _expert_down_kernel(hb_ref, wd_ref, gate_ref, res_ref, o_ref, acc_ref, *, n_experts):
    e = pl.program_id(2)

    @pl.when(e == 0)
    def _init():
        acc_ref[...] = jnp.zeros(acc_ref.shape, F32)

    gates = gate_ref[...]
    lane = lax.broadcasted_iota(I32, gates.shape, 1)
    g = jnp.sum(jnp.where(lane == e, gates, 0.0), axis=1, keepdims=True)
    acc_ref[...] += g * jnp.dot(hb_ref[0], wd_ref[0], preferred_element_type=F32)

    @pl.when(e == n_experts - 1)
    def _finish():
        o_ref[...] = res_ref[...] + acc_ref[...]


def expert_down(hb, w_down, gates, res, *, tm, tn):
    e, m, f = hb.shape
    d = w_down.shape[2]
    kernel = functools.partial(_expert_down_kernel, n_experts=e)
    return pl.pallas_call(
        kernel,
        grid=(m // tm, d // tn, e),
        in_specs=[pl.BlockSpec((1, tm, f), lambda i, j, ei: (ei, i, 0)),
                  pl.BlockSpec((1, f, tn), lambda i, j, ei: (ei, 0, j)),
                  pl.BlockSpec((tm, LANES), lambda i, j, ei: (i, 0)),
                  pl.BlockSpec((tm, tn), lambda i, j, ei: (i, j))],
        out_specs=pl.BlockSpec((tm, tn), lambda i, j, ei: (i, j)),
        out_shape=jax.ShapeDtypeStruct((m, d), F32),
        scratch_shapes=[pltpu.VMEM((tm, tn), F32)],
        compiler_params=_params(("parallel", "parallel", "arbitrary")),
    )(hb, w_down, gates, res)


def _split_w_in(w_in_layer, d_model):
    attn_dim = N_HEADS * HEAD_DIM
    kv_dim = N_KV_HEADS * HEAD_DIM
    sizes = (attn_dim, kv_dim, kv_dim, IDX_HEADS * IDX_DIM, IDX_DIM, IDX_HEADS, d_model, d_model,
             d_model, d_model)
    parts, off = [], 0
    for s in sizes:
        parts.append(w_in_layer[:, off:off + s])
        off += s
    assert off == w_in_layer.shape[1]
    return parts


def _layer_stack(x, p, weights, cache, *, past):
    bsz, length, d_model = x.shape
    m = bsz * length
    kv_dim = N_KV_HEADS * HEAD_DIM
    pos = past + jnp.arange(length, dtype=I32)
    row_pos = jnp.tile(pos, bsz)
    depth = weights["w_in"].shape[0]

    cos_q, sin_q = _rope_tables(row_pos, 512)
    cos_k, sin_k = _rope_tables(row_pos, kv_dim)
    cos_i, sin_i = _rope_tables(row_pos, LANES, rotary_width=IDX_DIM)
    ones_q = _block_ones(512)
    ones_k = _block_ones(kv_dim)

    n_keys = past + length
    n_sel = min(TOPK_MAX, n_keys // 4)
    if past == 0:
        tq_sel, tk, tq_attn = min(128, length), min(512, length), min(256, length)
        n_keys_pad = n_keys
    else:
        tq_sel = tq_attn = length
        tk = 384
        n_keys_pad = -(-n_keys // tk) * tk

    h = x.reshape(m, d_model)
    ks, vs, kis, convs = [], [], [], []
    for i in range(depth):
        wq, wk, wv, wqi, wki, wwi, wglu_a, wglu_g, wga, wgb = _split_w_in(weights["w_in"][i], d_model)
        w_kiwi = jnp.concatenate(
            [wki, wwi, jnp.zeros((d_model, LANES - IDX_DIM - IDX_HEADS), BF16)], axis=1)
        g_mix = weights["g_mix"][i]

        qn_gain = jnp.tile(weights["q_norm_g"][i].astype(F32), 512 // HEAD_DIM).reshape(1, 512)
        kn_gain = jnp.tile(weights["k_norm_g"][i].astype(F32), kv_dim // HEAD_DIM).reshape(1, kv_dim)

        def norm_rope(accs, tiled, rows, consts):
            y = _head_rms_norm(accs[0], consts[0], consts[1])
            y = _rope(y, rows[0], rows[1])
            return (y, y)

        (q_bf,) = fused_mm([h], [wq], [0],
                           lambda a, t, r, c: (_rope(_head_rms_norm(a[0], c[0], c[1]), r[0], r[1]),),
                           [BF16], gains=[g_mix], rows=[cos_q, sin_q], consts=[ones_q, qn_gain])
        k_f32, k_bf = fused_mm([h], [wk], [0], norm_rope, [F32, BF16], gains=[g_mix],
                               rows=[cos_k, sin_k], consts=[ones_k, kn_gain])
        v_f32, v_bf = fused_mm([h], [wv], [0], lambda a, t, r, c: (a[0], a[0]), [F32, BF16],
                               gains=[g_mix])
        (qi_bf,) = fused_mm([h], [wqi], [0], lambda a, t, r, c: (_rope(a[0], r[0], r[1]),),
                            [BF16], gains=[g_mix], rows=[cos_q, sin_q])
        (kiwi,) = fused_mm([h], [w_kiwi], [0], lambda a, t, r, c: (_rope(a[0], r[0], r[1]),),
                           [F32], gains=[g_mix], rows=[cos_i, sin_i])
        (glu,) = fused_mm([h], [wglu_a, wglu_g], [0, 0],
                          lambda a, t, r, c: (a[0] * _sigmoid(a[1]),), [F32], gains=[g_mix])
        gate_a, gate_b = fused_mm([h], [wga, wgb], [0, 0],
                                  lambda a, t, r, c: (_sigmoid(a[0]), _sigmoid(a[1])),
                                  [F32, F32], gains=[g_mix])

        ki_f32 = kiwi[:, :IDX_DIM]
        wi = kiwi[:, IDX_DIM:IDX_DIM + IDX_HEADS]

        k3 = k_bf.reshape(bsz, length, kv_dim)
        v3 = v_bf.reshape(bsz, length, kv_dim)
        ki3 = ki_f32.astype(BF16).reshape(bsz, length, IDX_DIM)
        if past:
            pad = n_keys_pad - n_keys
            k3 = jnp.concatenate([cache["k"][i].reshape(bsz, past, kv_dim).astype(BF16), k3,
                                  jnp.zeros((bsz, pad, kv_dim), BF16)], axis=1)
            v3 = jnp.concatenate([cache["v"][i].reshape(bsz, past, kv_dim).astype(BF16), v3,
                                  jnp.zeros((bsz, pad, kv_dim), BF16)], axis=1)
            ki3 = jnp.concatenate([cache["kidx"][i].astype(BF16), ki3,
                                   jnp.zeros((bsz, pad, IDX_DIM), BF16)], axis=1)
        bias = select_bias(qi_bf.reshape(bsz, length, -1), ki3, wi.reshape(bsz, length, IDX_HEADS),
                           tq=tq_sel, tk=tk, causal=(past == 0), q_pos0=past, n_valid=n_keys,
                           n_sel=n_sel)
        attn = masked_attention(q_bf.reshape(bsz, length, -1), k3, v3, bias, tq=tq_attn,
                                causal=(past == 0))
        attn = attn.reshape(m, -1)

        glu3 = glu.reshape(bsz, length, d_model)
        if past:
            ctx = cache["conv"][i]
        else:
            ctx = jnp.zeros((bsz, CONV_WIDTH - 1, d_model), F32)
        lead = jnp.zeros((bsz, CONV_HALO - (CONV_WIDTH - 1), d_model), F32)
        l_pad = -(-length // CONV_HALO) * CONV_HALO
        tail = jnp.zeros((bsz, l_pad - length, d_model), F32)
        hist = jnp.concatenate([lead, ctx, glu3, tail], axis=1)
        conv_state = hist[:, CONV_HALO - (CONV_WIDTH - 1) + length:CONV_HALO + length]
        z = conv_ln_silu(hist, weights["w_dw"][i], weights["b_dw"][i], weights["conv_ln_g"][i],
                         weights["conv_ln_b"][i], tm=min(256, l_pad))
        z = z[:, :length].reshape(m, d_model)

        (merged,) = fused_mm([attn, z], [weights["w_o_attn"][i], weights["w_o_conv"][i]], [0, 1],
                             lambda a, t, r, c: (t[0] * a[0] + t[1] * a[1],), [BF16],
                             tiled=[gate_a, gate_b])
        (h,) = fused_mm([merged], [weights["w_out"][i]], [0],
                        lambda a, t, r, c: (t[0] + a[0],), [F32], tiled=[h])

        j = i // 2
        if i % 2 == 0:
            (hb,) = fused_mm([h], [weights["ffn_w_gate"][j], weights["ffn_w_up"][j]], [0, 0],
                             lambda a, t, r, c: (_silu(a[0]) * a[1],), [BF16],
                             gains=[weights["g_ffn"][i]], tn=1408)
            (h,) = fused_mm([hb], [weights["ffn_w_down"][j]], [0],
                            lambda a, t, r, c: (t[0] + a[0],), [F32], tiled=[h])
        else:
            n_experts = weights["moe_router"].shape[2]
            w_router = jnp.concatenate(
                [weights["moe_router"][j], jnp.zeros((d_model, LANES - n_experts), BF16)], axis=1)
            tm_moe = _pick_tile(m, 512)
            u, gates = moe_router(h, weights["g_ffn"][i], w_router, n_experts, tm=tm_moe)
            hb = expert_up(u, weights["moe_w_gate"][j], weights["moe_w_up"][j], tm=tm_moe,
                           tn=_pick_tile(weights["moe_w_gate"].shape[3], 512))
            h = expert_down(hb, weights["moe_w_down"][j], gates, h, tm=tm_moe,
                            tn=_pick_tile(d_model, 512))

        (h,) = fused_mm([h, p[i].reshape(m, -1)], [weights["w_ple_gate"][i], weights["w_ple_proj"][i]],
                        [0, 1], lambda a, t, r, c: (t[0] + _sigmoid(a[0]) * a[1],), [F32],
                        gains=[weights["g_ple"][i], None], tiled=[h])

        ks.append(k_f32.reshape(bsz, length, N_KV_HEADS, HEAD_DIM))
        vs.append(v_f32.reshape(bsz, length, N_KV_HEADS, HEAD_DIM))
        kis.append(ki_f32.reshape(bsz, length, IDX_DIM))
        convs.append(conv_state)
    return (h.reshape(bsz, length, d_model), jnp.stack(ks), jnp.stack(vs), jnp.stack(kis),
            jnp.stack(convs))


def kernel(x_prompt, x_sample, cache_k, cache_v, cache_kidx, state_conv, p_prompt, p_sample, w_in, q_norm_g, k_norm_g, w_dw, b_dw, conv_ln_g, conv_ln_b, w_o_attn, w_o_conv, w_out, g_mix, g_ffn, ffn_w_gate, ffn_w_up, ffn_w_down, moe_router, moe_w_gate, moe_w_up, moe_w_down, g_ple, w_ple_gate, w_ple_proj):
    bf = lambda a: a.astype(BF16)
    weights = dict(
        w_in=bf(w_in), q_norm_g=q_norm_g, k_norm_g=k_norm_g, w_dw=w_dw, b_dw=b_dw,
        conv_ln_g=conv_ln_g, conv_ln_b=conv_ln_b, w_o_attn=bf(w_o_attn), w_o_conv=bf(w_o_conv),
        w_out=bf(w_out), g_mix=g_mix, g_ffn=g_ffn, ffn_w_gate=bf(ffn_w_gate),
        ffn_w_up=bf(ffn_w_up), ffn_w_down=bf(ffn_w_down), moe_router=bf(moe_router),
        moe_w_gate=bf(moe_w_gate), moe_w_up=bf(moe_w_up), moe_w_down=bf(moe_w_down), g_ple=g_ple,
        w_ple_gate=bf(w_ple_gate), w_ple_proj=bf(w_ple_proj))
    cache = dict(k=cache_k, v=cache_v, kidx=cache_kidx, conv=state_conv)
    past_len = cache_k.shape[2]
    y_p, k_p, v_p, ki_p, conv_p = _layer_stack(x_prompt, p_prompt, weights, None, past=0)
    y_s, k_s, v_s, ki_s, conv_s = _layer_stack(x_sample, p_sample, weights, cache, past=past_len)
    return (y_p, y_s, k_p, v_p, ki_p, conv_p, k_s, v_s, ki_s, conv_s)
```

```python
import functools

import jax
import jax.numpy as jnp
from jax import lax
from jax.experimental import pallas as pl
from jax.experimental.pallas import tpu as pltpu

F32 = jnp.float32
BF16 = jnp.bfloat16
I32 = jnp.int32

CHUNK = 64
CHUNK_SHIFT = CHUNK.bit_length() - 1
assert 1 << CHUNK_SHIFT == CHUNK
N_HEADS = 16
HEAD_DIM = 64
N_KV_HEADS = 4
Q_PER_KV = N_HEADS // N_KV_HEADS
IDX_HEADS = 8
IDX_DIM = 64
TOPK_MAX = 256
ROPE_THETA = 10000.0
CONV_WIDTH = 31
TOP_K_EXPERTS = 2
EPS = 1e-6

LANES = 128
CONV_HALO = 32
MASKED_BIAS = -1e30
INT_MIN = -2 ** 31
LOG2_E = 1.4426950408889634
KEY_NEG_INF = -2139095041
VMEM_LIMIT = 52 * 1024 * 1024


def _params(semantics):
    return pltpu.CompilerParams(dimension_semantics=semantics, vmem_limit_bytes=VMEM_LIMIT)


def _pick_tile(n, pref):
    if n <= pref:
        return n
    best = None
    for t in range(LANES, pref + 1, LANES):
        if n % t == 0:
            best = t
    assert best is not None, (n, pref)
    return best


def _fused_mm_kernel(*refs, n_lhs, has_gain, pair_lhs, n_tiled, n_rows, n_consts, n_out,
                     epilogue):
    it = iter(refs)
    lhs_refs = [next(it) for _ in range(n_lhs)]
    gain_refs = [next(it) if has_gain[a] else None for a in range(n_lhs)]
    rhs_refs = [next(it) for _ in range(len(pair_lhs))]
    tiled_refs = [next(it) for _ in range(n_tiled)]
    row_refs = [next(it) for _ in range(n_rows)]
    const_refs = [next(it) for _ in range(n_consts)]
    out_refs = [next(it) for _ in range(n_out)]
    stage_refs = [next(it) for _ in range(n_lhs)]

    @pl.when(pl.program_id(1) == 0)
    def _stage():
        for a in range(n_lhs):
            x = lhs_refs[a][...].astype(F32)
            if has_gain[a]:
                x = x * lax.rsqrt(jnp.mean(x * x, axis=-1, keepdims=True) + EPS)
                x = x * gain_refs[a][...]
            stage_refs[a][...] = x.astype(BF16)

    accs = [jnp.dot(stage_refs[pair_lhs[r]][...], rhs_refs[r][...], preferred_element_type=F32)
            for r in range(len(pair_lhs))]
    outs = epilogue(accs, [t[...] for t in tiled_refs], [t[...] for t in row_refs],
                    [t[...] for t in const_refs])
    for o_ref, o in zip(out_refs, outs):
        o_ref[...] = o.astype(o_ref.dtype)


def fused_mm(name, lhs, rhs, pair_lhs, epilogue, out_dtypes, *, gains=None, tiled=(), rows=(),
             consts=(), tm=512, tn=512):
    m = lhs[0].shape[0]
    n = rhs[0].shape[1]
    tm = _pick_tile(m, tm) if m % LANES == 0 else m
    tn = _pick_tile(n, tn)
    n_lhs = len(lhs)
    gains = list(gains) if gains is not None else [None] * n_lhs
    has_gain = tuple(g is not None for g in gains)

    in_specs, args = [], []
    for x in lhs:
        in_specs.append(pl.BlockSpec((tm, x.shape[1]), lambda i, j: (i, 0)))
        args.append(x)
    for g in gains:
        if g is not None:
            g2 = g.reshape(1, -1).astype(F32)
            in_specs.append(pl.BlockSpec(g2.shape, lambda i, j: (0, 0)))
            args.append(g2)
    for w in rhs:
        in_specs.append(pl.BlockSpec((w.shape[0], tn), lambda i, j: (0, j)))
        args.append(w)
    for t in tiled:
        in_specs.append(pl.BlockSpec((tm, tn), lambda i, j: (i, j)))
        args.append(t)
    for t in rows:
        in_specs.append(pl.BlockSpec((tm, t.shape[1]), lambda i, j: (i, 0)))
        args.append(t)
    for c in consts:
        in_specs.append(pl.BlockSpec(c.shape, lambda i, j: (0, 0)))
        args.append(c)

    kernel = functools.partial(
        _fused_mm_kernel, n_lhs=n_lhs, has_gain=has_gain, pair_lhs=tuple(pair_lhs),
        n_tiled=len(tiled), n_rows=len(rows), n_consts=len(consts), n_out=len(out_dtypes),
        epilogue=epilogue)
    outs = pl.pallas_call(
        kernel,
        grid=(m // tm, n // tn),
        in_specs=in_specs,
        out_specs=[pl.BlockSpec((tm, tn), lambda i, j: (i, j)) for _ in out_dtypes],
        out_shape=[jax.ShapeDtypeStruct((m, n), dt) for dt in out_dtypes],
        scratch_shapes=[pltpu.VMEM((tm, x.shape[1]), BF16) for x in lhs],
        compiler_params=_params(("parallel", "arbitrary")),
        name=name,
    )(*args)
    return outs


def _head_rms_norm(x, block_ones, gain):
    x2 = x * x
    hi = x2.astype(BF16)
    lo = (x2 - hi.astype(F32)).astype(BF16)
    ss = (jnp.dot(hi, block_ones, preferred_element_type=F32)
          + jnp.dot(lo, block_ones, preferred_element_type=F32))
    return x * lax.rsqrt(ss * (1.0 / HEAD_DIM) + EPS) * gain


def _rope(x, cos, sin_signed):
    n = x.shape[1]
    lane = lax.broadcasted_iota(I32, x.shape, 1)
    first_half = (lane & (HEAD_DIM // 2)) == 0
    partner = jnp.where(first_half, pltpu.roll(x, n - HEAD_DIM // 2, 1),
                        pltpu.roll(x, HEAD_DIM // 2, 1))
    return x * cos + partner * sin_signed


def _sigmoid(x):
    return 1.0 / (1.0 + jnp.exp(-x))


def _silu(x):
    return x * _sigmoid(x)


def _rope_tables(pos, width, rotary_width=None):
    half = HEAD_DIM // 2
    inv = ROPE_THETA ** (-jnp.arange(half, dtype=F32) / half)
    ang = pos.astype(F32)[:, None] * inv[None, :]
    cos = jnp.cos(ang)
    sin = jnp.sin(ang)
    cos64 = jnp.concatenate([cos, cos], axis=-1)
    sin64 = jnp.concatenate([-sin, sin], axis=-1)
    rotary_width = width if rotary_width is None else rotary_width
    reps = rotary_width // HEAD_DIM
    cos_t = jnp.tile(cos64, (1, reps))
    sin_t = jnp.tile(sin64, (1, reps))
    if rotary_width < width:
        pad = width - rotary_width
        cos_t = jnp.concatenate([cos_t, jnp.ones((pos.shape[0], pad), F32)], axis=-1)
        sin_t = jnp.concatenate([sin_t, jnp.zeros((pos.shape[0], pad), F32)], axis=-1)
    return cos_t, sin_t


def _block_ones(width):
    head = jnp.arange(width, dtype=I32) // HEAD_DIM
    return (head[:, None] == head[None, :]).astype(BF16)


def _float_key(x):
    b = lax.bitcast_convert_type(x, I32)
    return jnp.where(b >= 0, b, b ^ jnp.int32(0x7FFFFFFF))


def _select_kernel(qi_ref, ki_ref, wi_ref, tri_ref, bias_ref, keys_ref, *, tq, tk, nkb, causal,
                   q_pos0, n_valid, n_sel):
    qb = pl.program_id(1)
    if causal:
        n_act = jnp.minimum(nkb, ((qb + 1) * tq + tk - 1) // tk)
    else:
        n_act = nkb
    q_pos = q_pos0 + qb * tq + lax.broadcasted_iota(I32, (tq, tk), 0)
    q_chunk = lax.shift_right_logical(q_pos, CHUNK_SHIFT)
    w = (wi_ref[0] * (IDX_HEADS ** -0.5)) * (IDX_DIM ** -0.5)
    qi = qi_ref[0]

    def score_block(kb, carry):
        start = pl.multiple_of(kb * tk, tk)
        ki = ki_ref[0, pl.ds(start, tk), :]
        s = jnp.zeros((tq, tk), F32)
        for h in range(IDX_HEADS):
            r = lax.dot_general(qi[:, h * IDX_DIM:(h + 1) * IDX_DIM], ki,
                                (((1,), (1,)), ((), ())), preferred_element_type=F32)
            s = s + jnp.maximum(r, 0.0) * w[:, h:h + 1]
        k_pos = start + lax.broadcasted_iota(I32, (tq, tk), 1)
        admissible = (lax.shift_right_logical(k_pos, CHUNK_SHIFT) <= q_chunk) & (k_pos < n_valid)
        s = jnp.where(admissible, s, -jnp.inf)
        keys_ref[kb] = _float_key(s)
        return carry

    lax.fori_loop(0, n_act, score_block, 0)

    def count(pred):
        def body(kb, acc):
            hit = jnp.where(pred(keys_ref[kb]), 1.0, 0.0)
            for c in range(tk // LANES):
                acc = acc + hit[:, c * LANES:(c + 1) * LANES]
            return acc
        acc = lax.fori_loop(0, n_act, body, jnp.zeros((tq, LANES), F32))
        return jnp.sum(acc, axis=1, keepdims=True)

    def bit_step(i, prefix):
        bit = jnp.left_shift(jnp.int32(1), 31 - i)
        cand = prefix | bit
        cand_signed = cand ^ jnp.int32(INT_MIN)
        cnt = count(lambda k: k >= cand_signed)
        return jnp.where(cnt >= n_sel, cand, prefix)

    prefix = lax.fori_loop(0, 32, bit_step, jnp.zeros((tq, 1), I32))
    tau = prefix ^ jnp.int32(INT_MIN)
    need = n_sel - count(lambda k: k > tau)

    def emit_block(kb, seen):
        key = keys_ref[kb]
        eq = key == tau
        rank = seen + jnp.dot(jnp.where(eq, 1.0, 0.0).astype(BF16), tri_ref[...],
                              preferred_element_type=F32)
        sel = ((key > tau) | (eq & (rank <= need))) & (key > KEY_NEG_INF)
        bias_ref[0, kb] = jnp.where(sel, 0.0, MASKED_BIAS).astype(bias_ref.dtype)
        return rank[:, tk - 1:tk]

    lax.fori_loop(0, n_act, emit_block, jnp.zeros((tq, 1), F32))

    def fill_block(kb, carry):
        bias_ref[0, kb] = jnp.full((tq, tk), MASKED_BIAS, bias_ref.dtype)
        return carry

    lax.fori_loop(n_act, nkb, fill_block, 0)


def select_bias(qi, ki, wi, *, tq, tk, causal, q_pos0, n_valid, n_sel):
    b, t_q, _ = qi.shape
    n_k = ki.shape[1]
    nkb = n_k // tk
    idx = jnp.arange(tk, dtype=I32)
    tri = (idx[:, None] <= idx[None, :]).astype(BF16)
    kernel = functools.partial(_select_kernel, tq=tq, tk=tk, nkb=nkb, causal=causal,
                               q_pos0=q_pos0, n_valid=n_valid, n_sel=n_sel)
    return pl.pallas_call(
        kernel,
        grid=(b, t_q // tq),
        in_specs=[
            pl.BlockSpec((1, tq, qi.shape[2]), lambda bi, qb: (bi, qb, 0)),
            pl.BlockSpec((1, n_k, ki.shape[2]), lambda bi, qb: (bi, 0, 0)),
            pl.BlockSpec((1, tq, wi.shape[2]), lambda bi, qb: (bi, qb, 0)),
            pl.BlockSpec((tk, tk), lambda bi, qb: (0, 0)),
        ],
        out_specs=pl.BlockSpec((1, nkb, tq, tk), lambda bi, qb: (bi, 0, qb, 0)),
        out_shape=jax.ShapeDtypeStruct((b, nkb, t_q, tk), BF16),
        scratch_shapes=[pltpu.VMEM((nkb, tq, tk), I32)],
        compiler_params=_params(("parallel", "parallel")),
        name="select_bias",
    )(qi, ki, wi, tri)


def _attn_kernel(q_ref, kt_ref, v_ref, bias_ref, o_ref, qs_ref, m_ref, acc_ref, *, tq, tk, nkb,
                 causal):
    qb = pl.program_id(1)
    kb = pl.program_id(2)
    last = jnp.minimum(nkb - 1, ((qb + 1) * tq - 1) // tk) if causal else nkb - 1
    rows = Q_PER_KV * tq

    @pl.when(kb == 0)
    def _init():
        m_ref[...] = jnp.full(m_ref.shape, MASKED_BIAS, F32)
        acc_ref[...] = jnp.zeros(acc_ref.shape, F32)
        for h in range(N_HEADS):
            g, r = divmod(h, Q_PER_KV)
            qh = q_ref[0, :, h * HEAD_DIM:(h + 1) * HEAD_DIM].astype(F32) * (HEAD_DIM ** -0.5)
            qs_ref[g, r * tq:(r + 1) * tq, :] = qh.astype(qs_ref.dtype)

    @pl.when(kb <= last)
    def _step():
        bias = bias_ref[0, 0].astype(F32)
        for g in range(N_KV_HEADS):
            s = jnp.dot(qs_ref[g], kt_ref[0, g], preferred_element_type=F32)
            s = (s * LOG2_E).reshape(Q_PER_KV, tq, tk) + bias[None]
            s = s.reshape(rows, tk)
            m_prev = m_ref[g]
            m_blk = jnp.max(s, axis=1, keepdims=True)
            m_new = jnp.maximum(m_prev, jnp.broadcast_to(m_blk, m_prev.shape))
            alpha = jnp.exp2(m_prev - m_new)
            p = jnp.concatenate(
                [jnp.exp2(s[:, c * LANES:(c + 1) * LANES] - m_new) for c in range(tk // LANES)],
                axis=1).astype(BF16)
            acc_ref[g] = alpha * acc_ref[g] + jnp.dot(p, v_ref[0, g], preferred_element_type=F32)
            m_ref[g] = m_new

    @pl.when(kb == nkb - 1)
    def _finish():
        for h in range(N_HEADS):
            g, r = divmod(h, Q_PER_KV)
            a = acc_ref[g, r * tq:(r + 1) * tq, :]
            out = a[:, :HEAD_DIM] / a[:, HEAD_DIM:HEAD_DIM + 1]
            o_ref[0, :, h * HEAD_DIM:(h + 1) * HEAD_DIM] = out.astype(o_ref.dtype)


def masked_attention(q, k, v, bias, *, tq, causal):
    b, t_q, d_q = q.shape
    n_k = k.shape[1]
    nkb, tk = bias.shape[1], bias.shape[3]
    kt = jnp.transpose(k.reshape(b, n_k, N_KV_HEADS, HEAD_DIM), (0, 2, 3, 1))
    v4 = jnp.transpose(v.reshape(b, n_k, N_KV_HEADS, HEAD_DIM), (0, 2, 1, 3))
    v_ext = jnp.concatenate(
        [v4, jnp.ones((b, N_KV_HEADS, n_k, 1), BF16),
         jnp.zeros((b, N_KV_HEADS, n_k, LANES - HEAD_DIM - 1), BF16)], axis=-1)

    def kv_block(qb, kb):
        return jnp.minimum(kb, ((qb + 1) * tq - 1) // tk) if causal else kb

    kernel = functools.partial(_attn_kernel, tq=tq, tk=tk, nkb=nkb, causal=causal)
    rows = Q_PER_KV * tq
    return pl.pallas_call(
        kernel,
        grid=(b, t_q // tq, nkb),
        in_specs=[
            pl.BlockSpec((1, tq, d_q), lambda bi, qb, kb: (bi, qb, 0)),
            pl.BlockSpec((1, N_KV_HEADS, HEAD_DIM, tk),
                         lambda bi, qb, kb: (bi, 0, 0, kv_block(qb, kb))),
            pl.BlockSpec((1, N_KV_HEADS, tk, LANES),
                         lambda bi, qb, kb: (bi, 0, kv_block(qb, kb), 0)),
            pl.BlockSpec((1, 1, tq, tk), lambda bi, qb, kb: (bi, kv_block(qb, kb), qb, 0)),
        ],
        out_specs=pl.BlockSpec((1, tq, d_q), lambda bi, qb, kb: (bi, qb, 0)),
        out_shape=jax.ShapeDtypeStruct((b, t_q, d_q), BF16),
        scratch_shapes=[pltpu.VMEM((N_KV_HEADS, rows, HEAD_DIM), BF16),
                        pltpu.VMEM((N_KV_HEADS, rows, LANES), F32),
                        pltpu.VMEM((N_KV_HEADS, rows, LANES), F32)],
        compiler_params=_params(("parallel", "parallel", "arbitrary")),
        name="masked_attention",
    )(q, kt, v_ext, bias)


def _conv_kernel(cur_ref, nxt_ref, w_ref, b_ref, g_ref, beta_ref, o_ref, buf_ref, conv_ref, *,
                 tm, cw):
    buf_ref[0:tm, :] = cur_ref[0]
    buf_ref[tm:tm + CONV_HALO, :] = nxt_ref[0]
    lead = CONV_HALO - (CONV_WIDTH - 1)
    c_total = buf_ref.shape[1]
    rw = min(tm, 128)
    for c in range(c_total // cw):
        cs = slice(c * cw, (c + 1) * cw)
        for r in range(tm // rw):
            acc = jnp.zeros((rw, cw), F32) + b_ref[:, cs]
            for j in range(CONV_WIDTH):
                r0 = lead + j + r * rw
                acc = acc + w_ref[j:j + 1, cs] * buf_ref[r0:r0 + rw, cs]
            conv_ref[r * rw:(r + 1) * rw, cs] = acc
    x = conv_ref[...]
    mu = jnp.mean(x, axis=-1, keepdims=True)
    xc = x - mu
    y = xc * lax.rsqrt(jnp.mean(xc * xc, axis=-1, keepdims=True) + EPS)
    y = y * g_ref[...] + beta_ref[...]
    o_ref[0] = _silu(y).astype(o_ref.dtype)


def conv_ln_silu(hist, w_dw, b_dw, ln_g, ln_b, *, tm):
    b, lp, c = hist.shape
    l = lp - CONV_HALO
    kernel = functools.partial(_conv_kernel, tm=tm, cw=LANES)
    vec = lambda a: a.reshape(1, c).astype(F32)
    return pl.pallas_call(
        kernel,
        grid=(b, l // tm),
        in_specs=[
            pl.BlockSpec((1, tm, c), lambda bi, i: (bi, i, 0)),
            pl.BlockSpec((1, CONV_HALO, c), lambda bi, i: (bi, (i + 1) * (tm // CONV_HALO), 0)),
            pl.BlockSpec((CONV_WIDTH, c), lambda bi, i: (0, 0)),
            pl.BlockSpec((1, c), lambda bi, i: (0, 0)),
            pl.BlockSpec((1, c), lambda bi, i: (0, 0)),
            pl.BlockSpec((1, c), lambda bi, i: (0, 0)),
        ],
        out_specs=pl.BlockSpec((1, tm, c), lambda bi, i: (bi, i, 0)),
        out_shape=jax.ShapeDtypeStruct((b, l, c), BF16),
        scratch_shapes=[pltpu.VMEM((tm + CONV_HALO, c), F32), pltpu.VMEM((tm, c), F32)],
        compiler_params=_params(("parallel", "parallel")),
        name="conv_ln_silu",
    )(hist, hist, w_dw.astype(F32), vec(b_dw), vec(ln_g), vec(ln_b))


def _router_kernel(h_ref, g_ref, wr_ref, u_ref, gate_ref, *, n_experts):
    x = h_ref[...]
    x = x * lax.rsqrt(jnp.mean(x * x, axis=-1, keepdims=True) + EPS) * g_ref[...]
    u = x.astype(BF16)
    u_ref[...] = u
    logits = jnp.dot(u, wr_ref[...], preferred_element_type=F32)
    lane = lax.broadcasted_iota(I32, logits.shape, 1).astype(F32)
    logits = jnp.where(lane < n_experts, logits, -jnp.inf)
    top1 = jnp.max(logits, axis=1, keepdims=True)
    idx1 = jnp.min(jnp.where(logits == top1, lane, float(LANES)), axis=1, keepdims=True)
    rest = jnp.where(lane == idx1, -jnp.inf, logits)
    top2 = jnp.max(rest, axis=1, keepdims=True)
    idx2 = jnp.min(jnp.where(rest == top2, lane, float(LANES)), axis=1, keepdims=True)
    e2 = jnp.exp(top2 - top1)
    denom = 1.0 + e2
    gate_ref[...] = jnp.where(lane == idx1, 1.0 / denom, jnp.where(lane == idx2, e2 / denom, 0.0))


def moe_router(h, gain, w_router_padded, n_experts, *, tm):
    m, d = h.shape
    kernel = functools.partial(_router_kernel, n_experts=n_experts)
    return pl.pallas_call(
        kernel,
        grid=(m // tm,),
        in_specs=[pl.BlockSpec((tm, d), lambda i: (i, 0)),
                  pl.BlockSpec((1, d), lambda i: (0, 0)),
                  pl.BlockSpec((d, LANES), lambda i: (0, 0))],
        out_specs=[pl.BlockSpec((tm, d), lambda i: (i, 0)),
                   pl.BlockSpec((tm, LANES), lambda i: (i, 0))],
        out_shape=[jax.ShapeDtypeStruct((m, d), BF16), jax.ShapeDtypeStruct((m, LANES), F32)],
        compiler_params=_params(("parallel",)),
        name="moe_router",
    )(h, gain.reshape(1, d).astype(F32), w_router_padded)


def _expert_up_kernel(u_ref, wg_ref, wu_ref, o_ref):
    u = u_ref[...]
    a = jnp.dot(u, wg_ref[0], preferred_element_type=F32)
    b = jnp.dot(u, wu_ref[0], preferred_element_type=F32)
    o_ref[0] = (_silu(a) * b).astype(o_ref.dtype)


def expert_up(u, w_gate, w_up, *, tm, tn):
    m, d = u.shape
    e, _, f = w_gate.shape
    return pl.pallas_call(
        _expert_up_kernel,
        grid=(e, m // tm, f // tn),
        in_specs=[pl.BlockSpec((tm, d), lambda ei, i, j: (i, 0)),
                  pl.BlockSpec((1, d, tn), lambda ei, i, j: (ei, 0, j)),
                  pl.BlockSpec((1, d, tn), lambda ei, i, j: (ei, 0, j))],
        out_specs=pl.BlockSpec((1, tm, tn), lambda ei, i, j: (ei, i, j)),
        out_shape=jax.ShapeDtypeStruct((e, m, f), BF16),
        compiler_params=_params(("parallel", "parallel", "parallel")),
        name="expert_up",
    )(u, w_gate, w_up)


def _expert_down_kernel(hb_ref, wd_ref, gate_ref, res_ref, o_ref, acc_ref, *, n_experts):
    e = pl.program_id(2)

    @pl.when(e == 0)
    def _init():
        acc_ref[...] = jnp.zeros(acc_ref.shape, F32)

    gates = gate_ref[...]
    lane = lax.broadcasted_iota(I32, gates.shape, 1)
    g = jnp.sum(jnp.where(lane == e, gates, 0.0), axis=1, keepdims=True)
    acc_ref[...] += g * jnp.dot(hb_ref[0], wd_ref[0], preferred_element_type=F32)

    @pl.when(e == n_experts - 1)
    def _finish():
        o_ref[...] = res_ref[...] + acc_ref[...]


def expert_down(hb, w_down, gates, res, *, tm, tn):
    e, m, f = hb.shape
    d = w_down.shape[2]
    kernel = functools.partial(_expert_down_kernel, n_experts=e)
    return pl.pallas_call(
        kernel,
        grid=(m // tm, d // tn, e),
        in_specs=[pl.BlockSpec((1, tm, f), lambda i, j, ei: (ei, i, 0)),
                  pl.BlockSpec((1, f, tn), lambda i, j, ei: (ei, 0, j)),
                  pl.BlockSpec((tm, LANES), lambda i, j, ei: (i, 0)),
                  pl.BlockSpec((tm, tn), lambda i, j, ei: (i, j))],
        out_specs=pl.BlockSpec((tm, tn), lambda i, j, ei: (i, j)),
        out_shape=jax.ShapeDtypeStruct((m, d), F32),
        scratch_shapes=[pltpu.VMEM((tm, tn), F32)],
        compiler_params=_params(("parallel", "parallel", "arbitrary")),
        name="expert_down",
    )(hb, w_down, gates, res)


def _split_w_in(w_in_layer, d_model):
    attn_dim = N_HEADS * HEAD_DIM
    kv_dim = N_KV_HEADS * HEAD_DIM
    sizes = (attn_dim, kv_dim, kv_dim, IDX_HEADS * IDX_DIM, IDX_DIM, IDX_HEADS, d_model, d_model,
             d_model, d_model)
    parts, off = [], 0
    for s in sizes:
        parts.append(w_in_layer[:, off:off + s])
        off += s
    assert off == w_in_layer.shape[1]
    return parts


def _layer_stack(x, p, weights, cache, *, past):
    bsz, length, d_model = x.shape
    m = bsz * length
    kv_dim = N_KV_HEADS * HEAD_DIM
    pos = past + jnp.arange(length, dtype=I32)
    row_pos = jnp.tile(pos, bsz)
    depth = weights["w_in"].shape[0]

    cos_q, sin_q = _rope_tables(row_pos, 512)
    cos_k, sin_k = _rope_tables(row_pos, kv_dim)
    cos_i, sin_i = _rope_tables(row_pos, LANES, rotary_width=IDX_DIM)
    ones_q = _block_ones(512)
    ones_k = _block_ones(kv_dim)

    n_keys = past + length
    n_sel = min(TOPK_MAX, n_keys // 4)
    if past == 0:
        tq_sel, tk, tq_attn = min(128, length), min(512, length), min(256, length)
        n_keys_pad = n_keys
    else:
        tq_sel = tq_attn = length
        tk = 384
        n_keys_pad = -(-n_keys // tk) * tk

    h = x.reshape(m, d_model)
    ks, vs, kis, convs = [], [], [], []
    for i in range(depth):
        wq, wk, wv, wqi, wki, wwi, wglu_a, wglu_g, wga, wgb = _split_w_in(weights["w_in"][i], d_model)
        w_kiwi = jnp.concatenate(
            [wki, wwi, jnp.zeros((d_model, LANES - IDX_DIM - IDX_HEADS), BF16)], axis=1)
        g_mix = weights["g_mix"][i]

        qn_gain = jnp.tile(weights["q_norm_g"][i].astype(F32), 512 // HEAD_DIM).reshape(1, 512)
        kn_gain = jnp.tile(weights["k_norm_g"][i].astype(F32), kv_dim // HEAD_DIM).reshape(1, kv_dim)

        def norm_rope(accs, tiled, rows, consts):
            y = _head_rms_norm(accs[0], consts[0], consts[1])
            y = _rope(y, rows[0], rows[1])
            return (y, y)

        (q_bf,) = fused_mm("proj_q", [h], [wq], [0],
                           lambda a, t, r, c: (_rope(_head_rms_norm(a[0], c[0], c[1]), r[0], r[1]),),
                           [BF16], gains=[g_mix], rows=[cos_q, sin_q], consts=[ones_q, qn_gain])
        k_f32, k_bf = fused_mm("proj_k", [h], [wk], [0], norm_rope, [F32, BF16], gains=[g_mix],
                               rows=[cos_k, sin_k], consts=[ones_k, kn_gain])
        v_f32, v_bf = fused_mm("proj_v", [h], [wv], [0], lambda a, t, r, c: (a[0], a[0]),
                               [F32, BF16], gains=[g_mix])
        (qi_bf,) = fused_mm("proj_qi", [h], [wqi], [0],
                            lambda a, t, r, c: (_rope(a[0], r[0], r[1]),),
                            [BF16], gains=[g_mix], rows=[cos_q, sin_q])
        (kiwi,) = fused_mm("proj_kiwi", [h], [w_kiwi], [0],
                           lambda a, t, r, c: (_rope(a[0], r[0], r[1]),),
                           [F32], gains=[g_mix], rows=[cos_i, sin_i])
        (glu,) = fused_mm("proj_glu", [h], [wglu_a, wglu_g], [0, 0],
                          lambda a, t, r, c: (a[0] * _sigmoid(a[1]),), [F32], gains=[g_mix])
        gate_a, gate_b = fused_mm("proj_gates", [h], [wga, wgb], [0, 0],
                                  lambda a, t, r, c: (_sigmoid(a[0]), _sigmoid(a[1])),
                                  [F32, F32], gains=[g_mix])

        ki_f32 = kiwi[:, :IDX_DIM]
        wi = kiwi[:, IDX_DIM:IDX_DIM + IDX_HEADS]

        k3 = k_bf.reshape(bsz, length, kv_dim)
        v3 = v_bf.reshape(bsz, length, kv_dim)
        ki3 = ki_f32.astype(BF16).reshape(bsz, length, IDX_DIM)
        if past:
            pad = n_keys_pad - n_keys
            k3 = jnp.concatenate([cache["k"][i].reshape(bsz, past, kv_dim).astype(BF16), k3,
                                  jnp.zeros((bsz, pad, kv_dim), BF16)], axis=1)
            v3 = jnp.concatenate([cache["v"][i].reshape(bsz, past, kv_dim).astype(BF16), v3,
                                  jnp.zeros((bsz, pad, kv_dim), BF16)], axis=1)
            ki3 = jnp.concatenate([cache["kidx"][i].astype(BF16), ki3,
                                   jnp.zeros((bsz, pad, IDX_DIM), BF16)], axis=1)
        bias = select_bias(qi_bf.reshape(bsz, length, -1), ki3, wi.reshape(bsz, length, IDX_HEADS),
                           tq=tq_sel, tk=tk, causal=(past == 0), q_pos0=past, n_valid=n_keys,
                           n_sel=n_sel)
        attn = masked_attention(q_bf.reshape(bsz, length, -1), k3, v3, bias, tq=tq_attn,
                                causal=(past == 0))
        attn = attn.reshape(m, -1)

        glu3 = glu.reshape(bsz, length, d_model)
        if past:
            ctx = cache["conv"][i]
        else:
            ctx = jnp.zeros((bsz, CONV_WIDTH - 1, d_model), F32)
        lead = jnp.zeros((bsz, CONV_HALO - (CONV_WIDTH - 1), d_model), F32)
        l_pad = -(-length // CONV_HALO) * CONV_HALO
        tail = jnp.zeros((bsz, l_pad - length, d_model), F32)
        hist = jnp.concatenate([lead, ctx, glu3, tail], axis=1)
        conv_state = hist[:, CONV_HALO - (CONV_WIDTH - 1) + length:CONV_HALO + length]
        z = conv_ln_silu(hist, weights["w_dw"][i], weights["b_dw"][i], weights["conv_ln_g"][i],
                         weights["conv_ln_b"][i], tm=min(256, l_pad))
        z = z[:, :length].reshape(m, d_model)

        (merged,) = fused_mm("mix", [attn, z], [weights["w_o_attn"][i], weights["w_o_conv"][i]],
                             [0, 1], lambda a, t, r, c: (t[0] * a[0] + t[1] * a[1],), [BF16],
                             tiled=[gate_a, gate_b])
        (h,) = fused_mm("out_proj", [merged], [weights["w_out"][i]], [0],
                        lambda a, t, r, c: (t[0] + a[0],), [F32], tiled=[h])

        j = i // 2
        if i % 2 == 0:
            (hb,) = fused_mm("ffn_up", [h], [weights["ffn_w_gate"][j], weights["ffn_w_up"][j]],
                             [0, 0], lambda a, t, r, c: (_silu(a[0]) * a[1],), [BF16],
                             gains=[weights["g_ffn"][i]], tn=1408)
            (h,) = fused_mm("ffn_down", [hb], [weights["ffn_w_down"][j]], [0],
                            lambda a, t, r, c: (t[0] + a[0],), [F32], tiled=[h])
        else:
            n_experts = weights["moe_router"].shape[2]
            w_router = jnp.concatenate(
                [weights["moe_router"][j], jnp.zeros((d_model, LANES - n_experts), BF16)], axis=1)
            tm_moe = _pick_tile(m, 512)
            u, gates = moe_router(h, weights["g_ffn"][i], w_router, n_experts, tm=tm_moe)
            hb = expert_up(u, weights["moe_w_gate"][j], weights["moe_w_up"][j], tm=tm_moe,
                           tn=_pick_tile(weights["moe_w_gate"].shape[3], 512))
            h = expert_down(hb, weights["moe_w_down"][j], gates, h, tm=tm_moe,
                            tn=_pick_tile(d_model, 512))

        (h,) = fused_mm("ple", [h, p[i].reshape(m, -1)],
                        [weights["w_ple_gate"][i], weights["w_ple_proj"][i]],
                        [0, 1], lambda a, t, r, c: (t[0] + _sigmoid(a[0]) * a[1],), [F32],
                        gains=[weights["g_ple"][i], None], tiled=[h])

        ks.append(k_f32.reshape(bsz, length, N_KV_HEADS, HEAD_DIM))
        vs.append(v_f32.reshape(bsz, length, N_KV_HEADS, HEAD_DIM))
        kis.append(ki_f32.reshape(bsz, length, IDX_DIM))
        convs.append(conv_state)
    return (h.reshape(bsz, length, d_model), jnp.stack(ks), jnp.stack(vs), jnp.stack(kis),
            jnp.stack(convs))


def kernel(x_prompt, x_sample, cache_k, cache_v, cache_kidx, state_conv, p_prompt, p_sample, w_in, q_norm_g, k_norm_g, w_dw, b_dw, conv_ln_g, conv_ln_b, w_o_attn, w_o_conv, w_out, g_mix, g_ffn, ffn_w_gate, ffn_w_up, ffn_w_down, moe_router, moe_w_gate, moe_w_up, moe_w_down, g_ple, w_ple_gate, w_ple_proj):
    bf = lambda a: a.astype(BF16)
    weights = dict(
        w_in=bf(w_in), q_norm_g=q_norm_g, k_norm_g=k_norm_g, w_dw=w_dw, b_dw=b_dw,
        conv_ln_g=conv_ln_g, conv_ln_b=conv_ln_b, w_o_attn=bf(w_o_attn), w_o_conv=bf(w_o_conv),
        w_out=bf(w_out), g_mix=g_mix, g_ffn=g_ffn, ffn_w_gate=bf(ffn_w_gate),
        ffn_w_up=bf(ffn_w_up), ffn_w_down=bf(ffn_w_down), moe_router=bf(moe_router),
        moe_w_gate=bf(moe_w_gate), moe_w_up=bf(moe_w_up), moe_w_down=bf(moe_w_down), g_ple=g_ple,
        w_ple_gate=bf(w_ple_gate), w_ple_proj=bf(w_ple_proj))
    cache = dict(k=cache_k, v=cache_v, kidx=cache_kidx, conv=state_conv)
    past_len = cache_k.shape[2]
    y_p, k_p, v_p, ki_p, conv_p = _layer_stack(x_prompt, p_prompt, weights, None, past=0)
    y_s, k_s, v_s, ki_s, conv_s = _layer_stack(x_sample, p_sample, weights, cache, past=past_len)
    return (y_p, y_s, k_p, v_p, ki_p, conv_p, k_s, v_s, ki_s, conv_s)
```

```python
import functools

import jax
import jax.numpy as jnp
from jax import lax
from jax.experimental import pallas as pl
from jax.experimental.pallas import tpu as pltpu

F32 = jnp.float32
BF16 = jnp.bfloat16
I32 = jnp.int32
I16 = jnp.int16

CHUNK = 64
CHUNK_SHIFT = CHUNK.bit_length() - 1
assert 1 << CHUNK_SHIFT == CHUNK
N_HEADS = 16
HEAD_DIM = 64
N_KV_HEADS = 4
Q_PER_KV = N_HEADS // N_KV_HEADS
IDX_HEADS = 8
IDX_DIM = 64
TOPK_MAX = 256
ROPE_THETA = 10000.0
CONV_WIDTH = 31
TOP_K_EXPERTS = 2
EPS = 1e-6

LANES = 128
CONV_HALO = 32
MASKED_BIAS = -1e30
LOG2_E = 1.4426950408889634
HALF_RANGE = 1 << 15
HI_NEG_INF = (-2139095041) >> 16
VMEM_LIMIT = 52 * 1024 * 1024


def _params(semantics):
    return pltpu.CompilerParams(dimension_semantics=semantics, vmem_limit_bytes=VMEM_LIMIT)


def _pick_tile(n, pref):
    if n <= pref:
        return n
    best = None
    for t in range(LANES, pref + 1, LANES):
        if n % t == 0:
            best = t
    assert best is not None, (n, pref)
    return best


def _fused_mm_kernel(*refs, n_lhs, has_gain, pair_lhs, n_tiled, n_rows, n_consts, n_out,
                     epilogue):
    it = iter(refs)
    lhs_refs = [next(it) for _ in range(n_lhs)]
    gain_refs = [next(it) if has_gain[a] else None for a in range(n_lhs)]
    rhs_refs = [next(it) for _ in range(len(pair_lhs))]
    tiled_refs = [next(it) for _ in range(n_tiled)]
    row_refs = [next(it) for _ in range(n_rows)]
    const_refs = [next(it) for _ in range(n_consts)]
    out_refs = [next(it) for _ in range(n_out)]
    stage_refs = [next(it) for _ in range(n_lhs)]

    @pl.when(pl.program_id(1) == 0)
    def _stage():
        for a in range(n_lhs):
            x = lhs_refs[a][...].astype(F32)
            if has_gain[a]:
                x = x * lax.rsqrt(jnp.mean(x * x, axis=-1, keepdims=True) + EPS)
                x = x * gain_refs[a][...]
            stage_refs[a][...] = x.astype(BF16)

    accs = [jnp.dot(stage_refs[pair_lhs[r]][...], rhs_refs[r][...], preferred_element_type=F32)
            for r in range(len(pair_lhs))]
    outs = epilogue(accs, [t[...] for t in tiled_refs], [t[...] for t in row_refs],
                    [t[...] for t in const_refs])
    for o_ref, o in zip(out_refs, outs):
        o_ref[...] = o.astype(o_ref.dtype)


def fused_mm(name, lhs, rhs, pair_lhs, epilogue, out_dtypes, *, gains=None, tiled=(), rows=(),
             consts=(), tm=512, tn=512):
    m = lhs[0].shape[0]
    n = rhs[0].shape[1]
    tm = _pick_tile(m, tm) if m % LANES == 0 else m
    tn = _pick_tile(n, tn)
    n_lhs = len(lhs)
    gains = list(gains) if gains is not None else [None] * n_lhs
    has_gain = tuple(g is not None for g in gains)

    in_specs, args = [], []
    for x in lhs:
        in_specs.append(pl.BlockSpec((tm, x.shape[1]), lambda i, j: (i, 0)))
        args.append(x)
    for g in gains:
        if g is not None:
            g2 = g.reshape(1, -1).astype(F32)
            in_specs.append(pl.BlockSpec(g2.shape, lambda i, j: (0, 0)))
            args.append(g2)
    for w in rhs:
        in_specs.append(pl.BlockSpec((w.shape[0], tn), lambda i, j: (0, j)))
        args.append(w)
    for t in tiled:
        in_specs.append(pl.BlockSpec((tm, tn), lambda i, j: (i, j)))
        args.append(t)
    for t in rows:
        in_specs.append(pl.BlockSpec((tm, t.shape[1]), lambda i, j: (i, 0)))
        args.append(t)
    for c in consts:
        in_specs.append(pl.BlockSpec(c.shape, lambda i, j: (0, 0)))
        args.append(c)

    kernel = functools.partial(
        _fused_mm_kernel, n_lhs=n_lhs, has_gain=has_gain, pair_lhs=tuple(pair_lhs),
        n_tiled=len(tiled), n_rows=len(rows), n_consts=len(consts), n_out=len(out_dtypes),
        epilogue=epilogue)
    outs = pl.pallas_call(
        kernel,
        grid=(m // tm, n // tn),
        in_specs=in_specs,
        out_specs=[pl.BlockSpec((tm, tn), lambda i, j: (i, j)) for _ in out_dtypes],
        out_shape=[jax.ShapeDtypeStruct((m, n), dt) for dt in out_dtypes],
        scratch_shapes=[pltpu.VMEM((tm, x.shape[1]), BF16) for x in lhs],
        compiler_params=_params(("parallel", "arbitrary")),
        name=name,
    )(*args)
    return outs


def _head_rms_norm(x, block_ones, gain):
    x2 = x * x
    hi = x2.astype(BF16)
    lo = (x2 - hi.astype(F32)).astype(BF16)
    ss = (jnp.dot(hi, block_ones, preferred_element_type=F32)
          + jnp.dot(lo, block_ones, preferred_element_type=F32))
    return x * lax.rsqrt(ss * (1.0 / HEAD_DIM) + EPS) * gain


def _rope(x, cos, sin_signed):
    n = x.shape[1]
    lane = lax.broadcasted_iota(I32, x.shape, 1)
    first_half = (lane & (HEAD_DIM // 2)) == 0
    partner = jnp.where(first_half, pltpu.roll(x, n - HEAD_DIM // 2, 1),
                        pltpu.roll(x, HEAD_DIM // 2, 1))
    return x * cos + partner * sin_signed


def _sigmoid(x):
    return 1.0 / (1.0 + jnp.exp(-x))


def _silu(x):
    return x * _sigmoid(x)


def _rope_tables(pos, width, rotary_width=None):
    half = HEAD_DIM // 2
    inv = ROPE_THETA ** (-jnp.arange(half, dtype=F32) / half)
    ang = pos.astype(F32)[:, None] * inv[None, :]
    cos = jnp.cos(ang)
    sin = jnp.sin(ang)
    cos64 = jnp.concatenate([cos, cos], axis=-1)
    sin64 = jnp.concatenate([-sin, sin], axis=-1)
    rotary_width = width if rotary_width is None else rotary_width
    reps = rotary_width // HEAD_DIM
    cos_t = jnp.tile(cos64, (1, reps))
    sin_t = jnp.tile(sin64, (1, reps))
    if rotary_width < width:
        pad = width - rotary_width
        cos_t = jnp.concatenate([cos_t, jnp.ones((pos.shape[0], pad), F32)], axis=-1)
        sin_t = jnp.concatenate([sin_t, jnp.zeros((pos.shape[0], pad), F32)], axis=-1)
    return cos_t, sin_t


def _block_ones(width):
    head = jnp.arange(width, dtype=I32) // HEAD_DIM
    return (head[:, None] == head[None, :]).astype(BF16)


def _float_key(x):
    b = lax.bitcast_convert_type(x, I32)
    return jnp.where(b >= 0, b, b ^ jnp.int32(0x7FFFFFFF))


def _select_kernel(qi_ref, ki_ref, wi_ref, tri_ref, bias_ref, hi_ref, lo_ref, e_ref, *, tq, tk, nkb,
                   causal, q_pos0, n_valid, n_sel):
    qb = pl.program_id(1)
    n_chunks = tk // LANES
    if causal:
        n_act = jnp.minimum(nkb, ((qb + 1) * tq + tk - 1) // tk)
    else:
        n_act = nkb
    q_pos = q_pos0 + qb * tq + lax.broadcasted_iota(I32, (tq, tk), 0)
    q_chunk = lax.shift_right_logical(q_pos, CHUNK_SHIFT)
    w = (wi_ref[0] * (IDX_HEADS ** -0.5)) * (IDX_DIM ** -0.5)
    qi = qi_ref[0]

    def score_block(kb, carry):
        start = pl.multiple_of(kb * tk, tk)
        ki = ki_ref[0, pl.ds(start, tk), :]
        s = jnp.zeros((tq, tk), F32)
        for h in range(IDX_HEADS):
            r = lax.dot_general(qi[:, h * IDX_DIM:(h + 1) * IDX_DIM], ki,
                                (((1,), (1,)), ((), ())), preferred_element_type=F32)
            s = s + jnp.maximum(r, 0.0) * w[:, h:h + 1]
        k_pos = start + lax.broadcasted_iota(I32, (tq, tk), 1)
        admissible = (lax.shift_right_logical(k_pos, CHUNK_SHIFT) <= q_chunk) & (k_pos < n_valid)
        key = _float_key(jnp.where(admissible, s, -jnp.inf))
        hi_ref[kb] = lax.shift_right_arithmetic(key, 16).astype(I16)
        lo_ref[kb] = ((key & 0xFFFF) - HALF_RANGE).astype(I16)
        return carry

    lax.fori_loop(0, n_act, score_block, 0)

    def lanes16(x):
        return jnp.broadcast_to(x, (tq, LANES)).astype(I16)

    def count(hit_fn):
        def body(kb, acc):
            for c in range(n_chunks):
                acc = acc + jnp.where(hit_fn(kb, c), jnp.int16(1), jnp.int16(0))
            return acc
        acc = lax.fori_loop(0, n_act, body, jnp.zeros((tq, LANES), I16))
        return jnp.sum(acc.astype(F32), axis=1, keepdims=True)

    def chunk(ref, kb, c):
        return ref[kb, :, c * LANES:(c + 1) * LANES]

    def search16(src_ref, base):
        def step(i, prefix):
            cand = prefix | jnp.left_shift(jnp.int32(1), 15 - i)
            cb = lanes16(cand - HALF_RANGE)
            cnt = base + count(lambda kb, c: chunk(src_ref, kb, c) >= cb)
            return jnp.where(cnt >= n_sel, cand, prefix)
        return lax.fori_loop(0, 16, step, jnp.zeros((tq, 1), I32)) - HALF_RANGE

    tau_hi32 = search16(hi_ref, 0.0)
    tau_hi = lanes16(tau_hi32)

    def bucket_block(kb, carry):
        for c in range(n_chunks):
            cs = slice(c * LANES, (c + 1) * LANES)
            e_ref[kb, :, cs] = jnp.where(hi_ref[kb, :, cs] == tau_hi, lo_ref[kb, :, cs],
                                         jnp.int16(-HALF_RANGE))
        return carry

    lax.fori_loop(0, n_act, bucket_block, 0)
    above = count(lambda kb, c: chunk(hi_ref, kb, c) > tau_hi)
    tau_lo32 = search16(e_ref, above)
    tau_lo = lanes16(tau_lo32)

    n_gt = above + count(lambda kb, c: chunk(e_ref, kb, c) > tau_lo)
    n_eq = count(lambda kb, c: (chunk(hi_ref, kb, c) == tau_hi) & (chunk(lo_ref, kb, c) == tau_lo))
    need = n_sel - n_gt
    all_ties_taken = jnp.max(jnp.where(need >= n_eq, 0.0, 1.0)) == 0.0

    def emit_all_ties(kb, carry):
        for c in range(n_chunks):
            cs = slice(c * LANES, (c + 1) * LANES)
            hi = hi_ref[kb, :, cs]
            sel = (hi > tau_hi) | ((hi == tau_hi) & (lo_ref[kb, :, cs] >= tau_lo))
            sel = sel & (hi > HI_NEG_INF)
            bias_ref[0, kb, :, cs] = jnp.where(sel, jnp.zeros((), bias_ref.dtype),
                                               jnp.full((), MASKED_BIAS, bias_ref.dtype))
        return carry

    def emit_ranked(kb, seen):
        hi = hi_ref[kb].astype(I32)
        lo = lo_ref[kb].astype(I32)
        in_bucket = hi == tau_hi32
        eq = in_bucket & (lo == tau_lo32)
        gt = (hi > tau_hi32) | (in_bucket & (lo > tau_lo32))
        rank = seen + jnp.dot(jnp.where(eq, 1.0, 0.0).astype(BF16), tri_ref[...],
                              preferred_element_type=F32)
        sel = (gt | (eq & (rank <= need))) & (hi > HI_NEG_INF)
        bias_ref[0, kb] = jnp.where(sel, 0.0, MASKED_BIAS).astype(bias_ref.dtype)
        return rank[:, tk - 1:tk]

    @pl.when(all_ties_taken)
    def _fast():
        lax.fori_loop(0, n_act, emit_all_ties, 0)

    @pl.when(jnp.logical_not(all_ties_taken))
    def _ranked():
        lax.fori_loop(0, n_act, emit_ranked, jnp.zeros((tq, 1), F32))

    def fill_block(kb, carry):
        bias_ref[0, kb] = jnp.full((tq, tk), MASKED_BIAS, bias_ref.dtype)
        return carry

    lax.fori_loop(n_act, nkb, fill_block, 0)


def select_bias(qi, ki, wi, *, tq, tk, causal, q_pos0, n_valid, n_sel):
    b, t_q, _ = qi.shape
    n_k = ki.shape[1]
    nkb = n_k // tk
    idx = jnp.arange(tk, dtype=I32)
    tri = (idx[:, None] <= idx[None, :]).astype(BF16)
    kernel = functools.partial(_select_kernel, tq=tq, tk=tk, nkb=nkb, causal=causal,
                               q_pos0=q_pos0, n_valid=n_valid, n_sel=n_sel)
    return pl.pallas_call(
        kernel,
        grid=(b, t_q // tq),
        in_specs=[
            pl.BlockSpec((1, tq, qi.shape[2]), lambda bi, qb: (bi, qb, 0)),
            pl.BlockSpec((1, n_k, ki.shape[2]), lambda bi, qb: (bi, 0, 0)),
            pl.BlockSpec((1, tq, wi.shape[2]), lambda bi, qb: (bi, qb, 0)),
            pl.BlockSpec((tk, tk), lambda bi, qb: (0, 0)),
        ],
        out_specs=pl.BlockSpec((1, nkb, tq, tk), lambda bi, qb: (bi, 0, qb, 0)),
        out_shape=jax.ShapeDtypeStruct((b, nkb, t_q, tk), BF16),
        scratch_shapes=[pltpu.VMEM((nkb, tq, tk), I16) for _ in range(3)],
        compiler_params=_params(("parallel", "parallel")),
        name="select_bias",
    )(qi, ki, wi, tri)


def _attn_kernel(q_ref, kt_ref, v_ref, bias_ref, o_ref, qs_ref, m_ref, acc_ref, *, tq, tk, nkb,
                 causal):
    qb = pl.program_id(1)
    kb = pl.program_id(2)
    last = jnp.minimum(nkb - 1, ((qb + 1) * tq - 1) // tk) if causal else nkb - 1
    rows = Q_PER_KV * tq

    @pl.when(kb == 0)
    def _init():
        m_ref[...] = jnp.full(m_ref.shape, MASKED_BIAS, F32)
        acc_ref[...] = jnp.zeros(acc_ref.shape, F32)
        for h in range(N_HEADS):
            g, r = divmod(h, Q_PER_KV)
            qh = q_ref[0, :, h * HEAD_DIM:(h + 1) * HEAD_DIM].astype(F32) * (HEAD_DIM ** -0.5)
            qs_ref[g, r * tq:(r + 1) * tq, :] = qh.astype(qs_ref.dtype)

    @pl.when(kb <= last)
    def _step():
        bias = bias_ref[0, 0].astype(F32)
        for g in range(N_KV_HEADS):
            s = jnp.dot(qs_ref[g], kt_ref[0, g], preferred_element_type=F32)
            s = (s * LOG2_E).reshape(Q_PER_KV, tq, tk) + bias[None]
            s = s.reshape(rows, tk)
            m_prev = m_ref[g]
            m_blk = jnp.max(s, axis=1, keepdims=True)
            m_new = jnp.maximum(m_prev, jnp.broadcast_to(m_blk, m_prev.shape))
            alpha = jnp.exp2(m_prev - m_new)
            p = jnp.concatenate(
                [jnp.exp2(s[:, c * LANES:(c + 1) * LANES] - m_new) for c in range(tk // LANES)],
                axis=1).astype(BF16)
            acc_ref[g] = alpha * acc_ref[g] + jnp.dot(p, v_ref[0, g], preferred_element_type=F32)
            m_ref[g] = m_new

    @pl.when(kb == nkb - 1)
    def _finish():
        for h in range(N_HEADS):
            g, r = divmod(h, Q_PER_KV)
            a = acc_ref[g, r * tq:(r + 1) * tq, :]
            out = a[:, :HEAD_DIM] / a[:, HEAD_DIM:HEAD_DIM + 1]
            o_ref[0, :, h * HEAD_DIM:(h + 1) * HEAD_DIM] = out.astype(o_ref.dtype)


def masked_attention(q, k, v, bias, *, tq, causal):
    b, t_q, d_q = q.shape
    n_k = k.shape[1]
    nkb, tk = bias.shape[1], bias.shape[3]
    kt = jnp.transpose(k.reshape(b, n_k, N_KV_HEADS, HEAD_DIM), (0, 2, 3, 1))
    v4 = jnp.transpose(v.reshape(b, n_k, N_KV_HEADS, HEAD_DIM), (0, 2, 1, 3))
    v_ext = jnp.concatenate(
        [v4, jnp.ones((b, N_KV_HEADS, n_k, 1), BF16),
         jnp.zeros((b, N_KV_HEADS, n_k, LANES - HEAD_DIM - 1), BF16)], axis=-1)

    def kv_block(qb, kb):
        return jnp.minimum(kb, ((qb + 1) * tq - 1) // tk) if causal else kb

    kernel = functools.partial(_attn_kernel, tq=tq, tk=tk, nkb=nkb, causal=causal)
    rows = Q_PER_KV * tq
    return pl.pallas_call(
        kernel,
        grid=(b, t_q // tq, nkb),
        in_specs=[
            pl.BlockSpec((1, tq, d_q), lambda bi, qb, kb: (bi, qb, 0)),
            pl.BlockSpec((1, N_KV_HEADS, HEAD_DIM, tk),
                         lambda bi, qb, kb: (bi, 0, 0, kv_block(qb, kb))),
            pl.BlockSpec((1, N_KV_HEADS, tk, LANES),
                         lambda bi, qb, kb: (bi, 0, kv_block(qb, kb), 0)),
            pl.BlockSpec((1, 1, tq, tk), lambda bi, qb, kb: (bi, kv_block(qb, kb), qb, 0)),
        ],
        out_specs=pl.BlockSpec((1, tq, d_q), lambda bi, qb, kb: (bi, qb, 0)),
        out_shape=jax.ShapeDtypeStruct((b, t_q, d_q), BF16),
        scratch_shapes=[pltpu.VMEM((N_KV_HEADS, rows, HEAD_DIM), BF16),
                        pltpu.VMEM((N_KV_HEADS, rows, LANES), F32),
                        pltpu.VMEM((N_KV_HEADS, rows, LANES), F32)],
        compiler_params=_params(("parallel", "parallel", "arbitrary")),
        name="masked_attention",
    )(q, kt, v_ext, bias)


def _conv_kernel(cur_ref, nxt_ref, w_ref, b_ref, g_ref, beta_ref, o_ref, buf_ref, conv_ref, *,
                 tm, cw):
    buf_ref[0:tm, :] = cur_ref[0]
    buf_ref[tm:tm + CONV_HALO, :] = nxt_ref[0]
    lead = CONV_HALO - (CONV_WIDTH - 1)
    c_total = buf_ref.shape[1]
    rw = min(tm, 128)
    for c in range(c_total // cw):
        cs = slice(c * cw, (c + 1) * cw)
        for r in range(tm // rw):
            acc = jnp.zeros((rw, cw), F32) + b_ref[:, cs]
            for j in range(CONV_WIDTH):
                r0 = lead + j + r * rw
                acc = acc + w_ref[j:j + 1, cs] * buf_ref[r0:r0 + rw, cs]
            conv_ref[r * rw:(r + 1) * rw, cs] = acc
    x = conv_ref[...]
    mu = jnp.mean(x, axis=-1, keepdims=True)
    xc = x - mu
    y = xc * lax.rsqrt(jnp.mean(xc * xc, axis=-1, keepdims=True) + EPS)
    y = y * g_ref[...] + beta_ref[...]
    o_ref[0] = _silu(y).astype(o_ref.dtype)


def conv_ln_silu(hist, w_dw, b_dw, ln_g, ln_b, *, tm):
    b, lp, c = hist.shape
    l = lp - CONV_HALO
    kernel = functools.partial(_conv_kernel, tm=tm, cw=LANES)
    vec = lambda a: a.reshape(1, c).astype(F32)
    return pl.pallas_call(
        kernel,
        grid=(b, l // tm),
        in_specs=[
            pl.BlockSpec((1, tm, c), lambda bi, i: (bi, i, 0)),
            pl.BlockSpec((1, CONV_HALO, c), lambda bi, i: (bi, (i + 1) * (tm // CONV_HALO), 0)),
            pl.BlockSpec((CONV_WIDTH, c), lambda bi, i: (0, 0)),
            pl.BlockSpec((1, c), lambda bi, i: (0, 0)),
            pl.BlockSpec((1, c), lambda bi, i: (0, 0)),
            pl.BlockSpec((1, c), lambda bi, i: (0, 0)),
        ],
        out_specs=pl.BlockSpec((1, tm, c), lambda bi, i: (bi, i, 0)),
        out_shape=jax.ShapeDtypeStruct((b, l, c), BF16),
        scratch_shapes=[pltpu.VMEM((tm + CONV_HALO, c), F32), pltpu.VMEM((tm, c), F32)],
        compiler_params=_params(("parallel", "parallel")),
        name="conv_ln_silu",
    )(hist, hist, w_dw.astype(F32), vec(b_dw), vec(ln_g), vec(ln_b))


def _router_kernel(h_ref, g_ref, wr_ref, u_ref, gate_ref, *, n_experts):
    x = h_ref[...]
    x = x * lax.rsqrt(jnp.mean(x * x, axis=-1, keepdims=True) + EPS) * g_ref[...]
    u = x.astype(BF16)
    u_ref[...] = u
    logits = jnp.dot(u, wr_ref[...], preferred_element_type=F32)
    lane = lax.broadcasted_iota(I32, logits.shape, 1).astype(F32)
    logits = jnp.where(lane < n_experts, logits, -jnp.inf)
    top1 = jnp.max(logits, axis=1, keepdims=True)
    idx1 = jnp.min(jnp.where(logits == top1, lane, float(LANES)), axis=1, keepdims=True)
    rest = jnp.where(lane == idx1, -jnp.inf, logits)
    top2 = jnp.max(rest, axis=1, keepdims=True)
    idx2 = jnp.min(jnp.where(rest == top2, lane, float(LANES)), axis=1, keepdims=True)
    e2 = jnp.exp(top2 - top1)
    denom = 1.0 + e2
    gate_ref[...] = jnp.where(lane == idx1, 1.0 / denom, jnp.where(lane == idx2, e2 / denom, 0.0))


def moe_router(h, gain, w_router_padded, n_experts, *, tm):
    m, d = h.shape
    kernel = functools.partial(_router_kernel, n_experts=n_experts)
    return pl.pallas_call(
        kernel,
        grid=(m // tm,),
        in_specs=[pl.BlockSpec((tm, d), lambda i: (i, 0)),
                  pl.BlockSpec((1, d), lambda i: (0, 0)),
                  pl.BlockSpec((d, LANES), lambda i: (0, 0))],
        out_specs=[pl.BlockSpec((tm, d), lambda i: (i, 0)),
                   pl.BlockSpec((tm, LANES), lambda i: (i, 0))],
        out_shape=[jax.ShapeDtypeStruct((m, d), BF16), jax.ShapeDtypeStruct((m, LANES), F32)],
        compiler_params=_params(("parallel",)),
        name="moe_router",
    )(h, gain.reshape(1, d).astype(F32), w_router_padded)


def _expert_up_kernel(u_ref, wg_ref, wu_ref, o_ref):
    u = u_ref[...]
    a = jnp.dot(u, wg_ref[0], preferred_element_type=F32)
    b = jnp.dot(u, wu_ref[0], preferred_element_type=F32)
    o_ref[0] = (_silu(a) * b).astype(o_ref.dtype)


def expert_up(u, w_gate, w_up, *, tm, tn):
    m, d = u.shape
    e, _, f = w_gate.shape
    return pl.pallas_call(
        _expert_up_kernel,
        grid=(e, m // tm, f // tn),
        in_specs=[pl.BlockSpec((tm, d), lambda ei, i, j: (i, 0)),
                  pl.BlockSpec((1, d, tn), lambda ei, i, j: (ei, 0, j)),
                  pl.BlockSpec((1, d, tn), lambda ei, i, j: (ei, 0, j))],
        out_specs=pl.BlockSpec((1, tm, tn), lambda ei, i, j: (ei, i, j)),
        out_shape=jax.ShapeDtypeStruct((e, m, f), BF16),
        compiler_params=_params(("parallel", "parallel", "parallel")),
        name="expert_up",
    )(u, w_gate, w_up)


def _expert_down_kernel(hb_ref, wd_ref, gate_ref, res_ref, o_ref, acc_ref, *, n_experts):
    e = pl.program_id(2)

    @pl.when(e == 0)
    def _init():
        acc_ref[...] = jnp.zeros(acc_ref.shape, F32)

    gates = gate_ref[...]
    lane = lax.broadcasted_iota(I32, gates.shape, 1)
    g = jnp.sum(jnp.where(lane == e, gates, 0.0), axis=1, keepdims=True)
    acc_ref[...] += g * jnp.dot(hb_ref[0], wd_ref[0], preferred_element_type=F32)

    @pl.when(e == n_experts - 1)
    def _finish():
        o_ref[...] = res_ref[...] + acc_ref[...]


def expert_down(hb, w_down, gates, res, *, tm, tn):
    e, m, f = hb.shape
    d = w_down.shape[2]
    kernel = functools.partial(_expert_down_kernel, n_experts=e)
    return pl.pallas_call(
        kernel,
        grid=(m // tm, d // tn, e),
        in_specs=[pl.BlockSpec((1, tm, f), lambda i, j, ei: (ei, i, 0)),
                  pl.BlockSpec((1, f, tn), lambda i, j, ei: (ei, 0, j)),
                  pl.BlockSpec((tm, LANES), lambda i, j, ei: (i, 0)),
                  pl.BlockSpec((tm, tn), lambda i, j, ei: (i, j))],
        out_specs=pl.BlockSpec((tm, tn), lambda i, j, ei: (i, j)),
        out_shape=jax.ShapeDtypeStruct((m, d), F32),
        scratch_shapes=[pltpu.VMEM((tm, tn), F32)],
        compiler_params=_params(("parallel", "parallel", "arbitrary")),
        name="expert_down",
    )(hb, w_down, gates, res)


def _split_w_in(w_in_layer, d_model):
    attn_dim = N_HEADS * HEAD_DIM
    kv_dim = N_KV_HEADS * HEAD_DIM
    sizes = (attn_dim, kv_dim, kv_dim, IDX_HEADS * IDX_DIM, IDX_DIM, IDX_HEADS, d_model, d_model,
             d_model, d_model)
    parts, off = [], 0
    for s in sizes:
        parts.append(w_in_layer[:, off:off + s])
        off += s
    assert off == w_in_layer.shape[1]
    return parts


def _layer_stack(x, p, weights, cache, *, past):
    bsz, length, d_model = x.shape
    m = bsz * length
    kv_dim = N_KV_HEADS * HEAD_DIM
    pos = past + jnp.arange(length, dtype=I32)
    row_pos = jnp.tile(pos, bsz)
    depth = weights["w_in"].shape[0]

    cos_q, sin_q = _rope_tables(row_pos, 512)
    cos_k, sin_k = _rope_tables(row_pos, kv_dim)
    cos_i, sin_i = _rope_tables(row_pos, LANES, rotary_width=IDX_DIM)
    ones_q = _block_ones(512)
    ones_k = _block_ones(kv_dim)

    n_keys = past + length
    n_sel = min(TOPK_MAX, n_keys // 4)
    if past == 0:
        tq_sel, tk, tq_attn = min(128, length), min(512, length), min(256, length)
        n_keys_pad = n_keys
    else:
        tq_sel = tq_attn = length
        tk = 384
        n_keys_pad = -(-n_keys // tk) * tk

    h = x.reshape(m, d_model)
    ks, vs, kis, convs = [], [], [], []
    for i in range(depth):
        wq, wk, wv, wqi, wki, wwi, wglu_a, wglu_g, wga, wgb = _split_w_in(weights["w_in"][i], d_model)
        w_kiwi = jnp.concatenate(
            [wki, wwi, jnp.zeros((d_model, LANES - IDX_DIM - IDX_HEADS), BF16)], axis=1)
        g_mix = weights["g_mix"][i]

        qn_gain = jnp.tile(weights["q_norm_g"][i].astype(F32), 512 // HEAD_DIM).reshape(1, 512)
        kn_gain = jnp.tile(weights["k_norm_g"][i].astype(F32), kv_dim // HEAD_DIM).reshape(1, kv_dim)

        def norm_rope(accs, tiled, rows, consts):
            y = _head_rms_norm(accs[0], consts[0], consts[1])
            y = _rope(y, rows[0], rows[1])
            return (y, y)

        (q_bf,) = fused_mm("proj_q", [h], [wq], [0],
                           lambda a, t, r, c: (_rope(_head_rms_norm(a[0], c[0], c[1]), r[0], r[1]),),
                           [BF16], gains=[g_mix], rows=[cos_q, sin_q], consts=[ones_q, qn_gain])
        k_f32, k_bf = fused_mm("proj_k", [h], [wk], [0], norm_rope, [F32, BF16], gains=[g_mix],
                               rows=[cos_k, sin_k], consts=[ones_k, kn_gain])
        v_f32, v_bf = fused_mm("proj_v", [h], [wv], [0], lambda a, t, r, c: (a[0], a[0]),
                               [F32, BF16], gains=[g_mix])
        (qi_bf,) = fused_mm("proj_qi", [h], [wqi], [0],
                            lambda a, t, r, c: (_rope(a[0], r[0], r[1]),),
                            [BF16], gains=[g_mix], rows=[cos_q, sin_q])
        (kiwi,) = fused_mm("proj_kiwi", [h], [w_kiwi], [0],
                           lambda a, t, r, c: (_rope(a[0], r[0], r[1]),),
                           [F32], gains=[g_mix], rows=[cos_i, sin_i])
        (glu,) = fused_mm("proj_glu", [h], [wglu_a, wglu_g], [0, 0],
                          lambda a, t, r, c: (a[0] * _sigmoid(a[1]),), [F32], gains=[g_mix])
        gate_a, gate_b = fused_mm("proj_gates", [h], [wga, wgb], [0, 0],
                                  lambda a, t, r, c: (_sigmoid(a[0]), _sigmoid(a[1])),
                                  [F32, F32], gains=[g_mix])

        ki_f32 = kiwi[:, :IDX_DIM]
        wi = kiwi[:, IDX_DIM:IDX_DIM + IDX_HEADS]

        k3 = k_bf.reshape(bsz, length, kv_dim)
        v3 = v_bf.reshape(bsz, length, kv_dim)
        ki3 = ki_f32.astype(BF16).reshape(bsz, length, IDX_DIM)
        if past:
            pad = n_keys_pad - n_keys
            k3 = jnp.concatenate([cache["k"][i].reshape(bsz, past, kv_dim).astype(BF16), k3,
                                  jnp.zeros((bsz, pad, kv_dim), BF16)], axis=1)
            v3 = jnp.concatenate([cache["v"][i].reshape(bsz, past, kv_dim).astype(BF16), v3,
                                  jnp.zeros((bsz, pad, kv_dim), BF16)], axis=1)
            ki3 = jnp.concatenate([cache["kidx"][i].astype(BF16), ki3,
                                   jnp.zeros((bsz, pad, IDX_DIM), BF16)], axis=1)
        bias = select_bias(qi_bf.reshape(bsz, length, -1), ki3, wi.reshape(bsz, length, IDX_HEADS),
                           tq=tq_sel, tk=tk, causal=(past == 0), q_pos0=past, n_valid=n_keys,
                           n_sel=n_sel)
        attn = masked_attention(q_bf.reshape(bsz, length, -1), k3, v3, bias, tq=tq_attn,
                                causal=(past == 0))
        attn = attn.reshape(m, -1)

        glu3 = glu.reshape(bsz, length, d_model)
        if past:
            ctx = cache["conv"][i]
        else:
            ctx = jnp.zeros((bsz, CONV_WIDTH - 1, d_model), F32)
        lead = jnp.zeros((bsz, CONV_HALO - (CONV_WIDTH - 1), d_model), F32)
        l_pad = -(-length // CONV_HALO) * CONV_HALO
        tail = jnp.zeros((bsz, l_pad - length, d_model), F32)
        hist = jnp.concatenate([lead, ctx, glu3, tail], axis=1)
        conv_state = hist[:, CONV_HALO - (CONV_WIDTH - 1) + length:CONV_HALO + length]
        z = conv_ln_silu(hist, weights["w_dw"][i], weights["b_dw"][i], weights["conv_ln_g"][i],
                         weights["conv_ln_b"][i], tm=min(256, l_pad))
        z = z[:, :length].reshape(m, d_model)

        (merged,) = fused_mm("mix", [attn, z], [weights["w_o_attn"][i], weights["w_o_conv"][i]],
                             [0, 1], lambda a, t, r, c: (t[0] * a[0] + t[1] * a[1],), [BF16],
                             tiled=[gate_a, gate_b])
        (h,) = fused_mm("out_proj", [merged], [weights["w_out"][i]], [0],
                        lambda a, t, r, c: (t[0] + a[0],), [F32], tiled=[h])

        j = i // 2
        if i % 2 == 0:
            (hb,) = fused_mm("ffn_up", [h], [weights["ffn_w_gate"][j], weights["ffn_w_up"][j]],
                             [0, 0], lambda a, t, r, c: (_silu(a[0]) * a[1],), [BF16],
                             gains=[weights["g_ffn"][i]], tn=1408)
            (h,) = fused_mm("ffn_down", [hb], [weights["ffn_w_down"][j]], [0],
                            lambda a, t, r, c: (t[0] + a[0],), [F32], tiled=[h])
        else:
            n_experts = weights["moe_router"].shape[2]
            w_router = jnp.concatenate(
                [weights["moe_router"][j], jnp.zeros((d_model, LANES - n_experts), BF16)], axis=1)
            tm_moe = _pick_tile(m, 512)
            u, gates = moe_router(h, weights["g_ffn"][i], w_router, n_experts, tm=tm_moe)
            hb = expert_up(u, weights["moe_w_gate"][j], weights["moe_w_up"][j], tm=tm_moe,
                           tn=_pick_tile(weights["moe_w_gate"].shape[3], 512))
            h = expert_down(hb, weights["moe_w_down"][j], gates, h, tm=tm_moe,
                            tn=_pick_tile(d_model, 512))

        (h,) = fused_mm("ple", [h, p[i].reshape(m, -1)],
                        [weights["w_ple_gate"][i], weights["w_ple_proj"][i]],
                        [0, 1], lambda a, t, r, c: (t[0] + _sigmoid(a[0]) * a[1],), [F32],
                        gains=[weights["g_ple"][i], None], tiled=[h])

        ks.append(k_f32.reshape(bsz, length, N_KV_HEADS, HEAD_DIM))
        vs.append(v_f32.reshape(bsz, length, N_KV_HEADS, HEAD_DIM))
        kis.append(ki_f32.reshape(bsz, length, IDX_DIM))
        convs.append(conv_state)
    return (h.reshape(bsz, length, d_model), jnp.stack(ks), jnp.stack(vs), jnp.stack(kis),
            jnp.stack(convs))


def kernel(x_prompt, x_sample, cache_k, cache_v, cache_kidx, state_conv, p_prompt, p_sample, w_in, q_norm_g, k_norm_g, w_dw, b_dw, conv_ln_g, conv_ln_b, w_o_attn, w_o_conv, w_out, g_mix, g_ffn, ffn_w_gate, ffn_w_up, ffn_w_down, moe_router, moe_w_gate, moe_w_up, moe_w_down, g_ple, w_ple_gate, w_ple_proj):
    bf = lambda a: a.astype(BF16)
    weights = dict(
        w_in=bf(w_in), q_norm_g=q_norm_g, k_norm_g=k_norm_g, w_dw=w_dw, b_dw=b_dw,
        conv_ln_g=conv_ln_g, conv_ln_b=conv_ln_b, w_o_attn=bf(w_o_attn), w_o_conv=bf(w_o_conv),
        w_out=bf(w_out), g_mix=g_mix, g_ffn=g_ffn, ffn_w_gate=bf(ffn_w_gate),
        ffn_w_up=bf(ffn_w_up), ffn_w_down=bf(ffn_w_down), moe_router=bf(moe_router),
        moe_w_gate=bf(moe_w_gate), moe_w_up=bf(moe_w_up), moe_w_down=bf(moe_w_down), g_ple=g_ple,
        w_ple_gate=bf(w_ple_gate), w_ple_proj=bf(w_ple_proj))
    cache = dict(k=cache_k, v=cache_v, kidx=cache_kidx, conv=state_conv)
    past_len = cache_k.shape[2]
    y_p, k_p, v_p, ki_p, conv_p = _layer_stack(x_prompt, p_prompt, weights, None, past=0)
    y_s, k_s, v_s, ki_s, conv_s = _layer_stack(x_sample, p_sample, weights, cache, past=past_len)
    return (y_p, y_s, k_p, v_p, ki_p, conv_p, k_s, v_s, ki_s, conv_s)
```

```python
import functools

import jax
import jax.numpy as jnp
from jax import lax
from jax.experimental import pallas as pl
from jax.experimental.pallas import tpu as pltpu

F32 = jnp.float32
BF16 = jnp.bfloat16
I32 = jnp.int32
I16 = jnp.int16

CHUNK = 64
CHUNK_SHIFT = CHUNK.bit_length() - 1
assert 1 << CHUNK_SHIFT == CHUNK
N_HEADS = 16
HEAD_DIM = 64
N_KV_HEADS = 4
Q_PER_KV = N_HEADS // N_KV_HEADS
IDX_HEADS = 8
IDX_DIM = 64
TOPK_MAX = 256
ROPE_THETA = 10000.0
CONV_WIDTH = 31
TOP_K_EXPERTS = 2
EPS = 1e-6

LANES = 128
CONV_HALO = 32
MASKED_BIAS = -1e30
LOG2_E = 1.4426950408889634
HALF_RANGE = 1 << 15
SEARCH_ROWS = 128
MOE_ROW_BLOCK = 512
HI_NEG_INF = (-2139095041) >> 16
VMEM_LIMIT = 52 * 1024 * 1024


def _params(semantics):
    return pltpu.CompilerParams(dimension_semantics=semantics, vmem_limit_bytes=VMEM_LIMIT)


def _pick_tile(n, pref):
    if n <= pref:
        return n
    best = None
    for t in range(LANES, pref + 1, LANES):
        if n % t == 0:
            best = t
    assert best is not None, (n, pref)
    return best


def _fused_mm_kernel(*refs, n_lhs, has_gain, pair_lhs, n_tiled, n_rows, n_consts, n_out,
                     epilogue):
    it = iter(refs)
    lhs_refs = [next(it) for _ in range(n_lhs)]
    gain_refs = [next(it) if has_gain[a] else None for a in range(n_lhs)]
    rhs_refs = [next(it) for _ in range(len(pair_lhs))]
    tiled_refs = [next(it) for _ in range(n_tiled)]
    row_refs = [next(it) for _ in range(n_rows)]
    const_refs = [next(it) for _ in range(n_consts)]
    out_refs = [next(it) for _ in range(n_out)]
    stage_refs = [next(it) for _ in range(n_lhs)]

    @pl.when(pl.program_id(1) == 0)
    def _stage():
        for a in range(n_lhs):
            x = lhs_refs[a][...].astype(F32)
            if has_gain[a]:
                x = x * lax.rsqrt(jnp.mean(x * x, axis=-1, keepdims=True) + EPS)
                x = x * gain_refs[a][...]
            stage_refs[a][...] = x.astype(BF16)

    accs = [jnp.dot(stage_refs[pair_lhs[r]][...], rhs_refs[r][...], preferred_element_type=F32)
            for r in range(len(pair_lhs))]
    outs = epilogue(accs, [t[...] for t in tiled_refs], [t[...] for t in row_refs],
                    [t[...] for t in const_refs])
    for o_ref, o in zip(out_refs, outs):
        o_ref[...] = o.astype(o_ref.dtype)


def fused_mm(name, lhs, rhs, pair_lhs, epilogue, out_dtypes, *, gains=None, tiled=(), rows=(),
             consts=(), tm=512, tn=512):
    m = lhs[0].shape[0]
    n = rhs[0].shape[1]
    tm = _pick_tile(m, tm) if m % LANES == 0 else m
    tn = _pick_tile(n, tn)
    n_lhs = len(lhs)
    gains = list(gains) if gains is not None else [None] * n_lhs
    has_gain = tuple(g is not None for g in gains)

    in_specs, args = [], []
    for x in lhs:
        in_specs.append(pl.BlockSpec((tm, x.shape[1]), lambda i, j: (i, 0)))
        args.append(x)
    for g in gains:
        if g is not None:
            g2 = g.reshape(1, -1).astype(F32)
            in_specs.append(pl.BlockSpec(g2.shape, lambda i, j: (0, 0)))
            args.append(g2)
    for w in rhs:
        in_specs.append(pl.BlockSpec((w.shape[0], tn), lambda i, j: (0, j)))
        args.append(w)
    for t in tiled:
        in_specs.append(pl.BlockSpec((tm, tn), lambda i, j: (i, j)))
        args.append(t)
    for t in rows:
        in_specs.append(pl.BlockSpec((tm, t.shape[1]), lambda i, j: (i, 0)))
        args.append(t)
    for c in consts:
        in_specs.append(pl.BlockSpec(c.shape, lambda i, j: (0, 0)))
        args.append(c)

    kernel = functools.partial(
        _fused_mm_kernel, n_lhs=n_lhs, has_gain=has_gain, pair_lhs=tuple(pair_lhs),
        n_tiled=len(tiled), n_rows=len(rows), n_consts=len(consts), n_out=len(out_dtypes),
        epilogue=epilogue)
    outs = pl.pallas_call(
        kernel,
        grid=(m // tm, n // tn),
        in_specs=in_specs,
        out_specs=[pl.BlockSpec((tm, tn), lambda i, j: (i, j)) for _ in out_dtypes],
        out_shape=[jax.ShapeDtypeStruct((m, n), dt) for dt in out_dtypes],
        scratch_shapes=[pltpu.VMEM((tm, x.shape[1]), BF16) for x in lhs],
        compiler_params=_params(("parallel", "arbitrary")),
        name=name,
    )(*args)
    return outs


def _head_rms_norm(x, block_ones, gain):
    x2 = x * x
    hi = x2.astype(BF16)
    lo = (x2 - hi.astype(F32)).astype(BF16)
    ss = (jnp.dot(hi, block_ones, preferred_element_type=F32)
          + jnp.dot(lo, block_ones, preferred_element_type=F32))
    return x * lax.rsqrt(ss * (1.0 / HEAD_DIM) + EPS) * gain


def _rope(x, cos, sin_signed):
    n = x.shape[1]
    lane = lax.broadcasted_iota(I32, x.shape, 1)
    first_half = (lane & (HEAD_DIM // 2)) == 0
    partner = jnp.where(first_half, pltpu.roll(x, n - HEAD_DIM // 2, 1),
                        pltpu.roll(x, HEAD_DIM // 2, 1))
    return x * cos + partner * sin_signed


def _sigmoid(x):
    return 1.0 / (1.0 + jnp.exp(-x))


def _silu(x):
    return x * _sigmoid(x)


def _rope_tables(pos, width, rotary_width=None):
    half = HEAD_DIM // 2
    inv = ROPE_THETA ** (-jnp.arange(half, dtype=F32) / half)
    ang = pos.astype(F32)[:, None] * inv[None, :]
    cos = jnp.cos(ang)
    sin = jnp.sin(ang)
    cos64 = jnp.concatenate([cos, cos], axis=-1)
    sin64 = jnp.concatenate([-sin, sin], axis=-1)
    rotary_width = width if rotary_width is None else rotary_width
    reps = rotary_width // HEAD_DIM
    cos_t = jnp.tile(cos64, (1, reps))
    sin_t = jnp.tile(sin64, (1, reps))
    if rotary_width < width:
        pad = width - rotary_width
        cos_t = jnp.concatenate([cos_t, jnp.ones((pos.shape[0], pad), F32)], axis=-1)
        sin_t = jnp.concatenate([sin_t, jnp.zeros((pos.shape[0], pad), F32)], axis=-1)
    return cos_t, sin_t


def _block_ones(width):
    head = jnp.arange(width, dtype=I32) // HEAD_DIM
    return (head[:, None] == head[None, :]).astype(BF16)


def _float_key(x):
    b = lax.bitcast_convert_type(x, I32)
    return jnp.where(b >= 0, b, b ^ jnp.int32(0x7FFFFFFF))


def _select_kernel(qi_ref, ki_ref, wi_ref, tri_ref, bias_ref, hi_ref, lo_ref, e_ref, *, tq, tk, nkb,
                   causal, q_pos0, n_valid, n_sel, rows):
    qb = pl.program_id(1)
    n_chunks = tk // LANES
    n_sub = tq // rows
    if causal:
        n_act = jnp.minimum(nkb, ((qb + 1) * tq + tk - 1) // tk)
    else:
        n_act = nkb
    q_pos = q_pos0 + qb * tq + lax.broadcasted_iota(I32, (tq, tk), 0)
    q_chunk = lax.shift_right_logical(q_pos, CHUNK_SHIFT)
    w = (wi_ref[0] * (IDX_HEADS ** -0.5)) * (IDX_DIM ** -0.5)
    qi = qi_ref[0]

    def score_block(kb, carry):
        start = pl.multiple_of(kb * tk, tk)
        ki = ki_ref[0, pl.ds(start, tk), :]
        s = jnp.zeros((tq, tk), F32)
        for h in range(IDX_HEADS):
            r = lax.dot_general(qi[:, h * IDX_DIM:(h + 1) * IDX_DIM], ki,
                                (((1,), (1,)), ((), ())), preferred_element_type=F32)
            s = s + jnp.maximum(r, 0.0) * w[:, h:h + 1]
        k_pos = start + lax.broadcasted_iota(I32, (tq, tk), 1)
        admissible = (lax.shift_right_logical(k_pos, CHUNK_SHIFT) <= q_chunk) & (k_pos < n_valid)
        key = _float_key(jnp.where(admissible, s, -jnp.inf))
        hi = lax.shift_right_arithmetic(key, 16).astype(I16)
        lo = ((key & 0xFFFF) - HALF_RANGE).astype(I16)
        for sub in range(n_sub):
            hi_ref[sub, kb] = hi[sub * rows:(sub + 1) * rows]
            lo_ref[sub, kb] = lo[sub * rows:(sub + 1) * rows]
        return carry

    lax.fori_loop(0, n_act, score_block, 0)
    for sub in range(n_sub):
        _select_rows(hi_ref.at[sub], lo_ref.at[sub], e_ref, tri_ref,
                     bias_ref.at[0, :, sub * rows:(sub + 1) * rows, :],
                     n_act=n_act, rows=rows, tk=tk, n_sel=n_sel)

    def fill_block(kb, carry):
        bias_ref[0, kb] = jnp.full((tq, tk), MASKED_BIAS, bias_ref.dtype)
        return carry

    lax.fori_loop(n_act, nkb, fill_block, 0)


def _select_rows(hi_ref, lo_ref, e_ref, tri_ref, out_ref, *, n_act, rows, tk, n_sel):
    tq = rows
    n_chunks = tk // LANES

    ones_mat = jnp.ones((LANES, LANES), BF16)

    def count(hit_fn):
        def body(kb, acc):
            for c in range(n_chunks):
                acc = acc + jnp.where(hit_fn(kb, c), jnp.int16(1), jnp.int16(0))
            return acc
        acc = lax.fori_loop(0, n_act, body, jnp.zeros((tq, LANES), I16))
        return jnp.dot(acc.astype(F32).astype(BF16), ones_mat, preferred_element_type=F32)

    def chunk(ref, kb, c):
        return ref[kb, :, c * LANES:(c + 1) * LANES]

    def search16(src_ref, base):
        def step(i, prefix):
            cand = prefix | jnp.left_shift(jnp.int32(1), 15 - i)
            cb = (cand - HALF_RANGE).astype(I16)
            cnt = base + count(lambda kb, c: chunk(src_ref, kb, c) >= cb)
            return jnp.where(cnt >= n_sel, cand, prefix)
        return lax.fori_loop(0, 16, step, jnp.zeros((tq, LANES), I32)) - HALF_RANGE

    tau_hi_rep = search16(hi_ref, 0.0)
    tau_hi = tau_hi_rep.astype(I16)

    def bucket_block(kb, carry):
        for c in range(n_chunks):
            cs = slice(c * LANES, (c + 1) * LANES)
            e_ref[kb, :, cs] = jnp.where(hi_ref[kb, :, cs] == tau_hi, lo_ref[kb, :, cs],
                                         jnp.int16(-HALF_RANGE))
        return carry

    lax.fori_loop(0, n_act, bucket_block, 0)
    above = count(lambda kb, c: chunk(hi_ref, kb, c) > tau_hi)
    tau_lo_rep = search16(e_ref, above)
    tau_lo = tau_lo_rep.astype(I16)

    n_gt = above + count(lambda kb, c: chunk(e_ref, kb, c) > tau_lo)
    n_eq = count(lambda kb, c: (chunk(hi_ref, kb, c) == tau_hi) & (chunk(lo_ref, kb, c) == tau_lo))
    need_rep = n_sel - n_gt
    all_ties_taken = jnp.max(jnp.where(need_rep >= n_eq, 0.0, 1.0)) == 0.0
    tau_hi32 = tau_hi_rep[:, :1]
    tau_lo32 = tau_lo_rep[:, :1]
    need = need_rep[:, :1]

    def emit_all_ties(kb, carry):
        for c in range(n_chunks):
            cs = slice(c * LANES, (c + 1) * LANES)
            hi = hi_ref[kb, :, cs]
            sel = (hi > tau_hi) | ((hi == tau_hi) & (lo_ref[kb, :, cs] >= tau_lo))
            sel = sel & (hi > HI_NEG_INF)
            out_ref[kb, :, cs] = jnp.where(sel, jnp.zeros((), out_ref.dtype),
                                           jnp.full((), MASKED_BIAS, out_ref.dtype))
        return carry

    def emit_ranked(kb, seen):
        hi = hi_ref[kb].astype(I32)
        lo = lo_ref[kb].astype(I32)
        in_bucket = hi == tau_hi32
        eq = in_bucket & (lo == tau_lo32)
        gt = (hi > tau_hi32) | (in_bucket & (lo > tau_lo32))
        rank = seen + jnp.dot(jnp.where(eq, 1.0, 0.0).astype(BF16), tri_ref[...],
                              preferred_element_type=F32)
        sel = (gt | (eq & (rank <= need))) & (hi > HI_NEG_INF)
        out_ref[kb] = jnp.where(sel, 0.0, MASKED_BIAS).astype(out_ref.dtype)
        return rank[:, tk - 1:tk]

    @pl.when(all_ties_taken)
    def _fast():
        lax.fori_loop(0, n_act, emit_all_ties, 0)

    @pl.when(jnp.logical_not(all_ties_taken))
    def _ranked():
        lax.fori_loop(0, n_act, emit_ranked, jnp.zeros((tq, 1), F32))


def select_bias(qi, ki, wi, *, tq, tk, causal, q_pos0, n_valid, n_sel):
    b, t_q, _ = qi.shape
    n_k = ki.shape[1]
    nkb = n_k // tk
    idx = jnp.arange(tk, dtype=I32)
    tri = (idx[:, None] <= idx[None, :]).astype(BF16)
    rows = min(tq, SEARCH_ROWS)
    kernel = functools.partial(_select_kernel, tq=tq, tk=tk, nkb=nkb, causal=causal,
                               q_pos0=q_pos0, n_valid=n_valid, n_sel=n_sel, rows=rows)
    return pl.pallas_call(
        kernel,
        grid=(b, t_q // tq),
        in_specs=[
            pl.BlockSpec((1, tq, qi.shape[2]), lambda bi, qb: (bi, qb, 0)),
            pl.BlockSpec((1, n_k, ki.shape[2]), lambda bi, qb: (bi, 0, 0)),
            pl.BlockSpec((1, tq, wi.shape[2]), lambda bi, qb: (bi, qb, 0)),
            pl.BlockSpec((tk, tk), lambda bi, qb: (0, 0)),
        ],
        out_specs=pl.BlockSpec((1, nkb, tq, tk), lambda bi, qb: (bi, 0, qb, 0)),
        out_shape=jax.ShapeDtypeStruct((b, nkb, t_q, tk), BF16),
        scratch_shapes=[pltpu.VMEM((tq // rows, nkb, rows, tk), I16),
                        pltpu.VMEM((tq // rows, nkb, rows, tk), I16),
                        pltpu.VMEM((nkb, rows, tk), I16)],
        compiler_params=_params(("parallel", "parallel")),
        name="select_bias",
    )(qi, ki, wi, tri)


def _attn_kernel(q_ref, kt_ref, v_ref, bias_ref, o_ref, qs_ref, m_ref, acc_ref, *, tq, tk, nkb,
                 causal):
    qb = pl.program_id(1)
    kb = pl.program_id(2)
    last = jnp.minimum(nkb - 1, ((qb + 1) * tq - 1) // tk) if causal else nkb - 1
    rows = Q_PER_KV * tq

    @pl.when(kb == 0)
    def _init():
        m_ref[...] = jnp.full(m_ref.shape, MASKED_BIAS, F32)
        acc_ref[...] = jnp.zeros(acc_ref.shape, F32)
        for h in range(N_HEADS):
            g, r = divmod(h, Q_PER_KV)
            qh = q_ref[0, :, h * HEAD_DIM:(h + 1) * HEAD_DIM].astype(F32) * (HEAD_DIM ** -0.5)
            qs_ref[g, r * tq:(r + 1) * tq, :] = qh.astype(qs_ref.dtype)

    @pl.when(kb <= last)
    def _step():
        bias = bias_ref[0, 0].astype(F32)
        for g in range(N_KV_HEADS):
            s = jnp.dot(qs_ref[g], kt_ref[0, g], preferred_element_type=F32)
            s = (s * LOG2_E).reshape(Q_PER_KV, tq, tk) + bias[None]
            s = s.reshape(rows, tk)
            m_prev = m_ref[g]
            m_blk = jnp.max(s, axis=1, keepdims=True)
            m_new = jnp.maximum(m_prev, jnp.broadcast_to(m_blk, m_prev.shape))
            alpha = jnp.exp2(m_prev - m_new)
            p = jnp.concatenate(
                [jnp.exp2(s[:, c * LANES:(c + 1) * LANES] - m_new) for c in range(tk // LANES)],
                axis=1).astype(BF16)
            acc_ref[g] = alpha * acc_ref[g] + jnp.dot(p, v_ref[0, g], preferred_element_type=F32)
            m_ref[g] = m_new

    @pl.when(kb == nkb - 1)
    def _finish():
        for h in range(N_HEADS):
            g, r = divmod(h, Q_PER_KV)
            a = acc_ref[g, r * tq:(r + 1) * tq, :]
            out = a[:, :HEAD_DIM] / a[:, HEAD_DIM:HEAD_DIM + 1]
            o_ref[0, :, h * HEAD_DIM:(h + 1) * HEAD_DIM] = out.astype(o_ref.dtype)


def masked_attention(q, k, v, bias, *, tq, causal):
    b, t_q, d_q = q.shape
    n_k = k.shape[1]
    nkb, tk = bias.shape[1], bias.shape[3]
    kt = jnp.transpose(k.reshape(b, n_k, N_KV_HEADS, HEAD_DIM), (0, 2, 3, 1))
    v4 = jnp.transpose(v.reshape(b, n_k, N_KV_HEADS, HEAD_DIM), (0, 2, 1, 3))
    v_ext = jnp.concatenate(
        [v4, jnp.ones((b, N_KV_HEADS, n_k, 1), BF16),
         jnp.zeros((b, N_KV_HEADS, n_k, LANES - HEAD_DIM - 1), BF16)], axis=-1)

    def kv_block(qb, kb):
        return jnp.minimum(kb, ((qb + 1) * tq - 1) // tk) if causal else kb

    kernel = functools.partial(_attn_kernel, tq=tq, tk=tk, nkb=nkb, causal=causal)
    rows = Q_PER_KV * tq
    return pl.pallas_call(
        kernel,
        grid=(b, t_q // tq, nkb),
        in_specs=[
            pl.BlockSpec((1, tq, d_q), lambda bi, qb, kb: (bi, qb, 0)),
            pl.BlockSpec((1, N_KV_HEADS, HEAD_DIM, tk),
                         lambda bi, qb, kb: (bi, 0, 0, kv_block(qb, kb))),
            pl.BlockSpec((1, N_KV_HEADS, tk, LANES),
                         lambda bi, qb, kb: (bi, 0, kv_block(qb, kb), 0)),
            pl.BlockSpec((1, 1, tq, tk), lambda bi, qb, kb: (bi, kv_block(qb, kb), qb, 0)),
        ],
        out_specs=pl.BlockSpec((1, tq, d_q), lambda bi, qb, kb: (bi, qb, 0)),
        out_shape=jax.ShapeDtypeStruct((b, t_q, d_q), BF16),
        scratch_shapes=[pltpu.VMEM((N_KV_HEADS, rows, HEAD_DIM), BF16),
                        pltpu.VMEM((N_KV_HEADS, rows, LANES), F32),
                        pltpu.VMEM((N_KV_HEADS, rows, LANES), F32)],
        compiler_params=_params(("parallel", "parallel", "arbitrary")),
        name="masked_attention",
    )(q, kt, v_ext, bias)


def _conv_kernel(cur_ref, nxt_ref, w_ref, b_ref, g_ref, beta_ref, o_ref, buf_ref, conv_ref, *,
                 tm, cw):
    buf_ref[0:tm, :] = cur_ref[0]
    buf_ref[tm:tm + CONV_HALO, :] = nxt_ref[0]
    lead = CONV_HALO - (CONV_WIDTH - 1)
    c_total = buf_ref.shape[1]
    rw = min(tm, 128)
    for c in range(c_total // cw):
        cs = slice(c * cw, (c + 1) * cw)
        for r in range(tm // rw):
            acc = jnp.zeros((rw, cw), F32) + b_ref[:, cs]
            for j in range(CONV_WIDTH):
                r0 = lead + j + r * rw
                acc = acc + w_ref[j:j + 1, cs] * buf_ref[r0:r0 + rw, cs]
            conv_ref[r * rw:(r + 1) * rw, cs] = acc
    x = conv_ref[...]
    mu = jnp.mean(x, axis=-1, keepdims=True)
    xc = x - mu
    y = xc * lax.rsqrt(jnp.mean(xc * xc, axis=-1, keepdims=True) + EPS)
    y = y * g_ref[...] + beta_ref[...]
    o_ref[0] = _silu(y).astype(o_ref.dtype)


def conv_ln_silu(hist, w_dw, b_dw, ln_g, ln_b, *, tm):
    b, lp, c = hist.shape
    l = lp - CONV_HALO
    kernel = functools.partial(_conv_kernel, tm=tm, cw=LANES)
    vec = lambda a: a.reshape(1, c).astype(F32)
    return pl.pallas_call(
        kernel,
        grid=(b, l // tm),
        in_specs=[
            pl.BlockSpec((1, tm, c), lambda bi, i: (bi, i, 0)),
            pl.BlockSpec((1, CONV_HALO, c), lambda bi, i: (bi, (i + 1) * (tm // CONV_HALO), 0)),
            pl.BlockSpec((CONV_WIDTH, c), lambda bi, i: (0, 0)),
            pl.BlockSpec((1, c), lambda bi, i: (0, 0)),
            pl.BlockSpec((1, c), lambda bi, i: (0, 0)),
            pl.BlockSpec((1, c), lambda bi, i: (0, 0)),
        ],
        out_specs=pl.BlockSpec((1, tm, c), lambda bi, i: (bi, i, 0)),
        out_shape=jax.ShapeDtypeStruct((b, l, c), BF16),
        scratch_shapes=[pltpu.VMEM((tm + CONV_HALO, c), F32), pltpu.VMEM((tm, c), F32)],
        compiler_params=_params(("parallel", "parallel")),
        name="conv_ln_silu",
    )(hist, hist, w_dw.astype(F32), vec(b_dw), vec(ln_g), vec(ln_b))


ROUTE_IDX1, ROUTE_IDX2, ROUTE_GATE1, ROUTE_GATE2, ROUTE_RANK1, ROUTE_RANK2 = range(6)


def _router_kernel(h_ref, g_ref, wr_ref, lower_ref, u_ref, route_ref, counts_ref, seen_ref, *,
                   n_experts):
    @pl.when(pl.program_id(0) == 0)
    def _init():
        seen_ref[...] = jnp.zeros(seen_ref.shape, F32)

    x = h_ref[...]
    u = x * lax.rsqrt(jnp.mean(x * x, axis=-1, keepdims=True) + EPS) * g_ref[...]
    u_ref[...] = u
    logits = jnp.dot(u.astype(BF16), wr_ref[...], preferred_element_type=F32)
    lane = lax.broadcasted_iota(I32, logits.shape, 1).astype(F32)
    logits = jnp.where(lane < n_experts, logits, -jnp.inf)
    top1 = jnp.max(logits, axis=1, keepdims=True)
    idx1 = jnp.min(jnp.where(logits == top1, lane, float(LANES)), axis=1, keepdims=True)
    rest = jnp.where(lane == idx1, -jnp.inf, logits)
    top2 = jnp.max(rest, axis=1, keepdims=True)
    idx2 = jnp.min(jnp.where(rest == top2, lane, float(LANES)), axis=1, keepdims=True)
    e2 = jnp.exp(top2 - top1)
    denom = 1.0 + e2

    hot1 = jnp.where(lane == idx1, 1.0, 0.0)
    hot2 = jnp.where(lane == idx2, 1.0, 0.0)
    hot = hot1 + hot2
    before = seen_ref[...] + jnp.dot(lower_ref[...], hot.astype(BF16), preferred_element_type=F32)
    rank1 = jnp.sum(hot1 * before, axis=1, keepdims=True)
    rank2 = jnp.sum(hot2 * before, axis=1, keepdims=True)
    seen_ref[...] += jnp.sum(hot, axis=0, keepdims=True)
    counts_ref[...] = seen_ref[...]

    route = jnp.zeros(logits.shape, F32)
    for slot, val in ((ROUTE_IDX1, idx1), (ROUTE_IDX2, idx2), (ROUTE_GATE1, 1.0 / denom),
                      (ROUTE_GATE2, e2 / denom), (ROUTE_RANK1, rank1), (ROUTE_RANK2, rank2)):
        route = jnp.where(lane == slot, val, route)
    route_ref[...] = route


def moe_router(h, gain, w_router_padded, n_experts, *, tm):
    m, d = h.shape
    row = jnp.arange(tm, dtype=I32)
    lower = (row[None, :] < row[:, None]).astype(BF16)
    kernel = functools.partial(_router_kernel, n_experts=n_experts)
    return pl.pallas_call(
        kernel,
        grid=(m // tm,),
        in_specs=[pl.BlockSpec((tm, d), lambda i: (i, 0)),
                  pl.BlockSpec((1, d), lambda i: (0, 0)),
                  pl.BlockSpec((d, LANES), lambda i: (0, 0)),
                  pl.BlockSpec((tm, tm), lambda i: (0, 0))],
        out_specs=[pl.BlockSpec((tm, d), lambda i: (i, 0)),
                   pl.BlockSpec((tm, LANES), lambda i: (i, 0)),
                   pl.BlockSpec((1, LANES), lambda i: (0, 0))],
        out_shape=[jax.ShapeDtypeStruct((m, d), F32), jax.ShapeDtypeStruct((m, LANES), F32),
                   jax.ShapeDtypeStruct((1, LANES), F32)],
        scratch_shapes=[pltpu.VMEM((1, LANES), F32)],
        compiler_params=_params(("arbitrary",)),
        name="moe_router",
    )(h, gain.reshape(1, d).astype(F32), w_router_padded, lower)


def _scatter_rows_kernel(d1_ref, d2_ref, u_ref, init_ref, xs_ref, sem):
    del init_ref
    tm = u_ref.shape[0]

    def row_copy(j, dest):
        return pltpu.make_async_copy(u_ref.at[pl.ds(j, 1)], xs_ref.at[pl.ds(dest, 1)], sem)

    def issue(j, carry):
        row_copy(j, d1_ref[0, 0, j]).start()
        row_copy(j, d2_ref[0, 0, j]).start()
        return carry

    lax.fori_loop(0, tm, issue, 0)

    def drain(j, carry):
        row_copy(j, 0).wait()
        row_copy(j, 0).wait()
        return carry

    lax.fori_loop(0, tm, drain, 0)


def scatter_rows(u, dest1, dest2, n_rows, *, tm):
    m, d = u.shape
    smem_idx = lambda a: a.reshape(m // tm, 1, tm)
    idx_spec = pl.BlockSpec((1, 1, tm), lambda i: (i, 0, 0), memory_space=pltpu.SMEM)
    return pl.pallas_call(
        _scatter_rows_kernel,
        grid=(m // tm,),
        in_specs=[idx_spec, idx_spec,
                  pl.BlockSpec((tm, d), lambda i: (i, 0)),
                  pl.BlockSpec(memory_space=pl.ANY)],
        out_specs=pl.BlockSpec(memory_space=pl.ANY),
        out_shape=jax.ShapeDtypeStruct((n_rows, d), u.dtype),
        scratch_shapes=[pltpu.SemaphoreType.DMA],
        input_output_aliases={3: 0},
        compiler_params=_params(("arbitrary",)),
        name="moe_scatter_rows",
    )(smem_idx(dest1), smem_idx(dest2), u, jnp.zeros((n_rows, d), u.dtype))


def _combine_kernel(d1_ref, d2_ref, rows_ref, route_ref, h_ref, o_ref, buf_ref, sem):
    tm = h_ref.shape[0]

    def row_copy(slot, j, src):
        return pltpu.make_async_copy(rows_ref.at[pl.ds(src, 1)], buf_ref.at[slot, pl.ds(j, 1)], sem)

    def issue(j, carry):
        row_copy(0, j, d1_ref[0, 0, j]).start()
        row_copy(1, j, d2_ref[0, 0, j]).start()
        return carry

    lax.fori_loop(0, tm, issue, 0)

    def drain(j, carry):
        row_copy(0, j, 0).wait()
        row_copy(1, j, 0).wait()
        return carry

    lax.fori_loop(0, tm, drain, 0)
    route = route_ref[...]
    g1 = route[:, ROUTE_GATE1:ROUTE_GATE1 + 1]
    g2 = route[:, ROUTE_GATE2:ROUTE_GATE2 + 1]
    o_ref[...] = h_ref[...] + (g1 * buf_ref[0] + g2 * buf_ref[1])


def combine_rows(rows, dest1, dest2, route, h, *, tm):
    m, d = h.shape
    smem_idx = lambda a: a.reshape(m // tm, 1, tm)
    idx_spec = pl.BlockSpec((1, 1, tm), lambda i: (i, 0, 0), memory_space=pltpu.SMEM)
    return pl.pallas_call(
        _combine_kernel,
        grid=(m // tm,),
        in_specs=[idx_spec, idx_spec,
                  pl.BlockSpec(memory_space=pl.ANY),
                  pl.BlockSpec((tm, LANES), lambda i: (i, 0)),
                  pl.BlockSpec((tm, d), lambda i: (i, 0))],
        out_specs=pl.BlockSpec((tm, d), lambda i: (i, 0)),
        out_shape=jax.ShapeDtypeStruct((m, d), F32),
        scratch_shapes=[pltpu.VMEM((2, tm, d), F32), pltpu.SemaphoreType.DMA],
        compiler_params=_params(("arbitrary",)),
        name="moe_combine_rows",
    )(smem_idx(dest1), smem_idx(dest2), rows, route, h)


def _expert_up_kernel(blk_expert_ref, n_used_ref, xs_ref, wg_ref, wu_ref, o_ref, stage_ref):
    del blk_expert_ref
    i = pl.program_id(0)

    @pl.when(pl.program_id(1) == 0)
    def _stage():
        stage_ref[...] = xs_ref[...].astype(BF16)

    @pl.when(i < n_used_ref[0])
    def _compute():
        u = stage_ref[...]
        a = jnp.dot(u, wg_ref[0], preferred_element_type=F32)
        b = jnp.dot(u, wu_ref[0], preferred_element_type=F32)
        o_ref[...] = (_silu(a) * b).astype(o_ref.dtype)

    @pl.when(i >= n_used_ref[0])
    def _skip():
        o_ref[...] = jnp.zeros(o_ref.shape, o_ref.dtype)


def expert_up(xs, w_gate, w_up, blk_expert, n_used, *, blk, tn):
    n_rows, d = xs.shape
    f = w_gate.shape[2]
    return pl.pallas_call(
        _expert_up_kernel,
        grid_spec=pltpu.PrefetchScalarGridSpec(
            num_scalar_prefetch=2,
            grid=(n_rows // blk, f // tn),
            in_specs=[pl.BlockSpec((blk, d), lambda i, j, be, nu: (i, 0)),
                      pl.BlockSpec((1, d, tn), lambda i, j, be, nu: (be[i], 0, j)),
                      pl.BlockSpec((1, d, tn), lambda i, j, be, nu: (be[i], 0, j))],
            out_specs=pl.BlockSpec((blk, tn), lambda i, j, be, nu: (i, j)),
            scratch_shapes=[pltpu.VMEM((blk, d), BF16)]),
        out_shape=jax.ShapeDtypeStruct((n_rows, f), BF16),
        compiler_params=_params(("parallel", "arbitrary")),
        name="expert_up",
    )(blk_expert, n_used, xs, w_gate, w_up)


def _expert_down_kernel(blk_expert_ref, n_used_ref, hb_ref, wd_ref, o_ref):
    del blk_expert_ref
    i = pl.program_id(0)

    @pl.when(i < n_used_ref[0])
    def _compute():
        o_ref[...] = jnp.dot(hb_ref[...], wd_ref[0], preferred_element_type=F32)

    @pl.when(i >= n_used_ref[0])
    def _skip():
        o_ref[...] = jnp.zeros(o_ref.shape, o_ref.dtype)


def expert_down(hb, w_down, blk_expert, n_used, *, blk, tn):
    n_rows, f = hb.shape
    d = w_down.shape[2]
    return pl.pallas_call(
        _expert_down_kernel,
        grid_spec=pltpu.PrefetchScalarGridSpec(
            num_scalar_prefetch=2,
            grid=(n_rows // blk, d // tn),
            in_specs=[pl.BlockSpec((blk, f), lambda i, j, be, nu: (i, 0)),
                      pl.BlockSpec((1, f, tn), lambda i, j, be, nu: (be[i], 0, j))],
            out_specs=pl.BlockSpec((blk, tn), lambda i, j, be, nu: (i, j))),
        out_shape=jax.ShapeDtypeStruct((n_rows, d), F32),
        compiler_params=_params(("parallel", "parallel")),
        name="expert_down",
    )(blk_expert, n_used, hb, w_down)


def moe_swiglu(h, gain, w_router, w_gate, w_up, w_down):
    m, d = h.shape
    n_experts = w_router.shape[1]
    blk = MOE_ROW_BLOCK if m >= 8 * MOE_ROW_BLOCK else LANES
    tm = _pick_tile(m, 256)
    w_router_padded = jnp.concatenate([w_router, jnp.zeros((d, LANES - n_experts), BF16)], axis=1)
    u, route, counts = moe_router(h, gain, w_router_padded, n_experts, tm=_pick_tile(m, 512))

    counts = counts[0, :n_experts].astype(I32)
    padded = (counts + blk - 1) // blk * blk
    pad_end = jnp.cumsum(padded)
    pad_start = pad_end - padded
    idx1 = route[:, ROUTE_IDX1].astype(I32)
    idx2 = route[:, ROUTE_IDX2].astype(I32)
    dest1 = pad_start[idx1] + route[:, ROUTE_RANK1].astype(I32)
    dest2 = pad_start[idx2] + route[:, ROUTE_RANK2].astype(I32)
    n_blocks = -(-(TOP_K_EXPERTS * m + n_experts * (blk - 1)) // blk)
    blk_start = jnp.arange(n_blocks, dtype=I32) * blk
    blk_expert = jnp.minimum(jnp.searchsorted(pad_end, blk_start, side="right"),
                             n_experts - 1).astype(I32)
    n_used = (pad_end[-1:] // blk).astype(I32)

    xs = scatter_rows(u, dest1, dest2, n_blocks * blk, tm=tm)
    hb = expert_up(xs, w_gate, w_up, blk_expert, n_used, blk=blk, tn=_pick_tile(w_gate.shape[2], 512))
    rows = expert_down(hb, w_down, blk_expert, n_used, blk=blk, tn=_pick_tile(d, 512))
    return combine_rows(rows, dest1, dest2, route, h, tm=tm)


def _split_w_in(w_in_layer, d_model):
    attn_dim = N_HEADS * HEAD_DIM
    kv_dim = N_KV_HEADS * HEAD_DIM
    sizes = (attn_dim, kv_dim, kv_dim, IDX_HEADS * IDX_DIM, IDX_DIM, IDX_HEADS, d_model, d_model,
             d_model, d_model)
    parts, off = [], 0
    for s in sizes:
        parts.append(w_in_layer[:, off:off + s])
        off += s
    assert off == w_in_layer.shape[1]
    return parts


def _layer_stack(x, p, weights, cache, *, past):
    bsz, length, d_model = x.shape
    m = bsz * length
    kv_dim = N_KV_HEADS * HEAD_DIM
    pos = past + jnp.arange(length, dtype=I32)
    row_pos = jnp.tile(pos, bsz)
    depth = weights["w_in"].shape[0]

    cos_q, sin_q = _rope_tables(row_pos, 512)
    cos_k, sin_k = _rope_tables(row_pos, kv_dim)
    cos_i, sin_i = _rope_tables(row_pos, LANES, rotary_width=IDX_DIM)
    ones_q = _block_ones(512)
    ones_k = _block_ones(kv_dim)

    n_keys = past + length
    n_sel = min(TOPK_MAX, n_keys // 4)
    if past == 0:
        tq_sel, tk, tq_attn = min(256, length), min(512, length), min(256, length)
        n_keys_pad = n_keys
    else:
        tq_sel = tq_attn = length
        tk = 384
        n_keys_pad = -(-n_keys // tk) * tk

    h = x.reshape(m, d_model)
    ks, vs, kis, convs = [], [], [], []
    for i in range(depth):
        wq, wk, wv, wqi, wki, wwi, wglu_a, wglu_g, wga, wgb = _split_w_in(weights["w_in"][i], d_model)
        w_kiwi = jnp.concatenate(
            [wki, wwi, jnp.zeros((d_model, LANES - IDX_DIM - IDX_HEADS), BF16)], axis=1)
        g_mix = weights["g_mix"][i]

        qn_gain = jnp.tile(weights["q_norm_g"][i].astype(F32), 512 // HEAD_DIM).reshape(1, 512)
        kn_gain = jnp.tile(weights["k_norm_g"][i].astype(F32), kv_dim // HEAD_DIM).reshape(1, kv_dim)

        def norm_rope(accs, tiled, rows, consts):
            y = _head_rms_norm(accs[0], consts[0], consts[1])
            y = _rope(y, rows[0], rows[1])
            return (y, y)

        (q_bf,) = fused_mm("proj_q", [h], [wq], [0],
                           lambda a, t, r, c: (_rope(_head_rms_norm(a[0], c[0], c[1]), r[0], r[1]),),
                           [BF16], gains=[g_mix], rows=[cos_q, sin_q], consts=[ones_q, qn_gain])
        k_f32, k_bf = fused_mm("proj_k", [h], [wk], [0], norm_rope, [F32, BF16], gains=[g_mix],
                               rows=[cos_k, sin_k], consts=[ones_k, kn_gain])
        v_f32, v_bf = fused_mm("proj_v", [h], [wv], [0], lambda a, t, r, c: (a[0], a[0]),
                               [F32, BF16], gains=[g_mix])
        (qi_bf,) = fused_mm("proj_qi", [h], [wqi], [0],
                            lambda a, t, r, c: (_rope(a[0], r[0], r[1]),),
                            [BF16], gains=[g_mix], rows=[cos_q, sin_q])
        (kiwi,) = fused_mm("proj_kiwi", [h], [w_kiwi], [0],
                           lambda a, t, r, c: (_rope(a[0], r[0], r[1]),),
                           [F32], gains=[g_mix], rows=[cos_i, sin_i])
        (glu,) = fused_mm("proj_glu", [h], [wglu_a, wglu_g], [0, 0],
                          lambda a, t, r, c: (a[0] * _sigmoid(a[1]),), [F32], gains=[g_mix])
        gate_a, gate_b = fused_mm("proj_gates", [h], [wga, wgb], [0, 0],
                                  lambda a, t, r, c: (_sigmoid(a[0]), _sigmoid(a[1])),
                                  [F32, F32], gains=[g_mix])

        ki_f32 = kiwi[:, :IDX_DIM]
        wi = kiwi[:, IDX_DIM:IDX_DIM + IDX_HEADS]

        k3 = k_bf.reshape(bsz, length, kv_dim)
        v3 = v_bf.reshape(bsz, length, kv_dim)
        ki3 = ki_f32.astype(BF16).reshape(bsz, length, IDX_DIM)
        if past:
            pad = n_keys_pad - n_keys
            k3 = jnp.concatenate([cache["k"][i].reshape(bsz, past, kv_dim).astype(BF16), k3,
                                  jnp.zeros((bsz, pad, kv_dim), BF16)], axis=1)
            v3 = jnp.concatenate([cache["v"][i].reshape(bsz, past, kv_dim).astype(BF16), v3,
                                  jnp.zeros((bsz, pad, kv_dim), BF16)], axis=1)
            ki3 = jnp.concatenate([cache["kidx"][i].astype(BF16), ki3,
                                   jnp.zeros((bsz, pad, IDX_DIM), BF16)], axis=1)
        bias = select_bias(qi_bf.reshape(bsz, length, -1), ki3, wi.reshape(bsz, length, IDX_HEADS),
                           tq=tq_sel, tk=tk, causal=(past == 0), q_pos0=past, n_valid=n_keys,
                           n_sel=n_sel)
        attn = masked_attention(q_bf.reshape(bsz, length, -1), k3, v3, bias, tq=tq_attn,
                                causal=(past == 0))
        attn = attn.reshape(m, -1)

        glu3 = glu.reshape(bsz, length, d_model)
        if past:
            ctx = cache["conv"][i]
        else:
            ctx = jnp.zeros((bsz, CONV_WIDTH - 1, d_model), F32)
        lead = jnp.zeros((bsz, CONV_HALO - (CONV_WIDTH - 1), d_model), F32)
        l_pad = -(-length // CONV_HALO) * CONV_HALO
        tail = jnp.zeros((bsz, l_pad - length, d_model), F32)
        hist = jnp.concatenate([lead, ctx, glu3, tail], axis=1)
        conv_state = hist[:, CONV_HALO - (CONV_WIDTH - 1) + length:CONV_HALO + length]
        z = conv_ln_silu(hist, weights["w_dw"][i], weights["b_dw"][i], weights["conv_ln_g"][i],
                         weights["conv_ln_b"][i], tm=min(256, l_pad))
        z = z[:, :length].reshape(m, d_model)

        (merged,) = fused_mm("mix", [attn, z], [weights["w_o_attn"][i], weights["w_o_conv"][i]],
                             [0, 1], lambda a, t, r, c: (t[0] * a[0] + t[1] * a[1],), [BF16],
                             tiled=[gate_a, gate_b])
        (h,) = fused_mm("out_proj", [merged], [weights["w_out"][i]], [0],
                        lambda a, t, r, c: (t[0] + a[0],), [F32], tiled=[h])

        j = i // 2
        if i % 2 == 0:
            (hb,) = fused_mm("ffn_up", [h], [weights["ffn_w_gate"][j], weights["ffn_w_up"][j]],
                             [0, 0], lambda a, t, r, c: (_silu(a[0]) * a[1],), [BF16],
                             gains=[weights["g_ffn"][i]], tn=1408)
            (h,) = fused_mm("ffn_down", [hb], [weights["ffn_w_down"][j]], [0],
                            lambda a, t, r, c: (t[0] + a[0],), [F32], tiled=[h])
        else:
            h = moe_swiglu(h, weights["g_ffn"][i], weights["moe_router"][j],
                           weights["moe_w_gate"][j], weights["moe_w_up"][j],
                           weights["moe_w_down"][j])

        (h,) = fused_mm("ple", [h, p[i].reshape(m, -1)],
                        [weights["w_ple_gate"][i], weights["w_ple_proj"][i]],
                        [0, 1], lambda a, t, r, c: (t[0] + _sigmoid(a[0]) * a[1],), [F32],
                        gains=[weights["g_ple"][i], None], tiled=[h])

        ks.append(k_f32.reshape(bsz, length, N_KV_HEADS, HEAD_DIM))
        vs.append(v_f32.reshape(bsz, length, N_KV_HEADS, HEAD_DIM))
        kis.append(ki_f32.reshape(bsz, length, IDX_DIM))
        convs.append(conv_state)
    return (h.reshape(bsz, length, d_model), jnp.stack(ks), jnp.stack(vs), jnp.stack(kis),
            jnp.stack(convs))


def kernel(x_prompt, x_sample, cache_k, cache_v, cache_kidx, state_conv, p_prompt, p_sample, w_in, q_norm_g, k_norm_g, w_dw, b_dw, conv_ln_g, conv_ln_b, w_o_attn, w_o_conv, w_out, g_mix, g_ffn, ffn_w_gate, ffn_w_up, ffn_w_down, moe_router, moe_w_gate, moe_w_up, moe_w_down, g_ple, w_ple_gate, w_ple_proj):
    bf = lambda a: a.astype(BF16)
    weights = dict(
        w_in=bf(w_in), q_norm_g=q_norm_g, k_norm_g=k_norm_g, w_dw=w_dw, b_dw=b_dw,
        conv_ln_g=conv_ln_g, conv_ln_b=conv_ln_b, w_o_attn=bf(w_o_attn), w_o_conv=bf(w_o_conv),
        w_out=bf(w_out), g_mix=g_mix, g_ffn=g_ffn, ffn_w_gate=bf(ffn_w_gate),
        ffn_w_up=bf(ffn_w_up), ffn_w_down=bf(ffn_w_down), moe_router=bf(moe_router),
        moe_w_gate=bf(moe_w_gate), moe_w_up=bf(moe_w_up), moe_w_down=bf(moe_w_down), g_ple=g_ple,
        w_ple_gate=bf(w_ple_gate), w_ple_proj=bf(w_ple_proj))
    cache = dict(k=cache_k, v=cache_v, kidx=cache_kidx, conv=state_conv)
    past_len = cache_k.shape[2]
    y_p, k_p, v_p, ki_p, conv_p = _layer_stack(x_prompt, p_prompt, weights, None, past=0)
    y_s, k_s, v_s, ki_s, conv_s = _layer_stack(x_sample, p_sample, weights, cache, past=past_len)
    return (y_p, y_s, k_p, v_p, ki_p, conv_p, k_s, v_s, ki_s, conv_s)
```

```python
import functools

import jax
import jax.numpy as jnp
from jax import lax
from jax.experimental import pallas as pl
from jax.experimental.pallas import tpu as pltpu

F32 = jnp.float32
BF16 = jnp.bfloat16
I32 = jnp.int32

CHUNK = 64
CHUNK_SHIFT = CHUNK.bit_length() - 1
assert 1 << CHUNK_SHIFT == CHUNK
N_HEADS = 16
HEAD_DIM = 64
N_KV_HEADS = 4
Q_PER_KV = N_HEADS // N_KV_HEADS
IDX_HEADS = 8
IDX_DIM = 64
TOPK_MAX = 256
ROPE_THETA = 10000.0
CONV_WIDTH = 31
TOP_K_EXPERTS = 2
EPS = 1e-6

LANES = 128
CONV_HALO = 32
MASKED_BIAS = -1e30
LOG2_E = 1.4426950408889634
INT_MIN = -2 ** 31
KEY_NEG_INF = -2139095041
SEARCH_ROWS = 128
EMIT_ROWS = 128
MOE_ROW_BLOCK = 512
VMEM_LIMIT = 52 * 1024 * 1024


def _params(semantics):
    return pltpu.CompilerParams(dimension_semantics=semantics, vmem_limit_bytes=VMEM_LIMIT)


def _pick_tile(n, pref):
    if n <= pref:
        return n
    best = None
    for t in range(LANES, pref + 1, LANES):
        if n % t == 0:
            best = t
    assert best is not None, (n, pref)
    return best


def _fused_mm_kernel(*refs, n_lhs, has_gain, pair_lhs, n_tiled, n_rows, n_consts, n_out,
                     epilogue):
    it = iter(refs)
    lhs_refs = [next(it) for _ in range(n_lhs)]
    gain_refs = [next(it) if has_gain[a] else None for a in range(n_lhs)]
    rhs_refs = [next(it) for _ in range(len(pair_lhs))]
    tiled_refs = [next(it) for _ in range(n_tiled)]
    row_refs = [next(it) for _ in range(n_rows)]
    const_refs = [next(it) for _ in range(n_consts)]
    out_refs = [next(it) for _ in range(n_out)]
    stage_refs = [next(it) for _ in range(n_lhs)]

    @pl.when(pl.program_id(1) == 0)
    def _stage():
        for a in range(n_lhs):
            x = lhs_refs[a][...].astype(F32)
            if has_gain[a]:
                x = x * lax.rsqrt(jnp.mean(x * x, axis=-1, keepdims=True) + EPS)
                x = x * gain_refs[a][...]
            stage_refs[a][...] = x.astype(BF16)

    accs = [jnp.dot(stage_refs[pair_lhs[r]][...], rhs_refs[r][...], preferred_element_type=F32)
            for r in range(len(pair_lhs))]
    outs = epilogue(accs, [t[...] for t in tiled_refs], [t[...] for t in row_refs],
                    [t[...] for t in const_refs])
    for o_ref, o in zip(out_refs, outs):
        o_ref[...] = o.astype(o_ref.dtype)


def fused_mm(name, lhs, rhs, pair_lhs, epilogue, out_dtypes, *, gains=None, tiled=(), rows=(),
             consts=(), tm=512, tn=512):
    m = lhs[0].shape[0]
    n = rhs[0].shape[1]
    tm = _pick_tile(m, tm) if m % LANES == 0 else m
    tn = _pick_tile(n, tn)
    n_lhs = len(lhs)
    gains = list(gains) if gains is not None else [None] * n_lhs
    has_gain = tuple(g is not None for g in gains)

    in_specs, args = [], []
    for x in lhs:
        in_specs.append(pl.BlockSpec((tm, x.shape[1]), lambda i, j: (i, 0)))
        args.append(x)
    for g in gains:
        if g is not None:
            g2 = g.reshape(1, -1).astype(F32)
            in_specs.append(pl.BlockSpec(g2.shape, lambda i, j: (0, 0)))
            args.append(g2)
    for w in rhs:
        in_specs.append(pl.BlockSpec((w.shape[0], tn), lambda i, j: (0, j)))
        args.append(w)
    for t in tiled:
        in_specs.append(pl.BlockSpec((tm, tn), lambda i, j: (i, j)))
        args.append(t)
    for t in rows:
        in_specs.append(pl.BlockSpec((tm, t.shape[1]), lambda i, j: (i, 0)))
        args.append(t)
    for c in consts:
        in_specs.append(pl.BlockSpec(c.shape, lambda i, j: (0, 0)))
        args.append(c)

    kernel = functools.partial(
        _fused_mm_kernel, n_lhs=n_lhs, has_gain=has_gain, pair_lhs=tuple(pair_lhs),
        n_tiled=len(tiled), n_rows=len(rows), n_consts=len(consts), n_out=len(out_dtypes),
        epilogue=epilogue)
    outs = pl.pallas_call(
        kernel,
        grid=(m // tm, n // tn),
        in_specs=in_specs,
        out_specs=[pl.BlockSpec((tm, tn), lambda i, j: (i, j)) for _ in out_dtypes],
        out_shape=[jax.ShapeDtypeStruct((m, n), dt) for dt in out_dtypes],
        scratch_shapes=[pltpu.VMEM((tm, x.shape[1]), BF16) for x in lhs],
        compiler_params=_params(("parallel", "arbitrary")),
        name=name,
    )(*args)
    return outs


def _head_rms_norm(x, block_ones, gain):
    x2 = x * x
    hi = x2.astype(BF16)
    lo = (x2 - hi.astype(F32)).astype(BF16)
    ss = (jnp.dot(hi, block_ones, preferred_element_type=F32)
          + jnp.dot(lo, block_ones, preferred_element_type=F32))
    return x * lax.rsqrt(ss * (1.0 / HEAD_DIM) + EPS) * gain


def _rope(x, cos, sin_signed):
    n = x.shape[1]
    lane = lax.broadcasted_iota(I32, x.shape, 1)
    first_half = (lane & (HEAD_DIM // 2)) == 0
    partner = jnp.where(first_half, pltpu.roll(x, n - HEAD_DIM // 2, 1),
                        pltpu.roll(x, HEAD_DIM // 2, 1))
    return x * cos + partner * sin_signed


def _sigmoid(x):
    return 1.0 / (1.0 + jnp.exp(-x))


def _silu(x):
    return x * _sigmoid(x)


def _rope_tables(pos, width, rotary_width=None):
    half = HEAD_DIM // 2
    inv = ROPE_THETA ** (-jnp.arange(half, dtype=F32) / half)
    ang = pos.astype(F32)[:, None] * inv[None, :]
    cos = jnp.cos(ang)
    sin = jnp.sin(ang)
    cos64 = jnp.concatenate([cos, cos], axis=-1)
    sin64 = jnp.concatenate([-sin, sin], axis=-1)
    rotary_width = width if rotary_width is None else rotary_width
    reps = rotary_width // HEAD_DIM
    cos_t = jnp.tile(cos64, (1, reps))
    sin_t = jnp.tile(sin64, (1, reps))
    if rotary_width < width:
        pad = width - rotary_width
        cos_t = jnp.concatenate([cos_t, jnp.ones((pos.shape[0], pad), F32)], axis=-1)
        sin_t = jnp.concatenate([sin_t, jnp.zeros((pos.shape[0], pad), F32)], axis=-1)
    return cos_t, sin_t


def _block_ones(width):
    head = jnp.arange(width, dtype=I32) // HEAD_DIM
    return (head[:, None] == head[None, :]).astype(BF16)


def _float_key(x):
    b = lax.bitcast_convert_type(x, I32)
    return jnp.where(b >= 0, b, b ^ jnp.int32(0x7FFFFFFF))


def _select_kernel(qi_ref, ki_ref, wi_ref, tri_ref, bias_ref, key_ref, *, tq, tk, nkb, causal,
                   q_pos0, n_valid, n_sel, rows):
    qb = pl.program_id(1)
    n_chunks = tk // LANES
    n_sub = tq // rows
    if causal:
        n_act = jnp.minimum(nkb, ((qb + 1) * tq + tk - 1) // tk)
    else:
        n_act = nkb
    q_pos = q_pos0 + qb * tq + lax.broadcasted_iota(I32, (tq, tk), 0)
    q_chunk = lax.shift_right_logical(q_pos, CHUNK_SHIFT)
    w = (wi_ref[0] * (IDX_HEADS ** -0.5)) * (IDX_DIM ** -0.5)
    qi = qi_ref[0]

    def score_block(kb, carry):
        start = pl.multiple_of(kb * tk, tk)
        ki = ki_ref[0, pl.ds(start, tk), :]
        s = jnp.zeros((tq, tk), F32)
        for h in range(IDX_HEADS):
            r = lax.dot_general(qi[:, h * IDX_DIM:(h + 1) * IDX_DIM], ki,
                                (((1,), (1,)), ((), ())), preferred_element_type=F32)
            s = s + jnp.maximum(r, 0.0) * w[:, h:h + 1]
        k_pos = start + lax.broadcasted_iota(I32, (tq, tk), 1)
        admissible = (lax.shift_right_logical(k_pos, CHUNK_SHIFT) <= q_chunk) & (k_pos < n_valid)
        key = _float_key(jnp.where(admissible, s, -jnp.inf))
        for sub in range(n_sub):
            key_ref[sub, kb] = key[sub * rows:(sub + 1) * rows]
        return carry

    lax.fori_loop(0, n_act, score_block, 0)
    for sub in range(n_sub):
        key_ref[sub, n_act] = jnp.full((rows, tk), INT_MIN, I32)
    for sub in range(n_sub):
        _select_rows(key_ref.at[sub], tri_ref, bias_ref.at[0, :, sub * rows:(sub + 1) * rows, :],
                     n_act=n_act, rows=rows, tk=tk, n_sel=n_sel)

    def fill_block(kb, carry):
        bias_ref[0, kb] = jnp.full((tq, tk), MASKED_BIAS, bias_ref.dtype)
        return carry

    lax.fori_loop(n_act, nkb, fill_block, 0)


def _select_rows(key_ref, tri_ref, out_ref, *, n_act, rows, tk, n_sel):
    tq = rows
    n_chunks = tk // LANES

    ones_mat = jnp.ones((LANES, LANES), BF16)

    def count(hit_fn):
        def pair(p, acc):
            for kb in (2 * p, 2 * p + 1):
                for c in range(n_chunks):
                    acc = acc + jnp.where(hit_fn(key_ref[kb, :, c * LANES:(c + 1) * LANES]), 1, 0)
            return acc
        acc = lax.fori_loop(0, (n_act + 1) // 2, pair, jnp.zeros((tq, LANES), I32))
        return jnp.dot(acc.astype(F32).astype(BF16), ones_mat, preferred_element_type=F32)

    def bit_step(i, prefix):
        cand = prefix | jnp.left_shift(jnp.int32(1), 31 - i)
        cand_signed = cand ^ jnp.int32(INT_MIN)
        cnt = count(lambda k: k >= cand_signed)
        return jnp.where(cnt >= n_sel, cand, prefix)

    tau_rep = lax.fori_loop(0, 32, bit_step, jnp.zeros((tq, LANES), I32)) ^ jnp.int32(INT_MIN)
    n_gt = count(lambda k: k > tau_rep)
    n_eq = count(lambda k: k == tau_rep)
    need_rep = n_sel - n_gt

    er = min(rows, EMIT_ROWS)
    for g in range(rows // er):
        rs = slice(g * er, (g + 1) * er)
        tau = tau_rep[rs, :1]
        need = need_rep[rs, :1]
        all_ties_taken = jnp.max(jnp.where(need_rep[rs] >= n_eq[rs], 0.0, 1.0)) == 0.0

        def emit_all_ties(kb, carry, rs=rs, tau=tau):
            key = key_ref[kb, rs, :]
            sel = (key >= tau) & (key > KEY_NEG_INF)
            out_ref[kb, rs, :] = jnp.where(sel, 0.0, MASKED_BIAS).astype(out_ref.dtype)
            return carry

        def emit_ranked(kb, seen, rs=rs, tau=tau, need=need):
            key = key_ref[kb, rs, :]
            eq = key == tau
            rank = seen + jnp.dot(jnp.where(eq, 1.0, 0.0).astype(BF16), tri_ref[...],
                                  preferred_element_type=F32)
            sel = ((key > tau) | (eq & (rank <= need))) & (key > KEY_NEG_INF)
            out_ref[kb, rs, :] = jnp.where(sel, 0.0, MASKED_BIAS).astype(out_ref.dtype)
            return rank[:, tk - 1:tk]

        @pl.when(all_ties_taken)
        def _fast(emit_all_ties=emit_all_ties):
            lax.fori_loop(0, n_act, emit_all_ties, 0)

        @pl.when(jnp.logical_not(all_ties_taken))
        def _ranked(emit_ranked=emit_ranked):
            lax.fori_loop(0, n_act, emit_ranked, jnp.zeros((er, 1), F32))


def select_bias(qi, ki, wi, *, tq, tk, causal, q_pos0, n_valid, n_sel):
    b, t_q, _ = qi.shape
    n_k = ki.shape[1]
    nkb = n_k // tk
    idx = jnp.arange(tk, dtype=I32)
    tri = (idx[:, None] <= idx[None, :]).astype(BF16)
    rows = min(tq, SEARCH_ROWS)
    kernel = functools.partial(_select_kernel, tq=tq, tk=tk, nkb=nkb, causal=causal,
                               q_pos0=q_pos0, n_valid=n_valid, n_sel=n_sel, rows=rows)
    return pl.pallas_call(
        kernel,
        grid=(b, t_q // tq),
        in_specs=[
            pl.BlockSpec((1, tq, qi.shape[2]), lambda bi, qb: (bi, qb, 0)),
            pl.BlockSpec((1, n_k, ki.shape[2]), lambda bi, qb: (bi, 0, 0)),
            pl.BlockSpec((1, tq, wi.shape[2]), lambda bi, qb: (bi, qb, 0)),
            pl.BlockSpec((tk, tk), lambda bi, qb: (0, 0)),
        ],
        out_specs=pl.BlockSpec((1, nkb, tq, tk), lambda bi, qb: (bi, 0, qb, 0)),
        out_shape=jax.ShapeDtypeStruct((b, nkb, t_q, tk), BF16),
        scratch_shapes=[pltpu.VMEM((tq // rows, nkb + 1, rows, tk), I32)],
        compiler_params=_params(("parallel", "parallel")),
        name="select_bias",
    )(qi, ki, wi, tri)


def _attn_kernel(q_ref, kt_ref, v_ref, bias_ref, o_ref, qs_ref, m_ref, acc_ref, *, tq, tk, nkb,
                 causal):
    qb = pl.program_id(1)
    kb = pl.program_id(2)
    last = jnp.minimum(nkb - 1, ((qb + 1) * tq - 1) // tk) if causal else nkb - 1
    rows = Q_PER_KV * tq

    @pl.when(kb == 0)
    def _init():
        m_ref[...] = jnp.full(m_ref.shape, MASKED_BIAS, F32)
        acc_ref[...] = jnp.zeros(acc_ref.shape, F32)
        for h in range(N_HEADS):
            g, r = divmod(h, Q_PER_KV)
            qh = q_ref[0, :, h * HEAD_DIM:(h + 1) * HEAD_DIM].astype(F32) * (HEAD_DIM ** -0.5)
            qs_ref[g, r * tq:(r + 1) * tq, :] = qh.astype(qs_ref.dtype)

    @pl.when(kb <= last)
    def _step():
        bias = bias_ref[0, 0].astype(F32)
        for g in range(N_KV_HEADS):
            s = jnp.dot(qs_ref[g], kt_ref[0, g], preferred_element_type=F32)
            s = (s * LOG2_E).reshape(Q_PER_KV, tq, tk) + bias[None]
            s = s.reshape(rows, tk)
            m_prev = m_ref[g]
            m_blk = jnp.max(s, axis=1, keepdims=True)
            m_new = jnp.maximum(m_prev, jnp.broadcast_to(m_blk, m_prev.shape))
            alpha = jnp.exp2(m_prev - m_new)
            p = jnp.concatenate(
                [jnp.exp2(s[:, c * LANES:(c + 1) * LANES] - m_new) for c in range(tk // LANES)],
                axis=1).astype(BF16)
            acc_ref[g] = alpha * acc_ref[g] + jnp.dot(p, v_ref[0, g], preferred_element_type=F32)
            m_ref[g] = m_new

    @pl.when(kb == nkb - 1)
    def _finish():
        for h in range(N_HEADS):
            g, r = divmod(h, Q_PER_KV)
            a = acc_ref[g, r * tq:(r + 1) * tq, :]
            out = a[:, :HEAD_DIM] / a[:, HEAD_DIM:HEAD_DIM + 1]
            o_ref[0, :, h * HEAD_DIM:(h + 1) * HEAD_DIM] = out.astype(o_ref.dtype)


def masked_attention(q, k, v, bias, *, tq, causal):
    b, t_q, d_q = q.shape
    n_k = k.shape[1]
    nkb, tk = bias.shape[1], bias.shape[3]
    kt = jnp.transpose(k.reshape(b, n_k, N_KV_HEADS, HEAD_DIM), (0, 2, 3, 1))
    v4 = jnp.transpose(v.reshape(b, n_k, N_KV_HEADS, HEAD_DIM), (0, 2, 1, 3))
    v_ext = jnp.concatenate(
        [v4, jnp.ones((b, N_KV_HEADS, n_k, 1), BF16),
         jnp.zeros((b, N_KV_HEADS, n_k, LANES - HEAD_DIM - 1), BF16)], axis=-1)

    def kv_block(qb, kb):
        return jnp.minimum(kb, ((qb + 1) * tq - 1) // tk) if causal else kb

    kernel = functools.partial(_attn_kernel, tq=tq, tk=tk, nkb=nkb, causal=causal)
    rows = Q_PER_KV * tq
    return pl.pallas_call(
        kernel,
        grid=(b, t_q // tq, nkb),
        in_specs=[
            pl.BlockSpec((1, tq, d_q), lambda bi, qb, kb: (bi, qb, 0)),
            pl.BlockSpec((1, N_KV_HEADS, HEAD_DIM, tk),
                         lambda bi, qb, kb: (bi, 0, 0, kv_block(qb, kb))),
            pl.BlockSpec((1, N_KV_HEADS, tk, LANES),
                         lambda bi, qb, kb: (bi, 0, kv_block(qb, kb), 0)),
            pl.BlockSpec((1, 1, tq, tk), lambda bi, qb, kb: (bi, kv_block(qb, kb), qb, 0)),
        ],
        out_specs=pl.BlockSpec((1, tq, d_q), lambda bi, qb, kb: (bi, qb, 0)),
        out_shape=jax.ShapeDtypeStruct((b, t_q, d_q), BF16),
        scratch_shapes=[pltpu.VMEM((N_KV_HEADS, rows, HEAD_DIM), BF16),
                        pltpu.VMEM((N_KV_HEADS, rows, LANES), F32),
                        pltpu.VMEM((N_KV_HEADS, rows, LANES), F32)],
        compiler_params=_params(("parallel", "parallel", "arbitrary")),
        name="masked_attention",
    )(q, kt, v_ext, bias)


def _conv_kernel(cur_ref, nxt_ref, w_ref, b_ref, g_ref, beta_ref, o_ref, buf_ref, conv_ref, *,
                 tm, cw):
    buf_ref[0:tm, :] = cur_ref[0]
    buf_ref[tm:tm + CONV_HALO, :] = nxt_ref[0]
    lead = CONV_HALO - (CONV_WIDTH - 1)
    c_total = buf_ref.shape[1]
    rw = min(tm, 128)
    for c in range(c_total // cw):
        cs = slice(c * cw, (c + 1) * cw)
        for r in range(tm // rw):
            acc = jnp.zeros((rw, cw), F32) + b_ref[:, cs]
            for j in range(CONV_WIDTH):
                r0 = lead + j + r * rw
                acc = acc + w_ref[j:j + 1, cs] * buf_ref[r0:r0 + rw, cs]
            conv_ref[r * rw:(r + 1) * rw, cs] = acc
    x = conv_ref[...]
    mu = jnp.mean(x, axis=-1, keepdims=True)
    xc = x - mu
    y = xc * lax.rsqrt(jnp.mean(xc * xc, axis=-1, keepdims=True) + EPS)
    y = y * g_ref[...] + beta_ref[...]
    o_ref[0] = _silu(y).astype(o_ref.dtype)


def conv_ln_silu(hist, w_dw, b_dw, ln_g, ln_b, *, tm):
    b, lp, c = hist.shape
    l = lp - CONV_HALO
    kernel = functools.partial(_conv_kernel, tm=tm, cw=LANES)
    vec = lambda a: a.reshape(1, c).astype(F32)
    return pl.pallas_call(
        kernel,
        grid=(b, l // tm),
        in_specs=[
            pl.BlockSpec((1, tm, c), lambda bi, i: (bi, i, 0)),
            pl.BlockSpec((1, CONV_HALO, c), lambda bi, i: (bi, (i + 1) * (tm // CONV_HALO), 0)),
            pl.BlockSpec((CONV_WIDTH, c), lambda bi, i: (0, 0)),
            pl.BlockSpec((1, c), lambda bi, i: (0, 0)),
            pl.BlockSpec((1, c), lambda bi, i: (0, 0)),
            pl.BlockSpec((1, c), lambda bi, i: (0, 0)),
        ],
        out_specs=pl.BlockSpec((1, tm, c), lambda bi, i: (bi, i, 0)),
        out_shape=jax.ShapeDtypeStruct((b, l, c), BF16),
        scratch_shapes=[pltpu.VMEM((tm + CONV_HALO, c), F32), pltpu.VMEM((tm, c), F32)],
        compiler_params=_params(("parallel", "parallel")),
        name="conv_ln_silu",
    )(hist, hist, w_dw.astype(F32), vec(b_dw), vec(ln_g), vec(ln_b))


ROUTE_IDX1, ROUTE_IDX2, ROUTE_GATE1, ROUTE_GATE2, ROUTE_RANK1, ROUTE_RANK2 = range(6)


def _router_kernel(h_ref, g_ref, wr_ref, lower_ref, u_ref, route_ref, counts_ref, seen_ref, *,
                   n_experts):
    @pl.when(pl.program_id(0) == 0)
    def _init():
        seen_ref[...] = jnp.zeros(seen_ref.shape, F32)

    x = h_ref[...]
    u = x * lax.rsqrt(jnp.mean(x * x, axis=-1, keepdims=True) + EPS) * g_ref[...]
    u_ref[...] = u
    logits = jnp.dot(u.astype(BF16), wr_ref[...], preferred_element_type=F32)
    lane = lax.broadcasted_iota(I32, logits.shape, 1).astype(F32)
    logits = jnp.where(lane < n_experts, logits, -jnp.inf)
    top1 = jnp.max(logits, axis=1, keepdims=True)
    idx1 = jnp.min(jnp.where(logits == top1, lane, float(LANES)), axis=1, keepdims=True)
    rest = jnp.where(lane == idx1, -jnp.inf, logits)
    top2 = jnp.max(rest, axis=1, keepdims=True)
    idx2 = jnp.min(jnp.where(rest == top2, lane, float(LANES)), axis=1, keepdims=True)
    e2 = jnp.exp(top2 - top1)
    denom = 1.0 + e2

    hot1 = jnp.where(lane == idx1, 1.0, 0.0)
    hot2 = jnp.where(lane == idx2, 1.0, 0.0)
    hot = hot1 + hot2
    before = seen_ref[...] + jnp.dot(lower_ref[...], hot.astype(BF16), preferred_element_type=F32)
    rank1 = jnp.sum(hot1 * before, axis=1, keepdims=True)
    rank2 = jnp.sum(hot2 * before, axis=1, keepdims=True)
    seen_ref[...] += jnp.sum(hot, axis=0, keepdims=True)
    counts_ref[...] = seen_ref[...]

    route = jnp.zeros(logits.shape, F32)
    for slot, val in ((ROUTE_IDX1, idx1), (ROUTE_IDX2, idx2), (ROUTE_GATE1, 1.0 / denom),
                      (ROUTE_GATE2, e2 / denom), (ROUTE_RANK1, rank1), (ROUTE_RANK2, rank2)):
        route = jnp.where(lane == slot, val, route)
    route_ref[...] = route


def moe_router(h, gain, w_router_padded, n_experts, *, tm):
    m, d = h.shape
    row = jnp.arange(tm, dtype=I32)
    lower = (row[None, :] < row[:, None]).astype(BF16)
    kernel = functools.partial(_router_kernel, n_experts=n_experts)
    return pl.pallas_call(
        kernel,
        grid=(m // tm,),
        in_specs=[pl.BlockSpec((tm, d), lambda i: (i, 0)),
                  pl.BlockSpec((1, d), lambda i: (0, 0)),
                  pl.BlockSpec((d, LANES), lambda i: (0, 0)),
                  pl.BlockSpec((tm, tm), lambda i: (0, 0))],
        out_specs=[pl.BlockSpec((tm, d), lambda i: (i, 0)),
                   pl.BlockSpec((tm, LANES), lambda i: (i, 0)),
                   pl.BlockSpec((1, LANES), lambda i: (0, 0))],
        out_shape=[jax.ShapeDtypeStruct((m, d), F32), jax.ShapeDtypeStruct((m, LANES), F32),
                   jax.ShapeDtypeStruct((1, LANES), F32)],
        scratch_shapes=[pltpu.VMEM((1, LANES), F32)],
        compiler_params=_params(("arbitrary",)),
        name="moe_router",
    )(h, gain.reshape(1, d).astype(F32), w_router_padded, lower)


def _scatter_rows_kernel(d1_ref, d2_ref, u_ref, init_ref, xs_ref, sem):
    del init_ref
    tm = u_ref.shape[0]

    def row_copy(j, dest):
        return pltpu.make_async_copy(u_ref.at[pl.ds(j, 1)], xs_ref.at[pl.ds(dest, 1)], sem)

    def issue(j, carry):
        row_copy(j, d1_ref[0, 0, j]).start()
        row_copy(j, d2_ref[0, 0, j]).start()
        return carry

    lax.fori_loop(0, tm, issue, 0)

    def drain(j, carry):
        row_copy(j, 0).wait()
        row_copy(j, 0).wait()
        return carry

    lax.fori_loop(0, tm, drain, 0)


def scatter_rows(u, dest1, dest2, n_rows, *, tm):
    m, d = u.shape
    smem_idx = lambda a: a.reshape(m // tm, 1, tm)
    idx_spec = pl.BlockSpec((1, 1, tm), lambda i: (i, 0, 0), memory_space=pltpu.SMEM)
    return pl.pallas_call(
        _scatter_rows_kernel,
        grid=(m // tm,),
        in_specs=[idx_spec, idx_spec,
                  pl.BlockSpec((tm, d), lambda i: (i, 0)),
                  pl.BlockSpec(memory_space=pl.ANY)],
        out_specs=pl.BlockSpec(memory_space=pl.ANY),
        out_shape=jax.ShapeDtypeStruct((n_rows, d), u.dtype),
        scratch_shapes=[pltpu.SemaphoreType.DMA],
        input_output_aliases={3: 0},
        compiler_params=_params(("arbitrary",)),
        name="moe_scatter_rows",
    )(smem_idx(dest1), smem_idx(dest2), u, jnp.zeros((n_rows, d), u.dtype))


def _combine_kernel(d1_ref, d2_ref, rows_ref, route_ref, h_ref, o_ref, buf_ref, sem):
    tm = h_ref.shape[0]

    def row_copy(slot, j, src):
        return pltpu.make_async_copy(rows_ref.at[pl.ds(src, 1)], buf_ref.at[slot, pl.ds(j, 1)], sem)

    def issue(j, carry):
        row_copy(0, j, d1_ref[0, 0, j]).start()
        row_copy(1, j, d2_ref[0, 0, j]).start()
        return carry

    lax.fori_loop(0, tm, issue, 0)

    def drain(j, carry):
        row_copy(0, j, 0).wait()
        row_copy(1, j, 0).wait()
        return carry

    lax.fori_loop(0, tm, drain, 0)
    route = route_ref[...]
    g1 = route[:, ROUTE_GATE1:ROUTE_GATE1 + 1]
    g2 = route[:, ROUTE_GATE2:ROUTE_GATE2 + 1]
    o_ref[...] = h_ref[...] + (g1 * buf_ref[0] + g2 * buf_ref[1])


def combine_rows(rows, dest1, dest2, route, h, *, tm):
    m, d = h.shape
    smem_idx = lambda a: a.reshape(m // tm, 1, tm)
    idx_spec = pl.BlockSpec((1, 1, tm), lambda i: (i, 0, 0), memory_space=pltpu.SMEM)
    return pl.pallas_call(
        _combine_kernel,
        grid=(m // tm,),
        in_specs=[idx_spec, idx_spec,
                  pl.BlockSpec(memory_space=pl.ANY),
                  pl.BlockSpec((tm, LANES), lambda i: (i, 0)),
                  pl.BlockSpec((tm, d), lambda i: (i, 0))],
        out_specs=pl.BlockSpec((tm, d), lambda i: (i, 0)),
        out_shape=jax.ShapeDtypeStruct((m, d), F32),
        scratch_shapes=[pltpu.VMEM((2, tm, d), F32), pltpu.SemaphoreType.DMA],
        compiler_params=_params(("arbitrary",)),
        name="moe_combine_rows",
    )(smem_idx(dest1), smem_idx(dest2), rows, route, h)


def _expert_up_kernel(blk_expert_ref, n_used_ref, xs_ref, wg_ref, wu_ref, o_ref, stage_ref):
    del blk_expert_ref
    i = pl.program_id(0)

    @pl.when(pl.program_id(1) == 0)
    def _stage():
        stage_ref[...] = xs_ref[...].astype(BF16)

    @pl.when(i < n_used_ref[0])
    def _compute():
        u = stage_ref[...]
        a = jnp.dot(u, wg_ref[0], preferred_element_type=F32)
        b = jnp.dot(u, wu_ref[0], preferred_element_type=F32)
        o_ref[...] = (_silu(a) * b).astype(o_ref.dtype)

    @pl.when(i >= n_used_ref[0])
    def _skip():
        o_ref[...] = jnp.zeros(o_ref.shape, o_ref.dtype)


def expert_up(xs, w_gate, w_up, blk_expert, n_used, *, blk, tn):
    n_rows, d = xs.shape
    f = w_gate.shape[2]
    return pl.pallas_call(
        _expert_up_kernel,
        grid_spec=pltpu.PrefetchScalarGridSpec(
            num_scalar_prefetch=2,
            grid=(n_rows // blk, f // tn),
            in_specs=[pl.BlockSpec((blk, d), lambda i, j, be, nu: (i, 0)),
                      pl.BlockSpec((1, d, tn), lambda i, j, be, nu: (be[i], 0, j)),
                      pl.BlockSpec((1, d, tn), lambda i, j, be, nu: (be[i], 0, j))],
            out_specs=pl.BlockSpec((blk, tn), lambda i, j, be, nu: (i, j)),
            scratch_shapes=[pltpu.VMEM((blk, d), BF16)]),
        out_shape=jax.ShapeDtypeStruct((n_rows, f), BF16),
        compiler_params=_params(("parallel", "arbitrary")),
        name="expert_up",
    )(blk_expert, n_used, xs, w_gate, w_up)


def _expert_down_kernel(blk_expert_ref, n_used_ref, hb_ref, wd_ref, o_ref):
    del blk_expert_ref
    i = pl.program_id(0)

    @pl.when(i < n_used_ref[0])
    def _compute():
        o_ref[...] = jnp.dot(hb_ref[...], wd_ref[0], preferred_element_type=F32)

    @pl.when(i >= n_used_ref[0])
    def _skip():
        o_ref[...] = jnp.zeros(o_ref.shape, o_ref.dtype)


def expert_down(hb, w_down, blk_expert, n_used, *, blk, tn):
    n_rows, f = hb.shape
    d = w_down.shape[2]
    return pl.pallas_call(
        _expert_down_kernel,
        grid_spec=pltpu.PrefetchScalarGridSpec(
            num_scalar_prefetch=2,
            grid=(n_rows // blk, d // tn),
            in_specs=[pl.BlockSpec((blk, f), lambda i, j, be, nu: (i, 0)),
                      pl.BlockSpec((1, f, tn), lambda i, j, be, nu: (be[i], 0, j))],
            out_specs=pl.BlockSpec((blk, tn), lambda i, j, be, nu: (i, j))),
        out_shape=jax.ShapeDtypeStruct((n_rows, d), F32),
        compiler_params=_params(("parallel", "parallel")),
        name="expert_down",
    )(blk_expert, n_used, hb, w_down)


def moe_swiglu(h, gain, w_router, w_gate, w_up, w_down):
    m, d = h.shape
    n_experts = w_router.shape[1]
    blk = MOE_ROW_BLOCK if m >= 8 * MOE_ROW_BLOCK else LANES
    tm = _pick_tile(m, 256)
    w_router_padded = jnp.concatenate([w_router, jnp.zeros((d, LANES - n_experts), BF16)], axis=1)
    u, route, counts = moe_router(h, gain, w_router_padded, n_experts, tm=_pick_tile(m, 512))

    counts = counts[0, :n_experts].astype(I32)
    padded = (counts + blk - 1) // blk * blk
    pad_end = jnp.cumsum(padded)
    pad_start = pad_end - padded
    idx1 = route[:, ROUTE_IDX1].astype(I32)
    idx2 = route[:, ROUTE_IDX2].astype(I32)
    dest1 = pad_start[idx1] + route[:, ROUTE_RANK1].astype(I32)
    dest2 = pad_start[idx2] + route[:, ROUTE_RANK2].astype(I32)
    n_blocks = -(-(TOP_K_EXPERTS * m + n_experts * (blk - 1)) // blk)
    blk_start = jnp.arange(n_blocks, dtype=I32) * blk
    blk_expert = jnp.minimum(jnp.searchsorted(pad_end, blk_start, side="right"),
                             n_experts - 1).astype(I32)
    n_used = (pad_end[-1:] // blk).astype(I32)

    xs = scatter_rows(u, dest1, dest2, n_blocks * blk, tm=tm)
    hb = expert_up(xs, w_gate, w_up, blk_expert, n_used, blk=blk, tn=_pick_tile(w_gate.shape[2], 512))
    rows = expert_down(hb, w_down, blk_expert, n_used, blk=blk, tn=_pick_tile(d, 512))
    return combine_rows(rows, dest1, dest2, route, h, tm=tm)


def _split_w_in(w_in_layer, d_model):
    attn_dim = N_HEADS * HEAD_DIM
    kv_dim = N_KV_HEADS * HEAD_DIM
    sizes = (attn_dim, kv_dim, kv_dim, IDX_HEADS * IDX_DIM, IDX_DIM, IDX_HEADS, d_model, d_model,
             d_model, d_model)
    parts, off = [], 0
    for s in sizes:
        parts.append(w_in_layer[:, off:off + s])
        off += s
    assert off == w_in_layer.shape[1]
    return parts


def _layer_stack(x, p, weights, cache, *, past):
    bsz, length, d_model = x.shape
    m = bsz * length
    kv_dim = N_KV_HEADS * HEAD_DIM
    pos = past + jnp.arange(length, dtype=I32)
    row_pos = jnp.tile(pos, bsz)
    depth = weights["w_in"].shape[0]

    cos_q, sin_q = _rope_tables(row_pos, 512)
    cos_k, sin_k = _rope_tables(row_pos, kv_dim)
    cos_i, sin_i = _rope_tables(row_pos, LANES, rotary_width=IDX_DIM)
    ones_q = _block_ones(512)
    ones_k = _block_ones(kv_dim)

    n_keys = past + length
    n_sel = min(TOPK_MAX, n_keys // 4)
    if past == 0:
        tq_sel, tk, tq_attn = min(256, length), min(512, length), min(256, length)
        n_keys_pad = n_keys
    else:
        tq_sel = tq_attn = length
        tk = 384
        n_keys_pad = -(-n_keys // tk) * tk

    h = x.reshape(m, d_model)
    ks, vs, kis, convs = [], [], [], []
    for i in range(depth):
        wq, wk, wv, wqi, wki, wwi, wglu_a, wglu_g, wga, wgb = _split_w_in(weights["w_in"][i], d_model)
        w_kiwi = jnp.concatenate(
            [wki, wwi, jnp.zeros((d_model, LANES - IDX_DIM - IDX_HEADS), BF16)], axis=1)
        g_mix = weights["g_mix"][i]

        qn_gain = jnp.tile(weights["q_norm_g"][i].astype(F32), 512 // HEAD_DIM).reshape(1, 512)
        kn_gain = jnp.tile(weights["k_norm_g"][i].astype(F32), kv_dim // HEAD_DIM).reshape(1, kv_dim)

        def norm_rope(accs, tiled, rows, consts):
            y = _head_rms_norm(accs[0], consts[0], consts[1])
            y = _rope(y, rows[0], rows[1])
            return (y, y)

        (q_bf,) = fused_mm("proj_q", [h], [wq], [0],
                           lambda a, t, r, c: (_rope(_head_rms_norm(a[0], c[0], c[1]), r[0], r[1]),),
                           [BF16], gains=[g_mix], rows=[cos_q, sin_q], consts=[ones_q, qn_gain])
        k_f32, k_bf = fused_mm("proj_k", [h], [wk], [0], norm_rope, [F32, BF16], gains=[g_mix],
                               rows=[cos_k, sin_k], consts=[ones_k, kn_gain])
        v_f32, v_bf = fused_mm("proj_v", [h], [wv], [0], lambda a, t, r, c: (a[0], a[0]),
                               [F32, BF16], gains=[g_mix])
        (qi_bf,) = fused_mm("proj_qi", [h], [wqi], [0],
                            lambda a, t, r, c: (_rope(a[0], r[0], r[1]),),
                            [BF16], gains=[g_mix], rows=[cos_q, sin_q])
        (kiwi,) = fused_mm("proj_kiwi", [h], [w_kiwi], [0],
                           lambda a, t, r, c: (_rope(a[0], r[0], r[1]),),
                           [F32], gains=[g_mix], rows=[cos_i, sin_i])
        (glu,) = fused_mm("proj_glu", [h], [wglu_a, wglu_g], [0, 0],
                          lambda a, t, r, c: (a[0] * _sigmoid(a[1]),), [F32], gains=[g_mix])
        gate_a, gate_b = fused_mm("proj_gates", [h], [wga, wgb], [0, 0],
                                  lambda a, t, r, c: (_sigmoid(a[0]), _sigmoid(a[1])),
                                  [F32, F32], gains=[g_mix])

        ki_f32 = kiwi[:, :IDX_DIM]
        wi = kiwi[:, IDX_DIM:IDX_DIM + IDX_HEADS]

        k3 = k_bf.reshape(bsz, length, kv_dim)
        v3 = v_bf.reshape(bsz, length, kv_dim)
        ki3 = ki_f32.astype(BF16).reshape(bsz, length, IDX_DIM)
        if past:
            pad = n_keys_pad - n_keys
            k3 = jnp.concatenate([cache["k"][i].reshape(bsz, past, kv_dim).astype(BF16), k3,
                                  jnp.zeros((bsz, pad, kv_dim), BF16)], axis=1)
            v3 = jnp.concatenate([cache["v"][i].reshape(bsz, past, kv_dim).astype(BF16), v3,
                                  jnp.zeros((bsz, pad, kv_dim), BF16)], axis=1)
            ki3 = jnp.concatenate([cache["kidx"][i].astype(BF16), ki3,
                                   jnp.zeros((bsz, pad, IDX_DIM), BF16)], axis=1)
        bias = select_bias(qi_bf.reshape(bsz, length, -1), ki3, wi.reshape(bsz, length, IDX_HEADS),
                           tq=tq_sel, tk=tk, causal=(past == 0), q_pos0=past, n_valid=n_keys,
                           n_sel=n_sel)
        attn = masked_attention(q_bf.reshape(bsz, length, -1), k3, v3, bias, tq=tq_attn,
                                causal=(past == 0))
        attn = attn.reshape(m, -1)

        glu3 = glu.reshape(bsz, length, d_model)
        if past:
            ctx = cache["conv"][i]
        else:
            ctx = jnp.zeros((bsz, CONV_WIDTH - 1, d_model), F32)
        lead = jnp.zeros((bsz, CONV_HALO - (CONV_WIDTH - 1), d_model), F32)
        l_pad = -(-length // CONV_HALO) * CONV_HALO
        tail = jnp.zeros((bsz, l_pad - length, d_model), F32)
        hist = jnp.concatenate([lead, ctx, glu3, tail], axis=1)
        conv_state = hist[:, CONV_HALO - (CONV_WIDTH - 1) + length:CONV_HALO + length]
        z = conv_ln_silu(hist, weights["w_dw"][i], weights["b_dw"][i], weights["conv_ln_g"][i],
                         weights["conv_ln_b"][i], tm=min(256, l_pad))
        z = z[:, :length].reshape(m, d_model)

        (merged,) = fused_mm("mix", [attn, z], [weights["w_o_attn"][i], weights["w_o_conv"][i]],
                             [0, 1], lambda a, t, r, c: (t[0] * a[0] + t[1] * a[1],), [BF16],
                             tiled=[gate_a, gate_b])
        (h,) = fused_mm("out_proj", [merged], [weights["w_out"][i]], [0],
                        lambda a, t, r, c: (t[0] + a[0],), [F32], tiled=[h])

        j = i // 2
        if i % 2 == 0:
            (hb,) = fused_mm("ffn_up", [h], [weights["ffn_w_gate"][j], weights["ffn_w_up"][j]],
                             [0, 0], lambda a, t, r, c: (_silu(a[0]) * a[1],), [BF16],
                             gains=[weights["g_ffn"][i]], tn=1408)
            (h,) = fused_mm("ffn_down", [hb], [weights["ffn_w_down"][j]], [0],
                            lambda a, t, r, c: (t[0] + a[0],), [F32], tiled=[h])
        else:
            h = moe_swiglu(h, weights["g_ffn"][i], weights["moe_router"][j],
                           weights["moe_w_gate"][j], weights["moe_w_up"][j],
                           weights["moe_w_down"][j])

        (h,) = fused_mm("ple", [h, p[i].reshape(m, -1)],
                        [weights["w_ple_gate"][i], weights["w_ple_proj"][i]],
                        [0, 1], lambda a, t, r, c: (t[0] + _sigmoid(a[0]) * a[1],), [F32],
                        gains=[weights["g_ple"][i], None], tiled=[h])

        ks.append(k_f32.reshape(bsz, length, N_KV_HEADS, HEAD_DIM))
        vs.append(v_f32.reshape(bsz, length, N_KV_HEADS, HEAD_DIM))
        kis.append(ki_f32.reshape(bsz, length, IDX_DIM))
        convs.append(conv_state)
    return (h.reshape(bsz, length, d_model), jnp.stack(ks), jnp.stack(vs), jnp.stack(kis),
            jnp.stack(convs))


def kernel(x_prompt, x_sample, cache_k, cache_v, cache_kidx, state_conv, p_prompt, p_sample, w_in, q_norm_g, k_norm_g, w_dw, b_dw, conv_ln_g, conv_ln_b, w_o_attn, w_o_conv, w_out, g_mix, g_ffn, ffn_w_gate, ffn_w_up, ffn_w_down, moe_router, moe_w_gate, moe_w_up, moe_w_down, g_ple, w_ple_gate, w_ple_proj):
    bf = lambda a: a.astype(BF16)
    weights = dict(
        w_in=bf(w_in), q_norm_g=q_norm_g, k_norm_g=k_norm_g, w_dw=w_dw, b_dw=b_dw,
        conv_ln_g=conv_ln_g, conv_ln_b=conv_ln_b, w_o_attn=bf(w_o_attn), w_o_conv=bf(w_o_conv),
        w_out=bf(w_out), g_mix=g_mix, g_ffn=g_ffn, ffn_w_gate=bf(ffn_w_gate),
        ffn_w_up=bf(ffn_w_up), ffn_w_down=bf(ffn_w_down), moe_router=bf(moe_router),
        moe_w_gate=bf(moe_w_gate), moe_w_up=bf(moe_w_up), moe_w_down=bf(moe_w_down), g_ple=g_ple,
        w_ple_gate=bf(w_ple_gate), w_ple_proj=bf(w_ple_proj))
    cache = dict(k=cache_k, v=cache_v, kidx=cache_kidx, conv=state_conv)
    past_len = cache_k.shape[2]
    y_p, k_p, v_p, ki_p, conv_p = _layer_stack(x_prompt, p_prompt, weights, None, past=0)
    y_s, k_s, v_s, ki_s, conv_s = _layer_stack(x_sample, p_sample, weights, cache, past=past_len)
    return (y_p, y_s, k_p, v_p, ki_p, conv_p, k_s, v_s, ki_s, conv_s)
```

```python
import functools

import jax
import jax.numpy as jnp
from jax import lax
from jax.experimental import pallas as pl
from jax.experimental.pallas import tpu as pltpu

F32 = jnp.float32
BF16 = jnp.bfloat16
I32 = jnp.int32

CHUNK = 64
CHUNK_SHIFT = CHUNK.bit_length() - 1
assert 1 << CHUNK_SHIFT == CHUNK
N_HEADS = 16
HEAD_DIM = 64
N_KV_HEADS = 4
Q_PER_KV = N_HEADS // N_KV_HEADS
IDX_HEADS = 8
IDX_DIM = 64
TOPK_MAX = 256
ROPE_THETA = 10000.0
CONV_WIDTH = 31
TOP_K_EXPERTS = 2
EPS = 1e-6

LANES = 128
CONV_HALO = 32
MASKED_BIAS = -1e30
LOG2_E = 1.4426950408889634
INT_MIN = -2 ** 31
KEY_NEG_INF = -2139095041
SEARCH_ROWS = 128
EMIT_ROWS = 128
MOE_ROW_BLOCK = 512
VMEM_LIMIT = 52 * 1024 * 1024


def _params(semantics):
    return pltpu.CompilerParams(dimension_semantics=semantics, vmem_limit_bytes=VMEM_LIMIT)


def _pick_tile(n, pref):
    if n <= pref:
        return n
    best = None
    for t in range(LANES, pref + 1, LANES):
        if n % t == 0:
            best = t
    assert best is not None, (n, pref)
    return best


def _rms_norm_kernel(x_ref, g_ref, o_ref):
    x = x_ref[...]
    y = x * lax.rsqrt(jnp.mean(x * x, axis=-1, keepdims=True) + EPS) * g_ref[...]
    o_ref[...] = y.astype(o_ref.dtype)


def rms_norm_rows(x, gain, *, tm=512):
    m, d = x.shape
    tm = _pick_tile(m, tm) if m % LANES == 0 else m
    return pl.pallas_call(
        _rms_norm_kernel,
        grid=(m // tm,),
        in_specs=[pl.BlockSpec((tm, d), lambda i: (i, 0)), pl.BlockSpec((1, d), lambda i: (0, 0))],
        out_specs=pl.BlockSpec((tm, d), lambda i: (i, 0)),
        out_shape=jax.ShapeDtypeStruct((m, d), BF16),
        compiler_params=_params(("parallel",)),
        name="rms_norm_rows",
    )(x, gain.reshape(1, d).astype(F32))


def _fused_mm_kernel(*refs, n_lhs, has_gain, staged, pair_lhs, n_tiled, n_rows, n_consts, n_out,
                     epilogue):
    it = iter(refs)
    lhs_refs = [next(it) for _ in range(n_lhs)]
    gain_refs = [next(it) if has_gain[a] else None for a in range(n_lhs)]
    rhs_refs = [next(it) for _ in range(len(pair_lhs))]
    tiled_refs = [next(it) for _ in range(n_tiled)]
    row_refs = [next(it) for _ in range(n_rows)]
    const_refs = [next(it) for _ in range(n_consts)]
    out_refs = [next(it) for _ in range(n_out)]
    operand_refs = [next(it) if staged[a] else lhs_refs[a] for a in range(n_lhs)]

    if any(staged):
        @pl.when(pl.program_id(1) == 0)
        def _stage():
            for a in range(n_lhs):
                if not staged[a]:
                    continue
                x = lhs_refs[a][...].astype(F32)
                if has_gain[a]:
                    x = x * lax.rsqrt(jnp.mean(x * x, axis=-1, keepdims=True) + EPS)
                    x = x * gain_refs[a][...]
                operand_refs[a][...] = x.astype(BF16)

    accs = [jnp.dot(operand_refs[pair_lhs[r]][...], rhs_refs[r][...], preferred_element_type=F32)
            for r in range(len(pair_lhs))]
    outs = epilogue(accs, [t[...] for t in tiled_refs], [t[...] for t in row_refs],
                    [t[...] for t in const_refs])
    for o_ref, o in zip(out_refs, outs):
        o_ref[...] = o.astype(o_ref.dtype)


def fused_mm(name, lhs, rhs, pair_lhs, epilogue, out_dtypes, *, gains=None, tiled=(), rows=(),
             consts=(), tm=512, tn=512):
    m = lhs[0].shape[0]
    n = rhs[0].shape[1]
    tm = _pick_tile(m, tm) if m % LANES == 0 else m
    tn = _pick_tile(n, tn)
    n_lhs = len(lhs)
    gains = list(gains) if gains is not None else [None] * n_lhs
    has_gain = tuple(g is not None for g in gains)
    staged = tuple(has_gain[a] or lhs[a].dtype != BF16 for a in range(n_lhs))

    in_specs, args = [], []
    for x in lhs:
        in_specs.append(pl.BlockSpec((tm, x.shape[1]), lambda i, j: (i, 0)))
        args.append(x)
    for g in gains:
        if g is not None:
            g2 = g.reshape(1, -1).astype(F32)
            in_specs.append(pl.BlockSpec(g2.shape, lambda i, j: (0, 0)))
            args.append(g2)
    for w in rhs:
        in_specs.append(pl.BlockSpec((w.shape[0], tn), lambda i, j: (0, j)))
        args.append(w)
    for t in tiled:
        in_specs.append(pl.BlockSpec((tm, tn), lambda i, j: (i, j)))
        args.append(t)
    for t in rows:
        in_specs.append(pl.BlockSpec((tm, t.shape[1]), lambda i, j: (i, 0)))
        args.append(t)
    for c in consts:
        in_specs.append(pl.BlockSpec(c.shape, lambda i, j: (0, 0)))
        args.append(c)

    kernel = functools.partial(
        _fused_mm_kernel, n_lhs=n_lhs, has_gain=has_gain, staged=staged, pair_lhs=tuple(pair_lhs),
        n_tiled=len(tiled), n_rows=len(rows), n_consts=len(consts), n_out=len(out_dtypes),
        epilogue=epilogue)
    outs = pl.pallas_call(
        kernel,
        grid=(m // tm, n // tn),
        in_specs=in_specs,
        out_specs=[pl.BlockSpec((tm, tn), lambda i, j: (i, j)) for _ in out_dtypes],
        out_shape=[jax.ShapeDtypeStruct((m, n), dt) for dt in out_dtypes],
        scratch_shapes=[pltpu.VMEM((tm, x.shape[1]), BF16) for a, x in enumerate(lhs) if staged[a]],
        compiler_params=_params(("parallel", "arbitrary")),
        name=name,
    )(*args)
    return outs


def _head_rms_norm(x, block_ones, gain):
    x2 = x * x
    hi = x2.astype(BF16)
    lo = (x2 - hi.astype(F32)).astype(BF16)
    ss = (jnp.dot(hi, block_ones, preferred_element_type=F32)
          + jnp.dot(lo, block_ones, preferred_element_type=F32))
    return x * lax.rsqrt(ss * (1.0 / HEAD_DIM) + EPS) * gain


def _rope(x, cos, sin_signed):
    n = x.shape[1]
    lane = lax.broadcasted_iota(I32, x.shape, 1)
    first_half = (lane & (HEAD_DIM // 2)) == 0
    partner = jnp.where(first_half, pltpu.roll(x, n - HEAD_DIM // 2, 1),
                        pltpu.roll(x, HEAD_DIM // 2, 1))
    return x * cos + partner * sin_signed


def _sigmoid(x):
    return 1.0 / (1.0 + jnp.exp(-x))


def _silu(x):
    return x * _sigmoid(x)


def _rope_tables(pos, width, rotary_width=None):
    half = HEAD_DIM // 2
    inv = ROPE_THETA ** (-jnp.arange(half, dtype=F32) / half)
    ang = pos.astype(F32)[:, None] * inv[None, :]
    cos = jnp.cos(ang)
    sin = jnp.sin(ang)
    cos64 = jnp.concatenate([cos, cos], axis=-1)
    sin64 = jnp.concatenate([-sin, sin], axis=-1)
    rotary_width = width if rotary_width is None else rotary_width
    reps = rotary_width // HEAD_DIM
    cos_t = jnp.tile(cos64, (1, reps))
    sin_t = jnp.tile(sin64, (1, reps))
    if rotary_width < width:
        pad = width - rotary_width
        cos_t = jnp.concatenate([cos_t, jnp.ones((pos.shape[0], pad), F32)], axis=-1)
        sin_t = jnp.concatenate([sin_t, jnp.zeros((pos.shape[0], pad), F32)], axis=-1)
    return cos_t, sin_t


def _block_ones(width):
    head = jnp.arange(width, dtype=I32) // HEAD_DIM
    return (head[:, None] == head[None, :]).astype(BF16)


def _float_key(x):
    b = lax.bitcast_convert_type(x, I32)
    return jnp.where(b >= 0, b, b ^ jnp.int32(0x7FFFFFFF))


def _select_kernel(qi_ref, ki_ref, wi_ref, tri_ref, bias_ref, key_ref, *, tq, tk, nkb, causal,
                   q_pos0, n_valid, n_sel, rows):
    qb = pl.program_id(1)
    n_chunks = tk // LANES
    n_sub = tq // rows
    if causal:
        n_act = jnp.minimum(nkb, ((qb + 1) * tq + tk - 1) // tk)
    else:
        n_act = nkb
    q_pos = q_pos0 + qb * tq + lax.broadcasted_iota(I32, (tq, tk), 0)
    q_chunk = lax.shift_right_logical(q_pos, CHUNK_SHIFT)
    w = (wi_ref[0] * (IDX_HEADS ** -0.5)) * (IDX_DIM ** -0.5)
    qi = qi_ref[0]

    def score_block(kb, carry):
        start = pl.multiple_of(kb * tk, tk)
        ki = ki_ref[0, pl.ds(start, tk), :]
        s = jnp.zeros((tq, tk), F32)
        for h in range(IDX_HEADS):
            r = lax.dot_general(qi[:, h * IDX_DIM:(h + 1) * IDX_DIM], ki,
                                (((1,), (1,)), ((), ())), preferred_element_type=F32)
            s = s + jnp.maximum(r, 0.0) * w[:, h:h + 1]
        k_pos = start + lax.broadcasted_iota(I32, (tq, tk), 1)
        admissible = (lax.shift_right_logical(k_pos, CHUNK_SHIFT) <= q_chunk) & (k_pos < n_valid)
        key = _float_key(jnp.where(admissible, s, -jnp.inf))
        for sub in range(n_sub):
            key_ref[sub, kb] = key[sub * rows:(sub + 1) * rows]
        return carry

    lax.fori_loop(0, n_act, score_block, 0)
    for sub in range(n_sub):
        key_ref[sub, n_act] = jnp.full((rows, tk), INT_MIN, I32)
    for sub in range(n_sub):
        _select_rows(key_ref.at[sub], tri_ref, bias_ref.at[0, :, sub * rows:(sub + 1) * rows, :],
                     n_act=n_act, rows=rows, tk=tk, n_sel=n_sel)

    def fill_block(kb, carry):
        bias_ref[0, kb] = jnp.full((tq, tk), MASKED_BIAS, bias_ref.dtype)
        return carry

    lax.fori_loop(n_act, nkb, fill_block, 0)


def _select_rows(key_ref, tri_ref, out_ref, *, n_act, rows, tk, n_sel):
    tq = rows
    n_chunks = tk // LANES

    ones_mat = jnp.ones((LANES, LANES), BF16)

    def count(hit_fn):
        def pair(p, acc):
            for kb in (2 * p, 2 * p + 1):
                for c in range(n_chunks):
                    acc = acc + jnp.where(hit_fn(key_ref[kb, :, c * LANES:(c + 1) * LANES]), 1, 0)
            return acc
        acc = lax.fori_loop(0, (n_act + 1) // 2, pair, jnp.zeros((tq, LANES), I32))
        return jnp.dot(acc.astype(F32).astype(BF16), ones_mat, preferred_element_type=F32)

    def bit_step(i, prefix):
        cand = prefix | jnp.left_shift(jnp.int32(1), 31 - i)
        cand_signed = cand ^ jnp.int32(INT_MIN)
        cnt = count(lambda k: k >= cand_signed)
        return jnp.where(cnt >= n_sel, cand, prefix)

    tau_rep = lax.fori_loop(0, 32, bit_step, jnp.zeros((tq, LANES), I32)) ^ jnp.int32(INT_MIN)
    n_gt = count(lambda k: k > tau_rep)
    n_eq = count(lambda k: k == tau_rep)
    need_rep = n_sel - n_gt

    er = min(rows, EMIT_ROWS)
    for g in range(rows // er):
        rs = slice(g * er, (g + 1) * er)
        tau = tau_rep[rs, :1]
        need = need_rep[rs, :1]
        all_ties_taken = jnp.max(jnp.where(need_rep[rs] >= n_eq[rs], 0.0, 1.0)) == 0.0

        def emit_all_ties(kb, carry, rs=rs, tau=tau):
            key = key_ref[kb, rs, :]
            sel = (key >= tau) & (key > KEY_NEG_INF)
            out_ref[kb, rs, :] = jnp.where(sel, 0.0, MASKED_BIAS).astype(out_ref.dtype)
            return carry

        def emit_ranked(kb, seen, rs=rs, tau=tau, need=need):
            key = key_ref[kb, rs, :]
            eq = key == tau
            rank = seen + jnp.dot(jnp.where(eq, 1.0, 0.0).astype(BF16), tri_ref[...],
                                  preferred_element_type=F32)
            sel = ((key > tau) | (eq & (rank <= need))) & (key > KEY_NEG_INF)
            out_ref[kb, rs, :] = jnp.where(sel, 0.0, MASKED_BIAS).astype(out_ref.dtype)
            return rank[:, tk - 1:tk]

        @pl.when(all_ties_taken)
        def _fast(emit_all_ties=emit_all_ties):
            lax.fori_loop(0, n_act, emit_all_ties, 0)

        @pl.when(jnp.logical_not(all_ties_taken))
        def _ranked(emit_ranked=emit_ranked):
            lax.fori_loop(0, n_act, emit_ranked, jnp.zeros((er, 1), F32))


def select_bias(qi, ki, wi, *, tq, tk, causal, q_pos0, n_valid, n_sel):
    b, t_q, _ = qi.shape
    n_k = ki.shape[1]
    nkb = n_k // tk
    idx = jnp.arange(tk, dtype=I32)
    tri = (idx[:, None] <= idx[None, :]).astype(BF16)
    rows = min(tq, SEARCH_ROWS)
    kernel = functools.partial(_select_kernel, tq=tq, tk=tk, nkb=nkb, causal=causal,
                               q_pos0=q_pos0, n_valid=n_valid, n_sel=n_sel, rows=rows)
    return pl.pallas_call(
        kernel,
        grid=(b, t_q // tq),
        in_specs=[
            pl.BlockSpec((1, tq, qi.shape[2]), lambda bi, qb: (bi, qb, 0)),
            pl.BlockSpec((1, n_k, ki.shape[2]), lambda bi, qb: (bi, 0, 0)),
            pl.BlockSpec((1, tq, wi.shape[2]), lambda bi, qb: (bi, qb, 0)),
            pl.BlockSpec((tk, tk), lambda bi, qb: (0, 0)),
        ],
        out_specs=pl.BlockSpec((1, nkb, tq, tk), lambda bi, qb: (bi, 0, qb, 0)),
        out_shape=jax.ShapeDtypeStruct((b, nkb, t_q, tk), BF16),
        scratch_shapes=[pltpu.VMEM((tq // rows, nkb + 1, rows, tk), I32)],
        compiler_params=_params(("parallel", "parallel")),
        name="select_bias",
    )(qi, ki, wi, tri)


def _attn_kernel(qb_ref, kb_ref, last_ref, q_ref, kt_ref, v_ref, bias_ref, o_ref, qs_ref, m_ref,
                 acc_ref, *, tq, tk):
    del qb_ref
    step = pl.program_id(1)
    rows = Q_PER_KV * tq

    @pl.when(kb_ref[step] == 0)
    def _init():
        m_ref[...] = jnp.full(m_ref.shape, MASKED_BIAS, F32)
        acc_ref[...] = jnp.zeros(acc_ref.shape, F32)
        for h in range(N_HEADS):
            g, r = divmod(h, Q_PER_KV)
            qh = q_ref[0, :, h * HEAD_DIM:(h + 1) * HEAD_DIM].astype(F32) * (HEAD_DIM ** -0.5)
            qs_ref[g, r * tq:(r + 1) * tq, :] = qh.astype(qs_ref.dtype)

    bias = bias_ref[0, 0].astype(F32)
    for g in range(N_KV_HEADS):
        s = jnp.dot(qs_ref[g], kt_ref[0, g], preferred_element_type=F32)
        s = (s * LOG2_E).reshape(Q_PER_KV, tq, tk) + bias[None]
        s = s.reshape(rows, tk)
        m_prev = m_ref[g]
        m_blk = jnp.max(s, axis=1, keepdims=True)
        m_new = jnp.maximum(m_prev, jnp.broadcast_to(m_blk, m_prev.shape))
        alpha = jnp.exp2(m_prev - m_new)
        p = jnp.concatenate(
            [jnp.exp2(s[:, c * LANES:(c + 1) * LANES] - m_new) for c in range(tk // LANES)],
            axis=1).astype(BF16)
        acc_ref[g] = alpha * acc_ref[g] + jnp.dot(p, v_ref[0, g], preferred_element_type=F32)
        m_ref[g] = m_new

    @pl.when(last_ref[step] == 1)
    def _finish():
        for h in range(N_HEADS):
            g, r = divmod(h, Q_PER_KV)
            a = acc_ref[g, r * tq:(r + 1) * tq, :]
            out = a[:, :HEAD_DIM] / a[:, HEAD_DIM:HEAD_DIM + 1]
            o_ref[0, :, h * HEAD_DIM:(h + 1) * HEAD_DIM] = out.astype(o_ref.dtype)


def masked_attention(q, k, v, bias, *, tq, causal):
    b, t_q, d_q = q.shape
    n_k = k.shape[1]
    nkb, tk = bias.shape[1], bias.shape[3]
    kt = jnp.transpose(k.reshape(b, n_k, N_KV_HEADS, HEAD_DIM), (0, 2, 3, 1))
    v4 = jnp.transpose(v.reshape(b, n_k, N_KV_HEADS, HEAD_DIM), (0, 2, 1, 3))
    v_ext = jnp.concatenate(
        [v4, jnp.ones((b, N_KV_HEADS, n_k, 1), BF16),
         jnp.zeros((b, N_KV_HEADS, n_k, LANES - HEAD_DIM - 1), BF16)], axis=-1)

    pairs = []
    for qb in range(t_q // tq):
        last = min(nkb - 1, ((qb + 1) * tq - 1) // tk) if causal else nkb - 1
        pairs += [(qb, kb, int(kb == last)) for kb in range(last + 1)]
    qb_ids, kb_ids, last_flags = (jnp.asarray(col, I32) for col in zip(*pairs))

    kernel = functools.partial(_attn_kernel, tq=tq, tk=tk)
    rows = Q_PER_KV * tq
    return pl.pallas_call(
        kernel,
        grid_spec=pltpu.PrefetchScalarGridSpec(
            num_scalar_prefetch=3,
            grid=(b, len(pairs)),
            in_specs=[
                pl.BlockSpec((1, tq, d_q), lambda bi, s, qbs, kbs, lf: (bi, qbs[s], 0)),
                pl.BlockSpec((1, N_KV_HEADS, HEAD_DIM, tk),
                             lambda bi, s, qbs, kbs, lf: (bi, 0, 0, kbs[s])),
                pl.BlockSpec((1, N_KV_HEADS, tk, LANES),
                             lambda bi, s, qbs, kbs, lf: (bi, 0, kbs[s], 0)),
                pl.BlockSpec((1, 1, tq, tk), lambda bi, s, qbs, kbs, lf: (bi, kbs[s], qbs[s], 0)),
            ],
            out_specs=pl.BlockSpec((1, tq, d_q), lambda bi, s, qbs, kbs, lf: (bi, qbs[s], 0)),
            scratch_shapes=[pltpu.VMEM((N_KV_HEADS, rows, HEAD_DIM), BF16),
                            pltpu.VMEM((N_KV_HEADS, rows, LANES), F32),
                            pltpu.VMEM((N_KV_HEADS, rows, LANES), F32)]),
        out_shape=jax.ShapeDtypeStruct((b, t_q, d_q), BF16),
        compiler_params=_params(("parallel", "arbitrary")),
        name="masked_attention",
    )(qb_ids, kb_ids, last_flags, q, kt, v_ext, bias)


def _conv_kernel(cur_ref, prev_ref, ctx_ref, w_ref, b_ref, g_ref, beta_ref, o_ref, buf_ref,
                 conv_ref, *, tm, cw):
    @pl.when(pl.program_id(1) == 0)
    def _first():
        buf_ref[0:CONV_HALO, :] = ctx_ref[0]

    @pl.when(pl.program_id(1) > 0)
    def _later():
        buf_ref[0:CONV_HALO, :] = prev_ref[0]

    buf_ref[CONV_HALO:CONV_HALO + tm, :] = cur_ref[0]
    lead = CONV_HALO - (CONV_WIDTH - 1)
    c_total = buf_ref.shape[1]
    rw = min(tm, 128)
    for c in range(c_total // cw):
        cs = slice(c * cw, (c + 1) * cw)
        for r in range(tm // rw):
            acc = jnp.zeros((rw, cw), F32) + b_ref[:, cs]
            for j in range(CONV_WIDTH):
                r0 = lead + j + r * rw
                acc = acc + w_ref[j:j + 1, cs] * buf_ref[r0:r0 + rw, cs]
            conv_ref[r * rw:(r + 1) * rw, cs] = acc
    x = conv_ref[...]
    mu = jnp.mean(x, axis=-1, keepdims=True)
    xc = x - mu
    y = xc * lax.rsqrt(jnp.mean(xc * xc, axis=-1, keepdims=True) + EPS)
    y = y * g_ref[...] + beta_ref[...]
    o_ref[0] = _silu(y).astype(o_ref.dtype)


def conv_ln_silu(y, ctx, w_dw, b_dw, ln_g, ln_b, *, tm):
    b, l, c = y.shape
    kernel = functools.partial(_conv_kernel, tm=tm, cw=LANES)
    vec = lambda a: a.reshape(1, c).astype(F32)
    halo_blocks = tm // CONV_HALO
    return pl.pallas_call(
        kernel,
        grid=(b, l // tm),
        in_specs=[
            pl.BlockSpec((1, tm, c), lambda bi, i: (bi, i, 0)),
            pl.BlockSpec((1, CONV_HALO, c),
                         lambda bi, i: (bi, jnp.maximum(i * halo_blocks - 1, 0), 0)),
            pl.BlockSpec((1, CONV_HALO, c), lambda bi, i: (bi, 0, 0)),
            pl.BlockSpec((CONV_WIDTH, c), lambda bi, i: (0, 0)),
            pl.BlockSpec((1, c), lambda bi, i: (0, 0)),
            pl.BlockSpec((1, c), lambda bi, i: (0, 0)),
            pl.BlockSpec((1, c), lambda bi, i: (0, 0)),
        ],
        out_specs=pl.BlockSpec((1, tm, c), lambda bi, i: (bi, i, 0)),
        out_shape=jax.ShapeDtypeStruct((b, l, c), BF16),
        scratch_shapes=[pltpu.VMEM((tm + CONV_HALO, c), F32), pltpu.VMEM((tm, c), F32)],
        compiler_params=_params(("parallel", "parallel")),
        name="conv_ln_silu",
    )(y, y, ctx, w_dw.astype(F32), vec(b_dw), vec(ln_g), vec(ln_b))


ROUTE_IDX1, ROUTE_IDX2, ROUTE_GATE1, ROUTE_GATE2, ROUTE_RANK1, ROUTE_RANK2 = range(6)


def _router_kernel(h_ref, g_ref, wr_ref, lower_ref, u_ref, route_ref, counts_ref, seen_ref, *,
                   n_experts):
    @pl.when(pl.program_id(0) == 0)
    def _init():
        seen_ref[...] = jnp.zeros(seen_ref.shape, F32)

    x = h_ref[...]
    u = x * lax.rsqrt(jnp.mean(x * x, axis=-1, keepdims=True) + EPS) * g_ref[...]
    u_ref[...] = u
    logits = jnp.dot(u.astype(BF16), wr_ref[...], preferred_element_type=F32)
    lane = lax.broadcasted_iota(I32, logits.shape, 1).astype(F32)
    logits = jnp.where(lane < n_experts, logits, -jnp.inf)
    top1 = jnp.max(logits, axis=1, keepdims=True)
    idx1 = jnp.min(jnp.where(logits == top1, lane, float(LANES)), axis=1, keepdims=True)
    rest = jnp.where(lane == idx1, -jnp.inf, logits)
    top2 = jnp.max(rest, axis=1, keepdims=True)
    idx2 = jnp.min(jnp.where(rest == top2, lane, float(LANES)), axis=1, keepdims=True)
    e2 = jnp.exp(top2 - top1)
    denom = 1.0 + e2

    hot1 = jnp.where(lane == idx1, 1.0, 0.0)
    hot2 = jnp.where(lane == idx2, 1.0, 0.0)
    hot = hot1 + hot2
    before = seen_ref[...] + jnp.dot(lower_ref[...], hot.astype(BF16), preferred_element_type=F32)
    rank1 = jnp.sum(hot1 * before, axis=1, keepdims=True)
    rank2 = jnp.sum(hot2 * before, axis=1, keepdims=True)
    seen_ref[...] += jnp.sum(hot, axis=0, keepdims=True)
    counts_ref[...] = seen_ref[...]

    route = jnp.zeros(logits.shape, F32)
    for slot, val in ((ROUTE_IDX1, idx1), (ROUTE_IDX2, idx2), (ROUTE_GATE1, 1.0 / denom),
                      (ROUTE_GATE2, e2 / denom), (ROUTE_RANK1, rank1), (ROUTE_RANK2, rank2)):
        route = jnp.where(lane == slot, val, route)
    route_ref[...] = route


def moe_router(h, gain, w_router_padded, n_experts, *, tm):
    m, d = h.shape
    row = jnp.arange(tm, dtype=I32)
    lower = (row[None, :] < row[:, None]).astype(BF16)
    kernel = functools.partial(_router_kernel, n_experts=n_experts)
    return pl.pallas_call(
        kernel,
        grid=(m // tm,),
        in_specs=[pl.BlockSpec((tm, d), lambda i: (i, 0)),
                  pl.BlockSpec((1, d), lambda i: (0, 0)),
                  pl.BlockSpec((d, LANES), lambda i: (0, 0)),
                  pl.BlockSpec((tm, tm), lambda i: (0, 0))],
        out_specs=[pl.BlockSpec((tm, d), lambda i: (i, 0)),
                   pl.BlockSpec((tm, LANES), lambda i: (i, 0)),
                   pl.BlockSpec((1, LANES), lambda i: (0, 0))],
        out_shape=[jax.ShapeDtypeStruct((m, d), F32), jax.ShapeDtypeStruct((m, LANES), F32),
                   jax.ShapeDtypeStruct((1, LANES), F32)],
        scratch_shapes=[pltpu.VMEM((1, LANES), F32)],
        compiler_params=_params(("arbitrary",)),
        name="moe_router",
    )(h, gain.reshape(1, d).astype(F32), w_router_padded, lower)


def _scatter_rows_kernel(d1_ref, d2_ref, u_ref, init_ref, xs_ref, sem):
    del init_ref
    tm = u_ref.shape[0]

    def row_copy(j, dest):
        return pltpu.make_async_copy(u_ref.at[pl.ds(j, 1)], xs_ref.at[pl.ds(dest, 1)], sem)

    def issue(j, carry):
        row_copy(j, d1_ref[0, 0, j]).start()
        row_copy(j, d2_ref[0, 0, j]).start()
        return carry

    lax.fori_loop(0, tm, issue, 0)

    def drain(j, carry):
        row_copy(j, 0).wait()
        row_copy(j, 0).wait()
        return carry

    lax.fori_loop(0, tm, drain, 0)


def scatter_rows(u, dest1, dest2, n_rows, *, tm):
    m, d = u.shape
    smem_idx = lambda a: a.reshape(m // tm, 1, tm)
    idx_spec = pl.BlockSpec((1, 1, tm), lambda i: (i, 0, 0), memory_space=pltpu.SMEM)
    return pl.pallas_call(
        _scatter_rows_kernel,
        grid=(m // tm,),
        in_specs=[idx_spec, idx_spec,
                  pl.BlockSpec((tm, d), lambda i: (i, 0)),
                  pl.BlockSpec(memory_space=pl.ANY)],
        out_specs=pl.BlockSpec(memory_space=pl.ANY),
        out_shape=jax.ShapeDtypeStruct((n_rows, d), u.dtype),
        scratch_shapes=[pltpu.SemaphoreType.DMA],
        input_output_aliases={3: 0},
        compiler_params=_params(("arbitrary",)),
        name="moe_scatter_rows",
    )(smem_idx(dest1), smem_idx(dest2), u, jnp.zeros((n_rows, d), u.dtype))


def _combine_kernel(d1_ref, d2_ref, rows_ref, route_ref, h_ref, o_ref, buf_ref, sem):
    tm = h_ref.shape[0]

    def row_copy(slot, j, src):
        return pltpu.make_async_copy(rows_ref.at[pl.ds(src, 1)], buf_ref.at[slot, pl.ds(j, 1)], sem)

    def issue(j, carry):
        row_copy(0, j, d1_ref[0, 0, j]).start()
        row_copy(1, j, d2_ref[0, 0, j]).start()
        return carry

    lax.fori_loop(0, tm, issue, 0)

    def drain(j, carry):
        row_copy(0, j, 0).wait()
        row_copy(1, j, 0).wait()
        return carry

    lax.fori_loop(0, tm, drain, 0)
    route = route_ref[...]
    g1 = route[:, ROUTE_GATE1:ROUTE_GATE1 + 1]
    g2 = route[:, ROUTE_GATE2:ROUTE_GATE2 + 1]
    o_ref[...] = h_ref[...] + (g1 * buf_ref[0] + g2 * buf_ref[1])


def combine_rows(rows, dest1, dest2, route, h, *, tm):
    m, d = h.shape
    smem_idx = lambda a: a.reshape(m // tm, 1, tm)
    idx_spec = pl.BlockSpec((1, 1, tm), lambda i: (i, 0, 0), memory_space=pltpu.SMEM)
    return pl.pallas_call(
        _combine_kernel,
        grid=(m // tm,),
        in_specs=[idx_spec, idx_spec,
                  pl.BlockSpec(memory_space=pl.ANY),
                  pl.BlockSpec((tm, LANES), lambda i: (i, 0)),
                  pl.BlockSpec((tm, d), lambda i: (i, 0))],
        out_specs=pl.BlockSpec((tm, d), lambda i: (i, 0)),
        out_shape=jax.ShapeDtypeStruct((m, d), F32),
        scratch_shapes=[pltpu.VMEM((2, tm, d), F32), pltpu.SemaphoreType.DMA],
        compiler_params=_params(("arbitrary",)),
        name="moe_combine_rows",
    )(smem_idx(dest1), smem_idx(dest2), rows, route, h)


def _expert_up_kernel(blk_expert_ref, n_used_ref, xs_ref, wg_ref, wu_ref, o_ref, stage_ref):
    del blk_expert_ref
    i = pl.program_id(0)

    @pl.when(pl.program_id(1) == 0)
    def _stage():
        stage_ref[...] = xs_ref[...].astype(BF16)

    @pl.when(i < n_used_ref[0])
    def _compute():
        u = stage_ref[...]
        a = jnp.dot(u, wg_ref[0], preferred_element_type=F32)
        b = jnp.dot(u, wu_ref[0], preferred_element_type=F32)
        o_ref[...] = (_silu(a) * b).astype(o_ref.dtype)

    @pl.when(i >= n_used_ref[0])
    def _skip():
        o_ref[...] = jnp.zeros(o_ref.shape, o_ref.dtype)


def expert_up(xs, w_gate, w_up, blk_expert, n_used, *, blk, tn):
    n_rows, d = xs.shape
    f = w_gate.shape[2]
    return pl.pallas_call(
        _expert_up_kernel,
        grid_spec=pltpu.PrefetchScalarGridSpec(
            num_scalar_prefetch=2,
            grid=(n_rows // blk, f // tn),
            in_specs=[pl.BlockSpec((blk, d), lambda i, j, be, nu: (i, 0)),
                      pl.BlockSpec((1, d, tn), lambda i, j, be, nu: (be[i], 0, j)),
                      pl.BlockSpec((1, d, tn), lambda i, j, be, nu: (be[i], 0, j))],
            out_specs=pl.BlockSpec((blk, tn), lambda i, j, be, nu: (i, j)),
            scratch_shapes=[pltpu.VMEM((blk, d), BF16)]),
        out_shape=jax.ShapeDtypeStruct((n_rows, f), BF16),
        compiler_params=_params(("parallel", "arbitrary")),
        name="expert_up",
    )(blk_expert, n_used, xs, w_gate, w_up)


def _expert_down_kernel(blk_expert_ref, n_used_ref, hb_ref, wd_ref, o_ref):
    del blk_expert_ref
    i = pl.program_id(0)

    @pl.when(i < n_used_ref[0])
    def _compute():
        o_ref[...] = jnp.dot(hb_ref[...], wd_ref[0], preferred_element_type=F32)

    @pl.when(i >= n_used_ref[0])
    def _skip():
        o_ref[...] = jnp.zeros(o_ref.shape, o_ref.dtype)


def expert_down(hb, w_down, blk_expert, n_used, *, blk, tn):
    n_rows, f = hb.shape
    d = w_down.shape[2]
    return pl.pallas_call(
        _expert_down_kernel,
        grid_spec=pltpu.PrefetchScalarGridSpec(
            num_scalar_prefetch=2,
            grid=(n_rows // blk, d // tn),
            in_specs=[pl.BlockSpec((blk, f), lambda i, j, be, nu: (i, 0)),
                      pl.BlockSpec((1, f, tn), lambda i, j, be, nu: (be[i], 0, j))],
            out_specs=pl.BlockSpec((blk, tn), lambda i, j, be, nu: (i, j))),
        out_shape=jax.ShapeDtypeStruct((n_rows, d), F32),
        compiler_params=_params(("parallel", "parallel")),
        name="expert_down",
    )(blk_expert, n_used, hb, w_down)


def moe_swiglu(h, gain, w_router, w_gate, w_up, w_down):
    m, d = h.shape
    n_experts = w_router.shape[1]
    blk = MOE_ROW_BLOCK if m >= 8 * MOE_ROW_BLOCK else LANES
    tm = _pick_tile(m, 256)
    w_router_padded = jnp.concatenate([w_router, jnp.zeros((d, LANES - n_experts), BF16)], axis=1)
    u, route, counts = moe_router(h, gain, w_router_padded, n_experts, tm=_pick_tile(m, 512))

    counts = counts[0, :n_experts].astype(I32)
    padded = (counts + blk - 1) // blk * blk
    pad_end = jnp.cumsum(padded)
    pad_start = pad_end - padded
    idx1 = route[:, ROUTE_IDX1].astype(I32)
    idx2 = route[:, ROUTE_IDX2].astype(I32)
    dest1 = pad_start[idx1] + route[:, ROUTE_RANK1].astype(I32)
    dest2 = pad_start[idx2] + route[:, ROUTE_RANK2].astype(I32)
    n_blocks = -(-(TOP_K_EXPERTS * m + n_experts * (blk - 1)) // blk)
    blk_start = jnp.arange(n_blocks, dtype=I32) * blk
    blk_expert = jnp.minimum(jnp.searchsorted(pad_end, blk_start, side="right"),
                             n_experts - 1).astype(I32)
    n_used = (pad_end[-1:] // blk).astype(I32)

    xs = scatter_rows(u, dest1, dest2, n_blocks * blk, tm=tm)
    hb = expert_up(xs, w_gate, w_up, blk_expert, n_used, blk=blk, tn=_pick_tile(w_gate.shape[2], 512))
    rows = expert_down(hb, w_down, blk_expert, n_used, blk=blk, tn=_pick_tile(d, 512))
    return combine_rows(rows, dest1, dest2, route, h, tm=tm)


def _split_w_in(w_in_layer, d_model):
    attn_dim = N_HEADS * HEAD_DIM
    kv_dim = N_KV_HEADS * HEAD_DIM
    sizes = (attn_dim, kv_dim, kv_dim, IDX_HEADS * IDX_DIM, IDX_DIM, IDX_HEADS, d_model, d_model,
             d_model, d_model)
    parts, off = [], 0
    for s in sizes:
        parts.append(w_in_layer[:, off:off + s])
        off += s
    assert off == w_in_layer.shape[1]
    return parts


def _layer_stack(x, p, weights, cache, *, past):
    bsz, length, d_model = x.shape
    m = bsz * length
    kv_dim = N_KV_HEADS * HEAD_DIM
    pos = past + jnp.arange(length, dtype=I32)
    row_pos = jnp.tile(pos, bsz)
    depth = weights["w_in"].shape[0]

    cos_q, sin_q = _rope_tables(row_pos, 512)
    cos_k, sin_k = _rope_tables(row_pos, kv_dim)
    cos_i, sin_i = _rope_tables(row_pos, LANES, rotary_width=IDX_DIM)
    ones_q = _block_ones(512)
    ones_k = _block_ones(kv_dim)

    n_keys = past + length
    n_sel = min(TOPK_MAX, n_keys // 4)
    if past == 0:
        tq_sel, tk, tq_attn = min(256, length), min(512, length), min(256, length)
        n_keys_pad = n_keys
    else:
        tq_sel = tq_attn = length
        tk = 384
        n_keys_pad = -(-n_keys // tk) * tk

    h = x.reshape(m, d_model)
    ks, vs, kis, convs = [], [], [], []
    for i in range(depth):
        wq, wk, wv, wqi, wki, wwi, wglu_a, wglu_g, wga, wgb = _split_w_in(weights["w_in"][i], d_model)
        w_kiwi = jnp.concatenate(
            [wki, wwi, jnp.zeros((d_model, LANES - IDX_DIM - IDX_HEADS), BF16)], axis=1)
        g_mix = weights["g_mix"][i]

        qn_gain = jnp.tile(weights["q_norm_g"][i].astype(F32), 512 // HEAD_DIM).reshape(1, 512)
        kn_gain = jnp.tile(weights["k_norm_g"][i].astype(F32), kv_dim // HEAD_DIM).reshape(1, kv_dim)

        def norm_rope(accs, tiled, rows, consts):
            y = _head_rms_norm(accs[0], consts[0], consts[1])
            y = _rope(y, rows[0], rows[1])
            return (y, y)

        u = rms_norm_rows(h, g_mix)
        (q_bf,) = fused_mm("proj_q", [u], [wq], [0],
                           lambda a, t, r, c: (_rope(_head_rms_norm(a[0], c[0], c[1]), r[0], r[1]),),
                           [BF16], rows=[cos_q, sin_q], consts=[ones_q, qn_gain])
        k_f32, k_bf = fused_mm("proj_k", [u], [wk], [0], norm_rope, [F32, BF16],
                               rows=[cos_k, sin_k], consts=[ones_k, kn_gain])
        v_f32, v_bf = fused_mm("proj_v", [u], [wv], [0], lambda a, t, r, c: (a[0], a[0]),
                               [F32, BF16])
        (qi_bf,) = fused_mm("proj_qi", [u], [wqi], [0],
                            lambda a, t, r, c: (_rope(a[0], r[0], r[1]),),
                            [BF16], rows=[cos_q, sin_q])
        (kiwi,) = fused_mm("proj_kiwi", [u], [w_kiwi], [0],
                           lambda a, t, r, c: (_rope(a[0], r[0], r[1]),),
                           [F32], rows=[cos_i, sin_i])
        (glu,) = fused_mm("proj_glu", [u], [wglu_a, wglu_g], [0, 0],
                          lambda a, t, r, c: (a[0] * _sigmoid(a[1]),), [F32])
        gate_a, gate_b = fused_mm("proj_gates", [u], [wga, wgb], [0, 0],
                                  lambda a, t, r, c: (_sigmoid(a[0]), _sigmoid(a[1])),
                                  [F32, F32])

        ki_f32 = kiwi[:, :IDX_DIM]
        wi = kiwi[:, IDX_DIM:IDX_DIM + IDX_HEADS]

        k3 = k_bf.reshape(bsz, length, kv_dim)
        v3 = v_bf.reshape(bsz, length, kv_dim)
        ki3 = ki_f32.astype(BF16).reshape(bsz, length, IDX_DIM)
        if past:
            pad = n_keys_pad - n_keys
            k3 = jnp.concatenate([cache["k"][i].reshape(bsz, past, kv_dim).astype(BF16), k3,
                                  jnp.zeros((bsz, pad, kv_dim), BF16)], axis=1)
            v3 = jnp.concatenate([cache["v"][i].reshape(bsz, past, kv_dim).astype(BF16), v3,
                                  jnp.zeros((bsz, pad, kv_dim), BF16)], axis=1)
            ki3 = jnp.concatenate([cache["kidx"][i].astype(BF16), ki3,
                                   jnp.zeros((bsz, pad, IDX_DIM), BF16)], axis=1)
        bias = select_bias(qi_bf.reshape(bsz, length, -1), ki3, wi.reshape(bsz, length, IDX_HEADS),
                           tq=tq_sel, tk=tk, causal=(past == 0), q_pos0=past, n_valid=n_keys,
                           n_sel=n_sel)
        attn = masked_attention(q_bf.reshape(bsz, length, -1), k3, v3, bias, tq=tq_attn,
                                causal=(past == 0))
        attn = attn.reshape(m, -1)

        glu3 = glu.reshape(bsz, length, d_model)
        if past:
            ctx = cache["conv"][i]
        else:
            ctx = jnp.zeros((bsz, CONV_WIDTH - 1, d_model), F32)
        lead = jnp.zeros((bsz, CONV_HALO - (CONV_WIDTH - 1), d_model), F32)
        ctx32 = jnp.concatenate([lead, ctx], axis=1)
        l_pad = -(-length // CONV_HALO) * CONV_HALO
        if l_pad == length:
            y_conv = glu3
        else:
            y_conv = jnp.concatenate(
                [glu3, jnp.zeros((bsz, l_pad - length, d_model), F32)], axis=1)
        if length >= CONV_WIDTH - 1:
            conv_state = glu3[:, length - (CONV_WIDTH - 1):]
        else:
            conv_state = jnp.concatenate([ctx, glu3], axis=1)[:, -(CONV_WIDTH - 1):]
        z = conv_ln_silu(y_conv, ctx32, weights["w_dw"][i], weights["b_dw"][i],
                         weights["conv_ln_g"][i], weights["conv_ln_b"][i], tm=min(256, l_pad))
        z = z[:, :length].reshape(m, d_model)

        (merged,) = fused_mm("mix", [attn, z], [weights["w_o_attn"][i], weights["w_o_conv"][i]],
                             [0, 1], lambda a, t, r, c: (t[0] * a[0] + t[1] * a[1],), [BF16],
                             tiled=[gate_a, gate_b])
        (h,) = fused_mm("out_proj", [merged], [weights["w_out"][i]], [0],
                        lambda a, t, r, c: (t[0] + a[0],), [F32], tiled=[h])

        j = i // 2
        if i % 2 == 0:
            (hb,) = fused_mm("ffn_up", [h], [weights["ffn_w_gate"][j], weights["ffn_w_up"][j]],
                             [0, 0], lambda a, t, r, c: (_silu(a[0]) * a[1],), [BF16],
                             gains=[weights["g_ffn"][i]], tn=1408)
            (h,) = fused_mm("ffn_down", [hb], [weights["ffn_w_down"][j]], [0],
                            lambda a, t, r, c: (t[0] + a[0],), [F32], tiled=[h])
        else:
            h = moe_swiglu(h, weights["g_ffn"][i], weights["moe_router"][j],
                           weights["moe_w_gate"][j], weights["moe_w_up"][j],
                           weights["moe_w_down"][j])

        (h,) = fused_mm("ple", [h, p[i].reshape(m, -1)],
                        [weights["w_ple_gate"][i], weights["w_ple_proj"][i]],
                        [0, 1], lambda a, t, r, c: (t[0] + _sigmoid(a[0]) * a[1],), [F32],
                        gains=[weights["g_ple"][i], None], tiled=[h])

        ks.append(k_f32.reshape(bsz, length, N_KV_HEADS, HEAD_DIM))
        vs.append(v_f32.reshape(bsz, length, N_KV_HEADS, HEAD_DIM))
        kis.append(ki_f32.reshape(bsz, length, IDX_DIM))
        convs.append(conv_state)
    return (h.reshape(bsz, length, d_model), jnp.stack(ks), jnp.stack(vs), jnp.stack(kis),
            jnp.stack(convs))


def kernel(x_prompt, x_sample, cache_k, cache_v, cache_kidx, state_conv, p_prompt, p_sample, w_in, q_norm_g, k_norm_g, w_dw, b_dw, conv_ln_g, conv_ln_b, w_o_attn, w_o_conv, w_out, g_mix, g_ffn, ffn_w_gate, ffn_w_up, ffn_w_down, moe_router, moe_w_gate, moe_w_up, moe_w_down, g_ple, w_ple_gate, w_ple_proj):
    bf = lambda a: a.astype(BF16)
    weights = dict(
        w_in=bf(w_in), q_norm_g=q_norm_g, k_norm_g=k_norm_g, w_dw=w_dw, b_dw=b_dw,
        conv_ln_g=conv_ln_g, conv_ln_b=conv_ln_b, w_o_attn=bf(w_o_attn), w_o_conv=bf(w_o_conv),
        w_out=bf(w_out), g_mix=g_mix, g_ffn=g_ffn, ffn_w_gate=bf(ffn_w_gate),
        ffn_w_up=bf(ffn_w_up), ffn_w_down=bf(ffn_w_down), moe_router=bf(moe_router),
        moe_w_gate=bf(moe_w_gate), moe_w_up=bf(moe_w_up), moe_w_down=bf(moe_w_down), g_ple=g_ple,
        w_ple_gate=bf(w_ple_gate), w_ple_proj=bf(w_ple_proj))
    cache = dict(k=cache_k, v=cache_v, kidx=cache_kidx, conv=state_conv)
    past_len = cache_k.shape[2]
    y_p, k_p, v_p, ki_p, conv_p = _layer_stack(x_prompt, p_prompt, weights, None, past=0)
    y_s, k_s, v_s, ki_s, conv_s = _layer_stack(x_sample, p_sample, weights, cache, past=past_len)
    return (y_p, y_s, k_p, v_p, ki_p, conv_p, k_s, v_s, ki_s, conv_s)
```

```python
import functools

import jax
import jax.numpy as jnp
from jax import lax
from jax.experimental import pallas as pl
from jax.experimental.pallas import tpu as pltpu

F32 = jnp.float32
BF16 = jnp.bfloat16
I32 = jnp.int32

CHUNK = 64
CHUNK_SHIFT = CHUNK.bit_length() - 1
assert 1 << CHUNK_SHIFT == CHUNK
N_HEADS = 16
HEAD_DIM = 64
N_KV_HEADS = 4
Q_PER_KV = N_HEADS // N_KV_HEADS
IDX_HEADS = 8
IDX_DIM = 64
TOPK_MAX = 256
ROPE_THETA = 10000.0
CONV_WIDTH = 31
TOP_K_EXPERTS = 2
EPS = 1e-6

LANES = 128
CONV_HALO = 32
MASKED_BIAS = -1e30
LOG2_E = 1.4426950408889634
INT_MIN = -2 ** 31
KEY_NEG_INF = -2139095041
SEARCH_ROWS = 128
EMIT_ROWS = 128
MOE_ROW_BLOCK = 512
VMEM_LIMIT = 52 * 1024 * 1024


def _params(semantics):
    return pltpu.CompilerParams(dimension_semantics=semantics, vmem_limit_bytes=VMEM_LIMIT)


def _pick_tile(n, pref):
    if n <= pref:
        return n
    best = None
    for t in range(LANES, pref + 1, LANES):
        if n % t == 0:
            best = t
    assert best is not None, (n, pref)
    return best


def _rms_norm_kernel(x_ref, g_ref, o_ref):
    x = x_ref[...]
    y = x * lax.rsqrt(jnp.mean(x * x, axis=-1, keepdims=True) + EPS) * g_ref[...]
    o_ref[...] = y.astype(o_ref.dtype)


def rms_norm_rows(x, gain, *, tm=512):
    m, d = x.shape
    tm = _pick_tile(m, tm) if m % LANES == 0 else m
    return pl.pallas_call(
        _rms_norm_kernel,
        grid=(m // tm,),
        in_specs=[pl.BlockSpec((tm, d), lambda i: (i, 0)), pl.BlockSpec((1, d), lambda i: (0, 0))],
        out_specs=pl.BlockSpec((tm, d), lambda i: (i, 0)),
        out_shape=jax.ShapeDtypeStruct((m, d), BF16),
        compiler_params=_params(("parallel",)),
        name="rms_norm_rows",
    )(x, gain.reshape(1, d).astype(F32))


def _fused_mm_kernel(*refs, n_lhs, has_gain, staged, pair_lhs, n_tiled, n_rows, n_consts, n_out,
                     epilogue):
    it = iter(refs)
    lhs_refs = [next(it) for _ in range(n_lhs)]
    gain_refs = [next(it) if has_gain[a] else None for a in range(n_lhs)]
    rhs_refs = [next(it) for _ in range(len(pair_lhs))]
    tiled_refs = [next(it) for _ in range(n_tiled)]
    row_refs = [next(it) for _ in range(n_rows)]
    const_refs = [next(it) for _ in range(n_consts)]
    out_refs = [next(it) for _ in range(n_out)]
    operand_refs = [next(it) if staged[a] else lhs_refs[a] for a in range(n_lhs)]

    if any(staged):
        @pl.when(pl.program_id(1) == 0)
        def _stage():
            for a in range(n_lhs):
                if not staged[a]:
                    continue
                x = lhs_refs[a][...].astype(F32)
                if has_gain[a]:
                    x = x * lax.rsqrt(jnp.mean(x * x, axis=-1, keepdims=True) + EPS)
                    x = x * gain_refs[a][...]
                operand_refs[a][...] = x.astype(BF16)

    accs = [jnp.dot(operand_refs[pair_lhs[r]][...], rhs_refs[r][...], preferred_element_type=F32)
            for r in range(len(pair_lhs))]
    outs = epilogue(accs, [t[...] for t in tiled_refs], [t[...] for t in row_refs],
                    [t[...] for t in const_refs])
    for o_ref, o in zip(out_refs, outs):
        o_ref[...] = o.astype(o_ref.dtype)


def fused_mm(name, lhs, rhs, pair_lhs, epilogue, out_dtypes, *, gains=None, tiled=(), rows=(),
             consts=(), tm=1024, tn=1024):
    m = lhs[0].shape[0]
    n = rhs[0].shape[1]
    tm = _pick_tile(m, tm) if m % LANES == 0 else m
    tn = _pick_tile(n, tn)
    n_lhs = len(lhs)
    gains = list(gains) if gains is not None else [None] * n_lhs
    has_gain = tuple(g is not None for g in gains)
    staged = tuple(has_gain[a] or lhs[a].dtype != BF16 for a in range(n_lhs))

    in_specs, args = [], []
    for x in lhs:
        in_specs.append(pl.BlockSpec((tm, x.shape[1]), lambda i, j: (i, 0)))
        args.append(x)
    for g in gains:
        if g is not None:
            g2 = g.reshape(1, -1).astype(F32)
            in_specs.append(pl.BlockSpec(g2.shape, lambda i, j: (0, 0)))
            args.append(g2)
    for w in rhs:
        in_specs.append(pl.BlockSpec((w.shape[0], tn), lambda i, j: (0, j)))
        args.append(w)
    for t in tiled:
        in_specs.append(pl.BlockSpec((tm, tn), lambda i, j: (i, j)))
        args.append(t)
    for t in rows:
        in_specs.append(pl.BlockSpec((tm, t.shape[1]), lambda i, j: (i, 0)))
        args.append(t)
    for c in consts:
        in_specs.append(pl.BlockSpec(c.shape, lambda i, j: (0, 0)))
        args.append(c)

    kernel = functools.partial(
        _fused_mm_kernel, n_lhs=n_lhs, has_gain=has_gain, staged=staged, pair_lhs=tuple(pair_lhs),
        n_tiled=len(tiled), n_rows=len(rows), n_consts=len(consts), n_out=len(out_dtypes),
        epilogue=epilogue)
    outs = pl.pallas_call(
        kernel,
        grid=(m // tm, n // tn),
        in_specs=in_specs,
        out_specs=[pl.BlockSpec((tm, tn), lambda i, j: (i, j)) for _ in out_dtypes],
        out_shape=[jax.ShapeDtypeStruct((m, n), dt) for dt in out_dtypes],
        scratch_shapes=[pltpu.VMEM((tm, x.shape[1]), BF16) for a, x in enumerate(lhs) if staged[a]],
        compiler_params=_params(("parallel", "arbitrary")),
        name=name,
    )(*args)
    return outs


def _head_rms_norm(x, block_ones, gain):
    x2 = x * x
    hi = x2.astype(BF16)
    lo = (x2 - hi.astype(F32)).astype(BF16)
    ss = (jnp.dot(hi, block_ones, preferred_element_type=F32)
          + jnp.dot(lo, block_ones, preferred_element_type=F32))
    return x * lax.rsqrt(ss * (1.0 / HEAD_DIM) + EPS) * gain


def _rope(x, cos, sin_signed):
    n = x.shape[1]
    lane = lax.broadcasted_iota(I32, x.shape, 1)
    first_half = (lane & (HEAD_DIM // 2)) == 0
    partner = jnp.where(first_half, pltpu.roll(x, n - HEAD_DIM // 2, 1),
                        pltpu.roll(x, HEAD_DIM // 2, 1))
    return x * cos + partner * sin_signed


def _sigmoid(x):
    return 1.0 / (1.0 + jnp.exp(-x))


def _silu(x):
    return x * _sigmoid(x)


def _rope_tables(pos, width, rotary_width=None):
    half = HEAD_DIM // 2
    inv = ROPE_THETA ** (-jnp.arange(half, dtype=F32) / half)
    ang = pos.astype(F32)[:, None] * inv[None, :]
    cos = jnp.cos(ang)
    sin = jnp.sin(ang)
    cos64 = jnp.concatenate([cos, cos], axis=-1)
    sin64 = jnp.concatenate([-sin, sin], axis=-1)
    rotary_width = width if rotary_width is None else rotary_width
    reps = rotary_width // HEAD_DIM
    cos_t = jnp.tile(cos64, (1, reps))
    sin_t = jnp.tile(sin64, (1, reps))
    if rotary_width < width:
        pad = width - rotary_width
        cos_t = jnp.concatenate([cos_t, jnp.ones((pos.shape[0], pad), F32)], axis=-1)
        sin_t = jnp.concatenate([sin_t, jnp.zeros((pos.shape[0], pad), F32)], axis=-1)
    return cos_t, sin_t


def _block_ones(width):
    head = jnp.arange(width, dtype=I32) // HEAD_DIM
    return (head[:, None] == head[None, :]).astype(BF16)


def _float_key(x):
    b = lax.bitcast_convert_type(x, I32)
    return jnp.where(b >= 0, b, b ^ jnp.int32(0x7FFFFFFF))


def _select_kernel(qi_ref, ki_ref, wi_ref, tri_ref, bias_ref, key_ref, *, tq, tk, nkb, causal,
                   q_pos0, n_valid, n_sel, rows):
    qb = pl.program_id(1)
    n_chunks = tk // LANES
    n_sub = tq // rows
    if causal:
        n_act = jnp.minimum(nkb, ((qb + 1) * tq + tk - 1) // tk)
    else:
        n_act = nkb
    q_pos = q_pos0 + qb * tq + lax.broadcasted_iota(I32, (tq, tk), 0)
    q_chunk = lax.shift_right_logical(q_pos, CHUNK_SHIFT)
    w = (wi_ref[0] * (IDX_HEADS ** -0.5)) * (IDX_DIM ** -0.5)
    qi = qi_ref[0]

    def score_block(kb, carry):
        start = pl.multiple_of(kb * tk, tk)
        ki = ki_ref[0, pl.ds(start, tk), :]
        s = jnp.zeros((tq, tk), F32)
        for h in range(IDX_HEADS):
            r = lax.dot_general(qi[:, h * IDX_DIM:(h + 1) * IDX_DIM], ki,
                                (((1,), (1,)), ((), ())), preferred_element_type=F32)
            s = s + jnp.maximum(r, 0.0) * w[:, h:h + 1]
        k_pos = start + lax.broadcasted_iota(I32, (tq, tk), 1)
        admissible = (lax.shift_right_logical(k_pos, CHUNK_SHIFT) <= q_chunk) & (k_pos < n_valid)
        key = _float_key(jnp.where(admissible, s, -jnp.inf))
        for sub in range(n_sub):
            key_ref[sub, kb] = key[sub * rows:(sub + 1) * rows]
        return carry

    lax.fori_loop(0, n_act, score_block, 0)
    for sub in range(n_sub):
        key_ref[sub, n_act] = jnp.full((rows, tk), INT_MIN, I32)
    for sub in range(n_sub):
        _select_rows(key_ref.at[sub], tri_ref, bias_ref.at[0, :, sub * rows:(sub + 1) * rows, :],
                     n_act=n_act, rows=rows, tk=tk, n_sel=n_sel)

    def fill_block(kb, carry):
        bias_ref[0, kb] = jnp.full((tq, tk), MASKED_BIAS, bias_ref.dtype)
        return carry

    lax.fori_loop(n_act, nkb, fill_block, 0)


def _select_rows(key_ref, tri_ref, out_ref, *, n_act, rows, tk, n_sel):
    tq = rows
    n_chunks = tk // LANES

    ones_mat = jnp.ones((LANES, LANES), BF16)

    def count(hit_fn):
        def pair(p, acc):
            for kb in (2 * p, 2 * p + 1):
                for c in range(n_chunks):
                    acc = acc + jnp.where(hit_fn(key_ref[kb, :, c * LANES:(c + 1) * LANES]), 1, 0)
            return acc
        acc = lax.fori_loop(0, (n_act + 1) // 2, pair, jnp.zeros((tq, LANES), I32))
        return jnp.dot(acc.astype(F32).astype(BF16), ones_mat, preferred_element_type=F32)

    def bit_step(i, prefix):
        cand = prefix | jnp.left_shift(jnp.int32(1), 31 - i)
        cand_signed = cand ^ jnp.int32(INT_MIN)
        cnt = count(lambda k: k >= cand_signed)
        return jnp.where(cnt >= n_sel, cand, prefix)

    tau_rep = lax.fori_loop(0, 32, bit_step, jnp.zeros((tq, LANES), I32)) ^ jnp.int32(INT_MIN)
    n_gt = count(lambda k: k > tau_rep)
    n_eq = count(lambda k: k == tau_rep)
    need_rep = n_sel - n_gt

    er = min(rows, EMIT_ROWS)
    for g in range(rows // er):
        rs = slice(g * er, (g + 1) * er)
        tau = tau_rep[rs, :1]
        need = need_rep[rs, :1]
        all_ties_taken = jnp.max(jnp.where(need_rep[rs] >= n_eq[rs], 0.0, 1.0)) == 0.0

        def emit_all_ties(kb, carry, rs=rs, tau=tau):
            key = key_ref[kb, rs, :]
            sel = (key >= tau) & (key > KEY_NEG_INF)
            out_ref[kb, rs, :] = jnp.where(sel, 0.0, MASKED_BIAS).astype(out_ref.dtype)
            return carry

        def emit_ranked(kb, seen, rs=rs, tau=tau, need=need):
            for c in range(n_chunks):
                cs = slice(c * LANES, (c + 1) * LANES)
                key = key_ref[kb, rs, cs]
                eq = key == tau
                rank = seen + jnp.dot(jnp.where(eq, 1.0, 0.0).astype(BF16), tri_ref[...],
                                      preferred_element_type=F32)
                sel = ((key > tau) | (eq & (rank <= need))) & (key > KEY_NEG_INF)
                out_ref[kb, rs, cs] = jnp.where(sel, 0.0, MASKED_BIAS).astype(out_ref.dtype)
                seen = rank[:, LANES - 1:LANES]
            return seen

        @pl.when(all_ties_taken)
        def _fast(emit_all_ties=emit_all_ties):
            lax.fori_loop(0, n_act, emit_all_ties, 0)

        @pl.when(jnp.logical_not(all_ties_taken))
        def _ranked(emit_ranked=emit_ranked):
            lax.fori_loop(0, n_act, emit_ranked, jnp.zeros((er, 1), F32))


def select_bias(qi, ki, wi, *, tq, tk, causal, q_pos0, n_valid, n_sel):
    b, t_q, _ = qi.shape
    n_k = ki.shape[1]
    nkb = n_k // tk
    idx = jnp.arange(LANES, dtype=I32)
    tri = (idx[:, None] <= idx[None, :]).astype(BF16)
    rows = min(tq, SEARCH_ROWS)
    kernel = functools.partial(_select_kernel, tq=tq, tk=tk, nkb=nkb, causal=causal,
                               q_pos0=q_pos0, n_valid=n_valid, n_sel=n_sel, rows=rows)
    return pl.pallas_call(
        kernel,
        grid=(b, t_q // tq),
        in_specs=[
            pl.BlockSpec((1, tq, qi.shape[2]), lambda bi, qb: (bi, qb, 0)),
            pl.BlockSpec((1, n_k, ki.shape[2]), lambda bi, qb: (bi, 0, 0)),
            pl.BlockSpec((1, tq, wi.shape[2]), lambda bi, qb: (bi, qb, 0)),
            pl.BlockSpec((LANES, LANES), lambda bi, qb: (0, 0)),
        ],
        out_specs=pl.BlockSpec((1, nkb, tq, tk), lambda bi, qb: (bi, 0, qb, 0)),
        out_shape=jax.ShapeDtypeStruct((b, nkb, t_q, tk), BF16),
        scratch_shapes=[pltpu.VMEM((tq // rows, nkb + 1, rows, tk), I32)],
        compiler_params=_params(("parallel", "parallel")),
        name="select_bias",
    )(qi, ki, wi, tri)


def _attn_kernel(qb_ref, kb_ref, last_ref, q_ref, kt_ref, v_ref, bias_ref, o_ref, qs_ref, m_ref,
                 acc_ref, *, tq, tk):
    del qb_ref
    step = pl.program_id(1)
    rows = Q_PER_KV * tq

    @pl.when(kb_ref[step] == 0)
    def _init():
        m_ref[...] = jnp.full(m_ref.shape, MASKED_BIAS, F32)
        acc_ref[...] = jnp.zeros(acc_ref.shape, F32)
        for h in range(N_HEADS):
            g, r = divmod(h, Q_PER_KV)
            qh = q_ref[0, :, h * HEAD_DIM:(h + 1) * HEAD_DIM].astype(F32) * (HEAD_DIM ** -0.5)
            qs_ref[g, r * tq:(r + 1) * tq, :] = qh.astype(qs_ref.dtype)

    bias = bias_ref[0, 0].astype(F32)
    for g in range(N_KV_HEADS):
        s = jnp.dot(qs_ref[g], kt_ref[0, g], preferred_element_type=F32)
        s = (s * LOG2_E).reshape(Q_PER_KV, tq, tk) + bias[None]
        s = s.reshape(rows, tk)
        m_prev = m_ref[g]
        m_blk = jnp.max(s, axis=1, keepdims=True)
        m_new = jnp.maximum(m_prev, jnp.broadcast_to(m_blk, m_prev.shape))
        alpha = jnp.exp2(m_prev - m_new)
        p = jnp.concatenate(
            [jnp.exp2(s[:, c * LANES:(c + 1) * LANES] - m_new) for c in range(tk // LANES)],
            axis=1).astype(BF16)
        acc_ref[g] = alpha * acc_ref[g] + jnp.dot(p, v_ref[0, g], preferred_element_type=F32)
        m_ref[g] = m_new

    @pl.when(last_ref[step] == 1)
    def _finish():
        for h in range(N_HEADS):
            g, r = divmod(h, Q_PER_KV)
            a = acc_ref[g, r * tq:(r + 1) * tq, :]
            out = a[:, :HEAD_DIM] / a[:, HEAD_DIM:HEAD_DIM + 1]
            o_ref[0, :, h * HEAD_DIM:(h + 1) * HEAD_DIM] = out.astype(o_ref.dtype)


def masked_attention(q, k, v, bias, *, tq, causal):
    b, t_q, d_q = q.shape
    n_k = k.shape[1]
    nkb, tk = bias.shape[1], bias.shape[3]
    kt = jnp.transpose(k.reshape(b, n_k, N_KV_HEADS, HEAD_DIM), (0, 2, 3, 1))
    v4 = jnp.transpose(v.reshape(b, n_k, N_KV_HEADS, HEAD_DIM), (0, 2, 1, 3))
    v_ext = jnp.concatenate(
        [v4, jnp.ones((b, N_KV_HEADS, n_k, 1), BF16),
         jnp.zeros((b, N_KV_HEADS, n_k, LANES - HEAD_DIM - 1), BF16)], axis=-1)

    pairs = []
    for qb in range(t_q // tq):
        last = min(nkb - 1, ((qb + 1) * tq - 1) // tk) if causal else nkb - 1
        pairs += [(qb, kb, int(kb == last)) for kb in range(last + 1)]
    qb_ids, kb_ids, last_flags = (jnp.asarray(col, I32) for col in zip(*pairs))

    kernel = functools.partial(_attn_kernel, tq=tq, tk=tk)
    rows = Q_PER_KV * tq
    return pl.pallas_call(
        kernel,
        grid_spec=pltpu.PrefetchScalarGridSpec(
            num_scalar_prefetch=3,
            grid=(b, len(pairs)),
            in_specs=[
                pl.BlockSpec((1, tq, d_q), lambda bi, s, qbs, kbs, lf: (bi, qbs[s], 0)),
                pl.BlockSpec((1, N_KV_HEADS, HEAD_DIM, tk),
                             lambda bi, s, qbs, kbs, lf: (bi, 0, 0, kbs[s])),
                pl.BlockSpec((1, N_KV_HEADS, tk, LANES),
                             lambda bi, s, qbs, kbs, lf: (bi, 0, kbs[s], 0)),
                pl.BlockSpec((1, 1, tq, tk), lambda bi, s, qbs, kbs, lf: (bi, kbs[s], qbs[s], 0)),
            ],
            out_specs=pl.BlockSpec((1, tq, d_q), lambda bi, s, qbs, kbs, lf: (bi, qbs[s], 0)),
            scratch_shapes=[pltpu.VMEM((N_KV_HEADS, rows, HEAD_DIM), BF16),
                            pltpu.VMEM((N_KV_HEADS, rows, LANES), F32),
                            pltpu.VMEM((N_KV_HEADS, rows, LANES), F32)]),
        out_shape=jax.ShapeDtypeStruct((b, t_q, d_q), BF16),
        compiler_params=_params(("parallel", "arbitrary")),
        name="masked_attention",
    )(qb_ids, kb_ids, last_flags, q, kt, v_ext, bias)


def _conv_kernel(cur_ref, prev_ref, ctx_ref, w_ref, b_ref, g_ref, beta_ref, o_ref, buf_ref,
                 conv_ref, *, tm, cw):
    @pl.when(pl.program_id(1) == 0)
    def _first():
        buf_ref[0:CONV_HALO, :] = ctx_ref[0]

    @pl.when(pl.program_id(1) > 0)
    def _later():
        buf_ref[0:CONV_HALO, :] = prev_ref[0]

    buf_ref[CONV_HALO:CONV_HALO + tm, :] = cur_ref[0]
    lead = CONV_HALO - (CONV_WIDTH - 1)
    c_total = buf_ref.shape[1]
    rw = min(tm, 128)
    for c in range(c_total // cw):
        cs = slice(c * cw, (c + 1) * cw)
        for r in range(tm // rw):
            acc = jnp.zeros((rw, cw), F32) + b_ref[:, cs]
            for j in range(CONV_WIDTH):
                r0 = lead + j + r * rw
                acc = acc + w_ref[j:j + 1, cs] * buf_ref[r0:r0 + rw, cs]
            conv_ref[r * rw:(r + 1) * rw, cs] = acc
    x = conv_ref[...]
    mu = jnp.mean(x, axis=-1, keepdims=True)
    xc = x - mu
    y = xc * lax.rsqrt(jnp.mean(xc * xc, axis=-1, keepdims=True) + EPS)
    y = y * g_ref[...] + beta_ref[...]
    o_ref[0] = _silu(y).astype(o_ref.dtype)


def conv_ln_silu(y, ctx, w_dw, b_dw, ln_g, ln_b, *, tm):
    b, l, c = y.shape
    kernel = functools.partial(_conv_kernel, tm=tm, cw=LANES)
    vec = lambda a: a.reshape(1, c).astype(F32)
    halo_blocks = tm // CONV_HALO
    return pl.pallas_call(
        kernel,
        grid=(b, l // tm),
        in_specs=[
            pl.BlockSpec((1, tm, c), lambda bi, i: (bi, i, 0)),
            pl.BlockSpec((1, CONV_HALO, c),
                         lambda bi, i: (bi, jnp.maximum(i * halo_blocks - 1, 0), 0)),
            pl.BlockSpec((1, CONV_HALO, c), lambda bi, i: (bi, 0, 0)),
            pl.BlockSpec((CONV_WIDTH, c), lambda bi, i: (0, 0)),
            pl.BlockSpec((1, c), lambda bi, i: (0, 0)),
            pl.BlockSpec((1, c), lambda bi, i: (0, 0)),
            pl.BlockSpec((1, c), lambda bi, i: (0, 0)),
        ],
        out_specs=pl.BlockSpec((1, tm, c), lambda bi, i: (bi, i, 0)),
        out_shape=jax.ShapeDtypeStruct((b, l, c), BF16),
        scratch_shapes=[pltpu.VMEM((tm + CONV_HALO, c), F32), pltpu.VMEM((tm, c), F32)],
        compiler_params=_params(("parallel", "parallel")),
        name="conv_ln_silu",
    )(y, y, ctx, w_dw.astype(F32), vec(b_dw), vec(ln_g), vec(ln_b))


ROUTE_IDX1, ROUTE_IDX2, ROUTE_GATE1, ROUTE_GATE2, ROUTE_RANK1, ROUTE_RANK2 = range(6)


def _router_kernel(h_ref, g_ref, wr_ref, lower_ref, u_ref, route_ref, counts_ref, seen_ref, *,
                   n_experts):
    @pl.when(pl.program_id(0) == 0)
    def _init():
        seen_ref[...] = jnp.zeros(seen_ref.shape, F32)

    x = h_ref[...]
    u = x * lax.rsqrt(jnp.mean(x * x, axis=-1, keepdims=True) + EPS) * g_ref[...]
    u_ref[...] = u
    logits = jnp.dot(u.astype(BF16), wr_ref[...], preferred_element_type=F32)
    lane = lax.broadcasted_iota(I32, logits.shape, 1).astype(F32)
    logits = jnp.where(lane < n_experts, logits, -jnp.inf)
    top1 = jnp.max(logits, axis=1, keepdims=True)
    idx1 = jnp.min(jnp.where(logits == top1, lane, float(LANES)), axis=1, keepdims=True)
    rest = jnp.where(lane == idx1, -jnp.inf, logits)
    top2 = jnp.max(rest, axis=1, keepdims=True)
    idx2 = jnp.min(jnp.where(rest == top2, lane, float(LANES)), axis=1, keepdims=True)
    e2 = jnp.exp(top2 - top1)
    denom = 1.0 + e2

    hot1 = jnp.where(lane == idx1, 1.0, 0.0)
    hot2 = jnp.where(lane == idx2, 1.0, 0.0)
    hot = hot1 + hot2
    before = seen_ref[...] + jnp.dot(lower_ref[...], hot.astype(BF16), preferred_element_type=F32)
    rank1 = jnp.sum(hot1 * before, axis=1, keepdims=True)
    rank2 = jnp.sum(hot2 * before, axis=1, keepdims=True)
    seen_ref[...] += jnp.sum(hot, axis=0, keepdims=True)
    counts_ref[...] = seen_ref[...]

    route = jnp.zeros(logits.shape, F32)
    for slot, val in ((ROUTE_IDX1, idx1), (ROUTE_IDX2, idx2), (ROUTE_GATE1, 1.0 / denom),
                      (ROUTE_GATE2, e2 / denom), (ROUTE_RANK1, rank1), (ROUTE_RANK2, rank2)):
        route = jnp.where(lane == slot, val, route)
    route_ref[...] = route


def moe_router(h, gain, w_router_padded, n_experts, *, tm):
    m, d = h.shape
    row = jnp.arange(tm, dtype=I32)
    lower = (row[None, :] < row[:, None]).astype(BF16)
    kernel = functools.partial(_router_kernel, n_experts=n_experts)
    return pl.pallas_call(
        kernel,
        grid=(m // tm,),
        in_specs=[pl.BlockSpec((tm, d), lambda i: (i, 0)),
                  pl.BlockSpec((1, d), lambda i: (0, 0)),
                  pl.BlockSpec((d, LANES), lambda i: (0, 0)),
                  pl.BlockSpec((tm, tm), lambda i: (0, 0))],
        out_specs=[pl.BlockSpec((tm, d), lambda i: (i, 0)),
                   pl.BlockSpec((tm, LANES), lambda i: (i, 0)),
                   pl.BlockSpec((1, LANES), lambda i: (0, 0))],
        out_shape=[jax.ShapeDtypeStruct((m, d), F32), jax.ShapeDtypeStruct((m, LANES), F32),
                   jax.ShapeDtypeStruct((1, LANES), F32)],
        scratch_shapes=[pltpu.VMEM((1, LANES), F32)],
        compiler_params=_params(("arbitrary",)),
        name="moe_router",
    )(h, gain.reshape(1, d).astype(F32), w_router_padded, lower)


def _scatter_rows_kernel(d1_ref, d2_ref, u_ref, init_ref, xs_ref, sem):
    del init_ref
    tm = u_ref.shape[0]

    def row_copy(j, dest):
        return pltpu.make_async_copy(u_ref.at[pl.ds(j, 1)], xs_ref.at[pl.ds(dest, 1)], sem)

    def issue(j, carry):
        row_copy(j, d1_ref[0, 0, j]).start(priority=0)
        row_copy(j, d2_ref[0, 0, j]).start(priority=1)
        return carry

    lax.fori_loop(0, tm, issue, 0)

    def drain(j, carry):
        row_copy(j, 0).wait()
        row_copy(j, 0).wait()
        return carry

    lax.fori_loop(0, tm, drain, 0)


def scatter_rows(u, dest1, dest2, n_rows, *, tm):
    m, d = u.shape
    smem_idx = lambda a: a.reshape(m // tm, 1, tm)
    idx_spec = pl.BlockSpec((1, 1, tm), lambda i: (i, 0, 0), memory_space=pltpu.SMEM)
    return pl.pallas_call(
        _scatter_rows_kernel,
        grid=(m // tm,),
        in_specs=[idx_spec, idx_spec,
                  pl.BlockSpec((tm, d), lambda i: (i, 0)),
                  pl.BlockSpec(memory_space=pl.ANY)],
        out_specs=pl.BlockSpec(memory_space=pl.ANY),
        out_shape=jax.ShapeDtypeStruct((n_rows, d), u.dtype),
        scratch_shapes=[pltpu.SemaphoreType.DMA],
        input_output_aliases={3: 0},
        compiler_params=_params(("arbitrary",)),
        name="moe_scatter_rows",
    )(smem_idx(dest1), smem_idx(dest2), u, jnp.zeros((n_rows, d), u.dtype))


def _combine_kernel(d1_ref, d2_ref, rows_ref, route_ref, h_ref, o_ref, buf_ref, sem):
    tm = h_ref.shape[0]

    def row_copy(slot, j, src):
        return pltpu.make_async_copy(rows_ref.at[pl.ds(src, 1)], buf_ref.at[slot, pl.ds(j, 1)], sem)

    def issue(j, carry):
        row_copy(0, j, d1_ref[0, 0, j]).start(priority=0)
        row_copy(1, j, d2_ref[0, 0, j]).start(priority=1)
        return carry

    lax.fori_loop(0, tm, issue, 0)

    def drain(j, carry):
        row_copy(0, j, 0).wait()
        row_copy(1, j, 0).wait()
        return carry

    lax.fori_loop(0, tm, drain, 0)
    route = route_ref[...]
    g1 = route[:, ROUTE_GATE1:ROUTE_GATE1 + 1]
    g2 = route[:, ROUTE_GATE2:ROUTE_GATE2 + 1]
    o_ref[...] = h_ref[...] + (g1 * buf_ref[0] + g2 * buf_ref[1])


def combine_rows(rows, dest1, dest2, route, h, *, tm):
    m, d = h.shape
    smem_idx = lambda a: a.reshape(m // tm, 1, tm)
    idx_spec = pl.BlockSpec((1, 1, tm), lambda i: (i, 0, 0), memory_space=pltpu.SMEM)
    return pl.pallas_call(
        _combine_kernel,
        grid=(m // tm,),
        in_specs=[idx_spec, idx_spec,
                  pl.BlockSpec(memory_space=pl.ANY),
                  pl.BlockSpec((tm, LANES), lambda i: (i, 0)),
                  pl.BlockSpec((tm, d), lambda i: (i, 0))],
        out_specs=pl.BlockSpec((tm, d), lambda i: (i, 0)),
        out_shape=jax.ShapeDtypeStruct((m, d), F32),
        scratch_shapes=[pltpu.VMEM((2, tm, d), F32), pltpu.SemaphoreType.DMA],
        compiler_params=_params(("arbitrary",)),
        name="moe_combine_rows",
    )(smem_idx(dest1), smem_idx(dest2), rows, route, h)


def _expert_up_kernel(blk_expert_ref, n_used_ref, xs_ref, wg_ref, wu_ref, o_ref, stage_ref):
    del blk_expert_ref
    i = pl.program_id(0)

    @pl.when(pl.program_id(1) == 0)
    def _stage():
        stage_ref[...] = xs_ref[...].astype(BF16)

    @pl.when(i < n_used_ref[0])
    def _compute():
        u = stage_ref[...]
        a = jnp.dot(u, wg_ref[0], preferred_element_type=F32)
        b = jnp.dot(u, wu_ref[0], preferred_element_type=F32)
        o_ref[...] = (_silu(a) * b).astype(o_ref.dtype)

    @pl.when(i >= n_used_ref[0])
    def _skip():
        o_ref[...] = jnp.zeros(o_ref.shape, o_ref.dtype)


def expert_up(xs, w_gate, w_up, blk_expert, n_used, *, blk, tn):
    n_rows, d = xs.shape
    f = w_gate.shape[2]
    return pl.pallas_call(
        _expert_up_kernel,
        grid_spec=pltpu.PrefetchScalarGridSpec(
            num_scalar_prefetch=2,
            grid=(n_rows // blk, f // tn),
            in_specs=[pl.BlockSpec((blk, d), lambda i, j, be, nu: (i, 0)),
                      pl.BlockSpec((1, d, tn), lambda i, j, be, nu: (be[i], 0, j)),
                      pl.BlockSpec((1, d, tn), lambda i, j, be, nu: (be[i], 0, j))],
            out_specs=pl.BlockSpec((blk, tn), lambda i, j, be, nu: (i, j)),
            scratch_shapes=[pltpu.VMEM((blk, d), BF16)]),
        out_shape=jax.ShapeDtypeStruct((n_rows, f), BF16),
        compiler_params=_params(("parallel", "arbitrary")),
        name="expert_up",
    )(blk_expert, n_used, xs, w_gate, w_up)


def _expert_down_kernel(blk_expert_ref, n_used_ref, hb_ref, wd_ref, o_ref):
    del blk_expert_ref
    i = pl.program_id(0)

    @pl.when(i < n_used_ref[0])
    def _compute():
        o_ref[...] = jnp.dot(hb_ref[...], wd_ref[0], preferred_element_type=F32)

    @pl.when(i >= n_used_ref[0])
    def _skip():
        o_ref[...] = jnp.zeros(o_ref.shape, o_ref.dtype)


def expert_down(hb, w_down, blk_expert, n_used, *, blk, tn):
    n_rows, f = hb.shape
    d = w_down.shape[2]
    return pl.pallas_call(
        _expert_down_kernel,
        grid_spec=pltpu.PrefetchScalarGridSpec(
            num_scalar_prefetch=2,
            grid=(n_rows // blk, d // tn),
            in_specs=[pl.BlockSpec((blk, f), lambda i, j, be, nu: (i, 0)),
                      pl.BlockSpec((1, f, tn), lambda i, j, be, nu: (be[i], 0, j))],
            out_specs=pl.BlockSpec((blk, tn), lambda i, j, be, nu: (i, j))),
        out_shape=jax.ShapeDtypeStruct((n_rows, d), F32),
        compiler_params=_params(("parallel", "parallel")),
        name="expert_down",
    )(blk_expert, n_used, hb, w_down)


def moe_swiglu(h, gain, w_router, w_gate, w_up, w_down):
    m, d = h.shape
    n_experts = w_router.shape[1]
    blk = MOE_ROW_BLOCK if m >= 8 * MOE_ROW_BLOCK else LANES
    tm = _pick_tile(m, 256)
    w_router_padded = jnp.concatenate([w_router, jnp.zeros((d, LANES - n_experts), BF16)], axis=1)
    u, route, counts = moe_router(h, gain, w_router_padded, n_experts, tm=_pick_tile(m, 512))

    counts = counts[0, :n_experts].astype(I32)
    padded = (counts + blk - 1) // blk * blk
    pad_end = jnp.cumsum(padded)
    pad_start = pad_end - padded
    idx1 = route[:, ROUTE_IDX1].astype(I32)
    idx2 = route[:, ROUTE_IDX2].astype(I32)
    dest1 = pad_start[idx1] + route[:, ROUTE_RANK1].astype(I32)
    dest2 = pad_start[idx2] + route[:, ROUTE_RANK2].astype(I32)
    n_blocks = -(-(TOP_K_EXPERTS * m + n_experts * (blk - 1)) // blk)
    blk_start = jnp.arange(n_blocks, dtype=I32) * blk
    blk_expert = jnp.minimum(jnp.searchsorted(pad_end, blk_start, side="right"),
                             n_experts - 1).astype(I32)
    n_used = (pad_end[-1:] // blk).astype(I32)

    xs = scatter_rows(u, dest1, dest2, n_blocks * blk, tm=tm)
    hb = expert_up(xs, w_gate, w_up, blk_expert, n_used, blk=blk, tn=_pick_tile(w_gate.shape[2], 512))
    rows = expert_down(hb, w_down, blk_expert, n_used, blk=blk, tn=_pick_tile(d, 512))
    return combine_rows(rows, dest1, dest2, route, h, tm=tm)


def _split_w_in(w_in_layer, d_model):
    attn_dim = N_HEADS * HEAD_DIM
    kv_dim = N_KV_HEADS * HEAD_DIM
    sizes = (attn_dim, kv_dim, kv_dim, IDX_HEADS * IDX_DIM, IDX_DIM, IDX_HEADS, d_model, d_model,
             d_model, d_model)
    parts, off = [], 0
    for s in sizes:
        parts.append(w_in_layer[:, off:off + s])
        off += s
    assert off == w_in_layer.shape[1]
    return parts


def _layer_stack(x, p, weights, cache, *, past):
    bsz, length, d_model = x.shape
    m = bsz * length
    kv_dim = N_KV_HEADS * HEAD_DIM
    pos = past + jnp.arange(length, dtype=I32)
    row_pos = jnp.tile(pos, bsz)
    depth = weights["w_in"].shape[0]

    cos_q, sin_q = _rope_tables(row_pos, 512)
    cos_k, sin_k = _rope_tables(row_pos, kv_dim)
    cos_i, sin_i = _rope_tables(row_pos, LANES, rotary_width=IDX_DIM)
    ones_q = _block_ones(512)
    ones_k = _block_ones(kv_dim)

    n_keys = past + length
    n_sel = min(TOPK_MAX, n_keys // 4)
    if past == 0:
        tq_sel, tk, tq_attn = min(256, length), min(512, length), min(256, length)
        n_keys_pad = n_keys
    else:
        tq_sel = tq_attn = length
        tk = 384
        n_keys_pad = -(-n_keys // tk) * tk

    h = x.reshape(m, d_model)
    ks, vs, kis, convs = [], [], [], []
    for i in range(depth):
        wq, wk, wv, wqi, wki, wwi, wglu_a, wglu_g, wga, wgb = _split_w_in(weights["w_in"][i], d_model)
        w_kiwi = jnp.concatenate(
            [wki, wwi, jnp.zeros((d_model, LANES - IDX_DIM - IDX_HEADS), BF16)], axis=1)
        g_mix = weights["g_mix"][i]

        qn_gain = jnp.tile(weights["q_norm_g"][i].astype(F32), 512 // HEAD_DIM).reshape(1, 512)
        kn_gain = jnp.tile(weights["k_norm_g"][i].astype(F32), kv_dim // HEAD_DIM).reshape(1, kv_dim)

        def norm_rope(accs, tiled, rows, consts):
            y = _head_rms_norm(accs[0], consts[0], consts[1])
            y = _rope(y, rows[0], rows[1])
            return (y, y)

        u = rms_norm_rows(h, g_mix)
        (q_bf,) = fused_mm("proj_q", [u], [wq], [0],
                           lambda a, t, r, c: (_rope(_head_rms_norm(a[0], c[0], c[1]), r[0], r[1]),),
                           [BF16], rows=[cos_q, sin_q], consts=[ones_q, qn_gain], tn=512)
        k_f32, k_bf = fused_mm("proj_k", [u], [wk], [0], norm_rope, [F32, BF16],
                               rows=[cos_k, sin_k], consts=[ones_k, kn_gain])
        v_f32, v_bf = fused_mm("proj_v", [u], [wv], [0], lambda a, t, r, c: (a[0], a[0]),
                               [F32, BF16])
        (qi_bf,) = fused_mm("proj_qi", [u], [wqi], [0],
                            lambda a, t, r, c: (_rope(a[0], r[0], r[1]),),
                            [BF16], rows=[cos_q, sin_q], tn=512)
        (kiwi,) = fused_mm("proj_kiwi", [u], [w_kiwi], [0],
                           lambda a, t, r, c: (_rope(a[0], r[0], r[1]),),
                           [F32], rows=[cos_i, sin_i])
        (glu,) = fused_mm("proj_glu", [u], [wglu_a, wglu_g], [0, 0],
                          lambda a, t, r, c: (a[0] * _sigmoid(a[1]),), [F32])
        gate_a, gate_b = fused_mm("proj_gates", [u], [wga, wgb], [0, 0],
                                  lambda a, t, r, c: (_sigmoid(a[0]), _sigmoid(a[1])),
                                  [F32, F32])

        ki_f32 = kiwi[:, :IDX_DIM]
        wi = kiwi[:, IDX_DIM:IDX_DIM + IDX_HEADS]

        k3 = k_bf.reshape(bsz, length, kv_dim)
        v3 = v_bf.reshape(bsz, length, kv_dim)
        ki3 = ki_f32.astype(BF16).reshape(bsz, length, IDX_DIM)
        if past:
            pad = n_keys_pad - n_keys
            k3 = jnp.concatenate([cache["k"][i].reshape(bsz, past, kv_dim).astype(BF16), k3,
                                  jnp.zeros((bsz, pad, kv_dim), BF16)], axis=1)
            v3 = jnp.concatenate([cache["v"][i].reshape(bsz, past, kv_dim).astype(BF16), v3,
                                  jnp.zeros((bsz, pad, kv_dim), BF16)], axis=1)
            ki3 = jnp.concatenate([cache["kidx"][i].astype(BF16), ki3,
                                   jnp.zeros((bsz, pad, IDX_DIM), BF16)], axis=1)
        bias = select_bias(qi_bf.reshape(bsz, length, -1), ki3, wi.reshape(bsz, length, IDX_HEADS),
                           tq=tq_sel, tk=tk, causal=(past == 0), q_pos0=past, n_valid=n_keys,
                           n_sel=n_sel)
        attn = masked_attention(q_bf.reshape(bsz, length, -1), k3, v3, bias, tq=tq_attn,
                                causal=(past == 0))
        attn = attn.reshape(m, -1)

        glu3 = glu.reshape(bsz, length, d_model)
        if past:
            ctx = cache["conv"][i]
        else:
            ctx = jnp.zeros((bsz, CONV_WIDTH - 1, d_model), F32)
        lead = jnp.zeros((bsz, CONV_HALO - (CONV_WIDTH - 1), d_model), F32)
        ctx32 = jnp.concatenate([lead, ctx], axis=1)
        l_pad = -(-length // CONV_HALO) * CONV_HALO
        if l_pad == length:
            y_conv = glu3
        else:
            y_conv = jnp.concatenate(
                [glu3, jnp.zeros((bsz, l_pad - length, d_model), F32)], axis=1)
        if length >= CONV_WIDTH - 1:
            conv_state = glu3[:, length - (CONV_WIDTH - 1):]
        else:
            conv_state = jnp.concatenate([ctx, glu3], axis=1)[:, -(CONV_WIDTH - 1):]
        z = conv_ln_silu(y_conv, ctx32, weights["w_dw"][i], weights["b_dw"][i],
                         weights["conv_ln_g"][i], weights["conv_ln_b"][i], tm=min(256, l_pad))
        z = z[:, :length].reshape(m, d_model)

        (merged,) = fused_mm("mix", [attn, z], [weights["w_o_attn"][i], weights["w_o_conv"][i]],
                             [0, 1], lambda a, t, r, c: (t[0] * a[0] + t[1] * a[1],), [BF16],
                             tiled=[gate_a, gate_b])
        (h,) = fused_mm("out_proj", [merged], [weights["w_out"][i]], [0],
                        lambda a, t, r, c: (t[0] + a[0],), [F32], tiled=[h])

        j = i // 2
        if i % 2 == 0:
            (hb,) = fused_mm("ffn_up", [h], [weights["ffn_w_gate"][j], weights["ffn_w_up"][j]],
                             [0, 0], lambda a, t, r, c: (_silu(a[0]) * a[1],), [BF16],
                             gains=[weights["g_ffn"][i]], tm=512, tn=1408)
            (h,) = fused_mm("ffn_down", [hb], [weights["ffn_w_down"][j]], [0],
                            lambda a, t, r, c: (t[0] + a[0],), [F32], tiled=[h])
        else:
            h = moe_swiglu(h, weights["g_ffn"][i], weights["moe_router"][j],
                           weights["moe_w_gate"][j], weights["moe_w_up"][j],
                           weights["moe_w_down"][j])

        (h,) = fused_mm("ple", [h, p[i].reshape(m, -1)],
                        [weights["w_ple_gate"][i], weights["w_ple_proj"][i]],
                        [0, 1], lambda a, t, r, c: (t[0] + _sigmoid(a[0]) * a[1],), [F32],
                        gains=[weights["g_ple"][i], None], tiled=[h])

        ks.append(k_f32.reshape(bsz, length, N_KV_HEADS, HEAD_DIM))
        vs.append(v_f32.reshape(bsz, length, N_KV_HEADS, HEAD_DIM))
        kis.append(ki_f32.reshape(bsz, length, IDX_DIM))
        convs.append(conv_state)
    return (h.reshape(bsz, length, d_model), jnp.stack(ks), jnp.stack(vs), jnp.stack(kis),
            jnp.stack(convs))


def kernel(x_prompt, x_sample, cache_k, cache_v, cache_kidx, state_conv, p_prompt, p_sample, w_in, q_norm_g, k_norm_g, w_dw, b_dw, conv_ln_g, conv_ln_b, w_o_attn, w_o_conv, w_out, g_mix, g_ffn, ffn_w_gate, ffn_w_up, ffn_w_down, moe_router, moe_w_gate, moe_w_up, moe_w_down, g_ple, w_ple_gate, w_ple_proj):
    bf = lambda a: a.astype(BF16)
    weights = dict(
        w_in=bf(w_in), q_norm_g=q_norm_g, k_norm_g=k_norm_g, w_dw=w_dw, b_dw=b_dw,
        conv_ln_g=conv_ln_g, conv_ln_b=conv_ln_b, w_o_attn=bf(w_o_attn), w_o_conv=bf(w_o_conv),
        w_out=bf(w_out), g_mix=g_mix, g_ffn=g_ffn, ffn_w_gate=bf(ffn_w_gate),
        ffn_w_up=bf(ffn_w_up), ffn_w_down=bf(ffn_w_down), moe_router=bf(moe_router),
        moe_w_gate=bf(moe_w_gate), moe_w_up=bf(moe_w_up), moe_w_down=bf(moe_w_down), g_ple=g_ple,
        w_ple_gate=bf(w_ple_gate), w_ple_proj=bf(w_ple_proj))
    cache = dict(k=cache_k, v=cache_v, kidx=cache_kidx, conv=state_conv)
    past_len = cache_k.shape[2]
    y_p, k_p, v_p, ki_p, conv_p = _layer_stack(x_prompt, p_prompt, weights, None, past=0)
    y_s, k_s, v_s, ki_s, conv_s = _layer_stack(x_sample, p_sample, weights, cache, past=past_len)
    return (y_p, y_s, k_p, v_p, ki_p, conv_p, k_s, v_s, ki_s, conv_s)
```

```python
import functools

import jax
import jax.numpy as jnp
from jax import lax
from jax.experimental import pallas as pl
from jax.experimental.pallas import tpu as pltpu

F32 = jnp.float32
BF16 = jnp.bfloat16
I32 = jnp.int32

CHUNK = 64
CHUNK_SHIFT = CHUNK.bit_length() - 1
assert 1 << CHUNK_SHIFT == CHUNK
N_HEADS = 16
HEAD_DIM = 64
N_KV_HEADS = 4
Q_PER_KV = N_HEADS // N_KV_HEADS
IDX_HEADS = 8
IDX_DIM = 64
TOPK_MAX = 256
ROPE_THETA = 10000.0
CONV_WIDTH = 31
TOP_K_EXPERTS = 2
EPS = 1e-6

LANES = 128
CONV_HALO = 32
MASKED_BIAS = -1e30
LOG2_E = 1.4426950408889634
INT_MIN = -2 ** 31
KEY_NEG_INF = -2139095041
SEARCH_ROWS = 128
EMIT_ROWS = 128
ATTN_HEADS_PER_DOT = Q_PER_KV
MOE_ROW_BLOCK = 512
VMEM_LIMIT = 52 * 1024 * 1024


def _params(semantics):
    return pltpu.CompilerParams(dimension_semantics=semantics, vmem_limit_bytes=VMEM_LIMIT)


def _pick_tile(n, pref):
    if n <= pref:
        return n
    best = None
    for t in range(LANES, pref + 1, LANES):
        if n % t == 0:
            best = t
    assert best is not None, (n, pref)
    return best


def _rms_norm_kernel(x_ref, g_ref, o_ref):
    x = x_ref[...]
    y = x * lax.rsqrt(jnp.mean(x * x, axis=-1, keepdims=True) + EPS) * g_ref[...]
    o_ref[...] = y.astype(o_ref.dtype)


def rms_norm_rows(x, gain, *, tm=512):
    m, d = x.shape
    tm = _pick_tile(m, tm) if m % LANES == 0 else m
    return pl.pallas_call(
        _rms_norm_kernel,
        grid=(m // tm,),
        in_specs=[pl.BlockSpec((tm, d), lambda i: (i, 0)), pl.BlockSpec((1, d), lambda i: (0, 0))],
        out_specs=pl.BlockSpec((tm, d), lambda i: (i, 0)),
        out_shape=jax.ShapeDtypeStruct((m, d), BF16),
        compiler_params=_params(("parallel",)),
        name="rms_norm_rows",
    )(x, gain.reshape(1, d).astype(F32))


def _fused_mm_kernel(*refs, n_lhs, has_gain, staged, pair_lhs, n_tiled, n_rows, n_consts, n_out,
                     epilogue):
    it = iter(refs)
    lhs_refs = [next(it) for _ in range(n_lhs)]
    gain_refs = [next(it) if has_gain[a] else None for a in range(n_lhs)]
    rhs_refs = [next(it) for _ in range(len(pair_lhs))]
    tiled_refs = [next(it) for _ in range(n_tiled)]
    row_refs = [next(it) for _ in range(n_rows)]
    const_refs = [next(it) for _ in range(n_consts)]
    out_refs = [next(it) for _ in range(n_out)]
    operand_refs = [next(it) if staged[a] else lhs_refs[a] for a in range(n_lhs)]

    if any(staged):
        @pl.when(pl.program_id(1) == 0)
        def _stage():
            for a in range(n_lhs):
                if not staged[a]:
                    continue
                x = lhs_refs[a][...].astype(F32)
                if has_gain[a]:
                    x = x * lax.rsqrt(jnp.mean(x * x, axis=-1, keepdims=True) + EPS)
                    x = x * gain_refs[a][...]
                operand_refs[a][...] = x.astype(BF16)

    accs = [jnp.dot(operand_refs[pair_lhs[r]][...], rhs_refs[r][...], preferred_element_type=F32)
            for r in range(len(pair_lhs))]
    outs = epilogue(accs, [t[...] for t in tiled_refs], [t[...] for t in row_refs],
                    [t[...] for t in const_refs])
    for o_ref, o in zip(out_refs, outs):
        o_ref[...] = o.astype(o_ref.dtype)


def fused_mm(name, lhs, rhs, pair_lhs, epilogue, out_dtypes, *, gains=None, tiled=(), rows=(),
             consts=(), tm=1024, tn=1024):
    m = lhs[0].shape[0]
    n = rhs[0].shape[1]
    tm = _pick_tile(m, tm) if m % LANES == 0 else m
    tn = _pick_tile(n, tn)
    n_lhs = len(lhs)
    gains = list(gains) if gains is not None else [None] * n_lhs
    has_gain = tuple(g is not None for g in gains)
    staged = tuple(has_gain[a] or lhs[a].dtype != BF16 for a in range(n_lhs))

    in_specs, args = [], []
    for x in lhs:
        in_specs.append(pl.BlockSpec((tm, x.shape[1]), lambda i, j: (i, 0)))
        args.append(x)
    for g in gains:
        if g is not None:
            g2 = g.reshape(1, -1).astype(F32)
            in_specs.append(pl.BlockSpec(g2.shape, lambda i, j: (0, 0)))
            args.append(g2)
    for w in rhs:
        in_specs.append(pl.BlockSpec((w.shape[0], tn), lambda i, j: (0, j)))
        args.append(w)
    for t in tiled:
        in_specs.append(pl.BlockSpec((tm, tn), lambda i, j: (i, j)))
        args.append(t)
    for t in rows:
        in_specs.append(pl.BlockSpec((tm, t.shape[1]), lambda i, j: (i, 0)))
        args.append(t)
    for c in consts:
        in_specs.append(pl.BlockSpec(c.shape, lambda i, j: (0, 0)))
        args.append(c)

    kernel = functools.partial(
        _fused_mm_kernel, n_lhs=n_lhs, has_gain=has_gain, staged=staged, pair_lhs=tuple(pair_lhs),
        n_tiled=len(tiled), n_rows=len(rows), n_consts=len(consts), n_out=len(out_dtypes),
        epilogue=epilogue)
    outs = pl.pallas_call(
        kernel,
        grid=(m // tm, n // tn),
        in_specs=in_specs,
        out_specs=[pl.BlockSpec((tm, tn), lambda i, j: (i, j)) for _ in out_dtypes],
        out_shape=[jax.ShapeDtypeStruct((m, n), dt) for dt in out_dtypes],
        scratch_shapes=[pltpu.VMEM((tm, x.shape[1]), BF16) for a, x in enumerate(lhs) if staged[a]],
        compiler_params=_params(("parallel", "arbitrary")),
        name=name,
    )(*args)
    return outs


def _head_rms_norm(x, block_ones, gain):
    x2 = x * x
    hi = x2.astype(BF16)
    lo = (x2 - hi.astype(F32)).astype(BF16)
    ss = (jnp.dot(hi, block_ones, preferred_element_type=F32)
          + jnp.dot(lo, block_ones, preferred_element_type=F32))
    return x * lax.rsqrt(ss * (1.0 / HEAD_DIM) + EPS) * gain


def _rope(x, cos, sin_signed):
    n = x.shape[1]
    lane = lax.broadcasted_iota(I32, x.shape, 1)
    first_half = (lane & (HEAD_DIM // 2)) == 0
    partner = jnp.where(first_half, pltpu.roll(x, n - HEAD_DIM // 2, 1),
                        pltpu.roll(x, HEAD_DIM // 2, 1))
    return x * cos + partner * sin_signed


def _sigmoid(x):
    return 1.0 / (1.0 + jnp.exp(-x))


def _silu(x):
    return x * _sigmoid(x)


def _rope_tables(pos, width, rotary_width=None):
    half = HEAD_DIM // 2
    inv = ROPE_THETA ** (-jnp.arange(half, dtype=F32) / half)
    ang = pos.astype(F32)[:, None] * inv[None, :]
    cos = jnp.cos(ang)
    sin = jnp.sin(ang)
    cos64 = jnp.concatenate([cos, cos], axis=-1)
    sin64 = jnp.concatenate([-sin, sin], axis=-1)
    rotary_width = width if rotary_width is None else rotary_width
    reps = rotary_width // HEAD_DIM
    cos_t = jnp.tile(cos64, (1, reps))
    sin_t = jnp.tile(sin64, (1, reps))
    if rotary_width < width:
        pad = width - rotary_width
        cos_t = jnp.concatenate([cos_t, jnp.ones((pos.shape[0], pad), F32)], axis=-1)
        sin_t = jnp.concatenate([sin_t, jnp.zeros((pos.shape[0], pad), F32)], axis=-1)
    return cos_t, sin_t


def _block_ones(width):
    head = jnp.arange(width, dtype=I32) // HEAD_DIM
    return (head[:, None] == head[None, :]).astype(BF16)


def _float_key(x):
    b = lax.bitcast_convert_type(x, I32)
    return jnp.where(b >= 0, b, b ^ jnp.int32(0x7FFFFFFF))


def _select_kernel(qi_ref, ki_ref, wi_ref, tri_ref, bias_ref, key_ref, *, tq, tk, nkb, causal,
                   q_pos0, n_valid, n_sel, rows):
    qb = pl.program_id(1)
    n_chunks = tk // LANES
    n_sub = tq // rows
    if causal:
        n_act = jnp.minimum(nkb, ((qb + 1) * tq + tk - 1) // tk)
    else:
        n_act = nkb
    q_pos = q_pos0 + qb * tq + lax.broadcasted_iota(I32, (tq, tk), 0)
    q_chunk = lax.shift_right_logical(q_pos, CHUNK_SHIFT)
    w = (wi_ref[0] * (IDX_HEADS ** -0.5)) * (IDX_DIM ** -0.5)
    qi = qi_ref[0]

    def score_block(kb, carry):
        start = pl.multiple_of(kb * tk, tk)
        ki = ki_ref[0, pl.ds(start, tk), :]
        s = jnp.zeros((tq, tk), F32)
        for h in range(IDX_HEADS):
            r = lax.dot_general(qi[:, h * IDX_DIM:(h + 1) * IDX_DIM], ki,
                                (((1,), (1,)), ((), ())), preferred_element_type=F32)
            s = s + jnp.maximum(r, 0.0) * w[:, h:h + 1]
        k_pos = start + lax.broadcasted_iota(I32, (tq, tk), 1)
        admissible = (lax.shift_right_logical(k_pos, CHUNK_SHIFT) <= q_chunk) & (k_pos < n_valid)
        key = _float_key(jnp.where(admissible, s, -jnp.inf))
        for sub in range(n_sub):
            key_ref[sub, kb] = key[sub * rows:(sub + 1) * rows]
        return carry

    lax.fori_loop(0, n_act, score_block, 0)
    for sub in range(n_sub):
        key_ref[sub, n_act] = jnp.full((rows, tk), INT_MIN, I32)
    for sub in range(n_sub):
        _select_rows(key_ref.at[sub], tri_ref, bias_ref.at[0, :, sub * rows:(sub + 1) * rows, :],
                     n_act=n_act, rows=rows, tk=tk, n_sel=n_sel)

    def fill_block(kb, carry):
        bias_ref[0, kb] = jnp.full((tq, tk), MASKED_BIAS, bias_ref.dtype)
        return carry

    lax.fori_loop(n_act, nkb, fill_block, 0)


def _select_rows(key_ref, tri_ref, out_ref, *, n_act, rows, tk, n_sel):
    tq = rows
    n_chunks = tk // LANES

    ones_mat = jnp.ones((LANES, LANES), BF16)

    def count(hit_fn):
        def pair(p, acc):
            for kb in (2 * p, 2 * p + 1):
                for c in range(n_chunks):
                    acc = acc + jnp.where(hit_fn(key_ref[kb, :, c * LANES:(c + 1) * LANES]), 1, 0)
            return acc
        acc = lax.fori_loop(0, (n_act + 1) // 2, pair, jnp.zeros((tq, LANES), I32))
        return jnp.dot(acc.astype(F32).astype(BF16), ones_mat, preferred_element_type=F32)

    def bit_step(i, prefix):
        cand = prefix | jnp.left_shift(jnp.int32(1), 31 - i)
        cand_signed = cand ^ jnp.int32(INT_MIN)
        cnt = count(lambda k: k >= cand_signed)
        return jnp.where(cnt >= n_sel, cand, prefix)

    tau_rep = lax.fori_loop(0, 32, bit_step, jnp.zeros((tq, LANES), I32)) ^ jnp.int32(INT_MIN)
    n_gt = count(lambda k: k > tau_rep)
    n_eq = count(lambda k: k == tau_rep)
    need_rep = n_sel - n_gt

    er = min(rows, EMIT_ROWS)
    for g in range(rows // er):
        rs = slice(g * er, (g + 1) * er)
        tau = tau_rep[rs, :1]
        need = need_rep[rs, :1]
        all_ties_taken = jnp.max(jnp.where(need_rep[rs] >= n_eq[rs], 0.0, 1.0)) == 0.0

        def emit_all_ties(kb, carry, rs=rs, tau=tau):
            key = key_ref[kb, rs, :]
            sel = (key >= tau) & (key > KEY_NEG_INF)
            out_ref[kb, rs, :] = jnp.where(sel, 0.0, MASKED_BIAS).astype(out_ref.dtype)
            return carry

        def emit_ranked(kb, seen, rs=rs, tau=tau, need=need):
            for c in range(n_chunks):
                cs = slice(c * LANES, (c + 1) * LANES)
                key = key_ref[kb, rs, cs]
                eq = key == tau
                rank = seen + jnp.dot(jnp.where(eq, 1.0, 0.0).astype(BF16), tri_ref[...],
                                      preferred_element_type=F32)
                sel = ((key > tau) | (eq & (rank <= need))) & (key > KEY_NEG_INF)
                out_ref[kb, rs, cs] = jnp.where(sel, 0.0, MASKED_BIAS).astype(out_ref.dtype)
                seen = rank[:, LANES - 1:LANES]
            return seen

        @pl.when(all_ties_taken)
        def _fast(emit_all_ties=emit_all_ties):
            lax.fori_loop(0, n_act, emit_all_ties, 0)

        @pl.when(jnp.logical_not(all_ties_taken))
        def _ranked(emit_ranked=emit_ranked):
            lax.fori_loop(0, n_act, emit_ranked, jnp.zeros((er, 1), F32))


def select_bias(qi, ki, wi, *, tq, tk, causal, q_pos0, n_valid, n_sel):
    b, t_q, _ = qi.shape
    n_k = ki.shape[1]
    nkb = n_k // tk
    idx = jnp.arange(LANES, dtype=I32)
    tri = (idx[:, None] <= idx[None, :]).astype(BF16)
    rows = min(tq, SEARCH_ROWS)
    kernel = functools.partial(_select_kernel, tq=tq, tk=tk, nkb=nkb, causal=causal,
                               q_pos0=q_pos0, n_valid=n_valid, n_sel=n_sel, rows=rows)
    return pl.pallas_call(
        kernel,
        grid=(b, t_q // tq),
        in_specs=[
            pl.BlockSpec((1, tq, qi.shape[2]), lambda bi, qb: (bi, qb, 0)),
            pl.BlockSpec((1, n_k, ki.shape[2]), lambda bi, qb: (bi, 0, 0)),
            pl.BlockSpec((1, tq, wi.shape[2]), lambda bi, qb: (bi, qb, 0)),
            pl.BlockSpec((LANES, LANES), lambda bi, qb: (0, 0)),
        ],
        out_specs=pl.BlockSpec((1, nkb, tq, tk), lambda bi, qb: (bi, 0, qb, 0)),
        out_shape=jax.ShapeDtypeStruct((b, nkb, t_q, tk), BF16),
        scratch_shapes=[pltpu.VMEM((tq // rows, nkb + 1, rows, tk), I32)],
        compiler_params=_params(("parallel", "parallel")),
        name="select_bias",
    )(qi, ki, wi, tri)


def _attn_kernel(qb_ref, kb_ref, last_ref, q_ref, kt_ref, v_ref, bias_ref, o_ref, qs_ref, m_ref,
                 acc_ref, *, tq, tk):
    del qb_ref
    step = pl.program_id(1)
    rows = Q_PER_KV * tq

    @pl.when(kb_ref[step] == 0)
    def _init():
        m_ref[...] = jnp.full(m_ref.shape, MASKED_BIAS, F32)
        acc_ref[...] = jnp.zeros(acc_ref.shape, F32)
        for h in range(N_HEADS):
            g, r = divmod(h, Q_PER_KV)
            qh = q_ref[0, :, h * HEAD_DIM:(h + 1) * HEAD_DIM].astype(F32) * (HEAD_DIM ** -0.5)
            qs_ref[g, r * tq:(r + 1) * tq, :] = qh.astype(qs_ref.dtype)

    bias = bias_ref[0, 0].astype(F32)
    sub_rows = ATTN_HEADS_PER_DOT * tq
    for g in range(N_KV_HEADS):
        for sub in range(Q_PER_KV // ATTN_HEADS_PER_DOT):
            rs = slice(sub * sub_rows, (sub + 1) * sub_rows)
            s = jnp.dot(qs_ref[g, rs, :], kt_ref[0, g], preferred_element_type=F32)
            s = (s * LOG2_E).reshape(ATTN_HEADS_PER_DOT, tq, tk) + bias[None]
            s = s.reshape(sub_rows, tk)
            m_prev = m_ref[g, rs, :]
            m_blk = jnp.max(s, axis=1, keepdims=True)
            m_new = jnp.maximum(m_prev, jnp.broadcast_to(m_blk, m_prev.shape))
            alpha = jnp.exp2(m_prev - m_new)
            p = jnp.concatenate(
                [jnp.exp2(s[:, c * LANES:(c + 1) * LANES] - m_new) for c in range(tk // LANES)],
                axis=1).astype(BF16)
            acc_ref[g, rs, :] = alpha * acc_ref[g, rs, :] + jnp.dot(
                p, v_ref[0, g], preferred_element_type=F32)
            m_ref[g, rs, :] = m_new

    @pl.when(last_ref[step] == 1)
    def _finish():
        for h in range(N_HEADS):
            g, r = divmod(h, Q_PER_KV)
            a = acc_ref[g, r * tq:(r + 1) * tq, :]
            out = a[:, :HEAD_DIM] / a[:, HEAD_DIM:HEAD_DIM + 1]
            o_ref[0, :, h * HEAD_DIM:(h + 1) * HEAD_DIM] = out.astype(o_ref.dtype)


def masked_attention(q, k, v, bias, *, tq, causal):
    b, t_q, d_q = q.shape
    n_k = k.shape[1]
    nkb, tk = bias.shape[1], bias.shape[3]
    kt = jnp.transpose(k.reshape(b, n_k, N_KV_HEADS, HEAD_DIM), (0, 2, 3, 1))
    v4 = jnp.transpose(v.reshape(b, n_k, N_KV_HEADS, HEAD_DIM), (0, 2, 1, 3))
    v_ext = jnp.concatenate(
        [v4, jnp.ones((b, N_KV_HEADS, n_k, 1), BF16),
         jnp.zeros((b, N_KV_HEADS, n_k, LANES - HEAD_DIM - 1), BF16)], axis=-1)

    pairs = []
    for qb in range(t_q // tq):
        last = min(nkb - 1, ((qb + 1) * tq - 1) // tk) if causal else nkb - 1
        pairs += [(qb, kb, int(kb == last)) for kb in range(last + 1)]
    qb_ids, kb_ids, last_flags = (jnp.asarray(col, I32) for col in zip(*pairs))

    kernel = functools.partial(_attn_kernel, tq=tq, tk=tk)
    rows = Q_PER_KV * tq
    return pl.pallas_call(
        kernel,
        grid_spec=pltpu.PrefetchScalarGridSpec(
            num_scalar_prefetch=3,
            grid=(b, len(pairs)),
            in_specs=[
                pl.BlockSpec((1, tq, d_q), lambda bi, s, qbs, kbs, lf: (bi, qbs[s], 0)),
                pl.BlockSpec((1, N_KV_HEADS, HEAD_DIM, tk),
                             lambda bi, s, qbs, kbs, lf: (bi, 0, 0, kbs[s])),
                pl.BlockSpec((1, N_KV_HEADS, tk, LANES),
                             lambda bi, s, qbs, kbs, lf: (bi, 0, kbs[s], 0)),
                pl.BlockSpec((1, 1, tq, tk), lambda bi, s, qbs, kbs, lf: (bi, kbs[s], qbs[s], 0)),
            ],
            out_specs=pl.BlockSpec((1, tq, d_q), lambda bi, s, qbs, kbs, lf: (bi, qbs[s], 0)),
            scratch_shapes=[pltpu.VMEM((N_KV_HEADS, rows, HEAD_DIM), BF16),
                            pltpu.VMEM((N_KV_HEADS, rows, LANES), F32),
                            pltpu.VMEM((N_KV_HEADS, rows, LANES), F32)]),
        out_shape=jax.ShapeDtypeStruct((b, t_q, d_q), BF16),
        compiler_params=_params(("parallel", "arbitrary")),
        name="masked_attention",
    )(qb_ids, kb_ids, last_flags, q, kt, v_ext, bias)


def _conv_kernel(cur_ref, prev_ref, ctx_ref, w_ref, b_ref, g_ref, beta_ref, o_ref, buf_ref,
                 conv_ref, *, tm, cw):
    @pl.when(pl.program_id(1) == 0)
    def _first():
        buf_ref[0:CONV_HALO, :] = ctx_ref[0]

    @pl.when(pl.program_id(1) > 0)
    def _later():
        buf_ref[0:CONV_HALO, :] = prev_ref[0]

    buf_ref[CONV_HALO:CONV_HALO + tm, :] = cur_ref[0]
    lead = CONV_HALO - (CONV_WIDTH - 1)
    c_total = buf_ref.shape[1]
    rw = min(tm, 128)
    for c in range(c_total // cw):
        cs = slice(c * cw, (c + 1) * cw)
        for r in range(tm // rw):
            acc = jnp.zeros((rw, cw), F32) + b_ref[:, cs]
            for j in range(CONV_WIDTH):
                r0 = lead + j + r * rw
                acc = acc + w_ref[j:j + 1, cs] * buf_ref[r0:r0 + rw, cs]
            conv_ref[r * rw:(r + 1) * rw, cs] = acc
    x = conv_ref[...]
    mu = jnp.mean(x, axis=-1, keepdims=True)
    xc = x - mu
    y = xc * lax.rsqrt(jnp.mean(xc * xc, axis=-1, keepdims=True) + EPS)
    y = y * g_ref[...] + beta_ref[...]
    o_ref[0] = _silu(y).astype(o_ref.dtype)


def conv_ln_silu(y, ctx, w_dw, b_dw, ln_g, ln_b, *, tm):
    b, l, c = y.shape
    kernel = functools.partial(_conv_kernel, tm=tm, cw=LANES)
    vec = lambda a: a.reshape(1, c).astype(F32)
    halo_blocks = tm // CONV_HALO
    return pl.pallas_call(
        kernel,
        grid=(b, l // tm),
        in_specs=[
            pl.BlockSpec((1, tm, c), lambda bi, i: (bi, i, 0)),
            pl.BlockSpec((1, CONV_HALO, c),
                         lambda bi, i: (bi, jnp.maximum(i * halo_blocks - 1, 0), 0)),
            pl.BlockSpec((1, CONV_HALO, c), lambda bi, i: (bi, 0, 0)),
            pl.BlockSpec((CONV_WIDTH, c), lambda bi, i: (0, 0)),
            pl.BlockSpec((1, c), lambda bi, i: (0, 0)),
            pl.BlockSpec((1, c), lambda bi, i: (0, 0)),
            pl.BlockSpec((1, c), lambda bi, i: (0, 0)),
        ],
        out_specs=pl.BlockSpec((1, tm, c), lambda bi, i: (bi, i, 0)),
        out_shape=jax.ShapeDtypeStruct((b, l, c), BF16),
        scratch_shapes=[pltpu.VMEM((tm + CONV_HALO, c), F32), pltpu.VMEM((tm, c), F32)],
        compiler_params=_params(("parallel", "parallel")),
        name="conv_ln_silu",
    )(y, y, ctx, w_dw.astype(F32), vec(b_dw), vec(ln_g), vec(ln_b))


ROUTE_IDX1, ROUTE_IDX2, ROUTE_GATE1, ROUTE_GATE2, ROUTE_RANK1, ROUTE_RANK2 = range(6)


def _router_kernel(h_ref, g_ref, wr_ref, lower_ref, u_ref, route_ref, counts_ref, seen_ref, *,
                   n_experts):
    @pl.when(pl.program_id(0) == 0)
    def _init():
        seen_ref[...] = jnp.zeros(seen_ref.shape, F32)

    x = h_ref[...]
    u = x * lax.rsqrt(jnp.mean(x * x, axis=-1, keepdims=True) + EPS) * g_ref[...]
    u_ref[...] = u
    logits = jnp.dot(u.astype(BF16), wr_ref[...], preferred_element_type=F32)
    lane = lax.broadcasted_iota(I32, logits.shape, 1).astype(F32)
    logits = jnp.where(lane < n_experts, logits, -jnp.inf)
    top1 = jnp.max(logits, axis=1, keepdims=True)
    idx1 = jnp.min(jnp.where(logits == top1, lane, float(LANES)), axis=1, keepdims=True)
    rest = jnp.where(lane == idx1, -jnp.inf, logits)
    top2 = jnp.max(rest, axis=1, keepdims=True)
    idx2 = jnp.min(jnp.where(rest == top2, lane, float(LANES)), axis=1, keepdims=True)
    e2 = jnp.exp(top2 - top1)
    denom = 1.0 + e2

    hot1 = jnp.where(lane == idx1, 1.0, 0.0)
    hot2 = jnp.where(lane == idx2, 1.0, 0.0)
    hot = hot1 + hot2
    before = seen_ref[...] + jnp.dot(lower_ref[...], hot.astype(BF16), preferred_element_type=F32)
    rank1 = jnp.sum(hot1 * before, axis=1, keepdims=True)
    rank2 = jnp.sum(hot2 * before, axis=1, keepdims=True)
    seen_ref[...] += jnp.sum(hot, axis=0, keepdims=True)
    counts_ref[...] = seen_ref[...]

    route = jnp.zeros(logits.shape, F32)
    for slot, val in ((ROUTE_IDX1, idx1), (ROUTE_IDX2, idx2), (ROUTE_GATE1, 1.0 / denom),
                      (ROUTE_GATE2, e2 / denom), (ROUTE_RANK1, rank1), (ROUTE_RANK2, rank2)):
        route = jnp.where(lane == slot, val, route)
    route_ref[...] = route


def moe_router(h, gain, w_router_padded, n_experts, *, tm):
    m, d = h.shape
    row = jnp.arange(tm, dtype=I32)
    lower = (row[None, :] < row[:, None]).astype(BF16)
    kernel = functools.partial(_router_kernel, n_experts=n_experts)
    return pl.pallas_call(
        kernel,
        grid=(m // tm,),
        in_specs=[pl.BlockSpec((tm, d), lambda i: (i, 0)),
                  pl.BlockSpec((1, d), lambda i: (0, 0)),
                  pl.BlockSpec((d, LANES), lambda i: (0, 0)),
                  pl.BlockSpec((tm, tm), lambda i: (0, 0))],
        out_specs=[pl.BlockSpec((tm, d), lambda i: (i, 0)),
                   pl.BlockSpec((tm, LANES), lambda i: (i, 0)),
                   pl.BlockSpec((1, LANES), lambda i: (0, 0))],
        out_shape=[jax.ShapeDtypeStruct((m, d), F32), jax.ShapeDtypeStruct((m, LANES), F32),
                   jax.ShapeDtypeStruct((1, LANES), F32)],
        scratch_shapes=[pltpu.VMEM((1, LANES), F32)],
        compiler_params=_params(("arbitrary",)),
        name="moe_router",
    )(h, gain.reshape(1, d).astype(F32), w_router_padded, lower)


def _scatter_rows_kernel(d1_ref, d2_ref, u_ref, init_ref, xs_ref, sem):
    del init_ref
    tm = u_ref.shape[0]

    def row_copy(j, dest):
        return pltpu.make_async_copy(u_ref.at[pl.ds(j, 1)], xs_ref.at[pl.ds(dest, 1)], sem)

    def issue(j, carry):
        row_copy(j, d1_ref[0, 0, j]).start(priority=0)
        row_copy(j, d2_ref[0, 0, j]).start(priority=1)
        return carry

    lax.fori_loop(0, tm, issue, 0)

    def drain(j, carry):
        row_copy(j, 0).wait()
        row_copy(j, 0).wait()
        return carry

    lax.fori_loop(0, tm, drain, 0)


def scatter_rows(u, dest1, dest2, n_rows, *, tm):
    m, d = u.shape
    smem_idx = lambda a: a.reshape(m // tm, 1, tm)
    idx_spec = pl.BlockSpec((1, 1, tm), lambda i: (i, 0, 0), memory_space=pltpu.SMEM)
    return pl.pallas_call(
        _scatter_rows_kernel,
        grid=(m // tm,),
        in_specs=[idx_spec, idx_spec,
                  pl.BlockSpec((tm, d), lambda i: (i, 0)),
                  pl.BlockSpec(memory_space=pl.ANY)],
        out_specs=pl.BlockSpec(memory_space=pl.ANY),
        out_shape=jax.ShapeDtypeStruct((n_rows, d), u.dtype),
        scratch_shapes=[pltpu.SemaphoreType.DMA],
        input_output_aliases={3: 0},
        compiler_params=_params(("arbitrary",)),
        name="moe_scatter_rows",
    )(smem_idx(dest1), smem_idx(dest2), u, jnp.zeros((n_rows, d), u.dtype))


def _combine_kernel(d1_ref, d2_ref, rows_ref, route_ref, h_ref, o_ref, buf_ref, sem):
    tm = h_ref.shape[0]

    def row_copy(slot, j, src):
        return pltpu.make_async_copy(rows_ref.at[pl.ds(src, 1)], buf_ref.at[slot, pl.ds(j, 1)], sem)

    def issue(j, carry):
        row_copy(0, j, d1_ref[0, 0, j]).start(priority=0)
        row_copy(1, j, d2_ref[0, 0, j]).start(priority=1)
        return carry

    lax.fori_loop(0, tm, issue, 0)

    def drain(j, carry):
        row_copy(0, j, 0).wait()
        row_copy(1, j, 0).wait()
        return carry

    lax.fori_loop(0, tm, drain, 0)
    route = route_ref[...]
    g1 = route[:, ROUTE_GATE1:ROUTE_GATE1 + 1]
    g2 = route[:, ROUTE_GATE2:ROUTE_GATE2 + 1]
    o_ref[...] = h_ref[...] + (g1 * buf_ref[0] + g2 * buf_ref[1])


def combine_rows(rows, dest1, dest2, route, h, *, tm):
    m, d = h.shape
    smem_idx = lambda a: a.reshape(m // tm, 1, tm)
    idx_spec = pl.BlockSpec((1, 1, tm), lambda i: (i, 0, 0), memory_space=pltpu.SMEM)
    return pl.pallas_call(
        _combine_kernel,
        grid=(m // tm,),
        in_specs=[idx_spec, idx_spec,
                  pl.BlockSpec(memory_space=pl.ANY),
                  pl.BlockSpec((tm, LANES), lambda i: (i, 0)),
                  pl.BlockSpec((tm, d), lambda i: (i, 0))],
        out_specs=pl.BlockSpec((tm, d), lambda i: (i, 0)),
        out_shape=jax.ShapeDtypeStruct((m, d), F32),
        scratch_shapes=[pltpu.VMEM((2, tm, d), F32), pltpu.SemaphoreType.DMA],
        compiler_params=_params(("arbitrary",)),
        name="moe_combine_rows",
    )(smem_idx(dest1), smem_idx(dest2), rows, route, h)


def _expert_up_kernel(blk_expert_ref, n_used_ref, xs_ref, wg_ref, wu_ref, o_ref, stage_ref):
    del blk_expert_ref
    i = pl.program_id(0)

    @pl.when(pl.program_id(1) == 0)
    def _stage():
        stage_ref[...] = xs_ref[...].astype(BF16)

    @pl.when(i < n_used_ref[0])
    def _compute():
        u = stage_ref[...]
        a = jnp.dot(u, wg_ref[0], preferred_element_type=F32)
        b = jnp.dot(u, wu_ref[0], preferred_element_type=F32)
        o_ref[...] = (_silu(a) * b).astype(o_ref.dtype)

    @pl.when(i >= n_used_ref[0])
    def _skip():
        o_ref[...] = jnp.zeros(o_ref.shape, o_ref.dtype)


def expert_up(xs, w_gate, w_up, blk_expert, n_used, *, blk, tn):
    n_rows, d = xs.shape
    f = w_gate.shape[2]
    return pl.pallas_call(
        _expert_up_kernel,
        grid_spec=pltpu.PrefetchScalarGridSpec(
            num_scalar_prefetch=2,
            grid=(n_rows // blk, f // tn),
            in_specs=[pl.BlockSpec((blk, d), lambda i, j, be, nu: (i, 0)),
                      pl.BlockSpec((1, d, tn), lambda i, j, be, nu: (be[i], 0, j)),
                      pl.BlockSpec((1, d, tn), lambda i, j, be, nu: (be[i], 0, j))],
            out_specs=pl.BlockSpec((blk, tn), lambda i, j, be, nu: (i, j)),
            scratch_shapes=[pltpu.VMEM((blk, d), BF16)]),
        out_shape=jax.ShapeDtypeStruct((n_rows, f), BF16),
        compiler_params=_params(("parallel", "arbitrary")),
        name="expert_up",
    )(blk_expert, n_used, xs, w_gate, w_up)


def _expert_down_kernel(blk_expert_ref, n_used_ref, hb_ref, wd_ref, o_ref):
    del blk_expert_ref
    i = pl.program_id(0)

    @pl.when(i < n_used_ref[0])
    def _compute():
        o_ref[...] = jnp.dot(hb_ref[...], wd_ref[0], preferred_element_type=F32)

    @pl.when(i >= n_used_ref[0])
    def _skip():
        o_ref[...] = jnp.zeros(o_ref.shape, o_ref.dtype)


def expert_down(hb, w_down, blk_expert, n_used, *, blk, tn):
    n_rows, f = hb.shape
    d = w_down.shape[2]
    return pl.pallas_call(
        _expert_down_kernel,
        grid_spec=pltpu.PrefetchScalarGridSpec(
            num_scalar_prefetch=2,
            grid=(n_rows // blk, d // tn),
            in_specs=[pl.BlockSpec((blk, f), lambda i, j, be, nu: (i, 0)),
                      pl.BlockSpec((1, f, tn), lambda i, j, be, nu: (be[i], 0, j))],
            out_specs=pl.BlockSpec((blk, tn), lambda i, j, be, nu: (i, j))),
        out_shape=jax.ShapeDtypeStruct((n_rows, d), F32),
        compiler_params=_params(("parallel", "parallel")),
        name="expert_down",
    )(blk_expert, n_used, hb, w_down)


def moe_swiglu(h, gain, w_router, w_gate, w_up, w_down):
    m, d = h.shape
    n_experts = w_router.shape[1]
    blk = MOE_ROW_BLOCK if m >= 8 * MOE_ROW_BLOCK else LANES
    tm = _pick_tile(m, 256)
    w_router_padded = jnp.concatenate([w_router, jnp.zeros((d, LANES - n_experts), BF16)], axis=1)
    u, route, counts = moe_router(h, gain, w_router_padded, n_experts, tm=_pick_tile(m, 512))

    counts = counts[0, :n_experts].astype(I32)
    padded = (counts + blk - 1) // blk * blk
    pad_end = jnp.cumsum(padded)
    pad_start = pad_end - padded
    idx1 = route[:, ROUTE_IDX1].astype(I32)
    idx2 = route[:, ROUTE_IDX2].astype(I32)
    dest1 = pad_start[idx1] + route[:, ROUTE_RANK1].astype(I32)
    dest2 = pad_start[idx2] + route[:, ROUTE_RANK2].astype(I32)
    n_blocks = -(-(TOP_K_EXPERTS * m + n_experts * (blk - 1)) // blk)
    blk_start = jnp.arange(n_blocks, dtype=I32) * blk
    blk_expert = jnp.minimum(jnp.searchsorted(pad_end, blk_start, side="right"),
                             n_experts - 1).astype(I32)
    n_used = (pad_end[-1:] // blk).astype(I32)

    xs = scatter_rows(u, dest1, dest2, n_blocks * blk, tm=tm)
    hb = expert_up(xs, w_gate, w_up, blk_expert, n_used, blk=blk, tn=_pick_tile(w_gate.shape[2], 1792))
    rows = expert_down(hb, w_down, blk_expert, n_used, blk=blk, tn=_pick_tile(d, 1024))
    return combine_rows(rows, dest1, dest2, route, h, tm=tm)


def _split_w_in(w_in_layer, d_model):
    attn_dim = N_HEADS * HEAD_DIM
    kv_dim = N_KV_HEADS * HEAD_DIM
    sizes = (attn_dim, kv_dim, kv_dim, IDX_HEADS * IDX_DIM, IDX_DIM, IDX_HEADS, d_model, d_model,
             d_model, d_model)
    parts, off = [], 0
    for s in sizes:
        parts.append(w_in_layer[:, off:off + s])
        off += s
    assert off == w_in_layer.shape[1]
    return parts


def _layer_stack(x, p, weights, cache, *, past):
    bsz, length, d_model = x.shape
    m = bsz * length
    kv_dim = N_KV_HEADS * HEAD_DIM
    pos = past + jnp.arange(length, dtype=I32)
    row_pos = jnp.tile(pos, bsz)
    depth = weights["w_in"].shape[0]

    cos_q, sin_q = _rope_tables(row_pos, 512)
    cos_k, sin_k = _rope_tables(row_pos, kv_dim)
    cos_i, sin_i = _rope_tables(row_pos, LANES, rotary_width=IDX_DIM)
    ones_q = _block_ones(512)
    ones_k = _block_ones(kv_dim)

    n_keys = past + length
    n_sel = min(TOPK_MAX, n_keys // 4)
    if past == 0:
        tq_sel, tk, tq_attn = min(256, length), min(512, length), min(256, length)
        n_keys_pad = n_keys
    else:
        tq_sel = tq_attn = length
        tk = 384
        n_keys_pad = -(-n_keys // tk) * tk

    h = x.reshape(m, d_model)
    ks, vs, kis, convs = [], [], [], []
    for i in range(depth):
        wq, wk, wv, wqi, wki, wwi, wglu_a, wglu_g, wga, wgb = _split_w_in(weights["w_in"][i], d_model)
        w_kiwi = jnp.concatenate(
            [wki, wwi, jnp.zeros((d_model, LANES - IDX_DIM - IDX_HEADS), BF16)], axis=1)
        g_mix = weights["g_mix"][i]

        qn_gain = jnp.tile(weights["q_norm_g"][i].astype(F32), 512 // HEAD_DIM).reshape(1, 512)
        kn_gain = jnp.tile(weights["k_norm_g"][i].astype(F32), kv_dim // HEAD_DIM).reshape(1, kv_dim)

        def norm_rope(accs, tiled, rows, consts):
            y = _head_rms_norm(accs[0], consts[0], consts[1])
            y = _rope(y, rows[0], rows[1])
            return (y, y)

        u = rms_norm_rows(h, g_mix)
        (q_bf,) = fused_mm("proj_q", [u], [wq], [0],
                           lambda a, t, r, c: (_rope(_head_rms_norm(a[0], c[0], c[1]), r[0], r[1]),),
                           [BF16], rows=[cos_q, sin_q], consts=[ones_q, qn_gain], tn=512)
        k_f32, k_bf = fused_mm("proj_k", [u], [wk], [0], norm_rope, [F32, BF16],
                               rows=[cos_k, sin_k], consts=[ones_k, kn_gain])
        v_f32, v_bf = fused_mm("proj_v", [u], [wv], [0], lambda a, t, r, c: (a[0], a[0]),
                               [F32, BF16])
        (qi_bf,) = fused_mm("proj_qi", [u], [wqi], [0],
                            lambda a, t, r, c: (_rope(a[0], r[0], r[1]),),
                            [BF16], rows=[cos_q, sin_q], tn=512)
        (kiwi,) = fused_mm("proj_kiwi", [u], [w_kiwi], [0],
                           lambda a, t, r, c: (_rope(a[0], r[0], r[1]),),
                           [F32], rows=[cos_i, sin_i])
        (glu,) = fused_mm("proj_glu", [u], [wglu_a, wglu_g], [0, 0],
                          lambda a, t, r, c: (a[0] * _sigmoid(a[1]),), [F32])
        gate_a, gate_b = fused_mm("proj_gates", [u], [wga, wgb], [0, 0],
                                  lambda a, t, r, c: (_sigmoid(a[0]), _sigmoid(a[1])),
                                  [F32, F32])

        ki_f32 = kiwi[:, :IDX_DIM]
        wi = kiwi[:, IDX_DIM:IDX_DIM + IDX_HEADS]

        k3 = k_bf.reshape(bsz, length, kv_dim)
        v3 = v_bf.reshape(bsz, length, kv_dim)
        ki3 = ki_f32.astype(BF16).reshape(bsz, length, IDX_DIM)
        if past:
            pad = n_keys_pad - n_keys
            k3 = jnp.concatenate([cache["k"][i].reshape(bsz, past, kv_dim).astype(BF16), k3,
                                  jnp.zeros((bsz, pad, kv_dim), BF16)], axis=1)
            v3 = jnp.concatenate([cache["v"][i].reshape(bsz, past, kv_dim).astype(BF16), v3,
                                  jnp.zeros((bsz, pad, kv_dim), BF16)], axis=1)
            ki3 = jnp.concatenate([cache["kidx"][i].astype(BF16), ki3,
                                   jnp.zeros((bsz, pad, IDX_DIM), BF16)], axis=1)
        bias = select_bias(qi_bf.reshape(bsz, length, -1), ki3, wi.reshape(bsz, length, IDX_HEADS),
                           tq=tq_sel, tk=tk, causal=(past == 0), q_pos0=past, n_valid=n_keys,
                           n_sel=n_sel)
        attn = masked_attention(q_bf.reshape(bsz, length, -1), k3, v3, bias, tq=tq_attn,
                                causal=(past == 0))
        attn = attn.reshape(m, -1)

        glu3 = glu.reshape(bsz, length, d_model)
        if past:
            ctx = cache["conv"][i]
        else:
            ctx = jnp.zeros((bsz, CONV_WIDTH - 1, d_model), F32)
        lead = jnp.zeros((bsz, CONV_HALO - (CONV_WIDTH - 1), d_model), F32)
        ctx32 = jnp.concatenate([lead, ctx], axis=1)
        l_pad = -(-length // CONV_HALO) * CONV_HALO
        if l_pad == length:
            y_conv = glu3
        else:
            y_conv = jnp.concatenate(
                [glu3, jnp.zeros((bsz, l_pad - length, d_model), F32)], axis=1)
        if length >= CONV_WIDTH - 1:
            conv_state = glu3[:, length - (CONV_WIDTH - 1):]
        else:
            conv_state = jnp.concatenate([ctx, glu3], axis=1)[:, -(CONV_WIDTH - 1):]
        z = conv_ln_silu(y_conv, ctx32, weights["w_dw"][i], weights["b_dw"][i],
                         weights["conv_ln_g"][i], weights["conv_ln_b"][i], tm=min(256, l_pad))
        z = z[:, :length].reshape(m, d_model)

        (merged,) = fused_mm("mix", [attn, z], [weights["w_o_attn"][i], weights["w_o_conv"][i]],
                             [0, 1], lambda a, t, r, c: (t[0] * a[0] + t[1] * a[1],), [BF16],
                             tiled=[gate_a, gate_b])
        (h,) = fused_mm("out_proj", [merged], [weights["w_out"][i]], [0],
                        lambda a, t, r, c: (t[0] + a[0],), [F32], tiled=[h])

        j = i // 2
        if i % 2 == 0:
            (hb,) = fused_mm("ffn_up", [h], [weights["ffn_w_gate"][j], weights["ffn_w_up"][j]],
                             [0, 0], lambda a, t, r, c: (_silu(a[0]) * a[1],), [BF16],
                             gains=[weights["g_ffn"][i]], tm=512, tn=1408)
            (h,) = fused_mm("ffn_down", [hb], [weights["ffn_w_down"][j]], [0],
                            lambda a, t, r, c: (t[0] + a[0],), [F32], tiled=[h])
        else:
            h = moe_swiglu(h, weights["g_ffn"][i], weights["moe_router"][j],
                           weights["moe_w_gate"][j], weights["moe_w_up"][j],
                           weights["moe_w_down"][j])

        (h,) = fused_mm("ple", [h, p[i].reshape(m, -1)],
                        [weights["w_ple_gate"][i], weights["w_ple_proj"][i]],
                        [0, 1], lambda a, t, r, c: (t[0] + _sigmoid(a[0]) * a[1],), [F32],
                        gains=[weights["g_ple"][i], None], tiled=[h])

        ks.append(k_f32.reshape(bsz, length, N_KV_HEADS, HEAD_DIM))
        vs.append(v_f32.reshape(bsz, length, N_KV_HEADS, HEAD_DIM))
        kis.append(ki_f32.reshape(bsz, length, IDX_DIM))
        convs.append(conv_state)
    return (h.reshape(bsz, length, d_model), jnp.stack(ks), jnp.stack(vs), jnp.stack(kis),
            jnp.stack(convs))


def kernel(x_prompt, x_sample, cache_k, cache_v, cache_kidx, state_conv, p_prompt, p_sample, w_in, q_norm_g, k_norm_g, w_dw, b_dw, conv_ln_g, conv_ln_b, w_o_attn, w_o_conv, w_out, g_mix, g_ffn, ffn_w_gate, ffn_w_up, ffn_w_down, moe_router, moe_w_gate, moe_w_up, moe_w_down, g_ple, w_ple_gate, w_ple_proj):
    bf = lambda a: a.astype(BF16)
    weights = dict(
        w_in=bf(w_in), q_norm_g=q_norm_g, k_norm_g=k_norm_g, w_dw=w_dw, b_dw=b_dw,
        conv_ln_g=conv_ln_g, conv_ln_b=conv_ln_b, w_o_attn=bf(w_o_attn), w_o_conv=bf(w_o_conv),
        w_out=bf(w_out), g_mix=g_mix, g_ffn=g_ffn, ffn_w_gate=bf(ffn_w_gate),
        ffn_w_up=bf(ffn_w_up), ffn_w_down=bf(ffn_w_down), moe_router=bf(moe_router),
        moe_w_gate=bf(moe_w_gate), moe_w_up=bf(moe_w_up), moe_w_down=bf(moe_w_down), g_ple=g_ple,
        w_ple_gate=bf(w_ple_gate), w_ple_proj=bf(w_ple_proj))
    cache = dict(k=cache_k, v=cache_v, kidx=cache_kidx, conv=state_conv)
    past_len = cache_k.shape[2]
    y_p, k_p, v_p, ki_p, conv_p = _layer_stack(x_prompt, p_prompt, weights, None, past=0)
    y_s, k_s, v_s, ki_s, conv_s = _layer_stack(x_sample, p_sample, weights, cache, past=past_len)
    return (y_p, y_s, k_p, v_p, ki_p, conv_p, k_s, v_s, ki_s, conv_s)
```

```python
import functools

import jax
import jax.numpy as jnp
from jax import lax
from jax.experimental import pallas as pl
from jax.experimental.pallas import tpu as pltpu

F32 = jnp.float32
BF16 = jnp.bfloat16
I32 = jnp.int32

CHUNK = 64
CHUNK_SHIFT = CHUNK.bit_length() - 1
assert 1 << CHUNK_SHIFT == CHUNK
N_HEADS = 16
HEAD_DIM = 64
N_KV_HEADS = 4
Q_PER_KV = N_HEADS // N_KV_HEADS
IDX_HEADS = 8
IDX_DIM = 64
TOPK_MAX = 256
ROPE_THETA = 10000.0
CONV_WIDTH = 31
TOP_K_EXPERTS = 2
EPS = 1e-6

LANES = 128
CONV_HALO = 32
MASKED_BIAS = -1e30
LOG2_E = 1.4426950408889634
INT_MIN = -2 ** 31
KEY_NEG_INF = -2139095041
SEARCH_ROWS = 128
EMIT_ROWS = 128
TOP_BITS = 15
ATTN_HEADS_PER_DOT = Q_PER_KV
MOE_ROW_BLOCK = 512
VMEM_LIMIT = 52 * 1024 * 1024


def _params(semantics):
    return pltpu.CompilerParams(dimension_semantics=semantics, vmem_limit_bytes=VMEM_LIMIT)


def _pick_tile(n, pref):
    if n <= pref:
        return n
    best = None
    for t in range(LANES, pref + 1, LANES):
        if n % t == 0:
            best = t
    assert best is not None, (n, pref)
    return best


def _rms_norm_kernel(x_ref, g_ref, o_ref):
    x = x_ref[...]
    y = x * lax.rsqrt(jnp.mean(x * x, axis=-1, keepdims=True) + EPS) * g_ref[...]
    o_ref[...] = y.astype(o_ref.dtype)


def rms_norm_rows(x, gain, *, tm=512):
    m, d = x.shape
    tm = _pick_tile(m, tm) if m % LANES == 0 else m
    return pl.pallas_call(
        _rms_norm_kernel,
        grid=(m // tm,),
        in_specs=[pl.BlockSpec((tm, d), lambda i: (i, 0)), pl.BlockSpec((1, d), lambda i: (0, 0))],
        out_specs=pl.BlockSpec((tm, d), lambda i: (i, 0)),
        out_shape=jax.ShapeDtypeStruct((m, d), BF16),
        compiler_params=_params(("parallel",)),
        name="rms_norm_rows",
    )(x, gain.reshape(1, d).astype(F32))


def _fused_mm_kernel(*refs, n_lhs, has_gain, staged, pair_lhs, n_tiled, n_rows, n_consts, n_out,
                     epilogue):
    it = iter(refs)
    lhs_refs = [next(it) for _ in range(n_lhs)]
    gain_refs = [next(it) if has_gain[a] else None for a in range(n_lhs)]
    rhs_refs = [next(it) for _ in range(len(pair_lhs))]
    tiled_refs = [next(it) for _ in range(n_tiled)]
    row_refs = [next(it) for _ in range(n_rows)]
    const_refs = [next(it) for _ in range(n_consts)]
    out_refs = [next(it) for _ in range(n_out)]
    operand_refs = [next(it) if staged[a] else lhs_refs[a] for a in range(n_lhs)]

    if any(staged):
        @pl.when(pl.program_id(1) == 0)
        def _stage():
            for a in range(n_lhs):
                if not staged[a]:
                    continue
                x = lhs_refs[a][...].astype(F32)
                if has_gain[a]:
                    x = x * lax.rsqrt(jnp.mean(x * x, axis=-1, keepdims=True) + EPS)
                    x = x * gain_refs[a][...]
                operand_refs[a][...] = x.astype(BF16)

    accs = [jnp.dot(operand_refs[pair_lhs[r]][...], rhs_refs[r][...], preferred_element_type=F32)
            for r in range(len(pair_lhs))]
    outs = epilogue(accs, [t[...] for t in tiled_refs], [t[...] for t in row_refs],
                    [t[...] for t in const_refs])
    for o_ref, o in zip(out_refs, outs):
        o_ref[...] = o.astype(o_ref.dtype)


def fused_mm(name, lhs, rhs, pair_lhs, epilogue, out_dtypes, *, gains=None, tiled=(), rows=(),
             consts=(), tm=1024, tn=1024):
    m = lhs[0].shape[0]
    n = rhs[0].shape[1]
    tm = _pick_tile(m, tm) if m % LANES == 0 else m
    tn = _pick_tile(n, tn)
    n_lhs = len(lhs)
    gains = list(gains) if gains is not None else [None] * n_lhs
    has_gain = tuple(g is not None for g in gains)
    staged = tuple(has_gain[a] or lhs[a].dtype != BF16 for a in range(n_lhs))

    in_specs, args = [], []
    for x in lhs:
        in_specs.append(pl.BlockSpec((tm, x.shape[1]), lambda i, j: (i, 0)))
        args.append(x)
    for g in gains:
        if g is not None:
            g2 = g.reshape(1, -1).astype(F32)
            in_specs.append(pl.BlockSpec(g2.shape, lambda i, j: (0, 0)))
            args.append(g2)
    for w in rhs:
        in_specs.append(pl.BlockSpec((w.shape[0], tn), lambda i, j: (0, j)))
        args.append(w)
    for t in tiled:
        in_specs.append(pl.BlockSpec((tm, tn), lambda i, j: (i, j)))
        args.append(t)
    for t in rows:
        in_specs.append(pl.BlockSpec((tm, t.shape[1]), lambda i, j: (i, 0)))
        args.append(t)
    for c in consts:
        in_specs.append(pl.BlockSpec(c.shape, lambda i, j: (0, 0)))
        args.append(c)

    kernel = functools.partial(
        _fused_mm_kernel, n_lhs=n_lhs, has_gain=has_gain, staged=staged, pair_lhs=tuple(pair_lhs),
        n_tiled=len(tiled), n_rows=len(rows), n_consts=len(consts), n_out=len(out_dtypes),
        epilogue=epilogue)
    outs = pl.pallas_call(
        kernel,
        grid=(m // tm, n // tn),
        in_specs=in_specs,
        out_specs=[pl.BlockSpec((tm, tn), lambda i, j: (i, j)) for _ in out_dtypes],
        out_shape=[jax.ShapeDtypeStruct((m, n), dt) for dt in out_dtypes],
        scratch_shapes=[pltpu.VMEM((tm, x.shape[1]), BF16) for a, x in enumerate(lhs) if staged[a]],
        compiler_params=_params(("parallel", "arbitrary")),
        name=name,
    )(*args)
    return outs


def _head_rms_norm(x, block_ones, gain):
    x2 = x * x
    hi = x2.astype(BF16)
    lo = (x2 - hi.astype(F32)).astype(BF16)
    ss = (jnp.dot(hi, block_ones, preferred_element_type=F32)
          + jnp.dot(lo, block_ones, preferred_element_type=F32))
    return x * lax.rsqrt(ss * (1.0 / HEAD_DIM) + EPS) * gain


def _rope(x, cos, sin_signed):
    n = x.shape[1]
    lane = lax.broadcasted_iota(I32, x.shape, 1)
    first_half = (lane & (HEAD_DIM // 2)) == 0
    partner = jnp.where(first_half, pltpu.roll(x, n - HEAD_DIM // 2, 1),
                        pltpu.roll(x, HEAD_DIM // 2, 1))
    return x * cos + partner * sin_signed


def _sigmoid(x):
    return 1.0 / (1.0 + jnp.exp(-x))


def _silu(x):
    return x * _sigmoid(x)


def _rope_tables(pos, width, rotary_width=None):
    half = HEAD_DIM // 2
    inv = ROPE_THETA ** (-jnp.arange(half, dtype=F32) / half)
    ang = pos.astype(F32)[:, None] * inv[None, :]
    cos = jnp.cos(ang)
    sin = jnp.sin(ang)
    cos64 = jnp.concatenate([cos, cos], axis=-1)
    sin64 = jnp.concatenate([-sin, sin], axis=-1)
    rotary_width = width if rotary_width is None else rotary_width
    reps = rotary_width // HEAD_DIM
    cos_t = jnp.tile(cos64, (1, reps))
    sin_t = jnp.tile(sin64, (1, reps))
    if rotary_width < width:
        pad = width - rotary_width
        cos_t = jnp.concatenate([cos_t, jnp.ones((pos.shape[0], pad), F32)], axis=-1)
        sin_t = jnp.concatenate([sin_t, jnp.zeros((pos.shape[0], pad), F32)], axis=-1)
    return cos_t, sin_t


def _block_ones(width):
    head = jnp.arange(width, dtype=I32) // HEAD_DIM
    return (head[:, None] == head[None, :]).astype(BF16)


def _float_key(x):
    b = lax.bitcast_convert_type(x, I32)
    return jnp.where(b >= 0, b, b ^ jnp.int32(0x7FFFFFFF))


def _select_kernel(qi_ref, ki_ref, wi_ref, tri_ref, bias_ref, key_ref, top_ref, *, tq, tk, nkb,
                   causal, q_pos0, n_valid, n_sel, rows):
    qb = pl.program_id(1)
    n_chunks = tk // LANES
    n_sub = tq // rows
    half = rows // 2
    if causal:
        n_act = jnp.minimum(nkb, ((qb + 1) * tq + tk - 1) // tk)
    else:
        n_act = nkb
    q_pos = q_pos0 + qb * tq + lax.broadcasted_iota(I32, (tq, tk), 0)
    q_chunk = lax.shift_right_logical(q_pos, CHUNK_SHIFT)
    w = (wi_ref[0] * (IDX_HEADS ** -0.5)) * (IDX_DIM ** -0.5)
    qi = qi_ref[0]

    def score_block(kb, carry):
        start = pl.multiple_of(kb * tk, tk)
        ki = ki_ref[0, pl.ds(start, tk), :]
        s = jnp.zeros((tq, tk), F32)
        for h in range(IDX_HEADS):
            r = lax.dot_general(qi[:, h * IDX_DIM:(h + 1) * IDX_DIM], ki,
                                (((1,), (1,)), ((), ())), preferred_element_type=F32)
            s = s + jnp.maximum(r, 0.0) * w[:, h:h + 1]
        k_pos = start + lax.broadcasted_iota(I32, (tq, tk), 1)
        admissible = (lax.shift_right_logical(k_pos, CHUNK_SHIFT) <= q_chunk) & (k_pos < n_valid)
        key = _float_key(jnp.where(admissible, s, -jnp.inf))
        top = lax.shift_right_logical(key ^ jnp.int32(INT_MIN), 32 - TOP_BITS)
        for sub in range(n_sub):
            key_ref[sub, kb] = key[sub * rows:(sub + 1) * rows]
            lo_rows = top[sub * rows:sub * rows + half]
            hi_rows = top[sub * rows + half:(sub + 1) * rows]
            top_ref[sub, kb] = lo_rows | jnp.left_shift(hi_rows, 16)
        return carry

    lax.fori_loop(0, n_act, score_block, 0)
    for sub in range(n_sub):
        key_ref[sub, n_act] = jnp.full((rows, tk), INT_MIN, I32)
        top_ref[sub, n_act] = jnp.zeros((half, tk), I32)
    for sub in range(n_sub):
        _select_rows(key_ref.at[sub], top_ref.at[sub], tri_ref,
                     bias_ref.at[0, :, sub * rows:(sub + 1) * rows, :],
                     n_act=n_act, rows=rows, tk=tk, n_sel=n_sel)

    def fill_block(kb, carry):
        bias_ref[0, kb] = jnp.full((tq, tk), MASKED_BIAS, bias_ref.dtype)
        return carry

    lax.fori_loop(n_act, nkb, fill_block, 0)


def _select_rows(key_ref, top_ref, tri_ref, out_ref, *, n_act, rows, tk, n_sel):
    tq = rows
    half = rows // 2
    n_chunks = tk // LANES

    ones_mat = jnp.ones((LANES, LANES), BF16)

    def count(hit_fn):
        def pair(p, acc):
            for kb in (2 * p, 2 * p + 1):
                for c in range(n_chunks):
                    acc = acc + jnp.where(hit_fn(key_ref[kb, :, c * LANES:(c + 1) * LANES]), 1, 0)
            return acc
        acc = lax.fori_loop(0, (n_act + 1) // 2, pair, jnp.zeros((tq, LANES), I32))
        return jnp.dot(acc.astype(F32).astype(BF16), ones_mat, preferred_element_type=F32)

    n_counted = (tk * 2) * ((n_act + 1) // 2)
    guard = 1 << TOP_BITS

    def top_step(i, prefix):
        cand = prefix | jnp.left_shift(jnp.int32(1), TOP_BITS - 1 - i)
        below = (cand - 1) | guard
        cw = below[:half] | jnp.left_shift(below[half:], 16)

        def pair(p, acc):
            for kb in (2 * p, 2 * p + 1):
                for c in range(n_chunks):
                    d = cw - top_ref[kb, :, c * LANES:(c + 1) * LANES]
                    acc = acc + (lax.shift_right_logical(d, TOP_BITS) & 0x00010001)
            return acc
        acc = lax.fori_loop(0, (n_act + 1) // 2, pair, jnp.zeros((half, LANES), I32))
        le = jnp.concatenate([acc & 0xFFFF, lax.shift_right_logical(acc, 16)], axis=0)
        n_le = jnp.dot(le.astype(F32).astype(BF16), ones_mat, preferred_element_type=F32)
        return jnp.where(n_counted - n_le >= n_sel, cand, prefix)

    top_prefix = lax.fori_loop(0, TOP_BITS, top_step, jnp.zeros((tq, LANES), I32))

    def bit_step(i, prefix):
        cand = prefix | jnp.left_shift(jnp.int32(1), 31 - i)
        cand_signed = cand ^ jnp.int32(INT_MIN)
        cnt = count(lambda k: k >= cand_signed)
        return jnp.where(cnt >= n_sel, cand, prefix)

    prefix = lax.fori_loop(TOP_BITS, 32, bit_step, jnp.left_shift(top_prefix, 32 - TOP_BITS))
    tau_rep = prefix ^ jnp.int32(INT_MIN)
    n_gt = count(lambda k: k > tau_rep)
    n_eq = count(lambda k: k == tau_rep)
    need_rep = n_sel - n_gt

    er = min(rows, EMIT_ROWS)
    for g in range(rows // er):
        rs = slice(g * er, (g + 1) * er)
        tau = tau_rep[rs, :1]
        need = need_rep[rs, :1]
        all_ties_taken = jnp.max(jnp.where(need_rep[rs] >= n_eq[rs], 0.0, 1.0)) == 0.0

        def emit_all_ties(kb, carry, rs=rs, tau=tau):
            key = key_ref[kb, rs, :]
            sel = (key >= tau) & (key > KEY_NEG_INF)
            out_ref[kb, rs, :] = jnp.where(sel, 0.0, MASKED_BIAS).astype(out_ref.dtype)
            return carry

        def emit_ranked(kb, seen, rs=rs, tau=tau, need=need):
            for c in range(n_chunks):
                cs = slice(c * LANES, (c + 1) * LANES)
                key = key_ref[kb, rs, cs]
                eq = key == tau
                rank = seen + jnp.dot(jnp.where(eq, 1.0, 0.0).astype(BF16), tri_ref[...],
                                      preferred_element_type=F32)
                sel = ((key > tau) | (eq & (rank <= need))) & (key > KEY_NEG_INF)
                out_ref[kb, rs, cs] = jnp.where(sel, 0.0, MASKED_BIAS).astype(out_ref.dtype)
                seen = rank[:, LANES - 1:LANES]
            return seen

        @pl.when(all_ties_taken)
        def _fast(emit_all_ties=emit_all_ties):
            lax.fori_loop(0, n_act, emit_all_ties, 0)

        @pl.when(jnp.logical_not(all_ties_taken))
        def _ranked(emit_ranked=emit_ranked):
            lax.fori_loop(0, n_act, emit_ranked, jnp.zeros((er, 1), F32))


def select_bias(qi, ki, wi, *, tq, tk, causal, q_pos0, n_valid, n_sel):
    b, t_q, _ = qi.shape
    n_k = ki.shape[1]
    nkb = n_k // tk
    idx = jnp.arange(LANES, dtype=I32)
    tri = (idx[:, None] <= idx[None, :]).astype(BF16)
    rows = min(tq, SEARCH_ROWS)
    kernel = functools.partial(_select_kernel, tq=tq, tk=tk, nkb=nkb, causal=causal,
                               q_pos0=q_pos0, n_valid=n_valid, n_sel=n_sel, rows=rows)
    return pl.pallas_call(
        kernel,
        grid=(b, t_q // tq),
        in_specs=[
            pl.BlockSpec((1, tq, qi.shape[2]), lambda bi, qb: (bi, qb, 0)),
            pl.BlockSpec((1, n_k, ki.shape[2]), lambda bi, qb: (bi, 0, 0)),
            pl.BlockSpec((1, tq, wi.shape[2]), lambda bi, qb: (bi, qb, 0)),
            pl.BlockSpec((LANES, LANES), lambda bi, qb: (0, 0)),
        ],
        out_specs=pl.BlockSpec((1, nkb, tq, tk), lambda bi, qb: (bi, 0, qb, 0)),
        out_shape=jax.ShapeDtypeStruct((b, nkb, t_q, tk), BF16),
        scratch_shapes=[pltpu.VMEM((tq // rows, nkb + 1, rows, tk), I32),
                        pltpu.VMEM((tq // rows, nkb + 1, rows // 2, tk), I32)],
        compiler_params=_params(("parallel", "parallel")),
        name="select_bias",
    )(qi, ki, wi, tri)


def _attn_kernel(qb_ref, kb_ref, last_ref, q_ref, kt_ref, v_ref, bias_ref, o_ref, qs_ref, m_ref,
                 acc_ref, *, tq, tk):
    del qb_ref
    step = pl.program_id(1)
    rows = Q_PER_KV * tq

    @pl.when(kb_ref[step] == 0)
    def _init():
        m_ref[...] = jnp.full(m_ref.shape, MASKED_BIAS, F32)
        acc_ref[...] = jnp.zeros(acc_ref.shape, F32)
        for h in range(N_HEADS):
            g, r = divmod(h, Q_PER_KV)
            qh = q_ref[0, :, h * HEAD_DIM:(h + 1) * HEAD_DIM].astype(F32) * (HEAD_DIM ** -0.5)
            qs_ref[g, r * tq:(r + 1) * tq, :] = qh.astype(qs_ref.dtype)

    bias = bias_ref[0, 0].astype(F32)
    sub_rows = ATTN_HEADS_PER_DOT * tq
    for g in range(N_KV_HEADS):
        for sub in range(Q_PER_KV // ATTN_HEADS_PER_DOT):
            rs = slice(sub * sub_rows, (sub + 1) * sub_rows)
            s = jnp.dot(qs_ref[g, rs, :], kt_ref[0, g], preferred_element_type=F32)
            s = (s * LOG2_E).reshape(ATTN_HEADS_PER_DOT, tq, tk) + bias[None]
            s = s.reshape(sub_rows, tk)
            m_prev = m_ref[g, rs, :]
            m_blk = jnp.max(s, axis=1, keepdims=True)
            m_new = jnp.maximum(m_prev, jnp.broadcast_to(m_blk, m_prev.shape))
            alpha = jnp.exp2(m_prev - m_new)
            p = jnp.concatenate(
                [jnp.exp2(s[:, c * LANES:(c + 1) * LANES] - m_new) for c in range(tk // LANES)],
                axis=1).astype(BF16)
            acc_ref[g, rs, :] = alpha * acc_ref[g, rs, :] + jnp.dot(
                p, v_ref[0, g], preferred_element_type=F32)
            m_ref[g, rs, :] = m_new

    @pl.when(last_ref[step] == 1)
    def _finish():
        for h in range(N_HEADS):
            g, r = divmod(h, Q_PER_KV)
            a = acc_ref[g, r * tq:(r + 1) * tq, :]
            out = a[:, :HEAD_DIM] / a[:, HEAD_DIM:HEAD_DIM + 1]
            o_ref[0, :, h * HEAD_DIM:(h + 1) * HEAD_DIM] = out.astype(o_ref.dtype)


def masked_attention(q, k, v, bias, *, tq, causal):
    b, t_q, d_q = q.shape
    n_k = k.shape[1]
    nkb, tk = bias.shape[1], bias.shape[3]
    kt = jnp.transpose(k.reshape(b, n_k, N_KV_HEADS, HEAD_DIM), (0, 2, 3, 1))
    v4 = jnp.transpose(v.reshape(b, n_k, N_KV_HEADS, HEAD_DIM), (0, 2, 1, 3))
    v_ext = jnp.concatenate(
        [v4, jnp.ones((b, N_KV_HEADS, n_k, 1), BF16),
         jnp.zeros((b, N_KV_HEADS, n_k, LANES - HEAD_DIM - 1), BF16)], axis=-1)

    pairs = []
    for qb in range(t_q // tq):
        last = min(nkb - 1, ((qb + 1) * tq - 1) // tk) if causal else nkb - 1
        pairs += [(qb, kb, int(kb == last)) for kb in range(last + 1)]
    qb_ids, kb_ids, last_flags = (jnp.asarray(col, I32) for col in zip(*pairs))

    kernel = functools.partial(_attn_kernel, tq=tq, tk=tk)
    rows = Q_PER_KV * tq
    return pl.pallas_call(
        kernel,
        grid_spec=pltpu.PrefetchScalarGridSpec(
            num_scalar_prefetch=3,
            grid=(b, len(pairs)),
            in_specs=[
                pl.BlockSpec((1, tq, d_q), lambda bi, s, qbs, kbs, lf: (bi, qbs[s], 0)),
                pl.BlockSpec((1, N_KV_HEADS, HEAD_DIM, tk),
                             lambda bi, s, qbs, kbs, lf: (bi, 0, 0, kbs[s])),
                pl.BlockSpec((1, N_KV_HEADS, tk, LANES),
                             lambda bi, s, qbs, kbs, lf: (bi, 0, kbs[s], 0)),
                pl.BlockSpec((1, 1, tq, tk), lambda bi, s, qbs, kbs, lf: (bi, kbs[s], qbs[s], 0)),
            ],
            out_specs=pl.BlockSpec((1, tq, d_q), lambda bi, s, qbs, kbs, lf: (bi, qbs[s], 0)),
            scratch_shapes=[pltpu.VMEM((N_KV_HEADS, rows, HEAD_DIM), BF16),
                            pltpu.VMEM((N_KV_HEADS, rows, LANES), F32),
                            pltpu.VMEM((N_KV_HEADS, rows, LANES), F32)]),
        out_shape=jax.ShapeDtypeStruct((b, t_q, d_q), BF16),
        compiler_params=_params(("parallel", "arbitrary")),
        name="masked_attention",
    )(qb_ids, kb_ids, last_flags, q, kt, v_ext, bias)


def _conv_kernel(cur_ref, prev_ref, ctx_ref, w_ref, b_ref, g_ref, beta_ref, o_ref, buf_ref,
                 conv_ref, *, tm, cw):
    @pl.when(pl.program_id(1) == 0)
    def _first():
        buf_ref[0:CONV_HALO, :] = ctx_ref[0]

    @pl.when(pl.program_id(1) > 0)
    def _later():
        buf_ref[0:CONV_HALO, :] = prev_ref[0]

    buf_ref[CONV_HALO:CONV_HALO + tm, :] = cur_ref[0]
    lead = CONV_HALO - (CONV_WIDTH - 1)
    c_total = buf_ref.shape[1]
    rw = min(tm, 128)
    for c in range(c_total // cw):
        cs = slice(c * cw, (c + 1) * cw)
        for r in range(tm // rw):
            acc = jnp.zeros((rw, cw), F32) + b_ref[:, cs]
            for j in range(CONV_WIDTH):
                r0 = lead + j + r * rw
                acc = acc + w_ref[j:j + 1, cs] * buf_ref[r0:r0 + rw, cs]
            conv_ref[r * rw:(r + 1) * rw, cs] = acc
    x = conv_ref[...]
    mu = jnp.mean(x, axis=-1, keepdims=True)
    xc = x - mu
    y = xc * lax.rsqrt(jnp.mean(xc * xc, axis=-1, keepdims=True) + EPS)
    y = y * g_ref[...] + beta_ref[...]
    o_ref[0] = _silu(y).astype(o_ref.dtype)


def conv_ln_silu(y, ctx, w_dw, b_dw, ln_g, ln_b, *, tm):
    b, l, c = y.shape
    kernel = functools.partial(_conv_kernel, tm=tm, cw=LANES)
    vec = lambda a: a.reshape(1, c).astype(F32)
    halo_blocks = tm // CONV_HALO
    return pl.pallas_call(
        kernel,
        grid=(b, l // tm),
        in_specs=[
            pl.BlockSpec((1, tm, c), lambda bi, i: (bi, i, 0)),
            pl.BlockSpec((1, CONV_HALO, c),
                         lambda bi, i: (bi, jnp.maximum(i * halo_blocks - 1, 0), 0)),
            pl.BlockSpec((1, CONV_HALO, c), lambda bi, i: (bi, 0, 0)),
            pl.BlockSpec((CONV_WIDTH, c), lambda bi, i: (0, 0)),
            pl.BlockSpec((1, c), lambda bi, i: (0, 0)),
            pl.BlockSpec((1, c), lambda bi, i: (0, 0)),
            pl.BlockSpec((1, c), lambda bi, i: (0, 0)),
        ],
        out_specs=pl.BlockSpec((1, tm, c), lambda bi, i: (bi, i, 0)),
        out_shape=jax.ShapeDtypeStruct((b, l, c), BF16),
        scratch_shapes=[pltpu.VMEM((tm + CONV_HALO, c), F32), pltpu.VMEM((tm, c), F32)],
        compiler_params=_params(("parallel", "parallel")),
        name="conv_ln_silu",
    )(y, y, ctx, w_dw.astype(F32), vec(b_dw), vec(ln_g), vec(ln_b))


ROUTE_IDX1, ROUTE_IDX2, ROUTE_GATE1, ROUTE_GATE2, ROUTE_RANK1, ROUTE_RANK2 = range(6)


def _router_kernel(h_ref, g_ref, wr_ref, lower_ref, u_ref, route_ref, counts_ref, seen_ref, *,
                   n_experts):
    @pl.when(pl.program_id(0) == 0)
    def _init():
        seen_ref[...] = jnp.zeros(seen_ref.shape, F32)

    x = h_ref[...]
    u = x * lax.rsqrt(jnp.mean(x * x, axis=-1, keepdims=True) + EPS) * g_ref[...]
    u_ref[...] = u
    logits = jnp.dot(u.astype(BF16), wr_ref[...], preferred_element_type=F32)
    lane = lax.broadcasted_iota(I32, logits.shape, 1).astype(F32)
    logits = jnp.where(lane < n_experts, logits, -jnp.inf)
    top1 = jnp.max(logits, axis=1, keepdims=True)
    idx1 = jnp.min(jnp.where(logits == top1, lane, float(LANES)), axis=1, keepdims=True)
    rest = jnp.where(lane == idx1, -jnp.inf, logits)
    top2 = jnp.max(rest, axis=1, keepdims=True)
    idx2 = jnp.min(jnp.where(rest == top2, lane, float(LANES)), axis=1, keepdims=True)
    e2 = jnp.exp(top2 - top1)
    denom = 1.0 + e2

    hot1 = jnp.where(lane == idx1, 1.0, 0.0)
    hot2 = jnp.where(lane == idx2, 1.0, 0.0)
    hot = hot1 + hot2
    before = seen_ref[...] + jnp.dot(lower_ref[...], hot.astype(BF16), preferred_element_type=F32)
    rank1 = jnp.sum(hot1 * before, axis=1, keepdims=True)
    rank2 = jnp.sum(hot2 * before, axis=1, keepdims=True)
    seen_ref[...] += jnp.sum(hot, axis=0, keepdims=True)
    counts_ref[...] = seen_ref[...]

    route = jnp.zeros(logits.shape, F32)
    for slot, val in ((ROUTE_IDX1, idx1), (ROUTE_IDX2, idx2), (ROUTE_GATE1, 1.0 / denom),
                      (ROUTE_GATE2, e2 / denom), (ROUTE_RANK1, rank1), (ROUTE_RANK2, rank2)):
        route = jnp.where(lane == slot, val, route)
    route_ref[...] = route


def moe_router(h, gain, w_router_padded, n_experts, *, tm):
    m, d = h.shape
    row = jnp.arange(tm, dtype=I32)
    lower = (row[None, :] < row[:, None]).astype(BF16)
    kernel = functools.partial(_router_kernel, n_experts=n_experts)
    return pl.pallas_call(
        kernel,
        grid=(m // tm,),
        in_specs=[pl.BlockSpec((tm, d), lambda i: (i, 0)),
                  pl.BlockSpec((1, d), lambda i: (0, 0)),
                  pl.BlockSpec((d, LANES), lambda i: (0, 0)),
                  pl.BlockSpec((tm, tm), lambda i: (0, 0))],
        out_specs=[pl.BlockSpec((tm, d), lambda i: (i, 0)),
                   pl.BlockSpec((tm, LANES), lambda i: (i, 0)),
                   pl.BlockSpec((1, LANES), lambda i: (0, 0))],
        out_shape=[jax.ShapeDtypeStruct((m, d), F32), jax.ShapeDtypeStruct((m, LANES), F32),
                   jax.ShapeDtypeStruct((1, LANES), F32)],
        scratch_shapes=[pltpu.VMEM((1, LANES), F32)],
        compiler_params=_params(("arbitrary",)),
        name="moe_router",
    )(h, gain.reshape(1, d).astype(F32), w_router_padded, lower)


def _scatter_rows_kernel(d1_ref, d2_ref, u_ref, init_ref, xs_ref, sem):
    del init_ref
    tm = u_ref.shape[0]

    def row_copy(j, dest):
        return pltpu.make_async_copy(u_ref.at[pl.ds(j, 1)], xs_ref.at[pl.ds(dest, 1)], sem)

    def issue(j, carry):
        row_copy(j, d1_ref[0, 0, j]).start(priority=0)
        row_copy(j, d2_ref[0, 0, j]).start(priority=1)
        return carry

    lax.fori_loop(0, tm, issue, 0)

    def drain(j, carry):
        row_copy(j, 0).wait()
        row_copy(j, 0).wait()
        return carry

    lax.fori_loop(0, tm, drain, 0)


def scatter_rows(u, dest1, dest2, n_rows, *, tm):
    m, d = u.shape
    smem_idx = lambda a: a.reshape(m // tm, 1, tm)
    idx_spec = pl.BlockSpec((1, 1, tm), lambda i: (i, 0, 0), memory_space=pltpu.SMEM)
    return pl.pallas_call(
        _scatter_rows_kernel,
        grid=(m // tm,),
        in_specs=[idx_spec, idx_spec,
                  pl.BlockSpec((tm, d), lambda i: (i, 0)),
                  pl.BlockSpec(memory_space=pl.ANY)],
        out_specs=pl.BlockSpec(memory_space=pl.ANY),
        out_shape=jax.ShapeDtypeStruct((n_rows, d), u.dtype),
        scratch_shapes=[pltpu.SemaphoreType.DMA],
        input_output_aliases={3: 0},
        compiler_params=_params(("arbitrary",)),
        name="moe_scatter_rows",
    )(smem_idx(dest1), smem_idx(dest2), u, jnp.zeros((n_rows, d), u.dtype))


def _combine_kernel(d1_ref, d2_ref, rows_ref, route_ref, h_ref, o_ref, buf_ref, sem):
    tm = h_ref.shape[0]

    def row_copy(slot, j, src):
        return pltpu.make_async_copy(rows_ref.at[pl.ds(src, 1)], buf_ref.at[slot, pl.ds(j, 1)], sem)

    def issue(j, carry):
        row_copy(0, j, d1_ref[0, 0, j]).start(priority=0)
        row_copy(1, j, d2_ref[0, 0, j]).start(priority=1)
        return carry

    lax.fori_loop(0, tm, issue, 0)

    def drain(j, carry):
        row_copy(0, j, 0).wait()
        row_copy(1, j, 0).wait()
        return carry

    lax.fori_loop(0, tm, drain, 0)
    route = route_ref[...]
    g1 = route[:, ROUTE_GATE1:ROUTE_GATE1 + 1]
    g2 = route[:, ROUTE_GATE2:ROUTE_GATE2 + 1]
    o_ref[...] = h_ref[...] + (g1 * buf_ref[0] + g2 * buf_ref[1])


def combine_rows(rows, dest1, dest2, route, h, *, tm):
    m, d = h.shape
    smem_idx = lambda a: a.reshape(m // tm, 1, tm)
    idx_spec = pl.BlockSpec((1, 1, tm), lambda i: (i, 0, 0), memory_space=pltpu.SMEM)
    return pl.pallas_call(
        _combine_kernel,
        grid=(m // tm,),
        in_specs=[idx_spec, idx_spec,
                  pl.BlockSpec(memory_space=pl.ANY),
                  pl.BlockSpec((tm, LANES), lambda i: (i, 0)),
                  pl.BlockSpec((tm, d), lambda i: (i, 0))],
        out_specs=pl.BlockSpec((tm, d), lambda i: (i, 0)),
        out_shape=jax.ShapeDtypeStruct((m, d), F32),
        scratch_shapes=[pltpu.VMEM((2, tm, d), F32), pltpu.SemaphoreType.DMA],
        compiler_params=_params(("arbitrary",)),
        name="moe_combine_rows",
    )(smem_idx(dest1), smem_idx(dest2), rows, route, h)


def _expert_up_kernel(blk_expert_ref, n_used_ref, xs_ref, wg_ref, wu_ref, o_ref, stage_ref):
    del blk_expert_ref
    i = pl.program_id(0)

    @pl.when(pl.program_id(1) == 0)
    def _stage():
        stage_ref[...] = xs_ref[...].astype(BF16)

    @pl.when(i < n_used_ref[0])
    def _compute():
        u = stage_ref[...]
        a = jnp.dot(u, wg_ref[0], preferred_element_type=F32)
        b = jnp.dot(u, wu_ref[0], preferred_element_type=F32)
        o_ref[...] = (_silu(a) * b).astype(o_ref.dtype)

    @pl.when(i >= n_used_ref[0])
    def _skip():
        o_ref[...] = jnp.zeros(o_ref.shape, o_ref.dtype)


def expert_up(xs, w_gate, w_up, blk_expert, n_used, *, blk, tn):
    n_rows, d = xs.shape
    f = w_gate.shape[2]
    return pl.pallas_call(
        _expert_up_kernel,
        grid_spec=pltpu.PrefetchScalarGridSpec(
            num_scalar_prefetch=2,
            grid=(n_rows // blk, f // tn),
            in_specs=[pl.BlockSpec((blk, d), lambda i, j, be, nu: (i, 0)),
                      pl.BlockSpec((1, d, tn), lambda i, j, be, nu: (be[i], 0, j)),
                      pl.BlockSpec((1, d, tn), lambda i, j, be, nu: (be[i], 0, j))],
            out_specs=pl.BlockSpec((blk, tn), lambda i, j, be, nu: (i, j)),
            scratch_shapes=[pltpu.VMEM((blk, d), BF16)]),
        out_shape=jax.ShapeDtypeStruct((n_rows, f), BF16),
        compiler_params=_params(("parallel", "arbitrary")),
        name="expert_up",
    )(blk_expert, n_used, xs, w_gate, w_up)


def _expert_down_kernel(blk_expert_ref, n_used_ref, hb_ref, wd_ref, o_ref):
    del blk_expert_ref
    i = pl.program_id(0)

    @pl.when(i < n_used_ref[0])
    def _compute():
        o_ref[...] = jnp.dot(hb_ref[...], wd_ref[0], preferred_element_type=F32)

    @pl.when(i >= n_used_ref[0])
    def _skip():
        o_ref[...] = jnp.zeros(o_ref.shape, o_ref.dtype)


def expert_down(hb, w_down, blk_expert, n_used, *, blk, tn):
    n_rows, f = hb.shape
    d = w_down.shape[2]
    return pl.pallas_call(
        _expert_down_kernel,
        grid_spec=pltpu.PrefetchScalarGridSpec(
            num_scalar_prefetch=2,
            grid=(n_rows // blk, d // tn),
            in_specs=[pl.BlockSpec((blk, f), lambda i, j, be, nu: (i, 0)),
                      pl.BlockSpec((1, f, tn), lambda i, j, be, nu: (be[i], 0, j))],
            out_specs=pl.BlockSpec((blk, tn), lambda i, j, be, nu: (i, j))),
        out_shape=jax.ShapeDtypeStruct((n_rows, d), F32),
        compiler_params=_params(("parallel", "parallel")),
        name="expert_down",
    )(blk_expert, n_used, hb, w_down)


def moe_swiglu(h, gain, w_router, w_gate, w_up, w_down):
    m, d = h.shape
    n_experts = w_router.shape[1]
    blk = MOE_ROW_BLOCK if m >= 8 * MOE_ROW_BLOCK else LANES
    tm = _pick_tile(m, 256)
    w_router_padded = jnp.concatenate([w_router, jnp.zeros((d, LANES - n_experts), BF16)], axis=1)
    u, route, counts = moe_router(h, gain, w_router_padded, n_experts, tm=_pick_tile(m, 512))

    counts = counts[0, :n_experts].astype(I32)
    padded = (counts + blk - 1) // blk * blk
    pad_end = jnp.cumsum(padded)
    pad_start = pad_end - padded
    idx1 = route[:, ROUTE_IDX1].astype(I32)
    idx2 = route[:, ROUTE_IDX2].astype(I32)
    dest1 = pad_start[idx1] + route[:, ROUTE_RANK1].astype(I32)
    dest2 = pad_start[idx2] + route[:, ROUTE_RANK2].astype(I32)
    n_blocks = -(-(TOP_K_EXPERTS * m + n_experts * (blk - 1)) // blk)
    blk_start = jnp.arange(n_blocks, dtype=I32) * blk
    blk_expert = jnp.minimum(jnp.searchsorted(pad_end, blk_start, side="right"),
                             n_experts - 1).astype(I32)
    n_used = (pad_end[-1:] // blk).astype(I32)

    xs = scatter_rows(u, dest1, dest2, n_blocks * blk, tm=tm)
    hb = expert_up(xs, w_gate, w_up, blk_expert, n_used, blk=blk, tn=_pick_tile(w_gate.shape[2], 1792))
    rows = expert_down(hb, w_down, blk_expert, n_used, blk=blk, tn=_pick_tile(d, 1024))
    return combine_rows(rows, dest1, dest2, route, h, tm=tm)


def _split_w_in(w_in_layer, d_model):
    attn_dim = N_HEADS * HEAD_DIM
    kv_dim = N_KV_HEADS * HEAD_DIM
    sizes = (attn_dim, kv_dim, kv_dim, IDX_HEADS * IDX_DIM, IDX_DIM, IDX_HEADS, d_model, d_model,
             d_model, d_model)
    parts, off = [], 0
    for s in sizes:
        parts.append(w_in_layer[:, off:off + s])
        off += s
    assert off == w_in_layer.shape[1]
    return parts


def _layer_stack(x, p, weights, cache, *, past):
    bsz, length, d_model = x.shape
    m = bsz * length
    kv_dim = N_KV_HEADS * HEAD_DIM
    pos = past + jnp.arange(length, dtype=I32)
    row_pos = jnp.tile(pos, bsz)
    depth = weights["w_in"].shape[0]

    cos_q, sin_q = _rope_tables(row_pos, 512)
    cos_k, sin_k = _rope_tables(row_pos, kv_dim)
    cos_i, sin_i = _rope_tables(row_pos, LANES, rotary_width=IDX_DIM)
    ones_q = _block_ones(512)
    ones_k = _block_ones(kv_dim)

    n_keys = past + length
    n_sel = min(TOPK_MAX, n_keys // 4)
    if past == 0:
        tq_sel, tk, tq_attn = min(256, length), min(512, length), min(256, length)
        n_keys_pad = n_keys
    else:
        tq_sel = tq_attn = length
        tk = 384
        n_keys_pad = -(-n_keys // tk) * tk

    h = x.reshape(m, d_model)
    ks, vs, kis, convs = [], [], [], []
    for i in range(depth):
        wq, wk, wv, wqi, wki, wwi, wglu_a, wglu_g, wga, wgb = _split_w_in(weights["w_in"][i], d_model)
        w_kiwi = jnp.concatenate(
            [wki, wwi, jnp.zeros((d_model, LANES - IDX_DIM - IDX_HEADS), BF16)], axis=1)
        g_mix = weights["g_mix"][i]

        qn_gain = jnp.tile(weights["q_norm_g"][i].astype(F32), 512 // HEAD_DIM).reshape(1, 512)
        kn_gain = jnp.tile(weights["k_norm_g"][i].astype(F32), kv_dim // HEAD_DIM).reshape(1, kv_dim)

        def norm_rope(accs, tiled, rows, consts):
            y = _head_rms_norm(accs[0], consts[0], consts[1])
            y = _rope(y, rows[0], rows[1])
            return (y, y)

        u = rms_norm_rows(h, g_mix)
        (q_bf,) = fused_mm("proj_q", [u], [wq], [0],
                           lambda a, t, r, c: (_rope(_head_rms_norm(a[0], c[0], c[1]), r[0], r[1]),),
                           [BF16], rows=[cos_q, sin_q], consts=[ones_q, qn_gain], tn=512)
        k_f32, k_bf = fused_mm("proj_k", [u], [wk], [0], norm_rope, [F32, BF16],
                               rows=[cos_k, sin_k], consts=[ones_k, kn_gain])
        v_f32, v_bf = fused_mm("proj_v", [u], [wv], [0], lambda a, t, r, c: (a[0], a[0]),
                               [F32, BF16])
        (qi_bf,) = fused_mm("proj_qi", [u], [wqi], [0],
                            lambda a, t, r, c: (_rope(a[0], r[0], r[1]),),
                            [BF16], rows=[cos_q, sin_q], tn=512)
        (kiwi,) = fused_mm("proj_kiwi", [u], [w_kiwi], [0],
                           lambda a, t, r, c: (_rope(a[0], r[0], r[1]),),
                           [F32], rows=[cos_i, sin_i])
        (glu,) = fused_mm("proj_glu", [u], [wglu_a, wglu_g], [0, 0],
                          lambda a, t, r, c: (a[0] * _sigmoid(a[1]),), [F32])
        gate_a, gate_b = fused_mm("proj_gates", [u], [wga, wgb], [0, 0],
                                  lambda a, t, r, c: (_sigmoid(a[0]), _sigmoid(a[1])),
                                  [F32, F32])

        ki_f32 = kiwi[:, :IDX_DIM]
        wi = kiwi[:, IDX_DIM:IDX_DIM + IDX_HEADS]

        k3 = k_bf.reshape(bsz, length, kv_dim)
        v3 = v_bf.reshape(bsz, length, kv_dim)
        ki3 = ki_f32.astype(BF16).reshape(bsz, length, IDX_DIM)
        if past:
            pad = n_keys_pad - n_keys
            k3 = jnp.concatenate([cache["k"][i].reshape(bsz, past, kv_dim).astype(BF16), k3,
                                  jnp.zeros((bsz, pad, kv_dim), BF16)], axis=1)
            v3 = jnp.concatenate([cache["v"][i].reshape(bsz, past, kv_dim).astype(BF16), v3,
                                  jnp.zeros((bsz, pad, kv_dim), BF16)], axis=1)
            ki3 = jnp.concatenate([cache["kidx"][i].astype(BF16), ki3,
                                   jnp.zeros((bsz, pad, IDX_DIM), BF16)], axis=1)
        bias = select_bias(qi_bf.reshape(bsz, length, -1), ki3, wi.reshape(bsz, length, IDX_HEADS),
                           tq=tq_sel, tk=tk, causal=(past == 0), q_pos0=past, n_valid=n_keys,
                           n_sel=n_sel)
        attn = masked_attention(q_bf.reshape(bsz, length, -1), k3, v3, bias, tq=tq_attn,
                                causal=(past == 0))
        attn = attn.reshape(m, -1)

        glu3 = glu.reshape(bsz, length, d_model)
        if past:
            ctx = cache["conv"][i]
        else:
            ctx = jnp.zeros((bsz, CONV_WIDTH - 1, d_model), F32)
        lead = jnp.zeros((bsz, CONV_HALO - (CONV_WIDTH - 1), d_model), F32)
        ctx32 = jnp.concatenate([lead, ctx], axis=1)
        l_pad = -(-length // CONV_HALO) * CONV_HALO
        if l_pad == length:
            y_conv = glu3
        else:
            y_conv = jnp.concatenate(
                [glu3, jnp.zeros((bsz, l_pad - length, d_model), F32)], axis=1)
        if length >= CONV_WIDTH - 1:
            conv_state = glu3[:, length - (CONV_WIDTH - 1):]
        else:
            conv_state = jnp.concatenate([ctx, glu3], axis=1)[:, -(CONV_WIDTH - 1):]
        z = conv_ln_silu(y_conv, ctx32, weights["w_dw"][i], weights["b_dw"][i],
                         weights["conv_ln_g"][i], weights["conv_ln_b"][i], tm=min(256, l_pad))
        z = z[:, :length].reshape(m, d_model)

        (merged,) = fused_mm("mix", [attn, z], [weights["w_o_attn"][i], weights["w_o_conv"][i]],
                             [0, 1], lambda a, t, r, c: (t[0] * a[0] + t[1] * a[1],), [BF16],
                             tiled=[gate_a, gate_b])
        (h,) = fused_mm("out_proj", [merged], [weights["w_out"][i]], [0],
                        lambda a, t, r, c: (t[0] + a[0],), [F32], tiled=[h])

        j = i // 2
        if i % 2 == 0:
            (hb,) = fused_mm("ffn_up", [h], [weights["ffn_w_gate"][j], weights["ffn_w_up"][j]],
                             [0, 0], lambda a, t, r, c: (_silu(a[0]) * a[1],), [BF16],
                             gains=[weights["g_ffn"][i]], tm=512, tn=1408)
            (h,) = fused_mm("ffn_down", [hb], [weights["ffn_w_down"][j]], [0],
                            lambda a, t, r, c: (t[0] + a[0],), [F32], tiled=[h])
        else:
            h = moe_swiglu(h, weights["g_ffn"][i], weights["moe_router"][j],
                           weights["moe_w_gate"][j], weights["moe_w_up"][j],
                           weights["moe_w_down"][j])

        (h,) = fused_mm("ple", [h, p[i].reshape(m, -1)],
                        [weights["w_ple_gate"][i], weights["w_ple_proj"][i]],
                        [0, 1], lambda a, t, r, c: (t[0] + _sigmoid(a[0]) * a[1],), [F32],
                        gains=[weights["g_ple"][i], None], tiled=[h])

        ks.append(k_f32.reshape(bsz, length, N_KV_HEADS, HEAD_DIM))
        vs.append(v_f32.reshape(bsz, length, N_KV_HEADS, HEAD_DIM))
        kis.append(ki_f32.reshape(bsz, length, IDX_DIM))
        convs.append(conv_state)
    return (h.reshape(bsz, length, d_model), jnp.stack(ks), jnp.stack(vs), jnp.stack(kis),
            jnp.stack(convs))


def kernel(x_prompt, x_sample, cache_k, cache_v, cache_kidx, state_conv, p_prompt, p_sample, w_in, q_norm_g, k_norm_g, w_dw, b_dw, conv_ln_g, conv_ln_b, w_o_attn, w_o_conv, w_out, g_mix, g_ffn, ffn_w_gate, ffn_w_up, ffn_w_down, moe_router, moe_w_gate, moe_w_up, moe_w_down, g_ple, w_ple_gate, w_ple_proj):
    bf = lambda a: a.astype(BF16)
    weights = dict(
        w_in=bf(w_in), q_norm_g=q_norm_g, k_norm_g=k_norm_g, w_dw=w_dw, b_dw=b_dw,
        conv_ln_g=conv_ln_g, conv_ln_b=conv_ln_b, w_o_attn=bf(w_o_attn), w_o_conv=bf(w_o_conv),
        w_out=bf(w_out), g_mix=g_mix, g_ffn=g_ffn, ffn_w_gate=bf(ffn_w_gate),
        ffn_w_up=bf(ffn_w_up), ffn_w_down=bf(ffn_w_down), moe_router=bf(moe_router),
        moe_w_gate=bf(moe_w_gate), moe_w_up=bf(moe_w_up), moe_w_down=bf(moe_w_down), g_ple=g_ple,
        w_ple_gate=bf(w_ple_gate), w_ple_proj=bf(w_ple_proj))
    cache = dict(k=cache_k, v=cache_v, kidx=cache_kidx, conv=state_conv)
    past_len = cache_k.shape[2]
    y_p, k_p, v_p, ki_p, conv_p = _layer_stack(x_prompt, p_prompt, weights, None, past=0)
    y_s, k_s, v_s, ki_s, conv_s = _layer_stack(x_sample, p_sample, weights, cache, past=past_len)
    return (y_p, y_s, k_p, v_p, ki_p, conv_p, k_s, v_s, ki_s, conv_s)
```

```python
import functools

import jax
import jax.numpy as jnp
from jax import lax
from jax.experimental import pallas as pl
from jax.experimental.pallas import tpu as pltpu

F32 = jnp.float32
BF16 = jnp.bfloat16
I32 = jnp.int32

CHUNK = 64
CHUNK_SHIFT = CHUNK.bit_length() - 1
assert 1 << CHUNK_SHIFT == CHUNK
N_HEADS = 16
HEAD_DIM = 64
N_KV_HEADS = 4
Q_PER_KV = N_HEADS // N_KV_HEADS
IDX_HEADS = 8
IDX_DIM = 64
TOPK_MAX = 256
ROPE_THETA = 10000.0
CONV_WIDTH = 31
TOP_K_EXPERTS = 2
EPS = 1e-6

LANES = 128
CONV_HALO = 32
MASKED_BIAS = -1e30
LOG2_E = 1.4426950408889634
INT_MIN = -2 ** 31
KEY_NEG_INF = -2139095041
SEARCH_ROWS = 128
EMIT_ROWS = 128
TOP_BITS = 15
DMA_LOOP_UNROLL = 8
ATTN_HEADS_PER_DOT = Q_PER_KV
MOE_ROW_BLOCK = 512
VMEM_LIMIT = 52 * 1024 * 1024


def _params(semantics):
    return pltpu.CompilerParams(dimension_semantics=semantics, vmem_limit_bytes=VMEM_LIMIT)


def _pick_tile(n, pref):
    if n <= pref:
        return n
    best = None
    for t in range(LANES, pref + 1, LANES):
        if n % t == 0:
            best = t
    assert best is not None, (n, pref)
    return best


def _rms_norm_kernel(x_ref, g_ref, o_ref):
    x = x_ref[...]
    y = x * lax.rsqrt(jnp.mean(x * x, axis=-1, keepdims=True) + EPS) * g_ref[...]
    o_ref[...] = y.astype(o_ref.dtype)


def rms_norm_rows(x, gain, *, tm=512):
    m, d = x.shape
    tm = _pick_tile(m, tm) if m % LANES == 0 else m
    return pl.pallas_call(
        _rms_norm_kernel,
        grid=(m // tm,),
        in_specs=[pl.BlockSpec((tm, d), lambda i: (i, 0)), pl.BlockSpec((1, d), lambda i: (0, 0))],
        out_specs=pl.BlockSpec((tm, d), lambda i: (i, 0)),
        out_shape=jax.ShapeDtypeStruct((m, d), BF16),
        compiler_params=_params(("parallel",)),
        name="rms_norm_rows",
    )(x, gain.reshape(1, d).astype(F32))


def _fused_mm_kernel(*refs, n_lhs, has_gain, staged, pair_lhs, n_tiled, n_rows, n_consts, n_out,
                     epilogue):
    it = iter(refs)
    lhs_refs = [next(it) for _ in range(n_lhs)]
    gain_refs = [next(it) if has_gain[a] else None for a in range(n_lhs)]
    rhs_refs = [next(it) for _ in range(len(pair_lhs))]
    tiled_refs = [next(it) for _ in range(n_tiled)]
    row_refs = [next(it) for _ in range(n_rows)]
    const_refs = [next(it) for _ in range(n_consts)]
    out_refs = [next(it) for _ in range(n_out)]
    operand_refs = [next(it) if staged[a] else lhs_refs[a] for a in range(n_lhs)]

    if any(staged):
        @pl.when(pl.program_id(1) == 0)
        def _stage():
            for a in range(n_lhs):
                if not staged[a]:
                    continue
                x = lhs_refs[a][...].astype(F32)
                if has_gain[a]:
                    x = x * lax.rsqrt(jnp.mean(x * x, axis=-1, keepdims=True) + EPS)
                    x = x * gain_refs[a][...]
                operand_refs[a][...] = x.astype(BF16)

    accs = [jnp.dot(operand_refs[pair_lhs[r]][...], rhs_refs[r][...], preferred_element_type=F32)
            for r in range(len(pair_lhs))]
    outs = epilogue(accs, [t[...] for t in tiled_refs], [t[...] for t in row_refs],
                    [t[...] for t in const_refs])
    for o_ref, o in zip(out_refs, outs):
        o_ref[...] = o.astype(o_ref.dtype)


def fused_mm(name, lhs, rhs, pair_lhs, epilogue, out_dtypes, *, gains=None, tiled=(), rows=(),
             consts=(), tm=1024, tn=1024):
    m = lhs[0].shape[0]
    n = rhs[0].shape[1]
    tm = _pick_tile(m, tm) if m % LANES == 0 else m
    tn = _pick_tile(n, tn)
    n_lhs = len(lhs)
    gains = list(gains) if gains is not None else [None] * n_lhs
    has_gain = tuple(g is not None for g in gains)
    staged = tuple(has_gain[a] or lhs[a].dtype != BF16 for a in range(n_lhs))

    in_specs, args = [], []
    for x in lhs:
        in_specs.append(pl.BlockSpec((tm, x.shape[1]), lambda i, j: (i, 0)))
        args.append(x)
    for g in gains:
        if g is not None:
            g2 = g.reshape(1, -1).astype(F32)
            in_specs.append(pl.BlockSpec(g2.shape, lambda i, j: (0, 0)))
            args.append(g2)
    for w in rhs:
        in_specs.append(pl.BlockSpec((w.shape[0], tn), lambda i, j: (0, j)))
        args.append(w)
    for t in tiled:
        in_specs.append(pl.BlockSpec((tm, tn), lambda i, j: (i, j)))
        args.append(t)
    for t in rows:
        in_specs.append(pl.BlockSpec((tm, t.shape[1]), lambda i, j: (i, 0)))
        args.append(t)
    for c in consts:
        in_specs.append(pl.BlockSpec(c.shape, lambda i, j: (0, 0)))
        args.append(c)

    kernel = functools.partial(
        _fused_mm_kernel, n_lhs=n_lhs, has_gain=has_gain, staged=staged, pair_lhs=tuple(pair_lhs),
        n_tiled=len(tiled), n_rows=len(rows), n_consts=len(consts), n_out=len(out_dtypes),
        epilogue=epilogue)
    outs = pl.pallas_call(
        kernel,
        grid=(m // tm, n // tn),
        in_specs=in_specs,
        out_specs=[pl.BlockSpec((tm, tn), lambda i, j: (i, j)) for _ in out_dtypes],
        out_shape=[jax.ShapeDtypeStruct((m, n), dt) for dt in out_dtypes],
        scratch_shapes=[pltpu.VMEM((tm, x.shape[1]), BF16) for a, x in enumerate(lhs) if staged[a]],
        compiler_params=_params(("parallel", "arbitrary")),
        name=name,
    )(*args)
    return outs


def _head_rms_norm(x, block_ones, gain):
    x2 = x * x
    hi = x2.astype(BF16)
    lo = (x2 - hi.astype(F32)).astype(BF16)
    ss = (jnp.dot(hi, block_ones, preferred_element_type=F32)
          + jnp.dot(lo, block_ones, preferred_element_type=F32))
    return x * lax.rsqrt(ss * (1.0 / HEAD_DIM) + EPS) * gain


def _rope(x, cos, sin_signed):
    n = x.shape[1]
    lane = lax.broadcasted_iota(I32, x.shape, 1)
    first_half = (lane & (HEAD_DIM // 2)) == 0
    partner = jnp.where(first_half, pltpu.roll(x, n - HEAD_DIM // 2, 1),
                        pltpu.roll(x, HEAD_DIM // 2, 1))
    return x * cos + partner * sin_signed


def _sigmoid(x):
    return 1.0 / (1.0 + jnp.exp(-x))


def _silu(x):
    return x * _sigmoid(x)


def _rope_tables(pos, width, rotary_width=None):
    half = HEAD_DIM // 2
    inv = ROPE_THETA ** (-jnp.arange(half, dtype=F32) / half)
    ang = pos.astype(F32)[:, None] * inv[None, :]
    cos = jnp.cos(ang)
    sin = jnp.sin(ang)
    cos64 = jnp.concatenate([cos, cos], axis=-1)
    sin64 = jnp.concatenate([-sin, sin], axis=-1)
    rotary_width = width if rotary_width is None else rotary_width
    reps = rotary_width // HEAD_DIM
    cos_t = jnp.tile(cos64, (1, reps))
    sin_t = jnp.tile(sin64, (1, reps))
    if rotary_width < width:
        pad = width - rotary_width
        cos_t = jnp.concatenate([cos_t, jnp.ones((pos.shape[0], pad), F32)], axis=-1)
        sin_t = jnp.concatenate([sin_t, jnp.zeros((pos.shape[0], pad), F32)], axis=-1)
    return cos_t, sin_t


def _block_ones(width):
    head = jnp.arange(width, dtype=I32) // HEAD_DIM
    return (head[:, None] == head[None, :]).astype(BF16)


def _float_key(x):
    b = lax.bitcast_convert_type(x, I32)
    return jnp.where(b >= 0, b, b ^ jnp.int32(0x7FFFFFFF))


def _select_kernel(qi_ref, ki_ref, wi_ref, tri_ref, bias_ref, key_ref, top_ref, *, tq, tk, nkb,
                   causal, q_pos0, n_valid, n_sel, rows):
    qb = pl.program_id(1)
    n_chunks = tk // LANES
    n_sub = tq // rows
    half = rows // 2
    if causal:
        n_act = jnp.minimum(nkb, ((qb + 1) * tq + tk - 1) // tk)
    else:
        n_act = nkb
    q_pos = q_pos0 + qb * tq + lax.broadcasted_iota(I32, (tq, tk), 0)
    q_chunk = lax.shift_right_logical(q_pos, CHUNK_SHIFT)
    w = (wi_ref[0] * (IDX_HEADS ** -0.5)) * (IDX_DIM ** -0.5)
    qi = qi_ref[0]

    def score_block(kb, carry):
        start = pl.multiple_of(kb * tk, tk)
        ki = ki_ref[0, pl.ds(start, tk), :]
        s = jnp.zeros((tq, tk), F32)
        for h in range(IDX_HEADS):
            r = lax.dot_general(qi[:, h * IDX_DIM:(h + 1) * IDX_DIM], ki,
                                (((1,), (1,)), ((), ())), preferred_element_type=F32)
            s = s + jnp.maximum(r, 0.0) * w[:, h:h + 1]
        k_pos = start + lax.broadcasted_iota(I32, (tq, tk), 1)
        admissible = (lax.shift_right_logical(k_pos, CHUNK_SHIFT) <= q_chunk) & (k_pos < n_valid)
        key = _float_key(jnp.where(admissible, s, -jnp.inf))
        top = lax.shift_right_logical(key ^ jnp.int32(INT_MIN), 32 - TOP_BITS)
        for sub in range(n_sub):
            key_ref[sub, kb] = key[sub * rows:(sub + 1) * rows]
            lo_rows = top[sub * rows:sub * rows + half]
            hi_rows = top[sub * rows + half:(sub + 1) * rows]
            top_ref[sub, kb] = lo_rows | jnp.left_shift(hi_rows, 16)
        return carry

    lax.fori_loop(0, n_act, score_block, 0)
    for sub in range(n_sub):
        key_ref[sub, n_act] = jnp.full((rows, tk), INT_MIN, I32)
        top_ref[sub, n_act] = jnp.zeros((half, tk), I32)
    for sub in range(n_sub):
        _select_rows(key_ref.at[sub], top_ref.at[sub], tri_ref,
                     bias_ref.at[0, :, sub * rows:(sub + 1) * rows, :],
                     n_act=n_act, rows=rows, tk=tk, n_sel=n_sel)

    def fill_block(kb, carry):
        bias_ref[0, kb] = jnp.full((tq, tk), MASKED_BIAS, bias_ref.dtype)
        return carry

    lax.fori_loop(n_act, nkb, fill_block, 0)


def _select_rows(key_ref, top_ref, tri_ref, out_ref, *, n_act, rows, tk, n_sel):
    tq = rows
    half = rows // 2
    n_chunks = tk // LANES

    ones_mat = jnp.ones((LANES, LANES), BF16)

    def count(hit_fn):
        def pair(p, acc):
            for kb in (2 * p, 2 * p + 1):
                for c in range(n_chunks):
                    acc = acc + jnp.where(hit_fn(key_ref[kb, :, c * LANES:(c + 1) * LANES]), 1, 0)
            return acc
        acc = lax.fori_loop(0, (n_act + 1) // 2, pair, jnp.zeros((tq, LANES), I32))
        return jnp.dot(acc.astype(F32).astype(BF16), ones_mat, preferred_element_type=F32)

    n_counted = (tk * 2) * ((n_act + 1) // 2)
    guard = 1 << TOP_BITS

    def top_step(i, prefix):
        cand = prefix | jnp.left_shift(jnp.int32(1), TOP_BITS - 1 - i)
        below = (cand - 1) | guard
        cw = below[:half] | jnp.left_shift(below[half:], 16)

        def pair(p, acc):
            for kb in (2 * p, 2 * p + 1):
                for c in range(n_chunks):
                    d = cw - top_ref[kb, :, c * LANES:(c + 1) * LANES]
                    acc = acc + (lax.shift_right_logical(d, TOP_BITS) & 0x00010001)
            return acc
        acc = lax.fori_loop(0, (n_act + 1) // 2, pair, jnp.zeros((half, LANES), I32))
        le = jnp.concatenate([acc & 0xFFFF, lax.shift_right_logical(acc, 16)], axis=0)
        n_le = jnp.dot(le.astype(F32).astype(BF16), ones_mat, preferred_element_type=F32)
        return jnp.where(n_counted - n_le >= n_sel, cand, prefix)

    top_prefix = lax.fori_loop(0, TOP_BITS, top_step, jnp.zeros((tq, LANES), I32))

    def bit_step(i, prefix):
        cand = prefix | jnp.left_shift(jnp.int32(1), 31 - i)
        cand_signed = cand ^ jnp.int32(INT_MIN)
        cnt = count(lambda k: k >= cand_signed)
        return jnp.where(cnt >= n_sel, cand, prefix)

    prefix = lax.fori_loop(TOP_BITS, 32, bit_step, jnp.left_shift(top_prefix, 32 - TOP_BITS))
    tau_rep = prefix ^ jnp.int32(INT_MIN)
    n_gt = count(lambda k: k > tau_rep)
    n_eq = count(lambda k: k == tau_rep)
    need_rep = n_sel - n_gt

    er = min(rows, EMIT_ROWS)
    for g in range(rows // er):
        rs = slice(g * er, (g + 1) * er)
        tau = tau_rep[rs, :1]
        need = need_rep[rs, :1]
        all_ties_taken = jnp.max(jnp.where(need_rep[rs] >= n_eq[rs], 0.0, 1.0)) == 0.0

        def emit_all_ties(kb, carry, rs=rs, tau=tau):
            key = key_ref[kb, rs, :]
            sel = (key >= tau) & (key > KEY_NEG_INF)
            out_ref[kb, rs, :] = jnp.where(sel, 0.0, MASKED_BIAS).astype(out_ref.dtype)
            return carry

        def emit_ranked(kb, seen, rs=rs, tau=tau, need=need):
            for c in range(n_chunks):
                cs = slice(c * LANES, (c + 1) * LANES)
                key = key_ref[kb, rs, cs]
                eq = key == tau
                rank = seen + jnp.dot(jnp.where(eq, 1.0, 0.0).astype(BF16), tri_ref[...],
                                      preferred_element_type=F32)
                sel = ((key > tau) | (eq & (rank <= need))) & (key > KEY_NEG_INF)
                out_ref[kb, rs, cs] = jnp.where(sel, 0.0, MASKED_BIAS).astype(out_ref.dtype)
                seen = rank[:, LANES - 1:LANES]
            return seen

        @pl.when(all_ties_taken)
        def _fast(emit_all_ties=emit_all_ties):
            lax.fori_loop(0, n_act, emit_all_ties, 0)

        @pl.when(jnp.logical_not(all_ties_taken))
        def _ranked(emit_ranked=emit_ranked):
            lax.fori_loop(0, n_act, emit_ranked, jnp.zeros((er, 1), F32))


def select_bias(qi, ki, wi, *, tq, tk, causal, q_pos0, n_valid, n_sel):
    b, t_q, _ = qi.shape
    n_k = ki.shape[1]
    nkb = n_k // tk
    idx = jnp.arange(LANES, dtype=I32)
    tri = (idx[:, None] <= idx[None, :]).astype(BF16)
    rows = min(tq, SEARCH_ROWS)
    kernel = functools.partial(_select_kernel, tq=tq, tk=tk, nkb=nkb, causal=causal,
                               q_pos0=q_pos0, n_valid=n_valid, n_sel=n_sel, rows=rows)
    return pl.pallas_call(
        kernel,
        grid=(b, t_q // tq),
        in_specs=[
            pl.BlockSpec((1, tq, qi.shape[2]), lambda bi, qb: (bi, qb, 0)),
            pl.BlockSpec((1, n_k, ki.shape[2]), lambda bi, qb: (bi, 0, 0)),
            pl.BlockSpec((1, tq, wi.shape[2]), lambda bi, qb: (bi, qb, 0)),
            pl.BlockSpec((LANES, LANES), lambda bi, qb: (0, 0)),
        ],
        out_specs=pl.BlockSpec((1, nkb, tq, tk), lambda bi, qb: (bi, 0, qb, 0)),
        out_shape=jax.ShapeDtypeStruct((b, nkb, t_q, tk), BF16),
        scratch_shapes=[pltpu.VMEM((tq // rows, nkb + 1, rows, tk), I32),
                        pltpu.VMEM((tq // rows, nkb + 1, rows // 2, tk), I32)],
        compiler_params=_params(("parallel", "parallel")),
        name="select_bias",
    )(qi, ki, wi, tri)


def _attn_kernel(qb_ref, kb_ref, last_ref, q_ref, kt_ref, v_ref, bias_ref, o_ref, qs_ref, m_ref,
                 acc_ref, *, tq, tk):
    del qb_ref
    step = pl.program_id(1)
    rows = Q_PER_KV * tq

    @pl.when(kb_ref[step] == 0)
    def _init():
        m_ref[...] = jnp.full(m_ref.shape, MASKED_BIAS, F32)
        acc_ref[...] = jnp.zeros(acc_ref.shape, F32)
        for h in range(N_HEADS):
            g, r = divmod(h, Q_PER_KV)
            qh = q_ref[0, :, h * HEAD_DIM:(h + 1) * HEAD_DIM].astype(F32) * (HEAD_DIM ** -0.5)
            qs_ref[g, r * tq:(r + 1) * tq, :] = qh.astype(qs_ref.dtype)

    bias = bias_ref[0, 0].astype(F32)
    sub_rows = ATTN_HEADS_PER_DOT * tq
    for g in range(N_KV_HEADS):
        for sub in range(Q_PER_KV // ATTN_HEADS_PER_DOT):
            rs = slice(sub * sub_rows, (sub + 1) * sub_rows)
            s = jnp.dot(qs_ref[g, rs, :], kt_ref[0, g], preferred_element_type=F32)
            s = (s * LOG2_E).reshape(ATTN_HEADS_PER_DOT, tq, tk) + bias[None]
            s = s.reshape(sub_rows, tk)
            m_prev = m_ref[g, rs, :]
            m_blk = jnp.max(s, axis=1, keepdims=True)
            m_new = jnp.maximum(m_prev, jnp.broadcast_to(m_blk, m_prev.shape))
            alpha = jnp.exp2(m_prev - m_new)
            p = jnp.concatenate(
                [jnp.exp2(s[:, c * LANES:(c + 1) * LANES] - m_new) for c in range(tk // LANES)],
                axis=1).astype(BF16)
            acc_ref[g, rs, :] = alpha * acc_ref[g, rs, :] + jnp.dot(
                p, v_ref[0, g], preferred_element_type=F32)
            m_ref[g, rs, :] = m_new

    @pl.when(last_ref[step] == 1)
    def _finish():
        for h in range(N_HEADS):
            g, r = divmod(h, Q_PER_KV)
            a = acc_ref[g, r * tq:(r + 1) * tq, :]
            out = a[:, :HEAD_DIM] / a[:, HEAD_DIM:HEAD_DIM + 1]
            o_ref[0, :, h * HEAD_DIM:(h + 1) * HEAD_DIM] = out.astype(o_ref.dtype)


def masked_attention(q, k, v, bias, *, tq, causal):
    b, t_q, d_q = q.shape
    n_k = k.shape[1]
    nkb, tk = bias.shape[1], bias.shape[3]
    kt = jnp.transpose(k.reshape(b, n_k, N_KV_HEADS, HEAD_DIM), (0, 2, 3, 1))
    v4 = jnp.transpose(v.reshape(b, n_k, N_KV_HEADS, HEAD_DIM), (0, 2, 1, 3))
    v_ext = jnp.concatenate(
        [v4, jnp.ones((b, N_KV_HEADS, n_k, 1), BF16),
         jnp.zeros((b, N_KV_HEADS, n_k, LANES - HEAD_DIM - 1), BF16)], axis=-1)

    pairs = []
    for qb in range(t_q // tq):
        last = min(nkb - 1, ((qb + 1) * tq - 1) // tk) if causal else nkb - 1
        pairs += [(qb, kb, int(kb == last)) for kb in range(last + 1)]
    qb_ids, kb_ids, last_flags = (jnp.asarray(col, I32) for col in zip(*pairs))

    kernel = functools.partial(_attn_kernel, tq=tq, tk=tk)
    rows = Q_PER_KV * tq
    return pl.pallas_call(
        kernel,
        grid_spec=pltpu.PrefetchScalarGridSpec(
            num_scalar_prefetch=3,
            grid=(b, len(pairs)),
            in_specs=[
                pl.BlockSpec((1, tq, d_q), lambda bi, s, qbs, kbs, lf: (bi, qbs[s], 0)),
                pl.BlockSpec((1, N_KV_HEADS, HEAD_DIM, tk),
                             lambda bi, s, qbs, kbs, lf: (bi, 0, 0, kbs[s])),
                pl.BlockSpec((1, N_KV_HEADS, tk, LANES),
                             lambda bi, s, qbs, kbs, lf: (bi, 0, kbs[s], 0)),
                pl.BlockSpec((1, 1, tq, tk), lambda bi, s, qbs, kbs, lf: (bi, kbs[s], qbs[s], 0)),
            ],
            out_specs=pl.BlockSpec((1, tq, d_q), lambda bi, s, qbs, kbs, lf: (bi, qbs[s], 0)),
            scratch_shapes=[pltpu.VMEM((N_KV_HEADS, rows, HEAD_DIM), BF16),
                            pltpu.VMEM((N_KV_HEADS, rows, LANES), F32),
                            pltpu.VMEM((N_KV_HEADS, rows, LANES), F32)]),
        out_shape=jax.ShapeDtypeStruct((b, t_q, d_q), BF16),
        compiler_params=_params(("parallel", "arbitrary")),
        name="masked_attention",
    )(qb_ids, kb_ids, last_flags, q, kt, v_ext, bias)


def _conv_kernel(cur_ref, prev_ref, ctx_ref, w_ref, b_ref, g_ref, beta_ref, o_ref, buf_ref,
                 conv_ref, *, tm, cw):
    @pl.when(pl.program_id(1) == 0)
    def _first():
        buf_ref[0:CONV_HALO, :] = ctx_ref[0]

    @pl.when(pl.program_id(1) > 0)
    def _later():
        buf_ref[0:CONV_HALO, :] = prev_ref[0]

    buf_ref[CONV_HALO:CONV_HALO + tm, :] = cur_ref[0]
    lead = CONV_HALO - (CONV_WIDTH - 1)
    c_total = buf_ref.shape[1]
    rw = min(tm, 128)
    for c in range(c_total // cw):
        cs = slice(c * cw, (c + 1) * cw)
        for r in range(tm // rw):
            acc = jnp.zeros((rw, cw), F32) + b_ref[:, cs]
            for j in range(CONV_WIDTH):
                r0 = lead + j + r * rw
                acc = acc + w_ref[j:j + 1, cs] * buf_ref[r0:r0 + rw, cs]
            conv_ref[r * rw:(r + 1) * rw, cs] = acc
    x = conv_ref[...]
    mu = jnp.mean(x, axis=-1, keepdims=True)
    xc = x - mu
    y = xc * lax.rsqrt(jnp.mean(xc * xc, axis=-1, keepdims=True) + EPS)
    y = y * g_ref[...] + beta_ref[...]
    o_ref[0] = _silu(y).astype(o_ref.dtype)


def conv_ln_silu(y, ctx, w_dw, b_dw, ln_g, ln_b, *, tm):
    b, l, c = y.shape
    kernel = functools.partial(_conv_kernel, tm=tm, cw=LANES)
    vec = lambda a: a.reshape(1, c).astype(F32)
    halo_blocks = tm // CONV_HALO
    return pl.pallas_call(
        kernel,
        grid=(b, l // tm),
        in_specs=[
            pl.BlockSpec((1, tm, c), lambda bi, i: (bi, i, 0)),
            pl.BlockSpec((1, CONV_HALO, c),
                         lambda bi, i: (bi, jnp.maximum(i * halo_blocks - 1, 0), 0)),
            pl.BlockSpec((1, CONV_HALO, c), lambda bi, i: (bi, 0, 0)),
            pl.BlockSpec((CONV_WIDTH, c), lambda bi, i: (0, 0)),
            pl.BlockSpec((1, c), lambda bi, i: (0, 0)),
            pl.BlockSpec((1, c), lambda bi, i: (0, 0)),
            pl.BlockSpec((1, c), lambda bi, i: (0, 0)),
        ],
        out_specs=pl.BlockSpec((1, tm, c), lambda bi, i: (bi, i, 0)),
        out_shape=jax.ShapeDtypeStruct((b, l, c), BF16),
        scratch_shapes=[pltpu.VMEM((tm + CONV_HALO, c), F32), pltpu.VMEM((tm, c), F32)],
        compiler_params=_params(("parallel", "parallel")),
        name="conv_ln_silu",
    )(y, y, ctx, w_dw.astype(F32), vec(b_dw), vec(ln_g), vec(ln_b))


ROUTE_IDX1, ROUTE_IDX2, ROUTE_GATE1, ROUTE_GATE2, ROUTE_RANK1, ROUTE_RANK2 = range(6)


def _router_kernel(h_ref, g_ref, wr_ref, lower_ref, u_ref, route_ref, counts_ref, seen_ref, *,
                   n_experts):
    @pl.when(pl.program_id(0) == 0)
    def _init():
        seen_ref[...] = jnp.zeros(seen_ref.shape, F32)

    x = h_ref[...]
    u = x * lax.rsqrt(jnp.mean(x * x, axis=-1, keepdims=True) + EPS) * g_ref[...]
    u_ref[...] = u
    logits = jnp.dot(u.astype(BF16), wr_ref[...], preferred_element_type=F32)
    lane = lax.broadcasted_iota(I32, logits.shape, 1).astype(F32)
    logits = jnp.where(lane < n_experts, logits, -jnp.inf)
    top1 = jnp.max(logits, axis=1, keepdims=True)
    idx1 = jnp.min(jnp.where(logits == top1, lane, float(LANES)), axis=1, keepdims=True)
    rest = jnp.where(lane == idx1, -jnp.inf, logits)
    top2 = jnp.max(rest, axis=1, keepdims=True)
    idx2 = jnp.min(jnp.where(rest == top2, lane, float(LANES)), axis=1, keepdims=True)
    e2 = jnp.exp(top2 - top1)
    denom = 1.0 + e2

    hot1 = jnp.where(lane == idx1, 1.0, 0.0)
    hot2 = jnp.where(lane == idx2, 1.0, 0.0)
    hot = hot1 + hot2
    before = seen_ref[...] + jnp.dot(lower_ref[...], hot.astype(BF16), preferred_element_type=F32)
    rank1 = jnp.sum(hot1 * before, axis=1, keepdims=True)
    rank2 = jnp.sum(hot2 * before, axis=1, keepdims=True)
    seen_ref[...] += jnp.sum(hot, axis=0, keepdims=True)
    counts_ref[...] = seen_ref[...]

    route = jnp.zeros(logits.shape, F32)
    for slot, val in ((ROUTE_IDX1, idx1), (ROUTE_IDX2, idx2), (ROUTE_GATE1, 1.0 / denom),
                      (ROUTE_GATE2, e2 / denom), (ROUTE_RANK1, rank1), (ROUTE_RANK2, rank2)):
        route = jnp.where(lane == slot, val, route)
    route_ref[...] = route


def moe_router(h, gain, w_router_padded, n_experts, *, tm):
    m, d = h.shape
    row = jnp.arange(tm, dtype=I32)
    lower = (row[None, :] < row[:, None]).astype(BF16)
    kernel = functools.partial(_router_kernel, n_experts=n_experts)
    return pl.pallas_call(
        kernel,
        grid=(m // tm,),
        in_specs=[pl.BlockSpec((tm, d), lambda i: (i, 0)),
                  pl.BlockSpec((1, d), lambda i: (0, 0)),
                  pl.BlockSpec((d, LANES), lambda i: (0, 0)),
                  pl.BlockSpec((tm, tm), lambda i: (0, 0))],
        out_specs=[pl.BlockSpec((tm, d), lambda i: (i, 0)),
                   pl.BlockSpec((tm, LANES), lambda i: (i, 0)),
                   pl.BlockSpec((1, LANES), lambda i: (0, 0))],
        out_shape=[jax.ShapeDtypeStruct((m, d), F32), jax.ShapeDtypeStruct((m, LANES), F32),
                   jax.ShapeDtypeStruct((1, LANES), F32)],
        scratch_shapes=[pltpu.VMEM((1, LANES), F32)],
        compiler_params=_params(("arbitrary",)),
        name="moe_router",
    )(h, gain.reshape(1, d).astype(F32), w_router_padded, lower)


def _scatter_rows_kernel(d1_ref, d2_ref, u_ref, init_ref, xs_ref, sem):
    del init_ref
    tm = u_ref.shape[0]

    def row_copy(j, dest):
        return pltpu.make_async_copy(u_ref.at[pl.ds(j, 1)], xs_ref.at[pl.ds(dest, 1)], sem)

    def issue(j, carry):
        row_copy(j, d1_ref[0, 0, j]).start(priority=0)
        row_copy(j, d2_ref[0, 0, j]).start(priority=1)
        return carry

    lax.fori_loop(0, tm, issue, 0, unroll=DMA_LOOP_UNROLL)

    def drain(j, carry):
        row_copy(j, 0).wait()
        row_copy(j, 0).wait()
        return carry

    lax.fori_loop(0, tm, drain, 0, unroll=DMA_LOOP_UNROLL)


def scatter_rows(u, dest1, dest2, n_rows, *, tm):
    m, d = u.shape
    smem_idx = lambda a: a.reshape(m // tm, 1, tm)
    idx_spec = pl.BlockSpec((1, 1, tm), lambda i: (i, 0, 0), memory_space=pltpu.SMEM)
    return pl.pallas_call(
        _scatter_rows_kernel,
        grid=(m // tm,),
        in_specs=[idx_spec, idx_spec,
                  pl.BlockSpec((tm, d), lambda i: (i, 0)),
                  pl.BlockSpec(memory_space=pl.ANY)],
        out_specs=pl.BlockSpec(memory_space=pl.ANY),
        out_shape=jax.ShapeDtypeStruct((n_rows, d), u.dtype),
        scratch_shapes=[pltpu.SemaphoreType.DMA],
        input_output_aliases={3: 0},
        compiler_params=_params(("arbitrary",)),
        name="moe_scatter_rows",
    )(smem_idx(dest1), smem_idx(dest2), u, jnp.zeros((n_rows, d), u.dtype))


def _combine_kernel(d1_ref, d2_ref, rows_ref, route_ref, h_ref, o_ref, buf_ref, sem):
    tm = h_ref.shape[0]

    def row_copy(slot, j, src):
        return pltpu.make_async_copy(rows_ref.at[pl.ds(src, 1)], buf_ref.at[slot, pl.ds(j, 1)], sem)

    def issue(j, carry):
        row_copy(0, j, d1_ref[0, 0, j]).start(priority=0)
        row_copy(1, j, d2_ref[0, 0, j]).start(priority=1)
        return carry

    lax.fori_loop(0, tm, issue, 0, unroll=DMA_LOOP_UNROLL)

    def drain(j, carry):
        row_copy(0, j, 0).wait()
        row_copy(1, j, 0).wait()
        return carry

    lax.fori_loop(0, tm, drain, 0, unroll=DMA_LOOP_UNROLL)
    route = route_ref[...]
    g1 = route[:, ROUTE_GATE1:ROUTE_GATE1 + 1]
    g2 = route[:, ROUTE_GATE2:ROUTE_GATE2 + 1]
    o_ref[...] = h_ref[...] + (g1 * buf_ref[0] + g2 * buf_ref[1])


def combine_rows(rows, dest1, dest2, route, h, *, tm):
    m, d = h.shape
    smem_idx = lambda a: a.reshape(m // tm, 1, tm)
    idx_spec = pl.BlockSpec((1, 1, tm), lambda i: (i, 0, 0), memory_space=pltpu.SMEM)
    return pl.pallas_call(
        _combine_kernel,
        grid=(m // tm,),
        in_specs=[idx_spec, idx_spec,
                  pl.BlockSpec(memory_space=pl.ANY),
                  pl.BlockSpec((tm, LANES), lambda i: (i, 0)),
                  pl.BlockSpec((tm, d), lambda i: (i, 0))],
        out_specs=pl.BlockSpec((tm, d), lambda i: (i, 0)),
        out_shape=jax.ShapeDtypeStruct((m, d), F32),
        scratch_shapes=[pltpu.VMEM((2, tm, d), F32), pltpu.SemaphoreType.DMA],
        compiler_params=_params(("arbitrary",)),
        name="moe_combine_rows",
    )(smem_idx(dest1), smem_idx(dest2), rows, route, h)


def _expert_up_kernel(blk_expert_ref, n_used_ref, xs_ref, wg_ref, wu_ref, o_ref, stage_ref):
    del blk_expert_ref
    i = pl.program_id(0)

    @pl.when(pl.program_id(1) == 0)
    def _stage():
        stage_ref[...] = xs_ref[...].astype(BF16)

    @pl.when(i < n_used_ref[0])
    def _compute():
        u = stage_ref[...]
        a = jnp.dot(u, wg_ref[0], preferred_element_type=F32)
        b = jnp.dot(u, wu_ref[0], preferred_element_type=F32)
        o_ref[...] = (_silu(a) * b).astype(o_ref.dtype)

    @pl.when(i >= n_used_ref[0])
    def _skip():
        o_ref[...] = jnp.zeros(o_ref.shape, o_ref.dtype)


def expert_up(xs, w_gate, w_up, blk_expert, n_used, *, blk, tn):
    n_rows, d = xs.shape
    f = w_gate.shape[2]
    return pl.pallas_call(
        _expert_up_kernel,
        grid_spec=pltpu.PrefetchScalarGridSpec(
            num_scalar_prefetch=2,
            grid=(n_rows // blk, f // tn),
            in_specs=[pl.BlockSpec((blk, d), lambda i, j, be, nu: (i, 0)),
                      pl.BlockSpec((1, d, tn), lambda i, j, be, nu: (be[i], 0, j)),
                      pl.BlockSpec((1, d, tn), lambda i, j, be, nu: (be[i], 0, j))],
            out_specs=pl.BlockSpec((blk, tn), lambda i, j, be, nu: (i, j)),
            scratch_shapes=[pltpu.VMEM((blk, d), BF16)]),
        out_shape=jax.ShapeDtypeStruct((n_rows, f), BF16),
        compiler_params=_params(("parallel", "arbitrary")),
        name="expert_up",
    )(blk_expert, n_used, xs, w_gate, w_up)


def _expert_down_kernel(blk_expert_ref, n_used_ref, hb_ref, wd_ref, o_ref):
    del blk_expert_ref
    i = pl.program_id(0)

    @pl.when(i < n_used_ref[0])
    def _compute():
        o_ref[...] = jnp.dot(hb_ref[...], wd_ref[0], preferred_element_type=F32)

    @pl.when(i >= n_used_ref[0])
    def _skip():
        o_ref[...] = jnp.zeros(o_ref.shape, o_ref.dtype)


def expert_down(hb, w_down, blk_expert, n_used, *, blk, tn):
    n_rows, f = hb.shape
    d = w_down.shape[2]
    return pl.pallas_call(
        _expert_down_kernel,
        grid_spec=pltpu.PrefetchScalarGridSpec(
            num_scalar_prefetch=2,
            grid=(n_rows // blk, d // tn),
            in_specs=[pl.BlockSpec((blk, f), lambda i, j, be, nu: (i, 0)),
                      pl.BlockSpec((1, f, tn), lambda i, j, be, nu: (be[i], 0, j))],
            out_specs=pl.BlockSpec((blk, tn), lambda i, j, be, nu: (i, j))),
        out_shape=jax.ShapeDtypeStruct((n_rows, d), F32),
        compiler_params=_params(("parallel", "parallel")),
        name="expert_down",
    )(blk_expert, n_used, hb, w_down)


def moe_swiglu(h, gain, w_router, w_gate, w_up, w_down):
    m, d = h.shape
    n_experts = w_router.shape[1]
    blk = MOE_ROW_BLOCK if m >= 8 * MOE_ROW_BLOCK else LANES
    tm = _pick_tile(m, 256)
    w_router_padded = jnp.concatenate([w_router, jnp.zeros((d, LANES - n_experts), BF16)], axis=1)
    u, route, counts = moe_router(h, gain, w_router_padded, n_experts, tm=_pick_tile(m, 512))

    counts = counts[0, :n_experts].astype(I32)
    padded = (counts + blk - 1) // blk * blk
    pad_end = jnp.cumsum(padded)
    pad_start = pad_end - padded
    idx1 = route[:, ROUTE_IDX1].astype(I32)
    idx2 = route[:, ROUTE_IDX2].astype(I32)
    dest1 = pad_start[idx1] + route[:, ROUTE_RANK1].astype(I32)
    dest2 = pad_start[idx2] + route[:, ROUTE_RANK2].astype(I32)
    n_blocks = -(-(TOP_K_EXPERTS * m + n_experts * (blk - 1)) // blk)
    blk_start = jnp.arange(n_blocks, dtype=I32) * blk
    blk_expert = jnp.minimum(jnp.searchsorted(pad_end, blk_start, side="right"),
                             n_experts - 1).astype(I32)
    n_used = (pad_end[-1:] // blk).astype(I32)

    xs = scatter_rows(u, dest1, dest2, n_blocks * blk, tm=tm)
    hb = expert_up(xs, w_gate, w_up, blk_expert, n_used, blk=blk, tn=_pick_tile(w_gate.shape[2], 1792))
    rows = expert_down(hb, w_down, blk_expert, n_used, blk=blk, tn=_pick_tile(d, 1024))
    return combine_rows(rows, dest1, dest2, route, h, tm=tm)


def _split_w_in(w_in_layer, d_model):
    attn_dim = N_HEADS * HEAD_DIM
    kv_dim = N_KV_HEADS * HEAD_DIM
    sizes = (attn_dim, kv_dim, kv_dim, IDX_HEADS * IDX_DIM, IDX_DIM, IDX_HEADS, d_model, d_model,
             d_model, d_model)
    parts, off = [], 0
    for s in sizes:
        parts.append(w_in_layer[:, off:off + s])
        off += s
    assert off == w_in_layer.shape[1]
    return parts


def _layer_stack(x, p, weights, cache, *, past):
    bsz, length, d_model = x.shape
    m = bsz * length
    kv_dim = N_KV_HEADS * HEAD_DIM
    pos = past + jnp.arange(length, dtype=I32)
    row_pos = jnp.tile(pos, bsz)
    depth = weights["w_in"].shape[0]

    cos_q, sin_q = _rope_tables(row_pos, 512)
    cos_k, sin_k = _rope_tables(row_pos, kv_dim)
    cos_i, sin_i = _rope_tables(row_pos, LANES, rotary_width=IDX_DIM)
    ones_q = _block_ones(512)
    ones_k = _block_ones(kv_dim)

    n_keys = past + length
    n_sel = min(TOPK_MAX, n_keys // 4)
    if past == 0:
        tq_sel, tk, tq_attn = min(256, length), min(512, length), min(256, length)
        n_keys_pad = n_keys
    else:
        tq_sel = tq_attn = length
        tk = 384
        n_keys_pad = -(-n_keys // tk) * tk

    h = x.reshape(m, d_model)
    ks, vs, kis, convs = [], [], [], []
    for i in range(depth):
        wq, wk, wv, wqi, wki, wwi, wglu_a, wglu_g, wga, wgb = _split_w_in(weights["w_in"][i], d_model)
        w_kiwi = jnp.concatenate(
            [wki, wwi, jnp.zeros((d_model, LANES - IDX_DIM - IDX_HEADS), BF16)], axis=1)
        g_mix = weights["g_mix"][i]

        qn_gain = jnp.tile(weights["q_norm_g"][i].astype(F32), 512 // HEAD_DIM).reshape(1, 512)
        kn_gain = jnp.tile(weights["k_norm_g"][i].astype(F32), kv_dim // HEAD_DIM).reshape(1, kv_dim)

        def norm_rope(accs, tiled, rows, consts):
            y = _head_rms_norm(accs[0], consts[0], consts[1])
            y = _rope(y, rows[0], rows[1])
            return (y, y)

        u = rms_norm_rows(h, g_mix)
        (q_bf,) = fused_mm("proj_q", [u], [wq], [0],
                           lambda a, t, r, c: (_rope(_head_rms_norm(a[0], c[0], c[1]), r[0], r[1]),),
                           [BF16], rows=[cos_q, sin_q], consts=[ones_q, qn_gain], tn=512)
        k_f32, k_bf = fused_mm("proj_k", [u], [wk], [0], norm_rope, [F32, BF16],
                               rows=[cos_k, sin_k], consts=[ones_k, kn_gain])
        v_f32, v_bf = fused_mm("proj_v", [u], [wv], [0], lambda a, t, r, c: (a[0], a[0]),
                               [F32, BF16])
        (qi_bf,) = fused_mm("proj_qi", [u], [wqi], [0],
                            lambda a, t, r, c: (_rope(a[0], r[0], r[1]),),
                            [BF16], rows=[cos_q, sin_q], tn=512)
        (kiwi,) = fused_mm("proj_kiwi", [u], [w_kiwi], [0],
                           lambda a, t, r, c: (_rope(a[0], r[0], r[1]),),
                           [F32], rows=[cos_i, sin_i])
        (glu,) = fused_mm("proj_glu", [u], [wglu_a, wglu_g], [0, 0],
                          lambda a, t, r, c: (a[0] * _sigmoid(a[1]),), [F32])
        gate_a, gate_b = fused_mm("proj_gates", [u], [wga, wgb], [0, 0],
                                  lambda a, t, r, c: (_sigmoid(a[0]), _sigmoid(a[1])),
                                  [F32, F32])

        ki_f32 = kiwi[:, :IDX_DIM]
        wi = kiwi[:, IDX_DIM:IDX_DIM + IDX_HEADS]

        k3 = k_bf.reshape(bsz, length, kv_dim)
        v3 = v_bf.reshape(bsz, length, kv_dim)
        ki3 = ki_f32.astype(BF16).reshape(bsz, length, IDX_DIM)
        if past:
            pad = n_keys_pad - n_keys
            k3 = jnp.concatenate([cache["k"][i].reshape(bsz, past, kv_dim).astype(BF16), k3,
                                  jnp.zeros((bsz, pad, kv_dim), BF16)], axis=1)
            v3 = jnp.concatenate([cache["v"][i].reshape(bsz, past, kv_dim).astype(BF16), v3,
                                  jnp.zeros((bsz, pad, kv_dim), BF16)], axis=1)
            ki3 = jnp.concatenate([cache["kidx"][i].astype(BF16), ki3,
                                   jnp.zeros((bsz, pad, IDX_DIM), BF16)], axis=1)
        bias = select_bias(qi_bf.reshape(bsz, length, -1), ki3, wi.reshape(bsz, length, IDX_HEADS),
                           tq=tq_sel, tk=tk, causal=(past == 0), q_pos0=past, n_valid=n_keys,
                           n_sel=n_sel)
        attn = masked_attention(q_bf.reshape(bsz, length, -1), k3, v3, bias, tq=tq_attn,
                                causal=(past == 0))
        attn = attn.reshape(m, -1)

        glu3 = glu.reshape(bsz, length, d_model)
        if past:
            ctx = cache["conv"][i]
        else:
            ctx = jnp.zeros((bsz, CONV_WIDTH - 1, d_model), F32)
        lead = jnp.zeros((bsz, CONV_HALO - (CONV_WIDTH - 1), d_model), F32)
        ctx32 = jnp.concatenate([lead, ctx], axis=1)
        l_pad = -(-length // CONV_HALO) * CONV_HALO
        if l_pad == length:
            y_conv = glu3
        else:
            y_conv = jnp.concatenate(
                [glu3, jnp.zeros((bsz, l_pad - length, d_model), F32)], axis=1)
        if length >= CONV_WIDTH - 1:
            conv_state = glu3[:, length - (CONV_WIDTH - 1):]
        else:
            conv_state = jnp.concatenate([ctx, glu3], axis=1)[:, -(CONV_WIDTH - 1):]
        z = conv_ln_silu(y_conv, ctx32, weights["w_dw"][i], weights["b_dw"][i],
                         weights["conv_ln_g"][i], weights["conv_ln_b"][i], tm=min(256, l_pad))
        z = z[:, :length].reshape(m, d_model)

        (merged,) = fused_mm("mix", [attn, z], [weights["w_o_attn"][i], weights["w_o_conv"][i]],
                             [0, 1], lambda a, t, r, c: (t[0] * a[0] + t[1] * a[1],), [BF16],
                             tiled=[gate_a, gate_b])
        (h,) = fused_mm("out_proj", [merged], [weights["w_out"][i]], [0],
                        lambda a, t, r, c: (t[0] + a[0],), [F32], tiled=[h])

        j = i // 2
        if i % 2 == 0:
            (hb,) = fused_mm("ffn_up", [h], [weights["ffn_w_gate"][j], weights["ffn_w_up"][j]],
                             [0, 0], lambda a, t, r, c: (_silu(a[0]) * a[1],), [BF16],
                             gains=[weights["g_ffn"][i]], tn=1408)
            (h,) = fused_mm("ffn_down", [hb], [weights["ffn_w_down"][j]], [0],
                            lambda a, t, r, c: (t[0] + a[0],), [F32], tiled=[h])
        else:
            h = moe_swiglu(h, weights["g_ffn"][i], weights["moe_router"][j],
                           weights["moe_w_gate"][j], weights["moe_w_up"][j],
                           weights["moe_w_down"][j])

        (h,) = fused_mm("ple", [h, p[i].reshape(m, -1)],
                        [weights["w_ple_gate"][i], weights["w_ple_proj"][i]],
                        [0, 1], lambda a, t, r, c: (t[0] + _sigmoid(a[0]) * a[1],), [F32],
                        gains=[weights["g_ple"][i], None], tiled=[h])

        ks.append(k_f32.reshape(bsz, length, N_KV_HEADS, HEAD_DIM))
        vs.append(v_f32.reshape(bsz, length, N_KV_HEADS, HEAD_DIM))
        kis.append(ki_f32.reshape(bsz, length, IDX_DIM))
        convs.append(conv_state)
    return (h.reshape(bsz, length, d_model), jnp.stack(ks), jnp.stack(vs), jnp.stack(kis),
            jnp.stack(convs))


def kernel(x_prompt, x_sample, cache_k, cache_v, cache_kidx, state_conv, p_prompt, p_sample, w_in, q_norm_g, k_norm_g, w_dw, b_dw, conv_ln_g, conv_ln_b, w_o_attn, w_o_conv, w_out, g_mix, g_ffn, ffn_w_gate, ffn_w_up, ffn_w_down, moe_router, moe_w_gate, moe_w_up, moe_w_down, g_ple, w_ple_gate, w_ple_proj):
    bf = lambda a: a.astype(BF16)
    weights = dict(
        w_in=bf(w_in), q_norm_g=q_norm_g, k_norm_g=k_norm_g, w_dw=w_dw, b_dw=b_dw,
        conv_ln_g=conv_ln_g, conv_ln_b=conv_ln_b, w_o_attn=bf(w_o_attn), w_o_conv=bf(w_o_conv),
        w_out=bf(w_out), g_mix=g_mix, g_ffn=g_ffn, ffn_w_gate=bf(ffn_w_gate),
        ffn_w_up=bf(ffn_w_up), ffn_w_down=bf(ffn_w_down), moe_router=bf(moe_router),
        moe_w_gate=bf(moe_w_gate), moe_w_up=bf(moe_w_up), moe_w_down=bf(moe_w_down), g_ple=g_ple,
        w_ple_gate=bf(w_ple_gate), w_ple_proj=bf(w_ple_proj))
    cache = dict(k=cache_k, v=cache_v, kidx=cache_kidx, conv=state_conv)
    past_len = cache_k.shape[2]
    y_p, k_p, v_p, ki_p, conv_p = _layer_stack(x_prompt, p_prompt, weights, None, past=0)
    y_s, k_s, v_s, ki_s, conv_s = _layer_stack(x_sample, p_sample, weights, cache, past=past_len)
    return (y_p, y_s, k_p, v_p, ki_p, conv_p, k_s, v_s, ki_s, conv_s)
```

```python
import functools

import jax
import jax.numpy as jnp
from jax import lax
from jax.experimental import pallas as pl
from jax.experimental.pallas import tpu as pltpu

F32 = jnp.float32
BF16 = jnp.bfloat16
I32 = jnp.int32

CHUNK = 64
CHUNK_SHIFT = CHUNK.bit_length() - 1
assert 1 << CHUNK_SHIFT == CHUNK
N_HEADS = 16
HEAD_DIM = 64
N_KV_HEADS = 4
Q_PER_KV = N_HEADS // N_KV_HEADS
IDX_HEADS = 8
IDX_DIM = 64
TOPK_MAX = 256
ROPE_THETA = 10000.0
CONV_WIDTH = 31
TOP_K_EXPERTS = 2
EPS = 1e-6

LANES = 128
CONV_HALO = 32
MASKED_BIAS = -1e30
LOG2_E = 1.4426950408889634
INT_MIN = -2 ** 31
KEY_NEG_INF = -2139095041
SEARCH_ROWS = 128
EMIT_ROWS = 128
TOP_BITS = 15
DMA_LOOP_UNROLL = 8
ATTN_HEADS_PER_DOT = Q_PER_KV
MOE_ROW_BLOCK = 512
VMEM_LIMIT = 52 * 1024 * 1024


def _params(semantics):
    return pltpu.CompilerParams(dimension_semantics=semantics, vmem_limit_bytes=VMEM_LIMIT)


def _pick_tile(n, pref):
    if n <= pref:
        return n
    best = None
    for t in range(LANES, pref + 1, LANES):
        if n % t == 0:
            best = t
    assert best is not None, (n, pref)
    return best


def _rms_norm_kernel(x_ref, g_ref, o_ref):
    x = x_ref[...]
    y = x * lax.rsqrt(jnp.mean(x * x, axis=-1, keepdims=True) + EPS) * g_ref[...]
    o_ref[...] = y.astype(o_ref.dtype)


def rms_norm_rows(x, gain, *, tm=512):
    m, d = x.shape
    tm = _pick_tile(m, tm) if m % LANES == 0 else m
    return pl.pallas_call(
        _rms_norm_kernel,
        grid=(m // tm,),
        in_specs=[pl.BlockSpec((tm, d), lambda i: (i, 0)), pl.BlockSpec((1, d), lambda i: (0, 0))],
        out_specs=pl.BlockSpec((tm, d), lambda i: (i, 0)),
        out_shape=jax.ShapeDtypeStruct((m, d), BF16),
        compiler_params=_params(("parallel",)),
        name="rms_norm_rows",
    )(x, gain.reshape(1, d).astype(F32))


def _fused_mm_kernel(*refs, n_lhs, has_gain, staged, pair_lhs, n_tiled, n_rows, n_consts, n_out,
                     epilogue):
    it = iter(refs)
    lhs_refs = [next(it) for _ in range(n_lhs)]
    gain_refs = [next(it) if has_gain[a] else None for a in range(n_lhs)]
    rhs_refs = [next(it) for _ in range(len(pair_lhs))]
    tiled_refs = [next(it) for _ in range(n_tiled)]
    row_refs = [next(it) for _ in range(n_rows)]
    const_refs = [next(it) for _ in range(n_consts)]
    out_refs = [next(it) for _ in range(n_out)]
    operand_refs = [next(it) if staged[a] else lhs_refs[a] for a in range(n_lhs)]

    if any(staged):
        @pl.when(pl.program_id(1) == 0)
        def _stage():
            for a in range(n_lhs):
                if not staged[a]:
                    continue
                x = lhs_refs[a][...].astype(F32)
                if has_gain[a]:
                    x = x * lax.rsqrt(jnp.mean(x * x, axis=-1, keepdims=True) + EPS)
                    x = x * gain_refs[a][...]
                operand_refs[a][...] = x.astype(BF16)

    accs = [jnp.dot(operand_refs[pair_lhs[r]][...], rhs_refs[r][...], preferred_element_type=F32)
            for r in range(len(pair_lhs))]
    outs = epilogue(accs, [t[...] for t in tiled_refs], [t[...] for t in row_refs],
                    [t[...] for t in const_refs])
    for o_ref, o in zip(out_refs, outs):
        o_ref[...] = o.astype(o_ref.dtype)


def fused_mm(name, lhs, rhs, pair_lhs, epilogue, out_dtypes, *, gains=None, tiled=(), rows=(),
             consts=(), tm=1024, tn=1024):
    m = lhs[0].shape[0]
    n = rhs[0].shape[1]
    tm = _pick_tile(m, tm) if m % LANES == 0 else m
    tn = _pick_tile(n, tn)
    n_lhs = len(lhs)
    gains = list(gains) if gains is not None else [None] * n_lhs
    has_gain = tuple(g is not None for g in gains)
    staged = tuple(has_gain[a] or lhs[a].dtype != BF16 for a in range(n_lhs))

    in_specs, args = [], []
    for x in lhs:
        in_specs.append(pl.BlockSpec((tm, x.shape[1]), lambda i, j: (i, 0)))
        args.append(x)
    for g in gains:
        if g is not None:
            g2 = g.reshape(1, -1).astype(F32)
            in_specs.append(pl.BlockSpec(g2.shape, lambda i, j: (0, 0)))
            args.append(g2)
    for w in rhs:
        in_specs.append(pl.BlockSpec((w.shape[0], tn), lambda i, j: (0, j)))
        args.append(w)
    for t in tiled:
        in_specs.append(pl.BlockSpec((tm, tn), lambda i, j: (i, j)))
        args.append(t)
    for t in rows:
        in_specs.append(pl.BlockSpec((tm, t.shape[1]), lambda i, j: (i, 0)))
        args.append(t)
    for c in consts:
        in_specs.append(pl.BlockSpec(c.shape, lambda i, j: (0, 0)))
        args.append(c)

    kernel = functools.partial(
        _fused_mm_kernel, n_lhs=n_lhs, has_gain=has_gain, staged=staged, pair_lhs=tuple(pair_lhs),
        n_tiled=len(tiled), n_rows=len(rows), n_consts=len(consts), n_out=len(out_dtypes),
        epilogue=epilogue)
    outs = pl.pallas_call(
        kernel,
        grid=(m // tm, n // tn),
        in_specs=in_specs,
        out_specs=[pl.BlockSpec((tm, tn), lambda i, j: (i, j)) for _ in out_dtypes],
        out_shape=[jax.ShapeDtypeStruct((m, n), dt) for dt in out_dtypes],
        scratch_shapes=[pltpu.VMEM((tm, x.shape[1]), BF16) for a, x in enumerate(lhs) if staged[a]],
        compiler_params=_params(("parallel", "arbitrary")),
        name=name,
    )(*args)
    return outs


def _head_rms_norm(x, block_ones, gain):
    x2 = x * x
    hi = x2.astype(BF16)
    lo = (x2 - hi.astype(F32)).astype(BF16)
    ss = (jnp.dot(hi, block_ones, preferred_element_type=F32)
          + jnp.dot(lo, block_ones, preferred_element_type=F32))
    return x * lax.rsqrt(ss * (1.0 / HEAD_DIM) + EPS) * gain


def _rope(x, cos, sin_signed):
    n = x.shape[1]
    lane = lax.broadcasted_iota(I32, x.shape, 1)
    first_half = (lane & (HEAD_DIM // 2)) == 0
    partner = jnp.where(first_half, pltpu.roll(x, n - HEAD_DIM // 2, 1),
                        pltpu.roll(x, HEAD_DIM // 2, 1))
    return x * cos + partner * sin_signed


def _sigmoid(x):
    return 1.0 / (1.0 + jnp.exp(-x))


def _silu(x):
    return x * _sigmoid(x)


def _rope_tables(pos, width, rotary_width=None):
    half = HEAD_DIM // 2
    inv = ROPE_THETA ** (-jnp.arange(half, dtype=F32) / half)
    ang = pos.astype(F32)[:, None] * inv[None, :]
    cos = jnp.cos(ang)
    sin = jnp.sin(ang)
    cos64 = jnp.concatenate([cos, cos], axis=-1)
    sin64 = jnp.concatenate([-sin, sin], axis=-1)
    rotary_width = width if rotary_width is None else rotary_width
    reps = rotary_width // HEAD_DIM
    cos_t = jnp.tile(cos64, (1, reps))
    sin_t = jnp.tile(sin64, (1, reps))
    if rotary_width < width:
        pad = width - rotary_width
        cos_t = jnp.concatenate([cos_t, jnp.ones((pos.shape[0], pad), F32)], axis=-1)
        sin_t = jnp.concatenate([sin_t, jnp.zeros((pos.shape[0], pad), F32)], axis=-1)
    return cos_t, sin_t


def _block_ones(width):
    head = jnp.arange(width, dtype=I32) // HEAD_DIM
    return (head[:, None] == head[None, :]).astype(BF16)


def _float_key(x):
    b = lax.bitcast_convert_type(x, I32)
    return jnp.where(b >= 0, b, b ^ jnp.int32(0x7FFFFFFF))


def _select_kernel(qi_ref, ki_ref, wi_ref, tri_ref, bias_ref, key_ref, top_ref, *, tq, tk, nkb,
                   causal, q_pos0, n_valid, n_sel, rows):
    qb = pl.program_id(1)
    n_chunks = tk // LANES
    n_sub = tq // rows
    half = rows // 2
    if causal:
        n_act = jnp.minimum(nkb, ((qb + 1) * tq + tk - 1) // tk)
    else:
        n_act = nkb
    q_pos = q_pos0 + qb * tq + lax.broadcasted_iota(I32, (tq, tk), 0)
    q_chunk = lax.shift_right_logical(q_pos, CHUNK_SHIFT)
    w = (wi_ref[0] * (IDX_HEADS ** -0.5)) * (IDX_DIM ** -0.5)
    qi = qi_ref[0]

    def score_block(kb, carry):
        start = pl.multiple_of(kb * tk, tk)
        ki = ki_ref[0, pl.ds(start, tk), :]
        s = jnp.zeros((tq, tk), F32)
        for h in range(IDX_HEADS):
            r = lax.dot_general(qi[:, h * IDX_DIM:(h + 1) * IDX_DIM], ki,
                                (((1,), (1,)), ((), ())), preferred_element_type=F32)
            s = s + jnp.maximum(r, 0.0) * w[:, h:h + 1]
        k_pos = start + lax.broadcasted_iota(I32, (tq, tk), 1)
        admissible = (lax.shift_right_logical(k_pos, CHUNK_SHIFT) <= q_chunk) & (k_pos < n_valid)
        key = _float_key(jnp.where(admissible, s, -jnp.inf))
        top = lax.shift_right_logical(key ^ jnp.int32(INT_MIN), 32 - TOP_BITS)
        for sub in range(n_sub):
            key_ref[sub, kb] = key[sub * rows:(sub + 1) * rows]
            lo_rows = top[sub * rows:sub * rows + half]
            hi_rows = top[sub * rows + half:(sub + 1) * rows]
            top_ref[sub, kb] = lo_rows | jnp.left_shift(hi_rows, 16)
        return carry

    lax.fori_loop(0, n_act, score_block, 0)
    for sub in range(n_sub):
        key_ref[sub, n_act] = jnp.full((rows, tk), INT_MIN, I32)
        top_ref[sub, n_act] = jnp.zeros((half, tk), I32)
    for sub in range(n_sub):
        _select_rows(key_ref.at[sub], top_ref.at[sub], tri_ref,
                     bias_ref.at[0, :, sub * rows:(sub + 1) * rows, :],
                     n_act=n_act, rows=rows, tk=tk, n_sel=n_sel)

    def fill_block(kb, carry):
        bias_ref[0, kb] = jnp.full((tq, tk), MASKED_BIAS, bias_ref.dtype)
        return carry

    lax.fori_loop(n_act, nkb, fill_block, 0)


def _select_rows(key_ref, top_ref, tri_ref, out_ref, *, n_act, rows, tk, n_sel):
    tq = rows
    half = rows // 2
    n_chunks = tk // LANES

    ones_mat = jnp.ones((LANES, LANES), BF16)

    def count(hit_fn):
        def pair(p, acc):
            for kb in (2 * p, 2 * p + 1):
                for c in range(n_chunks):
                    acc = acc + jnp.where(hit_fn(key_ref[kb, :, c * LANES:(c + 1) * LANES]), 1, 0)
            return acc
        acc = lax.fori_loop(0, (n_act + 1) // 2, pair, jnp.zeros((tq, LANES), I32))
        return jnp.dot(acc.astype(F32).astype(BF16), ones_mat, preferred_element_type=F32)

    n_counted = (tk * 2) * ((n_act + 1) // 2)
    guard = 1 << TOP_BITS

    def top_step(i, prefix):
        cand = prefix | jnp.left_shift(jnp.int32(1), TOP_BITS - 1 - i)
        below = (cand - 1) | guard
        cw = below[:half] | jnp.left_shift(below[half:], 16)

        def pair(p, acc):
            for kb in (2 * p, 2 * p + 1):
                for c in range(n_chunks):
                    d = cw - top_ref[kb, :, c * LANES:(c + 1) * LANES]
                    acc = acc + (lax.shift_right_logical(d, TOP_BITS) & 0x00010001)
            return acc
        acc = lax.fori_loop(0, (n_act + 1) // 2, pair, jnp.zeros((half, LANES), I32))
        le = jnp.concatenate([acc & 0xFFFF, lax.shift_right_logical(acc, 16)], axis=0)
        n_le = jnp.dot(le.astype(F32).astype(BF16), ones_mat, preferred_element_type=F32)
        return jnp.where(n_counted - n_le >= n_sel, cand, prefix)

    top_prefix = lax.fori_loop(0, TOP_BITS, top_step, jnp.zeros((tq, LANES), I32))

    def bit_step(i, prefix):
        cand = prefix | jnp.left_shift(jnp.int32(1), 31 - i)
        cand_signed = cand ^ jnp.int32(INT_MIN)
        cnt = count(lambda k: k >= cand_signed)
        return jnp.where(cnt >= n_sel, cand, prefix)

    prefix = lax.fori_loop(TOP_BITS, 32, bit_step, jnp.left_shift(top_prefix, 32 - TOP_BITS))
    tau_rep = prefix ^ jnp.int32(INT_MIN)
    n_gt = count(lambda k: k > tau_rep)
    n_eq = count(lambda k: k == tau_rep)
    need_rep = n_sel - n_gt

    er = min(rows, EMIT_ROWS)
    for g in range(rows // er):
        rs = slice(g * er, (g + 1) * er)
        tau = tau_rep[rs, :1]
        need = need_rep[rs, :1]
        all_ties_taken = jnp.max(jnp.where(need_rep[rs] >= n_eq[rs], 0.0, 1.0)) == 0.0

        def emit_all_ties(kb, carry, rs=rs, tau=tau):
            key = key_ref[kb, rs, :]
            sel = (key >= tau) & (key > KEY_NEG_INF)
            out_ref[kb, rs, :] = jnp.where(sel, 0.0, MASKED_BIAS).astype(out_ref.dtype)
            return carry

        def emit_ranked(kb, seen, rs=rs, tau=tau, need=need):
            for c in range(n_chunks):
                cs = slice(c * LANES, (c + 1) * LANES)
                key = key_ref[kb, rs, cs]
                eq = key == tau
                rank = seen + jnp.dot(jnp.where(eq, 1.0, 0.0).astype(BF16), tri_ref[...],
                                      preferred_element_type=F32)
                sel = ((key > tau) | (eq & (rank <= need))) & (key > KEY_NEG_INF)
                out_ref[kb, rs, cs] = jnp.where(sel, 0.0, MASKED_BIAS).astype(out_ref.dtype)
                seen = rank[:, LANES - 1:LANES]
            return seen

        @pl.when(all_ties_taken)
        def _fast(emit_all_ties=emit_all_ties):
            lax.fori_loop(0, n_act, emit_all_ties, 0)

        @pl.when(jnp.logical_not(all_ties_taken))
        def _ranked(emit_ranked=emit_ranked):
            lax.fori_loop(0, n_act, emit_ranked, jnp.zeros((er, 1), F32))


def select_bias(qi, ki, wi, *, tq, tk, causal, q_pos0, n_valid, n_sel):
    b, t_q, _ = qi.shape
    n_k = ki.shape[1]
    nkb = n_k // tk
    idx = jnp.arange(LANES, dtype=I32)
    tri = (idx[:, None] <= idx[None, :]).astype(BF16)
    rows = min(tq, SEARCH_ROWS)
    kernel = functools.partial(_select_kernel, tq=tq, tk=tk, nkb=nkb, causal=causal,
                               q_pos0=q_pos0, n_valid=n_valid, n_sel=n_sel, rows=rows)
    return pl.pallas_call(
        kernel,
        grid=(b, t_q // tq),
        in_specs=[
            pl.BlockSpec((1, tq, qi.shape[2]), lambda bi, qb: (bi, qb, 0)),
            pl.BlockSpec((1, n_k, ki.shape[2]), lambda bi, qb: (bi, 0, 0)),
            pl.BlockSpec((1, tq, wi.shape[2]), lambda bi, qb: (bi, qb, 0)),
            pl.BlockSpec((LANES, LANES), lambda bi, qb: (0, 0)),
        ],
        out_specs=pl.BlockSpec((1, nkb, tq, tk), lambda bi, qb: (bi, 0, qb, 0)),
        out_shape=jax.ShapeDtypeStruct((b, nkb, t_q, tk), BF16),
        scratch_shapes=[pltpu.VMEM((tq // rows, nkb + 1, rows, tk), I32),
                        pltpu.VMEM((tq // rows, nkb + 1, rows // 2, tk), I32)],
        compiler_params=_params(("parallel", "parallel")),
        name="select_bias",
    )(qi, ki, wi, tri)


def _attn_kernel(qb_ref, kb_ref, last_ref, q_ref, kt_ref, v_ref, bias_ref, o_ref, qs_ref, m_ref,
                 acc_ref, *, tq, tk):
    del qb_ref
    step = pl.program_id(1)
    rows = Q_PER_KV * tq

    @pl.when(kb_ref[step] == 0)
    def _init():
        m_ref[...] = jnp.full(m_ref.shape, MASKED_BIAS, F32)
        acc_ref[...] = jnp.zeros(acc_ref.shape, F32)
        for h in range(N_HEADS):
            g, r = divmod(h, Q_PER_KV)
            qh = q_ref[0, :, h * HEAD_DIM:(h + 1) * HEAD_DIM].astype(F32) * (HEAD_DIM ** -0.5)
            qs_ref[g, r * tq:(r + 1) * tq, :] = qh.astype(qs_ref.dtype)

    bias = jnp.concatenate([bias_ref[0, j] for j in range(bias_ref.shape[1])],
                           axis=1).astype(F32)
    sub_rows = ATTN_HEADS_PER_DOT * tq
    for g in range(N_KV_HEADS):
        for sub in range(Q_PER_KV // ATTN_HEADS_PER_DOT):
            rs = slice(sub * sub_rows, (sub + 1) * sub_rows)
            s = jnp.dot(qs_ref[g, rs, :], kt_ref[0, g], preferred_element_type=F32)
            s = (s * LOG2_E).reshape(ATTN_HEADS_PER_DOT, tq, tk) + bias[None]
            s = s.reshape(sub_rows, tk)
            m_prev = m_ref[g, rs, :]
            m_blk = jnp.max(s, axis=1, keepdims=True)
            m_new = jnp.maximum(m_prev, jnp.broadcast_to(m_blk, m_prev.shape))
            alpha = jnp.exp2(m_prev - m_new)
            p = jnp.concatenate(
                [jnp.exp2(s[:, c * LANES:(c + 1) * LANES] - m_new) for c in range(tk // LANES)],
                axis=1).astype(BF16)
            acc_ref[g, rs, :] = alpha * acc_ref[g, rs, :] + jnp.dot(
                p, v_ref[0, g], preferred_element_type=F32)
            m_ref[g, rs, :] = m_new

    @pl.when(last_ref[step] == 1)
    def _finish():
        for h in range(N_HEADS):
            g, r = divmod(h, Q_PER_KV)
            a = acc_ref[g, r * tq:(r + 1) * tq, :]
            out = a[:, :HEAD_DIM] / a[:, HEAD_DIM:HEAD_DIM + 1]
            o_ref[0, :, h * HEAD_DIM:(h + 1) * HEAD_DIM] = out.astype(o_ref.dtype)


def masked_attention(q, k, v, bias, *, tq, kspan, causal):
    b, t_q, d_q = q.shape
    n_k = k.shape[1]
    tk = kspan * bias.shape[3]
    nkb = n_k // tk
    kt = jnp.transpose(k.reshape(b, n_k, N_KV_HEADS, HEAD_DIM), (0, 2, 3, 1))
    v4 = jnp.transpose(v.reshape(b, n_k, N_KV_HEADS, HEAD_DIM), (0, 2, 1, 3))
    v_ext = jnp.concatenate(
        [v4, jnp.ones((b, N_KV_HEADS, n_k, 1), BF16),
         jnp.zeros((b, N_KV_HEADS, n_k, LANES - HEAD_DIM - 1), BF16)], axis=-1)

    pairs = []
    for qb in range(t_q // tq):
        last = min(nkb - 1, ((qb + 1) * tq - 1) // tk) if causal else nkb - 1
        pairs += [(qb, kb, int(kb == last)) for kb in range(last + 1)]
    qb_ids, kb_ids, last_flags = (jnp.asarray(col, I32) for col in zip(*pairs))

    kernel = functools.partial(_attn_kernel, tq=tq, tk=tk)
    rows = Q_PER_KV * tq
    return pl.pallas_call(
        kernel,
        grid_spec=pltpu.PrefetchScalarGridSpec(
            num_scalar_prefetch=3,
            grid=(b, len(pairs)),
            in_specs=[
                pl.BlockSpec((1, tq, d_q), lambda bi, s, qbs, kbs, lf: (bi, qbs[s], 0)),
                pl.BlockSpec((1, N_KV_HEADS, HEAD_DIM, tk),
                             lambda bi, s, qbs, kbs, lf: (bi, 0, 0, kbs[s])),
                pl.BlockSpec((1, N_KV_HEADS, tk, LANES),
                             lambda bi, s, qbs, kbs, lf: (bi, 0, kbs[s], 0)),
                pl.BlockSpec((1, kspan, tq, bias.shape[3]),
                             lambda bi, s, qbs, kbs, lf: (bi, kbs[s], qbs[s], 0)),
            ],
            out_specs=pl.BlockSpec((1, tq, d_q), lambda bi, s, qbs, kbs, lf: (bi, qbs[s], 0)),
            scratch_shapes=[pltpu.VMEM((N_KV_HEADS, rows, HEAD_DIM), BF16),
                            pltpu.VMEM((N_KV_HEADS, rows, LANES), F32),
                            pltpu.VMEM((N_KV_HEADS, rows, LANES), F32)]),
        out_shape=jax.ShapeDtypeStruct((b, t_q, d_q), BF16),
        compiler_params=_params(("parallel", "arbitrary")),
        name="masked_attention",
    )(qb_ids, kb_ids, last_flags, q, kt, v_ext, bias)


def _conv_kernel(cur_ref, prev_ref, ctx_ref, w_ref, b_ref, g_ref, beta_ref, o_ref, buf_ref,
                 conv_ref, *, tm, cw):
    @pl.when(pl.program_id(1) == 0)
    def _first():
        buf_ref[0:CONV_HALO, :] = ctx_ref[0]

    @pl.when(pl.program_id(1) > 0)
    def _later():
        buf_ref[0:CONV_HALO, :] = prev_ref[0]

    buf_ref[CONV_HALO:CONV_HALO + tm, :] = cur_ref[0]
    lead = CONV_HALO - (CONV_WIDTH - 1)
    c_total = buf_ref.shape[1]
    rw = min(tm, 128)
    for c in range(c_total // cw):
        cs = slice(c * cw, (c + 1) * cw)
        for r in range(tm // rw):
            acc = jnp.zeros((rw, cw), F32) + b_ref[:, cs]
            for j in range(CONV_WIDTH):
                r0 = lead + j + r * rw
                acc = acc + w_ref[j:j + 1, cs] * buf_ref[r0:r0 + rw, cs]
            conv_ref[r * rw:(r + 1) * rw, cs] = acc
    x = conv_ref[...]
    mu = jnp.mean(x, axis=-1, keepdims=True)
    xc = x - mu
    y = xc * lax.rsqrt(jnp.mean(xc * xc, axis=-1, keepdims=True) + EPS)
    y = y * g_ref[...] + beta_ref[...]
    o_ref[0] = _silu(y).astype(o_ref.dtype)


def conv_ln_silu(y, ctx, w_dw, b_dw, ln_g, ln_b, *, tm):
    b, l, c = y.shape
    kernel = functools.partial(_conv_kernel, tm=tm, cw=LANES)
    vec = lambda a: a.reshape(1, c).astype(F32)
    halo_blocks = tm // CONV_HALO
    return pl.pallas_call(
        kernel,
        grid=(b, l // tm),
        in_specs=[
            pl.BlockSpec((1, tm, c), lambda bi, i: (bi, i, 0)),
            pl.BlockSpec((1, CONV_HALO, c),
                         lambda bi, i: (bi, jnp.maximum(i * halo_blocks - 1, 0), 0)),
            pl.BlockSpec((1, CONV_HALO, c), lambda bi, i: (bi, 0, 0)),
            pl.BlockSpec((CONV_WIDTH, c), lambda bi, i: (0, 0)),
            pl.BlockSpec((1, c), lambda bi, i: (0, 0)),
            pl.BlockSpec((1, c), lambda bi, i: (0, 0)),
            pl.BlockSpec((1, c), lambda bi, i: (0, 0)),
        ],
        out_specs=pl.BlockSpec((1, tm, c), lambda bi, i: (bi, i, 0)),
        out_shape=jax.ShapeDtypeStruct((b, l, c), BF16),
        scratch_shapes=[pltpu.VMEM((tm + CONV_HALO, c), F32), pltpu.VMEM((tm, c), F32)],
        compiler_params=_params(("parallel", "parallel")),
        name="conv_ln_silu",
    )(y, y, ctx, w_dw.astype(F32), vec(b_dw), vec(ln_g), vec(ln_b))


ROUTE_IDX1, ROUTE_IDX2, ROUTE_GATE1, ROUTE_GATE2, ROUTE_RANK1, ROUTE_RANK2 = range(6)


def _router_kernel(h_ref, g_ref, wr_ref, lower_ref, u_ref, route_ref, counts_ref, seen_ref, *,
                   n_experts):
    @pl.when(pl.program_id(0) == 0)
    def _init():
        seen_ref[...] = jnp.zeros(seen_ref.shape, F32)

    x = h_ref[...]
    u = x * lax.rsqrt(jnp.mean(x * x, axis=-1, keepdims=True) + EPS) * g_ref[...]
    u_ref[...] = u
    logits = jnp.dot(u.astype(BF16), wr_ref[...], preferred_element_type=F32)
    lane = lax.broadcasted_iota(I32, logits.shape, 1).astype(F32)
    logits = jnp.where(lane < n_experts, logits, -jnp.inf)
    top1 = jnp.max(logits, axis=1, keepdims=True)
    idx1 = jnp.min(jnp.where(logits == top1, lane, float(LANES)), axis=1, keepdims=True)
    rest = jnp.where(lane == idx1, -jnp.inf, logits)
    top2 = jnp.max(rest, axis=1, keepdims=True)
    idx2 = jnp.min(jnp.where(rest == top2, lane, float(LANES)), axis=1, keepdims=True)
    e2 = jnp.exp(top2 - top1)
    denom = 1.0 + e2

    hot1 = jnp.where(lane == idx1, 1.0, 0.0)
    hot2 = jnp.where(lane == idx2, 1.0, 0.0)
    hot = hot1 + hot2
    before = seen_ref[...] + jnp.dot(lower_ref[...], hot.astype(BF16), preferred_element_type=F32)
    rank1 = jnp.sum(hot1 * before, axis=1, keepdims=True)
    rank2 = jnp.sum(hot2 * before, axis=1, keepdims=True)
    seen_ref[...] += jnp.sum(hot, axis=0, keepdims=True)
    counts_ref[...] = seen_ref[...]

    route = jnp.zeros(logits.shape, F32)
    for slot, val in ((ROUTE_IDX1, idx1), (ROUTE_IDX2, idx2), (ROUTE_GATE1, 1.0 / denom),
                      (ROUTE_GATE2, e2 / denom), (ROUTE_RANK1, rank1), (ROUTE_RANK2, rank2)):
        route = jnp.where(lane == slot, val, route)
    route_ref[...] = route


def moe_router(h, gain, w_router_padded, n_experts, *, tm):
    m, d = h.shape
    row = jnp.arange(tm, dtype=I32)
    lower = (row[None, :] < row[:, None]).astype(BF16)
    kernel = functools.partial(_router_kernel, n_experts=n_experts)
    return pl.pallas_call(
        kernel,
        grid=(m // tm,),
        in_specs=[pl.BlockSpec((tm, d), lambda i: (i, 0)),
                  pl.BlockSpec((1, d), lambda i: (0, 0)),
                  pl.BlockSpec((d, LANES), lambda i: (0, 0)),
                  pl.BlockSpec((tm, tm), lambda i: (0, 0))],
        out_specs=[pl.BlockSpec((tm, d), lambda i: (i, 0)),
                   pl.BlockSpec((tm, LANES), lambda i: (i, 0)),
                   pl.BlockSpec((1, LANES), lambda i: (0, 0))],
        out_shape=[jax.ShapeDtypeStruct((m, d), F32), jax.ShapeDtypeStruct((m, LANES), F32),
                   jax.ShapeDtypeStruct((1, LANES), F32)],
        scratch_shapes=[pltpu.VMEM((1, LANES), F32)],
        compiler_params=_params(("arbitrary",)),
        name="moe_router",
    )(h, gain.reshape(1, d).astype(F32), w_router_padded, lower)


def _scatter_rows_kernel(d1_ref, d2_ref, u_ref, init_ref, xs_ref, sem):
    del init_ref
    tm = u_ref.shape[0]

    def row_copy(j, dest):
        return pltpu.make_async_copy(u_ref.at[pl.ds(j, 1)], xs_ref.at[pl.ds(dest, 1)], sem)

    def issue(j, carry):
        row_copy(j, d1_ref[0, 0, j]).start(priority=0)
        row_copy(j, d2_ref[0, 0, j]).start(priority=1)
        return carry

    lax.fori_loop(0, tm, issue, 0, unroll=DMA_LOOP_UNROLL)

    def drain(j, carry):
        row_copy(j, 0).wait()
        row_copy(j, 0).wait()
        return carry

    lax.fori_loop(0, tm, drain, 0, unroll=DMA_LOOP_UNROLL)


def scatter_rows(u, dest1, dest2, n_rows, *, tm):
    m, d = u.shape
    smem_idx = lambda a: a.reshape(m // tm, 1, tm)
    idx_spec = pl.BlockSpec((1, 1, tm), lambda i: (i, 0, 0), memory_space=pltpu.SMEM)
    return pl.pallas_call(
        _scatter_rows_kernel,
        grid=(m // tm,),
        in_specs=[idx_spec, idx_spec,
                  pl.BlockSpec((tm, d), lambda i: (i, 0)),
                  pl.BlockSpec(memory_space=pl.ANY)],
        out_specs=pl.BlockSpec(memory_space=pl.ANY),
        out_shape=jax.ShapeDtypeStruct((n_rows, d), u.dtype),
        scratch_shapes=[pltpu.SemaphoreType.DMA],
        input_output_aliases={3: 0},
        compiler_params=_params(("arbitrary",)),
        name="moe_scatter_rows",
    )(smem_idx(dest1), smem_idx(dest2), u, jnp.zeros((n_rows, d), u.dtype))


def _combine_kernel(d1_ref, d2_ref, rows_ref, route_ref, h_ref, o_ref, buf_ref, sem):
    tm = h_ref.shape[0]

    def row_copy(slot, j, src):
        return pltpu.make_async_copy(rows_ref.at[pl.ds(src, 1)], buf_ref.at[slot, pl.ds(j, 1)], sem)

    def issue(j, carry):
        row_copy(0, j, d1_ref[0, 0, j]).start(priority=0)
        row_copy(1, j, d2_ref[0, 0, j]).start(priority=1)
        return carry

    lax.fori_loop(0, tm, issue, 0, unroll=DMA_LOOP_UNROLL)

    def drain(j, carry):
        row_copy(0, j, 0).wait()
        row_copy(1, j, 0).wait()
        return carry

    lax.fori_loop(0, tm, drain, 0, unroll=DMA_LOOP_UNROLL)
    route = route_ref[...]
    g1 = route[:, ROUTE_GATE1:ROUTE_GATE1 + 1]
    g2 = route[:, ROUTE_GATE2:ROUTE_GATE2 + 1]
    o_ref[...] = h_ref[...] + (g1 * buf_ref[0] + g2 * buf_ref[1])


def combine_rows(rows, dest1, dest2, route, h, *, tm):
    m, d = h.shape
    smem_idx = lambda a: a.reshape(m // tm, 1, tm)
    idx_spec = pl.BlockSpec((1, 1, tm), lambda i: (i, 0, 0), memory_space=pltpu.SMEM)
    return pl.pallas_call(
        _combine_kernel,
        grid=(m // tm,),
        in_specs=[idx_spec, idx_spec,
                  pl.BlockSpec(memory_space=pl.ANY),
                  pl.BlockSpec((tm, LANES), lambda i: (i, 0)),
                  pl.BlockSpec((tm, d), lambda i: (i, 0))],
        out_specs=pl.BlockSpec((tm, d), lambda i: (i, 0)),
        out_shape=jax.ShapeDtypeStruct((m, d), F32),
        scratch_shapes=[pltpu.VMEM((2, tm, d), F32), pltpu.SemaphoreType.DMA],
        compiler_params=_params(("arbitrary",)),
        name="moe_combine_rows",
    )(smem_idx(dest1), smem_idx(dest2), rows, route, h)


def _expert_up_kernel(blk_expert_ref, n_used_ref, xs_ref, wg_ref, wu_ref, o_ref, stage_ref):
    del blk_expert_ref
    i = pl.program_id(0)

    @pl.when(pl.program_id(1) == 0)
    def _stage():
        stage_ref[...] = xs_ref[...].astype(BF16)

    @pl.when(i < n_used_ref[0])
    def _compute():
        u = stage_ref[...]
        a = jnp.dot(u, wg_ref[0], preferred_element_type=F32)
        b = jnp.dot(u, wu_ref[0], preferred_element_type=F32)
        o_ref[...] = (_silu(a) * b).astype(o_ref.dtype)

    @pl.when(i >= n_used_ref[0])
    def _skip():
        o_ref[...] = jnp.zeros(o_ref.shape, o_ref.dtype)


def expert_up(xs, w_gate, w_up, blk_expert, n_used, *, blk, tn):
    n_rows, d = xs.shape
    f = w_gate.shape[2]
    return pl.pallas_call(
        _expert_up_kernel,
        grid_spec=pltpu.PrefetchScalarGridSpec(
            num_scalar_prefetch=2,
            grid=(n_rows // blk, f // tn),
            in_specs=[pl.BlockSpec((blk, d), lambda i, j, be, nu: (i, 0)),
                      pl.BlockSpec((1, d, tn), lambda i, j, be, nu: (be[i], 0, j)),
                      pl.BlockSpec((1, d, tn), lambda i, j, be, nu: (be[i], 0, j))],
            out_specs=pl.BlockSpec((blk, tn), lambda i, j, be, nu: (i, j)),
            scratch_shapes=[pltpu.VMEM((blk, d), BF16)]),
        out_shape=jax.ShapeDtypeStruct((n_rows, f), BF16),
        compiler_params=_params(("parallel", "arbitrary")),
        name="expert_up",
    )(blk_expert, n_used, xs, w_gate, w_up)


def _expert_down_kernel(blk_expert_ref, n_used_ref, hb_ref, wd_ref, o_ref):
    del blk_expert_ref
    i = pl.program_id(0)

    @pl.when(i < n_used_ref[0])
    def _compute():
        o_ref[...] = jnp.dot(hb_ref[...], wd_ref[0], preferred_element_type=F32)

    @pl.when(i >= n_used_ref[0])
    def _skip():
        o_ref[...] = jnp.zeros(o_ref.shape, o_ref.dtype)


def expert_down(hb, w_down, blk_expert, n_used, *, blk, tn):
    n_rows, f = hb.shape
    d = w_down.shape[2]
    return pl.pallas_call(
        _expert_down_kernel,
        grid_spec=pltpu.PrefetchScalarGridSpec(
            num_scalar_prefetch=2,
            grid=(n_rows // blk, d // tn),
            in_specs=[pl.BlockSpec((blk, f), lambda i, j, be, nu: (i, 0)),
                      pl.BlockSpec((1, f, tn), lambda i, j, be, nu: (be[i], 0, j))],
            out_specs=pl.BlockSpec((blk, tn), lambda i, j, be, nu: (i, j))),
        out_shape=jax.ShapeDtypeStruct((n_rows, d), F32),
        compiler_params=_params(("parallel", "parallel")),
        name="expert_down",
    )(blk_expert, n_used, hb, w_down)


def moe_swiglu(h, gain, w_router, w_gate, w_up, w_down):
    m, d = h.shape
    n_experts = w_router.shape[1]
    blk = MOE_ROW_BLOCK if m >= 8 * MOE_ROW_BLOCK else LANES
    tm = _pick_tile(m, 256)
    w_router_padded = jnp.concatenate([w_router, jnp.zeros((d, LANES - n_experts), BF16)], axis=1)
    u, route, counts = moe_router(h, gain, w_router_padded, n_experts, tm=_pick_tile(m, 512))

    counts = counts[0, :n_experts].astype(I32)
    padded = (counts + blk - 1) // blk * blk
    pad_end = jnp.cumsum(padded)
    pad_start = pad_end - padded
    idx1 = route[:, ROUTE_IDX1].astype(I32)
    idx2 = route[:, ROUTE_IDX2].astype(I32)
    dest1 = pad_start[idx1] + route[:, ROUTE_RANK1].astype(I32)
    dest2 = pad_start[idx2] + route[:, ROUTE_RANK2].astype(I32)
    n_blocks = -(-(TOP_K_EXPERTS * m + n_experts * (blk - 1)) // blk)
    blk_start = jnp.arange(n_blocks, dtype=I32) * blk
    blk_expert = jnp.minimum(jnp.searchsorted(pad_end, blk_start, side="right"),
                             n_experts - 1).astype(I32)
    n_used = (pad_end[-1:] // blk).astype(I32)

    xs = scatter_rows(u, dest1, dest2, n_blocks * blk, tm=tm)
    hb = expert_up(xs, w_gate, w_up, blk_expert, n_used, blk=blk, tn=_pick_tile(w_gate.shape[2], 1792))
    rows = expert_down(hb, w_down, blk_expert, n_used, blk=blk, tn=_pick_tile(d, 1024))
    return combine_rows(rows, dest1, dest2, route, h, tm=tm)


def _split_w_in(w_in_layer, d_model):
    attn_dim = N_HEADS * HEAD_DIM
    kv_dim = N_KV_HEADS * HEAD_DIM
    sizes = (attn_dim, kv_dim, kv_dim, IDX_HEADS * IDX_DIM, IDX_DIM, IDX_HEADS, d_model, d_model,
             d_model, d_model)
    parts, off = [], 0
    for s in sizes:
        parts.append(w_in_layer[:, off:off + s])
        off += s
    assert off == w_in_layer.shape[1]
    return parts


def _layer_stack(x, p, weights, cache, *, past):
    bsz, length, d_model = x.shape
    m = bsz * length
    kv_dim = N_KV_HEADS * HEAD_DIM
    pos = past + jnp.arange(length, dtype=I32)
    row_pos = jnp.tile(pos, bsz)
    depth = weights["w_in"].shape[0]

    cos_q, sin_q = _rope_tables(row_pos, 512)
    cos_k, sin_k = _rope_tables(row_pos, kv_dim)
    cos_i, sin_i = _rope_tables(row_pos, LANES, rotary_width=IDX_DIM)
    ones_q = _block_ones(512)
    ones_k = _block_ones(kv_dim)

    n_keys = past + length
    n_sel = min(TOPK_MAX, n_keys // 4)
    if past == 0:
        tq_sel, tk, tq_attn = min(256, length), min(512, length), min(512, length)
        n_keys_pad = n_keys
        kspan = 2 if n_keys_pad % (2 * tk) == 0 else 1
    else:
        tq_sel = tq_attn = length
        tk = 384
        n_keys_pad = -(-n_keys // tk) * tk
        kspan = n_keys_pad // tk

    h = x.reshape(m, d_model)
    ks, vs, kis, convs = [], [], [], []
    for i in range(depth):
        wq, wk, wv, wqi, wki, wwi, wglu_a, wglu_g, wga, wgb = _split_w_in(weights["w_in"][i], d_model)
        w_kiwi = jnp.concatenate(
            [wki, wwi, jnp.zeros((d_model, LANES - IDX_DIM - IDX_HEADS), BF16)], axis=1)
        g_mix = weights["g_mix"][i]

        qn_gain = jnp.tile(weights["q_norm_g"][i].astype(F32), 512 // HEAD_DIM).reshape(1, 512)
        kn_gain = jnp.tile(weights["k_norm_g"][i].astype(F32), kv_dim // HEAD_DIM).reshape(1, kv_dim)

        def norm_rope(accs, tiled, rows, consts):
            y = _head_rms_norm(accs[0], consts[0], consts[1])
            y = _rope(y, rows[0], rows[1])
            return (y, y)

        u = rms_norm_rows(h, g_mix)
        (q_bf,) = fused_mm("proj_q", [u], [wq], [0],
                           lambda a, t, r, c: (_rope(_head_rms_norm(a[0], c[0], c[1]), r[0], r[1]),),
                           [BF16], rows=[cos_q, sin_q], consts=[ones_q, qn_gain], tn=512)
        k_f32, k_bf = fused_mm("proj_k", [u], [wk], [0], norm_rope, [F32, BF16],
                               rows=[cos_k, sin_k], consts=[ones_k, kn_gain])
        v_f32, v_bf = fused_mm("proj_v", [u], [wv], [0], lambda a, t, r, c: (a[0], a[0]),
                               [F32, BF16])
        (qi_bf,) = fused_mm("proj_qi", [u], [wqi], [0],
                            lambda a, t, r, c: (_rope(a[0], r[0], r[1]),),
                            [BF16], rows=[cos_q, sin_q], tn=512)
        (kiwi,) = fused_mm("proj_kiwi", [u], [w_kiwi], [0],
                           lambda a, t, r, c: (_rope(a[0], r[0], r[1]),),
                           [F32], rows=[cos_i, sin_i])
        (glu,) = fused_mm("proj_glu", [u], [wglu_a, wglu_g], [0, 0],
                          lambda a, t, r, c: (a[0] * _sigmoid(a[1]),), [F32])
        gate_a, gate_b = fused_mm("proj_gates", [u], [wga, wgb], [0, 0],
                                  lambda a, t, r, c: (_sigmoid(a[0]), _sigmoid(a[1])),
                                  [F32, F32])

        ki_f32 = kiwi[:, :IDX_DIM]
        wi = kiwi[:, IDX_DIM:IDX_DIM + IDX_HEADS]

        k3 = k_bf.reshape(bsz, length, kv_dim)
        v3 = v_bf.reshape(bsz, length, kv_dim)
        ki3 = ki_f32.astype(BF16).reshape(bsz, length, IDX_DIM)
        if past:
            pad = n_keys_pad - n_keys
            k3 = jnp.concatenate([cache["k"][i].reshape(bsz, past, kv_dim).astype(BF16), k3,
                                  jnp.zeros((bsz, pad, kv_dim), BF16)], axis=1)
            v3 = jnp.concatenate([cache["v"][i].reshape(bsz, past, kv_dim).astype(BF16), v3,
                                  jnp.zeros((bsz, pad, kv_dim), BF16)], axis=1)
            ki3 = jnp.concatenate([cache["kidx"][i].astype(BF16), ki3,
                                   jnp.zeros((bsz, pad, IDX_DIM), BF16)], axis=1)
        bias = select_bias(qi_bf.reshape(bsz, length, -1), ki3, wi.reshape(bsz, length, IDX_HEADS),
                           tq=tq_sel, tk=tk, causal=(past == 0), q_pos0=past, n_valid=n_keys,
                           n_sel=n_sel)
        attn = masked_attention(q_bf.reshape(bsz, length, -1), k3, v3, bias, tq=tq_attn,
                                kspan=kspan, causal=(past == 0))
        attn = attn.reshape(m, -1)

        glu3 = glu.reshape(bsz, length, d_model)
        if past:
            ctx = cache["conv"][i]
        else:
            ctx = jnp.zeros((bsz, CONV_WIDTH - 1, d_model), F32)
        lead = jnp.zeros((bsz, CONV_HALO - (CONV_WIDTH - 1), d_model), F32)
        ctx32 = jnp.concatenate([lead, ctx], axis=1)
        l_pad = -(-length // CONV_HALO) * CONV_HALO
        if l_pad == length:
            y_conv = glu3
        else:
            y_conv = jnp.concatenate(
                [glu3, jnp.zeros((bsz, l_pad - length, d_model), F32)], axis=1)
        if length >= CONV_WIDTH - 1:
            conv_state = glu3[:, length - (CONV_WIDTH - 1):]
        else:
            conv_state = jnp.concatenate([ctx, glu3], axis=1)[:, -(CONV_WIDTH - 1):]
        z = conv_ln_silu(y_conv, ctx32, weights["w_dw"][i], weights["b_dw"][i],
                         weights["conv_ln_g"][i], weights["conv_ln_b"][i], tm=min(256, l_pad))
        z = z[:, :length].reshape(m, d_model)

        (merged,) = fused_mm("mix", [attn, z], [weights["w_o_attn"][i], weights["w_o_conv"][i]],
                             [0, 1], lambda a, t, r, c: (t[0] * a[0] + t[1] * a[1],), [BF16],
                             tiled=[gate_a, gate_b])
        (h,) = fused_mm("out_proj", [merged], [weights["w_out"][i]], [0],
                        lambda a, t, r, c: (t[0] + a[0],), [F32], tiled=[h])

        j = i // 2
        if i % 2 == 0:
            (hb,) = fused_mm("ffn_up", [h], [weights["ffn_w_gate"][j], weights["ffn_w_up"][j]],
                             [0, 0], lambda a, t, r, c: (_silu(a[0]) * a[1],), [BF16],
                             gains=[weights["g_ffn"][i]], tn=1408)
            (h,) = fused_mm("ffn_down", [hb], [weights["ffn_w_down"][j]], [0],
                            lambda a, t, r, c: (t[0] + a[0],), [F32], tiled=[h])
        else:
            h = moe_swiglu(h, weights["g_ffn"][i], weights["moe_router"][j],
                           weights["moe_w_gate"][j], weights["moe_w_up"][j],
                           weights["moe_w_down"][j])

        (h,) = fused_mm("ple", [h, p[i].reshape(m, -1)],
                        [weights["w_ple_gate"][i], weights["w_ple_proj"][i]],
                        [0, 1], lambda a, t, r, c: (t[0] + _sigmoid(a[0]) * a[1],), [F32],
                        gains=[weights["g_ple"][i], None], tiled=[h])

        ks.append(k_f32.reshape(bsz, length, N_KV_HEADS, HEAD_DIM))
        vs.append(v_f32.reshape(bsz, length, N_KV_HEADS, HEAD_DIM))
        kis.append(ki_f32.reshape(bsz, length, IDX_DIM))
        convs.append(conv_state)
    return (h.reshape(bsz, length, d_model), jnp.stack(ks), jnp.stack(vs), jnp.stack(kis),
            jnp.stack(convs))


def kernel(x_prompt, x_sample, cache_k, cache_v, cache_kidx, state_conv, p_prompt, p_sample, w_in, q_norm_g, k_norm_g, w_dw, b_dw, conv_ln_g, conv_ln_b, w_o_attn, w_o_conv, w_out, g_mix, g_ffn, ffn_w_gate, ffn_w_up, ffn_w_down, moe_router, moe_w_gate, moe_w_up, moe_w_down, g_ple, w_ple_gate, w_ple_proj):
    bf = lambda a: a.astype(BF16)
    weights = dict(
        w_in=bf(w_in), q_norm_g=q_norm_g, k_norm_g=k_norm_g, w_dw=w_dw, b_dw=b_dw,
        conv_ln_g=conv_ln_g, conv_ln_b=conv_ln_b, w_o_attn=bf(w_o_attn), w_o_conv=bf(w_o_conv),
        w_out=bf(w_out), g_mix=g_mix, g_ffn=g_ffn, ffn_w_gate=bf(ffn_w_gate),
        ffn_w_up=bf(ffn_w_up), ffn_w_down=bf(ffn_w_down), moe_router=bf(moe_router),
        moe_w_gate=bf(moe_w_gate), moe_w_up=bf(moe_w_up), moe_w_down=bf(moe_w_down), g_ple=g_ple,
        w_ple_gate=bf(w_ple_gate), w_ple_proj=bf(w_ple_proj))
    cache = dict(k=cache_k, v=cache_v, kidx=cache_kidx, conv=state_conv)
    past_len = cache_k.shape[2]
    y_p, k_p, v_p, ki_p, conv_p = _layer_stack(x_prompt, p_prompt, weights, None, past=0)
    y_s, k_s, v_s, ki_s, conv_s = _layer_stack(x_sample, p_sample, weights, cache, past=past_len)
    return (y_p, y_s, k_p, v_p, ki_p, conv_p, k_s, v_s, ki_s, conv_s)
```

```python
import functools

import jax
import jax.numpy as jnp
from jax import lax
from jax.experimental import pallas as pl
from jax.experimental.pallas import tpu as pltpu

F32 = jnp.float32
BF16 = jnp.bfloat16
I32 = jnp.int32

CHUNK = 64
CHUNK_SHIFT = CHUNK.bit_length() - 1
assert 1 << CHUNK_SHIFT == CHUNK
N_HEADS = 16
HEAD_DIM = 64
N_KV_HEADS = 4
Q_PER_KV = N_HEADS // N_KV_HEADS
IDX_HEADS = 8
IDX_DIM = 64
TOPK_MAX = 256
ROPE_THETA = 10000.0
CONV_WIDTH = 31
TOP_K_EXPERTS = 2
EPS = 1e-6

LANES = 128
CONV_HALO = 32
MASKED_BIAS = -1e30
LOG2_E = 1.4426950408889634
INT_MIN = -2 ** 31
KEY_NEG_INF = -2139095041
SEARCH_ROWS = 128
EMIT_ROWS = 128
HALF_WORD_BITS = 16
TOP_BITS = HALF_WORD_BITS - 1
DMA_LOOP_UNROLL = 8
ATTN_HEADS_PER_DOT = Q_PER_KV
MOE_ROW_BLOCK = 512
VMEM_LIMIT = 52 * 1024 * 1024


def _params(semantics):
    return pltpu.CompilerParams(dimension_semantics=semantics, vmem_limit_bytes=VMEM_LIMIT)


def _pick_tile(n, pref):
    if n <= pref:
        return n
    best = None
    for t in range(LANES, pref + 1, LANES):
        if n % t == 0:
            best = t
    assert best is not None, (n, pref)
    return best


def _rms_norm_kernel(x_ref, g_ref, o_ref):
    x = x_ref[...]
    y = x * lax.rsqrt(jnp.mean(x * x, axis=-1, keepdims=True) + EPS) * g_ref[...]
    o_ref[...] = y.astype(o_ref.dtype)


def rms_norm_rows(x, gain, *, tm=512):
    m, d = x.shape
    tm = _pick_tile(m, tm) if m % LANES == 0 else m
    return pl.pallas_call(
        _rms_norm_kernel,
        grid=(m // tm,),
        in_specs=[pl.BlockSpec((tm, d), lambda i: (i, 0)), pl.BlockSpec((1, d), lambda i: (0, 0))],
        out_specs=pl.BlockSpec((tm, d), lambda i: (i, 0)),
        out_shape=jax.ShapeDtypeStruct((m, d), BF16),
        compiler_params=_params(("parallel",)),
        name="rms_norm_rows",
    )(x, gain.reshape(1, d).astype(F32))


def _fused_mm_kernel(*refs, n_lhs, has_gain, staged, pair_lhs, n_tiled, n_rows, n_consts, n_out,
                     epilogue):
    it = iter(refs)
    lhs_refs = [next(it) for _ in range(n_lhs)]
    gain_refs = [next(it) if has_gain[a] else None for a in range(n_lhs)]
    rhs_refs = [next(it) for _ in range(len(pair_lhs))]
    tiled_refs = [next(it) for _ in range(n_tiled)]
    row_refs = [next(it) for _ in range(n_rows)]
    const_refs = [next(it) for _ in range(n_consts)]
    out_refs = [next(it) for _ in range(n_out)]
    operand_refs = [next(it) if staged[a] else lhs_refs[a] for a in range(n_lhs)]

    if any(staged):
        @pl.when(pl.program_id(1) == 0)
        def _stage():
            for a in range(n_lhs):
                if not staged[a]:
                    continue
                x = lhs_refs[a][...].astype(F32)
                if has_gain[a]:
                    x = x * lax.rsqrt(jnp.mean(x * x, axis=-1, keepdims=True) + EPS)
                    x = x * gain_refs[a][...]
                operand_refs[a][...] = x.astype(BF16)

    accs = [jnp.dot(operand_refs[pair_lhs[r]][...], rhs_refs[r][...], preferred_element_type=F32)
            for r in range(len(pair_lhs))]
    outs = epilogue(accs, [t[...] for t in tiled_refs], [t[...] for t in row_refs],
                    [t[...] for t in const_refs])
    for o_ref, o in zip(out_refs, outs):
        o_ref[...] = o.astype(o_ref.dtype)


def fused_mm(name, lhs, rhs, pair_lhs, epilogue, out_dtypes, *, gains=None, tiled=(), rows=(),
             consts=(), tm=1024, tn=1024):
    m = lhs[0].shape[0]
    n = rhs[0].shape[1]
    tm = _pick_tile(m, tm) if m % LANES == 0 else m
    tn = _pick_tile(n, tn)
    n_lhs = len(lhs)
    gains = list(gains) if gains is not None else [None] * n_lhs
    has_gain = tuple(g is not None for g in gains)
    staged = tuple(has_gain[a] or lhs[a].dtype != BF16 for a in range(n_lhs))

    in_specs, args = [], []
    for x in lhs:
        in_specs.append(pl.BlockSpec((tm, x.shape[1]), lambda i, j: (i, 0)))
        args.append(x)
    for g in gains:
        if g is not None:
            g2 = g.reshape(1, -1).astype(F32)
            in_specs.append(pl.BlockSpec(g2.shape, lambda i, j: (0, 0)))
            args.append(g2)
    for w in rhs:
        in_specs.append(pl.BlockSpec((w.shape[0], tn), lambda i, j: (0, j)))
        args.append(w)
    for t in tiled:
        in_specs.append(pl.BlockSpec((tm, tn), lambda i, j: (i, j)))
        args.append(t)
    for t in rows:
        in_specs.append(pl.BlockSpec((tm, t.shape[1]), lambda i, j: (i, 0)))
        args.append(t)
    for c in consts:
        in_specs.append(pl.BlockSpec(c.shape, lambda i, j: (0, 0)))
        args.append(c)

    kernel = functools.partial(
        _fused_mm_kernel, n_lhs=n_lhs, has_gain=has_gain, staged=staged, pair_lhs=tuple(pair_lhs),
        n_tiled=len(tiled), n_rows=len(rows), n_consts=len(consts), n_out=len(out_dtypes),
        epilogue=epilogue)
    outs = pl.pallas_call(
        kernel,
        grid=(m // tm, n // tn),
        in_specs=in_specs,
        out_specs=[pl.BlockSpec((tm, tn), lambda i, j: (i, j)) for _ in out_dtypes],
        out_shape=[jax.ShapeDtypeStruct((m, n), dt) for dt in out_dtypes],
        scratch_shapes=[pltpu.VMEM((tm, x.shape[1]), BF16) for a, x in enumerate(lhs) if staged[a]],
        compiler_params=_params(("parallel", "arbitrary")),
        name=name,
    )(*args)
    return outs


def _head_rms_norm(x, block_ones, gain):
    x2 = x * x
    hi = x2.astype(BF16)
    lo = (x2 - hi.astype(F32)).astype(BF16)
    ss = (jnp.dot(hi, block_ones, preferred_element_type=F32)
          + jnp.dot(lo, block_ones, preferred_element_type=F32))
    return x * lax.rsqrt(ss * (1.0 / HEAD_DIM) + EPS) * gain


def _rope(x, cos, sin_signed):
    n = x.shape[1]
    lane = lax.broadcasted_iota(I32, x.shape, 1)
    first_half = (lane & (HEAD_DIM // 2)) == 0
    partner = jnp.where(first_half, pltpu.roll(x, n - HEAD_DIM // 2, 1),
                        pltpu.roll(x, HEAD_DIM // 2, 1))
    return x * cos + partner * sin_signed


def _sigmoid(x):
    return 1.0 / (1.0 + jnp.exp(-x))


def _silu(x):
    return x * _sigmoid(x)


def _rope_tables(pos, width, rotary_width=None):
    half = HEAD_DIM // 2
    inv = ROPE_THETA ** (-jnp.arange(half, dtype=F32) / half)
    ang = pos.astype(F32)[:, None] * inv[None, :]
    cos = jnp.cos(ang)
    sin = jnp.sin(ang)
    cos64 = jnp.concatenate([cos, cos], axis=-1)
    sin64 = jnp.concatenate([-sin, sin], axis=-1)
    rotary_width = width if rotary_width is None else rotary_width
    reps = rotary_width // HEAD_DIM
    cos_t = jnp.tile(cos64, (1, reps))
    sin_t = jnp.tile(sin64, (1, reps))
    if rotary_width < width:
        pad = width - rotary_width
        cos_t = jnp.concatenate([cos_t, jnp.ones((pos.shape[0], pad), F32)], axis=-1)
        sin_t = jnp.concatenate([sin_t, jnp.zeros((pos.shape[0], pad), F32)], axis=-1)
    return cos_t, sin_t


def _block_ones(width):
    head = jnp.arange(width, dtype=I32) // HEAD_DIM
    return (head[:, None] == head[None, :]).astype(BF16)


def _float_key(x):
    b = lax.bitcast_convert_type(x, I32)
    return jnp.where(b >= 0, b, b ^ jnp.int32(0x7FFFFFFF))


def _select_kernel(qi_ref, ki_ref, wi_ref, tri_ref, bias_ref, key_ref, top_ref, *, tq, tk, nkb,
                   causal, q_pos0, n_valid, n_sel, rows):
    qb = pl.program_id(1)
    n_chunks = tk // LANES
    n_sub = tq // rows
    half = rows // 2
    if causal:
        n_act = jnp.minimum(nkb, ((qb + 1) * tq + tk - 1) // tk)
    else:
        n_act = nkb
    q_pos = q_pos0 + qb * tq + lax.broadcasted_iota(I32, (tq, tk), 0)
    q_chunk = lax.shift_right_logical(q_pos, CHUNK_SHIFT)
    w = (wi_ref[0] * (IDX_HEADS ** -0.5)) * (IDX_DIM ** -0.5)
    qi = qi_ref[0]

    def score_block(kb, carry):
        start = pl.multiple_of(kb * tk, tk)
        ki = ki_ref[0, pl.ds(start, tk), :]
        s = jnp.zeros((tq, tk), F32)
        for h in range(IDX_HEADS):
            r = lax.dot_general(qi[:, h * IDX_DIM:(h + 1) * IDX_DIM], ki,
                                (((1,), (1,)), ((), ())), preferred_element_type=F32)
            s = s + jnp.maximum(r, 0.0) * w[:, h:h + 1]
        k_pos = start + lax.broadcasted_iota(I32, (tq, tk), 1)
        admissible = (lax.shift_right_logical(k_pos, CHUNK_SHIFT) <= q_chunk) & (k_pos < n_valid)
        key = _float_key(jnp.where(admissible, s, -jnp.inf))
        top = lax.shift_right_logical(key ^ jnp.int32(INT_MIN), 32 - TOP_BITS)
        for sub in range(n_sub):
            key_ref[sub, kb] = key[sub * rows:(sub + 1) * rows]
            lo_rows = top[sub * rows:sub * rows + half]
            hi_rows = top[sub * rows + half:(sub + 1) * rows]
            top_ref[sub, kb] = lo_rows | jnp.left_shift(hi_rows, HALF_WORD_BITS)
        return carry

    lax.fori_loop(0, n_act, score_block, 0)
    for sub in range(n_sub):
        key_ref[sub, n_act] = jnp.full((rows, tk), INT_MIN, I32)
        top_ref[sub, n_act] = jnp.zeros((half, tk), I32)
    top_prefixes = _search_top_bits(top_ref, n_sub=n_sub, n_act=n_act, rows=rows, tk=tk,
                                    n_sel=n_sel)
    for sub in range(n_sub):
        _select_rows(key_ref.at[sub], top_prefixes[sub], tri_ref,
                     bias_ref.at[0, :, sub * rows:(sub + 1) * rows, :],
                     n_act=n_act, rows=rows, tk=tk, n_sel=n_sel)

    def fill_block(kb, carry):
        bias_ref[0, kb] = jnp.full((tq, tk), MASKED_BIAS, bias_ref.dtype)
        return carry

    lax.fori_loop(n_act, nkb, fill_block, 0)


def _search_top_bits(top_ref, *, n_sub, n_act, rows, tk, n_sel):
    half = rows // 2
    n_chunks = tk // LANES
    ones_mat = jnp.ones((LANES, LANES), BF16)
    n_counted = (tk * 2) * ((n_act + 1) // 2)
    guard = 1 << TOP_BITS
    pair_ones = 1 | (1 << HALF_WORD_BITS)
    half_mask = (1 << HALF_WORD_BITS) - 1

    def top_step(i, prefixes):
        cands = [p | jnp.left_shift(jnp.int32(1), TOP_BITS - 1 - i) for p in prefixes]
        cws = []
        for cand in cands:
            below = (cand - 1) | guard
            cws.append(below[:half] | jnp.left_shift(below[half:], HALF_WORD_BITS))

        def pair(p, accs):
            accs = list(accs)
            for kb in (2 * p, 2 * p + 1):
                for c in range(n_chunks):
                    for sub in range(n_sub):
                        d = cws[sub] - top_ref[sub, kb, :, c * LANES:(c + 1) * LANES]
                        accs[sub] = accs[sub] + (lax.shift_right_logical(d, TOP_BITS) & pair_ones)
            return tuple(accs)

        accs = lax.fori_loop(0, (n_act + 1) // 2, pair,
                             tuple(jnp.zeros((half, LANES), I32) for _ in range(n_sub)))
        new = []
        for sub in range(n_sub):
            le = jnp.concatenate([accs[sub] & half_mask,
                                  lax.shift_right_logical(accs[sub], HALF_WORD_BITS)], axis=0)
            n_le = jnp.dot(le.astype(F32).astype(BF16), ones_mat, preferred_element_type=F32)
            new.append(jnp.where(n_counted - n_le >= n_sel, cands[sub], prefixes[sub]))
        return tuple(new)

    return lax.fori_loop(0, TOP_BITS, top_step,
                         tuple(jnp.zeros((rows, LANES), I32) for _ in range(n_sub)))


def _select_rows(key_ref, top_prefix, tri_ref, out_ref, *, n_act, rows, tk, n_sel):
    tq = rows
    n_chunks = tk // LANES

    ones_mat = jnp.ones((LANES, LANES), BF16)

    def count(hit_fn):
        def pair(p, acc):
            for kb in (2 * p, 2 * p + 1):
                for c in range(n_chunks):
                    acc = acc + jnp.where(hit_fn(key_ref[kb, :, c * LANES:(c + 1) * LANES]), 1, 0)
            return acc
        acc = lax.fori_loop(0, (n_act + 1) // 2, pair, jnp.zeros((tq, LANES), I32))
        return jnp.dot(acc.astype(F32).astype(BF16), ones_mat, preferred_element_type=F32)

    def bit_step(i, prefix):
        cand = prefix | jnp.left_shift(jnp.int32(1), 31 - i)
        cand_signed = cand ^ jnp.int32(INT_MIN)
        cnt = count(lambda k: k >= cand_signed)
        return jnp.where(cnt >= n_sel, cand, prefix)

    prefix = lax.fori_loop(TOP_BITS, 32, bit_step, jnp.left_shift(top_prefix, 32 - TOP_BITS))
    tau_rep = prefix ^ jnp.int32(INT_MIN)
    n_gt = count(lambda k: k > tau_rep)
    n_eq = count(lambda k: k == tau_rep)
    need_rep = n_sel - n_gt

    er = min(rows, EMIT_ROWS)
    for g in range(rows // er):
        rs = slice(g * er, (g + 1) * er)
        tau = tau_rep[rs, :1]
        need = need_rep[rs, :1]
        all_ties_taken = jnp.max(jnp.where(need_rep[rs] >= n_eq[rs], 0.0, 1.0)) == 0.0

        def emit_all_ties(kb, carry, rs=rs, tau=tau):
            key = key_ref[kb, rs, :]
            sel = (key >= tau) & (key > KEY_NEG_INF)
            out_ref[kb, rs, :] = jnp.where(sel, 0.0, MASKED_BIAS).astype(out_ref.dtype)
            return carry

        def emit_ranked(kb, seen, rs=rs, tau=tau, need=need):
            for c in range(n_chunks):
                cs = slice(c * LANES, (c + 1) * LANES)
                key = key_ref[kb, rs, cs]
                eq = key == tau
                rank = seen + jnp.dot(jnp.where(eq, 1.0, 0.0).astype(BF16), tri_ref[...],
                                      preferred_element_type=F32)
                sel = ((key > tau) | (eq & (rank <= need))) & (key > KEY_NEG_INF)
                out_ref[kb, rs, cs] = jnp.where(sel, 0.0, MASKED_BIAS).astype(out_ref.dtype)
                seen = rank[:, LANES - 1:LANES]
            return seen

        @pl.when(all_ties_taken)
        def _fast(emit_all_ties=emit_all_ties):
            lax.fori_loop(0, n_act, emit_all_ties, 0)

        @pl.when(jnp.logical_not(all_ties_taken))
        def _ranked(emit_ranked=emit_ranked):
            lax.fori_loop(0, n_act, emit_ranked, jnp.zeros((er, 1), F32))


def select_bias(qi, ki, wi, *, tq, tk, causal, q_pos0, n_valid, n_sel):
    b, t_q, _ = qi.shape
    n_k = ki.shape[1]
    nkb = n_k // tk
    idx = jnp.arange(LANES, dtype=I32)
    tri = (idx[:, None] <= idx[None, :]).astype(BF16)
    rows = min(tq, SEARCH_ROWS)
    kernel = functools.partial(_select_kernel, tq=tq, tk=tk, nkb=nkb, causal=causal,
                               q_pos0=q_pos0, n_valid=n_valid, n_sel=n_sel, rows=rows)
    return pl.pallas_call(
        kernel,
        grid=(b, t_q // tq),
        in_specs=[
            pl.BlockSpec((1, tq, qi.shape[2]), lambda bi, qb: (bi, qb, 0)),
            pl.BlockSpec((1, n_k, ki.shape[2]), lambda bi, qb: (bi, 0, 0)),
            pl.BlockSpec((1, tq, wi.shape[2]), lambda bi, qb: (bi, qb, 0)),
            pl.BlockSpec((LANES, LANES), lambda bi, qb: (0, 0)),
        ],
        out_specs=pl.BlockSpec((1, nkb, tq, tk), lambda bi, qb: (bi, 0, qb, 0)),
        out_shape=jax.ShapeDtypeStruct((b, nkb, t_q, tk), BF16),
        scratch_shapes=[pltpu.VMEM((tq // rows, nkb + 1, rows, tk), I32),
                        pltpu.VMEM((tq // rows, nkb + 1, rows // 2, tk), I32)],
        compiler_params=_params(("parallel", "parallel")),
        name="select_bias",
    )(qi, ki, wi, tri)


def _attn_kernel(qb_ref, kb_ref, last_ref, q_ref, kt_ref, v_ref, bias_ref, o_ref, qs_ref, m_ref,
                 acc_ref, *, tq, tk):
    del qb_ref
    step = pl.program_id(1)
    rows = Q_PER_KV * tq

    @pl.when(kb_ref[step] == 0)
    def _init():
        m_ref[...] = jnp.full(m_ref.shape, MASKED_BIAS, F32)
        acc_ref[...] = jnp.zeros(acc_ref.shape, F32)
        for h in range(N_HEADS):
            g, r = divmod(h, Q_PER_KV)
            qh = q_ref[0, :, h * HEAD_DIM:(h + 1) * HEAD_DIM].astype(F32) * (HEAD_DIM ** -0.5)
            qs_ref[g, r * tq:(r + 1) * tq, :] = qh.astype(qs_ref.dtype)

    bias = jnp.concatenate([bias_ref[0, j] for j in range(bias_ref.shape[1])],
                           axis=1).astype(F32)
    sub_rows = ATTN_HEADS_PER_DOT * tq
    for g in range(N_KV_HEADS):
        for sub in range(Q_PER_KV // ATTN_HEADS_PER_DOT):
            rs = slice(sub * sub_rows, (sub + 1) * sub_rows)
            s = jnp.dot(qs_ref[g, rs, :], kt_ref[0, g], preferred_element_type=F32)
            s = (s * LOG2_E).reshape(ATTN_HEADS_PER_DOT, tq, tk) + bias[None]
            s = s.reshape(sub_rows, tk)
            m_prev = m_ref[g, rs, :]
            m_blk = jnp.max(s, axis=1, keepdims=True)
            m_new = jnp.maximum(m_prev, jnp.broadcast_to(m_blk, m_prev.shape))
            alpha = jnp.exp2(m_prev - m_new)
            p = jnp.concatenate(
                [jnp.exp2(s[:, c * LANES:(c + 1) * LANES] - m_new) for c in range(tk // LANES)],
                axis=1).astype(BF16)
            acc_ref[g, rs, :] = alpha * acc_ref[g, rs, :] + jnp.dot(
                p, v_ref[0, g], preferred_element_type=F32)
            m_ref[g, rs, :] = m_new

    @pl.when(last_ref[step] == 1)
    def _finish():
        for h in range(N_HEADS):
            g, r = divmod(h, Q_PER_KV)
            a = acc_ref[g, r * tq:(r + 1) * tq, :]
            out = a[:, :HEAD_DIM] / a[:, HEAD_DIM:HEAD_DIM + 1]
            o_ref[0, :, h * HEAD_DIM:(h + 1) * HEAD_DIM] = out.astype(o_ref.dtype)


def masked_attention(q, k, v, bias, *, tq, kspan, causal):
    b, t_q, d_q = q.shape
    n_k = k.shape[1]
    tk = kspan * bias.shape[3]
    nkb = n_k // tk
    kt = jnp.transpose(k.reshape(b, n_k, N_KV_HEADS, HEAD_DIM), (0, 2, 3, 1))
    v4 = jnp.transpose(v.reshape(b, n_k, N_KV_HEADS, HEAD_DIM), (0, 2, 1, 3))
    v_ext = jnp.concatenate(
        [v4, jnp.ones((b, N_KV_HEADS, n_k, 1), BF16),
         jnp.zeros((b, N_KV_HEADS, n_k, LANES - HEAD_DIM - 1), BF16)], axis=-1)

    pairs = []
    for qb in range(t_q // tq):
        last = min(nkb - 1, ((qb + 1) * tq - 1) // tk) if causal else nkb - 1
        pairs += [(qb, kb, int(kb == last)) for kb in range(last + 1)]
    qb_ids, kb_ids, last_flags = (jnp.asarray(col, I32) for col in zip(*pairs))

    kernel = functools.partial(_attn_kernel, tq=tq, tk=tk)
    rows = Q_PER_KV * tq
    return pl.pallas_call(
        kernel,
        grid_spec=pltpu.PrefetchScalarGridSpec(
            num_scalar_prefetch=3,
            grid=(b, len(pairs)),
            in_specs=[
                pl.BlockSpec((1, tq, d_q), lambda bi, s, qbs, kbs, lf: (bi, qbs[s], 0)),
                pl.BlockSpec((1, N_KV_HEADS, HEAD_DIM, tk),
                             lambda bi, s, qbs, kbs, lf: (bi, 0, 0, kbs[s])),
                pl.BlockSpec((1, N_KV_HEADS, tk, LANES),
                             lambda bi, s, qbs, kbs, lf: (bi, 0, kbs[s], 0)),
                pl.BlockSpec((1, kspan, tq, bias.shape[3]),
                             lambda bi, s, qbs, kbs, lf: (bi, kbs[s], qbs[s], 0)),
            ],
            out_specs=pl.BlockSpec((1, tq, d_q), lambda bi, s, qbs, kbs, lf: (bi, qbs[s], 0)),
            scratch_shapes=[pltpu.VMEM((N_KV_HEADS, rows, HEAD_DIM), BF16),
                            pltpu.VMEM((N_KV_HEADS, rows, LANES), F32),
                            pltpu.VMEM((N_KV_HEADS, rows, LANES), F32)]),
        out_shape=jax.ShapeDtypeStruct((b, t_q, d_q), BF16),
        compiler_params=_params(("parallel", "arbitrary")),
        name="masked_attention",
    )(qb_ids, kb_ids, last_flags, q, kt, v_ext, bias)


def _conv_kernel(cur_ref, prev_ref, ctx_ref, w_ref, b_ref, g_ref, beta_ref, o_ref, buf_ref,
                 conv_ref, *, tm, cw):
    @pl.when(pl.program_id(1) == 0)
    def _first():
        buf_ref[0:CONV_HALO, :] = ctx_ref[0]

    @pl.when(pl.program_id(1) > 0)
    def _later():
        buf_ref[0:CONV_HALO, :] = prev_ref[0]

    buf_ref[CONV_HALO:CONV_HALO + tm, :] = cur_ref[0]
    lead = CONV_HALO - (CONV_WIDTH - 1)
    c_total = buf_ref.shape[1]
    rw = min(tm, 128)
    for c in range(c_total // cw):
        cs = slice(c * cw, (c + 1) * cw)
        for r in range(tm // rw):
            acc = jnp.zeros((rw, cw), F32) + b_ref[:, cs]
            for j in range(CONV_WIDTH):
                r0 = lead + j + r * rw
                acc = acc + w_ref[j:j + 1, cs] * buf_ref[r0:r0 + rw, cs]
            conv_ref[r * rw:(r + 1) * rw, cs] = acc
    x = conv_ref[...]
    mu = jnp.mean(x, axis=-1, keepdims=True)
    xc = x - mu
    y = xc * lax.rsqrt(jnp.mean(xc * xc, axis=-1, keepdims=True) + EPS)
    y = y * g_ref[...] + beta_ref[...]
    o_ref[0] = _silu(y).astype(o_ref.dtype)


def conv_ln_silu(y, ctx, w_dw, b_dw, ln_g, ln_b, *, tm):
    b, l, c = y.shape
    kernel = functools.partial(_conv_kernel, tm=tm, cw=LANES)
    vec = lambda a: a.reshape(1, c).astype(F32)
    halo_blocks = tm // CONV_HALO
    return pl.pallas_call(
        kernel,
        grid=(b, l // tm),
        in_specs=[
            pl.BlockSpec((1, tm, c), lambda bi, i: (bi, i, 0)),
            pl.BlockSpec((1, CONV_HALO, c),
                         lambda bi, i: (bi, jnp.maximum(i * halo_blocks - 1, 0), 0)),
            pl.BlockSpec((1, CONV_HALO, c), lambda bi, i: (bi, 0, 0)),
            pl.BlockSpec((CONV_WIDTH, c), lambda bi, i: (0, 0)),
            pl.BlockSpec((1, c), lambda bi, i: (0, 0)),
            pl.BlockSpec((1, c), lambda bi, i: (0, 0)),
            pl.BlockSpec((1, c), lambda bi, i: (0, 0)),
        ],
        out_specs=pl.BlockSpec((1, tm, c), lambda bi, i: (bi, i, 0)),
        out_shape=jax.ShapeDtypeStruct((b, l, c), BF16),
        scratch_shapes=[pltpu.VMEM((tm + CONV_HALO, c), F32), pltpu.VMEM((tm, c), F32)],
        compiler_params=_params(("parallel", "parallel")),
        name="conv_ln_silu",
    )(y, y, ctx, w_dw.astype(F32), vec(b_dw), vec(ln_g), vec(ln_b))


ROUTE_IDX1, ROUTE_IDX2, ROUTE_GATE1, ROUTE_GATE2, ROUTE_RANK1, ROUTE_RANK2 = range(6)


def _router_kernel(h_ref, g_ref, wr_ref, lower_ref, u_ref, route_ref, counts_ref, seen_ref, *,
                   n_experts):
    @pl.when(pl.program_id(0) == 0)
    def _init():
        seen_ref[...] = jnp.zeros(seen_ref.shape, F32)

    x = h_ref[...]
    u = x * lax.rsqrt(jnp.mean(x * x, axis=-1, keepdims=True) + EPS) * g_ref[...]
    u_ref[...] = u
    logits = jnp.dot(u.astype(BF16), wr_ref[...], preferred_element_type=F32)
    lane = lax.broadcasted_iota(I32, logits.shape, 1).astype(F32)
    logits = jnp.where(lane < n_experts, logits, -jnp.inf)
    top1 = jnp.max(logits, axis=1, keepdims=True)
    idx1 = jnp.min(jnp.where(logits == top1, lane, float(LANES)), axis=1, keepdims=True)
    rest = jnp.where(lane == idx1, -jnp.inf, logits)
    top2 = jnp.max(rest, axis=1, keepdims=True)
    idx2 = jnp.min(jnp.where(rest == top2, lane, float(LANES)), axis=1, keepdims=True)
    e2 = jnp.exp(top2 - top1)
    denom = 1.0 + e2

    hot1 = jnp.where(lane == idx1, 1.0, 0.0)
    hot2 = jnp.where(lane == idx2, 1.0, 0.0)
    hot = hot1 + hot2
    before = seen_ref[...] + jnp.dot(lower_ref[...], hot.astype(BF16), preferred_element_type=F32)
    rank1 = jnp.sum(hot1 * before, axis=1, keepdims=True)
    rank2 = jnp.sum(hot2 * before, axis=1, keepdims=True)
    seen_ref[...] += jnp.sum(hot, axis=0, keepdims=True)
    counts_ref[...] = seen_ref[...]

    route = jnp.zeros(logits.shape, F32)
    for slot, val in ((ROUTE_IDX1, idx1), (ROUTE_IDX2, idx2), (ROUTE_GATE1, 1.0 / denom),
                      (ROUTE_GATE2, e2 / denom), (ROUTE_RANK1, rank1), (ROUTE_RANK2, rank2)):
        route = jnp.where(lane == slot, val, route)
    route_ref[...] = route


def moe_router(h, gain, w_router_padded, n_experts, *, tm):
    m, d = h.shape
    row = jnp.arange(tm, dtype=I32)
    lower = (row[None, :] < row[:, None]).astype(BF16)
    kernel = functools.partial(_router_kernel, n_experts=n_experts)
    return pl.pallas_call(
        kernel,
        grid=(m // tm,),
        in_specs=[pl.BlockSpec((tm, d), lambda i: (i, 0)),
                  pl.BlockSpec((1, d), lambda i: (0, 0)),
                  pl.BlockSpec((d, LANES), lambda i: (0, 0)),
                  pl.BlockSpec((tm, tm), lambda i: (0, 0))],
        out_specs=[pl.BlockSpec((tm, d), lambda i: (i, 0)),
                   pl.BlockSpec((tm, LANES), lambda i: (i, 0)),
                   pl.BlockSpec((1, LANES), lambda i: (0, 0))],
        out_shape=[jax.ShapeDtypeStruct((m, d), F32), jax.ShapeDtypeStruct((m, LANES), F32),
                   jax.ShapeDtypeStruct((1, LANES), F32)],
        scratch_shapes=[pltpu.VMEM((1, LANES), F32)],
        compiler_params=_params(("arbitrary",)),
        name="moe_router",
    )(h, gain.reshape(1, d).astype(F32), w_router_padded, lower)


def _scatter_rows_kernel(d1_ref, d2_ref, u_ref, init_ref, xs_ref, sem):
    del init_ref
    tm = u_ref.shape[0]

    def row_copy(j, dest):
        return pltpu.make_async_copy(u_ref.at[pl.ds(j, 1)], xs_ref.at[pl.ds(dest, 1)], sem)

    def issue(j, carry):
        row_copy(j, d1_ref[0, 0, j]).start(priority=0)
        row_copy(j, d2_ref[0, 0, j]).start(priority=1)
        return carry

    lax.fori_loop(0, tm, issue, 0, unroll=DMA_LOOP_UNROLL)

    def drain(j, carry):
        row_copy(j, 0).wait()
        row_copy(j, 0).wait()
        return carry

    lax.fori_loop(0, tm, drain, 0, unroll=DMA_LOOP_UNROLL)


def scatter_rows(u, dest1, dest2, n_rows, *, tm):
    m, d = u.shape
    smem_idx = lambda a: a.reshape(m // tm, 1, tm)
    idx_spec = pl.BlockSpec((1, 1, tm), lambda i: (i, 0, 0), memory_space=pltpu.SMEM)
    return pl.pallas_call(
        _scatter_rows_kernel,
        grid=(m // tm,),
        in_specs=[idx_spec, idx_spec,
                  pl.BlockSpec((tm, d), lambda i: (i, 0)),
                  pl.BlockSpec(memory_space=pl.ANY)],
        out_specs=pl.BlockSpec(memory_space=pl.ANY),
        out_shape=jax.ShapeDtypeStruct((n_rows, d), u.dtype),
        scratch_shapes=[pltpu.SemaphoreType.DMA],
        input_output_aliases={3: 0},
        compiler_params=_params(("arbitrary",)),
        name="moe_scatter_rows",
    )(smem_idx(dest1), smem_idx(dest2), u, jnp.zeros((n_rows, d), u.dtype))


def _combine_kernel(d1_ref, d2_ref, rows_ref, route_ref, h_ref, o_ref, buf_ref, sem):
    tm = h_ref.shape[0]

    def row_copy(slot, j, src):
        return pltpu.make_async_copy(rows_ref.at[pl.ds(src, 1)], buf_ref.at[slot, pl.ds(j, 1)], sem)

    def issue(j, carry):
        row_copy(0, j, d1_ref[0, 0, j]).start(priority=0)
        row_copy(1, j, d2_ref[0, 0, j]).start(priority=1)
        return carry

    lax.fori_loop(0, tm, issue, 0, unroll=DMA_LOOP_UNROLL)

    def drain(j, carry):
        row_copy(0, j, 0).wait()
        row_copy(1, j, 0).wait()
        return carry

    lax.fori_loop(0, tm, drain, 0, unroll=DMA_LOOP_UNROLL)
    route = route_ref[...]
    g1 = route[:, ROUTE_GATE1:ROUTE_GATE1 + 1]
    g2 = route[:, ROUTE_GATE2:ROUTE_GATE2 + 1]
    o_ref[...] = h_ref[...] + (g1 * buf_ref[0] + g2 * buf_ref[1])


def combine_rows(rows, dest1, dest2, route, h, *, tm):
    m, d = h.shape
    smem_idx = lambda a: a.reshape(m // tm, 1, tm)
    idx_spec = pl.BlockSpec((1, 1, tm), lambda i: (i, 0, 0), memory_space=pltpu.SMEM)
    return pl.pallas_call(
        _combine_kernel,
        grid=(m // tm,),
        in_specs=[idx_spec, idx_spec,
                  pl.BlockSpec(memory_space=pl.ANY),
                  pl.BlockSpec((tm, LANES), lambda i: (i, 0)),
                  pl.BlockSpec((tm, d), lambda i: (i, 0))],
        out_specs=pl.BlockSpec((tm, d), lambda i: (i, 0)),
        out_shape=jax.ShapeDtypeStruct((m, d), F32),
        scratch_shapes=[pltpu.VMEM((2, tm, d), F32), pltpu.SemaphoreType.DMA],
        compiler_params=_params(("arbitrary",)),
        name="moe_combine_rows",
    )(smem_idx(dest1), smem_idx(dest2), rows, route, h)


def _expert_up_kernel(blk_expert_ref, n_used_ref, xs_ref, wg_ref, wu_ref, o_ref, stage_ref):
    del blk_expert_ref
    i = pl.program_id(0)

    @pl.when(pl.program_id(1) == 0)
    def _stage():
        stage_ref[...] = xs_ref[...].astype(BF16)

    @pl.when(i < n_used_ref[0])
    def _compute():
        u = stage_ref[...]
        a = jnp.dot(u, wg_ref[0], preferred_element_type=F32)
        b = jnp.dot(u, wu_ref[0], preferred_element_type=F32)
        o_ref[...] = (_silu(a) * b).astype(o_ref.dtype)

    @pl.when(i >= n_used_ref[0])
    def _skip():
        o_ref[...] = jnp.zeros(o_ref.shape, o_ref.dtype)


def expert_up(xs, w_gate, w_up, blk_expert, n_used, *, blk, tn):
    n_rows, d = xs.shape
    f = w_gate.shape[2]
    return pl.pallas_call(
        _expert_up_kernel,
        grid_spec=pltpu.PrefetchScalarGridSpec(
            num_scalar_prefetch=2,
            grid=(n_rows // blk, f // tn),
            in_specs=[pl.BlockSpec((blk, d), lambda i, j, be, nu: (i, 0)),
                      pl.BlockSpec((1, d, tn), lambda i, j, be, nu: (be[i], 0, j)),
                      pl.BlockSpec((1, d, tn), lambda i, j, be, nu: (be[i], 0, j))],
            out_specs=pl.BlockSpec((blk, tn), lambda i, j, be, nu: (i, j)),
            scratch_shapes=[pltpu.VMEM((blk, d), BF16)]),
        out_shape=jax.ShapeDtypeStruct((n_rows, f), BF16),
        compiler_params=_params(("parallel", "arbitrary")),
        name="expert_up",
    )(blk_expert, n_used, xs, w_gate, w_up)


def _expert_down_kernel(blk_expert_ref, n_used_ref, hb_ref, wd_ref, o_ref):
    del blk_expert_ref
    i = pl.program_id(0)

    @pl.when(i < n_used_ref[0])
    def _compute():
        o_ref[...] = jnp.dot(hb_ref[...], wd_ref[0], preferred_element_type=F32)

    @pl.when(i >= n_used_ref[0])
    def _skip():
        o_ref[...] = jnp.zeros(o_ref.shape, o_ref.dtype)


def expert_down(hb, w_down, blk_expert, n_used, *, blk, tn):
    n_rows, f = hb.shape
    d = w_down.shape[2]
    return pl.pallas_call(
        _expert_down_kernel,
        grid_spec=pltpu.PrefetchScalarGridSpec(
            num_scalar_prefetch=2,
            grid=(n_rows // blk, d // tn),
            in_specs=[pl.BlockSpec((blk, f), lambda i, j, be, nu: (i, 0)),
                      pl.BlockSpec((1, f, tn), lambda i, j, be, nu: (be[i], 0, j))],
            out_specs=pl.BlockSpec((blk, tn), lambda i, j, be, nu: (i, j))),
        out_shape=jax.ShapeDtypeStruct((n_rows, d), F32),
        compiler_params=_params(("parallel", "parallel")),
        name="expert_down",
    )(blk_expert, n_used, hb, w_down)


def moe_swiglu(h, gain, w_router, w_gate, w_up, w_down):
    m, d = h.shape
    n_experts = w_router.shape[1]
    blk = MOE_ROW_BLOCK if m >= 8 * MOE_ROW_BLOCK else LANES
    tm = _pick_tile(m, 256)
    w_router_padded = jnp.concatenate([w_router, jnp.zeros((d, LANES - n_experts), BF16)], axis=1)
    u, route, counts = moe_router(h, gain, w_router_padded, n_experts, tm=_pick_tile(m, 512))

    counts = counts[0, :n_experts].astype(I32)
    padded = (counts + blk - 1) // blk * blk
    pad_end = jnp.cumsum(padded)
    pad_start = pad_end - padded
    idx1 = route[:, ROUTE_IDX1].astype(I32)
    idx2 = route[:, ROUTE_IDX2].astype(I32)
    dest1 = pad_start[idx1] + route[:, ROUTE_RANK1].astype(I32)
    dest2 = pad_start[idx2] + route[:, ROUTE_RANK2].astype(I32)
    n_blocks = -(-(TOP_K_EXPERTS * m + n_experts * (blk - 1)) // blk)
    blk_start = jnp.arange(n_blocks, dtype=I32) * blk
    blk_expert = jnp.minimum(jnp.searchsorted(pad_end, blk_start, side="right"),
                             n_experts - 1).astype(I32)
    n_used = (pad_end[-1:] // blk).astype(I32)

    xs = scatter_rows(u, dest1, dest2, n_blocks * blk, tm=tm)
    hb = expert_up(xs, w_gate, w_up, blk_expert, n_used, blk=blk, tn=_pick_tile(w_gate.shape[2], 1792))
    rows = expert_down(hb, w_down, blk_expert, n_used, blk=blk, tn=_pick_tile(d, 1024))
    return combine_rows(rows, dest1, dest2, route, h, tm=tm)


def _split_w_in(w_in_layer, d_model):
    attn_dim = N_HEADS * HEAD_DIM
    kv_dim = N_KV_HEADS * HEAD_DIM
    sizes = (attn_dim, kv_dim, kv_dim, IDX_HEADS * IDX_DIM, IDX_DIM, IDX_HEADS, d_model, d_model,
             d_model, d_model)
    parts, off = [], 0
    for s in sizes:
        parts.append(w_in_layer[:, off:off + s])
        off += s
    assert off == w_in_layer.shape[1]
    return parts


def _layer_stack(x, p, weights, cache, *, past):
    bsz, length, d_model = x.shape
    m = bsz * length
    kv_dim = N_KV_HEADS * HEAD_DIM
    pos = past + jnp.arange(length, dtype=I32)
    row_pos = jnp.tile(pos, bsz)
    depth = weights["w_in"].shape[0]

    cos_q, sin_q = _rope_tables(row_pos, 512)
    cos_k, sin_k = _rope_tables(row_pos, kv_dim)
    cos_i, sin_i = _rope_tables(row_pos, LANES, rotary_width=IDX_DIM)
    ones_q = _block_ones(512)
    ones_k = _block_ones(kv_dim)

    n_keys = past + length
    n_sel = min(TOPK_MAX, n_keys // 4)
    if past == 0:
        tq_sel, tk, tq_attn = min(256, length), min(512, length), min(512, length)
        n_keys_pad = n_keys
        kspan = 2 if n_keys_pad % (2 * tk) == 0 else 1
    else:
        tq_sel = tq_attn = length
        n_keys_pad = -(-n_keys // LANES) * LANES
        tk = _pick_tile(n_keys_pad, 512)
        kspan = n_keys_pad // tk

    h = x.reshape(m, d_model)
    ks, vs, kis, convs = [], [], [], []
    for i in range(depth):
        wq, wk, wv, wqi, wki, wwi, wglu_a, wglu_g, wga, wgb = _split_w_in(weights["w_in"][i], d_model)
        w_kiwi = jnp.concatenate(
            [wki, wwi, jnp.zeros((d_model, LANES - IDX_DIM - IDX_HEADS), BF16)], axis=1)
        g_mix = weights["g_mix"][i]

        qn_gain = jnp.tile(weights["q_norm_g"][i].astype(F32), 512 // HEAD_DIM).reshape(1, 512)
        kn_gain = jnp.tile(weights["k_norm_g"][i].astype(F32), kv_dim // HEAD_DIM).reshape(1, kv_dim)

        def norm_rope(accs, tiled, rows, consts):
            y = _head_rms_norm(accs[0], consts[0], consts[1])
            y = _rope(y, rows[0], rows[1])
            return (y, y)

        u = rms_norm_rows(h, g_mix)
        (q_bf,) = fused_mm("proj_q", [u], [wq], [0],
                           lambda a, t, r, c: (_rope(_head_rms_norm(a[0], c[0], c[1]), r[0], r[1]),),
                           [BF16], rows=[cos_q, sin_q], consts=[ones_q, qn_gain], tn=512)
        k_f32, k_bf = fused_mm("proj_k", [u], [wk], [0], norm_rope, [F32, BF16],
                               rows=[cos_k, sin_k], consts=[ones_k, kn_gain])
        v_f32, v_bf = fused_mm("proj_v", [u], [wv], [0], lambda a, t, r, c: (a[0], a[0]),
                               [F32, BF16])
        (qi_bf,) = fused_mm("proj_qi", [u], [wqi], [0],
                            lambda a, t, r, c: (_rope(a[0], r[0], r[1]),),
                            [BF16], rows=[cos_q, sin_q], tn=512)
        (kiwi,) = fused_mm("proj_kiwi", [u], [w_kiwi], [0],
                           lambda a, t, r, c: (_rope(a[0], r[0], r[1]),),
                           [F32], rows=[cos_i, sin_i])
        (glu,) = fused_mm("proj_glu", [u], [wglu_a, wglu_g], [0, 0],
                          lambda a, t, r, c: (a[0] * _sigmoid(a[1]),), [F32])
        gate_a, gate_b = fused_mm("proj_gates", [u], [wga, wgb], [0, 0],
                                  lambda a, t, r, c: (_sigmoid(a[0]), _sigmoid(a[1])),
                                  [F32, F32])

        ki_f32 = kiwi[:, :IDX_DIM]
        wi = kiwi[:, IDX_DIM:IDX_DIM + IDX_HEADS]

        k3 = k_bf.reshape(bsz, length, kv_dim)
        v3 = v_bf.reshape(bsz, length, kv_dim)
        ki3 = ki_f32.astype(BF16).reshape(bsz, length, IDX_DIM)
        if past:
            pad = n_keys_pad - n_keys
            k3 = jnp.concatenate([cache["k"][i].reshape(bsz, past, kv_dim).astype(BF16), k3,
                                  jnp.zeros((bsz, pad, kv_dim), BF16)], axis=1)
            v3 = jnp.concatenate([cache["v"][i].reshape(bsz, past, kv_dim).astype(BF16), v3,
                                  jnp.zeros((bsz, pad, kv_dim), BF16)], axis=1)
            ki3 = jnp.concatenate([cache["kidx"][i].astype(BF16), ki3,
                                   jnp.zeros((bsz, pad, IDX_DIM), BF16)], axis=1)
        bias = select_bias(qi_bf.reshape(bsz, length, -1), ki3, wi.reshape(bsz, length, IDX_HEADS),
                           tq=tq_sel, tk=tk, causal=(past == 0), q_pos0=past, n_valid=n_keys,
                           n_sel=n_sel)
        attn = masked_attention(q_bf.reshape(bsz, length, -1), k3, v3, bias, tq=tq_attn,
                                kspan=kspan, causal=(past == 0))
        attn = attn.reshape(m, -1)

        glu3 = glu.reshape(bsz, length, d_model)
        if past:
            ctx = cache["conv"][i]
        else:
            ctx = jnp.zeros((bsz, CONV_WIDTH - 1, d_model), F32)
        lead = jnp.zeros((bsz, CONV_HALO - (CONV_WIDTH - 1), d_model), F32)
        ctx32 = jnp.concatenate([lead, ctx], axis=1)
        l_pad = -(-length // CONV_HALO) * CONV_HALO
        if l_pad == length:
            y_conv = glu3
        else:
            y_conv = jnp.concatenate(
                [glu3, jnp.zeros((bsz, l_pad - length, d_model), F32)], axis=1)
        if length >= CONV_WIDTH - 1:
            conv_state = glu3[:, length - (CONV_WIDTH - 1):]
        else:
            conv_state = jnp.concatenate([ctx, glu3], axis=1)[:, -(CONV_WIDTH - 1):]
        z = conv_ln_silu(y_conv, ctx32, weights["w_dw"][i], weights["b_dw"][i],
                         weights["conv_ln_g"][i], weights["conv_ln_b"][i], tm=min(256, l_pad))
        z = z[:, :length].reshape(m, d_model)

        (merged,) = fused_mm("mix", [attn, z], [weights["w_o_attn"][i], weights["w_o_conv"][i]],
                             [0, 1], lambda a, t, r, c: (t[0] * a[0] + t[1] * a[1],), [BF16],
                             tiled=[gate_a, gate_b])
        (h,) = fused_mm("out_proj", [merged], [weights["w_out"][i]], [0],
                        lambda a, t, r, c: (t[0] + a[0],), [F32], tiled=[h])

        j = i // 2
        if i % 2 == 0:
            (hb,) = fused_mm("ffn_up", [h], [weights["ffn_w_gate"][j], weights["ffn_w_up"][j]],
                             [0, 0], lambda a, t, r, c: (_silu(a[0]) * a[1],), [BF16],
                             gains=[weights["g_ffn"][i]], tn=1408)
            (h,) = fused_mm("ffn_down", [hb], [weights["ffn_w_down"][j]], [0],
                            lambda a, t, r, c: (t[0] + a[0],), [F32], tiled=[h])
        else:
            h = moe_swiglu(h, weights["g_ffn"][i], weights["moe_router"][j],
                           weights["moe_w_gate"][j], weights["moe_w_up"][j],
                           weights["moe_w_down"][j])

        (h,) = fused_mm("ple", [h, p[i].reshape(m, -1)],
                        [weights["w_ple_gate"][i], weights["w_ple_proj"][i]],
                        [0, 1], lambda a, t, r, c: (t[0] + _sigmoid(a[0]) * a[1],), [F32],
                        gains=[weights["g_ple"][i], None], tiled=[h])

        ks.append(k_f32.reshape(bsz, length, N_KV_HEADS, HEAD_DIM))
        vs.append(v_f32.reshape(bsz, length, N_KV_HEADS, HEAD_DIM))
        kis.append(ki_f32.reshape(bsz, length, IDX_DIM))
        convs.append(conv_state)
    return (h.reshape(bsz, length, d_model), jnp.stack(ks), jnp.stack(vs), jnp.stack(kis),
            jnp.stack(convs))


def kernel(x_prompt, x_sample, cache_k, cache_v, cache_kidx, state_conv, p_prompt, p_sample, w_in, q_norm_g, k_norm_g, w_dw, b_dw, conv_ln_g, conv_ln_b, w_o_attn, w_o_conv, w_out, g_mix, g_ffn, ffn_w_gate, ffn_w_up, ffn_w_down, moe_router, moe_w_gate, moe_w_up, moe_w_down, g_ple, w_ple_gate, w_ple_proj):
    bf = lambda a: a.astype(BF16)
    weights = dict(
        w_in=bf(w_in), q_norm_g=q_norm_g, k_norm_g=k_norm_g, w_dw=w_dw, b_dw=b_dw,
        conv_ln_g=conv_ln_g, conv_ln_b=conv_ln_b, w_o_attn=bf(w_o_attn), w_o_conv=bf(w_o_conv),
        w_out=bf(w_out), g_mix=g_mix, g_ffn=g_ffn, ffn_w_gate=bf(ffn_w_gate),
        ffn_w_up=bf(ffn_w_up), ffn_w_down=bf(ffn_w_down), moe_router=bf(moe_router),
        moe_w_gate=bf(moe_w_gate), moe_w_up=bf(moe_w_up), moe_w_down=bf(moe_w_down), g_ple=g_ple,
        w_ple_gate=bf(w_ple_gate), w_ple_proj=bf(w_ple_proj))
    cache = dict(k=cache_k, v=cache_v, kidx=cache_kidx, conv=state_conv)
    past_len = cache_k.shape[2]
    y_p, k_p, v_p, ki_p, conv_p = _layer_stack(x_prompt, p_prompt, weights, None, past=0)
    y_s, k_s, v_s, ki_s, conv_s = _layer_stack(x_sample, p_sample, weights, cache, past=past_len)
    return (y_p, y_s, k_p, v_p, ki_p, conv_p, k_s, v_s, ki_s, conv_s)
```

```python
import functools

import jax
import jax.numpy as jnp
from jax import lax
from jax.experimental import pallas as pl
from jax.experimental.pallas import tpu as pltpu

F32 = jnp.float32
BF16 = jnp.bfloat16
I32 = jnp.int32

CHUNK = 64
CHUNK_SHIFT = CHUNK.bit_length() - 1
assert 1 << CHUNK_SHIFT == CHUNK
N_HEADS = 16
HEAD_DIM = 64
N_KV_HEADS = 4
Q_PER_KV = N_HEADS // N_KV_HEADS
IDX_HEADS = 8
IDX_DIM = 64
TOPK_MAX = 256
ROPE_THETA = 10000.0
CONV_WIDTH = 31
TOP_K_EXPERTS = 2
EPS = 1e-6

LANES = 128
CONV_HALO = 32
MASKED_BIAS = -1e30
LOG2_E = 1.4426950408889634
INT_MIN = -2 ** 31
KEY_NEG_INF = -2139095041
SEARCH_ROWS = 128
EMIT_ROWS = 128
HALF_WORD_BITS = 16
TOP_BITS = HALF_WORD_BITS - 1
DMA_LOOP_UNROLL = 8
ATTN_HEADS_PER_DOT = Q_PER_KV
MOE_ROW_BLOCK = 512
VMEM_LIMIT = 52 * 1024 * 1024


def _params(semantics):
    return pltpu.CompilerParams(dimension_semantics=semantics, vmem_limit_bytes=VMEM_LIMIT)


def _pick_tile(n, pref):
    if n <= pref:
        return n
    best = None
    for t in range(LANES, pref + 1, LANES):
        if n % t == 0:
            best = t
    assert best is not None, (n, pref)
    return best


def _rms_norm_kernel(x_ref, g_ref, o_ref):
    x = x_ref[...]
    y = x * lax.rsqrt(jnp.mean(x * x, axis=-1, keepdims=True) + EPS) * g_ref[...]
    o_ref[...] = y.astype(o_ref.dtype)


def rms_norm_rows(x, gain, *, tm=512):
    m, d = x.shape
    tm = _pick_tile(m, tm) if m % LANES == 0 else m
    return pl.pallas_call(
        _rms_norm_kernel,
        grid=(m // tm,),
        in_specs=[pl.BlockSpec((tm, d), lambda i: (i, 0)), pl.BlockSpec((1, d), lambda i: (0, 0))],
        out_specs=pl.BlockSpec((tm, d), lambda i: (i, 0)),
        out_shape=jax.ShapeDtypeStruct((m, d), BF16),
        compiler_params=_params(("parallel",)),
        name="rms_norm_rows",
    )(x, gain.reshape(1, d).astype(F32))


def _fused_mm_kernel(*refs, n_lhs, has_gain, staged, pair_lhs, n_tiled, n_rows, n_consts, n_out,
                     epilogue):
    it = iter(refs)
    lhs_refs = [next(it) for _ in range(n_lhs)]
    gain_refs = [next(it) if has_gain[a] else None for a in range(n_lhs)]
    rhs_refs = [next(it) for _ in range(len(pair_lhs))]
    tiled_refs = [next(it) for _ in range(n_tiled)]
    row_refs = [next(it) for _ in range(n_rows)]
    const_refs = [next(it) for _ in range(n_consts)]
    out_refs = [next(it) for _ in range(n_out)]
    operand_refs = [next(it) if staged[a] else lhs_refs[a] for a in range(n_lhs)]

    if any(staged):
        @pl.when(pl.program_id(1) == 0)
        def _stage():
            for a in range(n_lhs):
                if not staged[a]:
                    continue
                x = lhs_refs[a][...].astype(F32)
                if has_gain[a]:
                    x = x * lax.rsqrt(jnp.mean(x * x, axis=-1, keepdims=True) + EPS)
                    x = x * gain_refs[a][...]
                operand_refs[a][...] = x.astype(BF16)

    accs = [jnp.dot(operand_refs[pair_lhs[r]][...], rhs_refs[r][...], preferred_element_type=F32)
            for r in range(len(pair_lhs))]
    outs = epilogue(accs, [t[...] for t in tiled_refs], [t[...] for t in row_refs],
                    [t[...] for t in const_refs])
    for o_ref, o in zip(out_refs, outs):
        o_ref[...] = o.astype(o_ref.dtype)


def fused_mm(name, lhs, rhs, pair_lhs, epilogue, out_dtypes, *, gains=None, tiled=(), rows=(),
             consts=(), tm=1024, tn=1024):
    m = lhs[0].shape[0]
    n = rhs[0].shape[1]
    tm = _pick_tile(m, tm) if m % LANES == 0 else m
    tn = _pick_tile(n, tn)
    n_lhs = len(lhs)
    gains = list(gains) if gains is not None else [None] * n_lhs
    has_gain = tuple(g is not None for g in gains)
    staged = tuple(has_gain[a] or lhs[a].dtype != BF16 for a in range(n_lhs))

    in_specs, args = [], []
    for x in lhs:
        in_specs.append(pl.BlockSpec((tm, x.shape[1]), lambda i, j: (i, 0)))
        args.append(x)
    for g in gains:
        if g is not None:
            g2 = g.reshape(1, -1).astype(F32)
            in_specs.append(pl.BlockSpec(g2.shape, lambda i, j: (0, 0)))
            args.append(g2)
    for w in rhs:
        in_specs.append(pl.BlockSpec((w.shape[0], tn), lambda i, j: (0, j)))
        args.append(w)
    for t in tiled:
        in_specs.append(pl.BlockSpec((tm, tn), lambda i, j: (i, j)))
        args.append(t)
    for t in rows:
        in_specs.append(pl.BlockSpec((tm, t.shape[1]), lambda i, j: (i, 0)))
        args.append(t)
    for c in consts:
        in_specs.append(pl.BlockSpec(c.shape, lambda i, j: (0, 0)))
        args.append(c)

    kernel = functools.partial(
        _fused_mm_kernel, n_lhs=n_lhs, has_gain=has_gain, staged=staged, pair_lhs=tuple(pair_lhs),
        n_tiled=len(tiled), n_rows=len(rows), n_consts=len(consts), n_out=len(out_dtypes),
        epilogue=epilogue)
    outs = pl.pallas_call(
        kernel,
        grid=(m // tm, n // tn),
        in_specs=in_specs,
        out_specs=[pl.BlockSpec((tm, tn), lambda i, j: (i, j)) for _ in out_dtypes],
        out_shape=[jax.ShapeDtypeStruct((m, n), dt) for dt in out_dtypes],
        scratch_shapes=[pltpu.VMEM((tm, x.shape[1]), BF16) for a, x in enumerate(lhs) if staged[a]],
        compiler_params=_params(("parallel", "arbitrary")),
        name=name,
    )(*args)
    return outs


def _head_rms_norm(x, block_ones, gain):
    x2 = x * x
    hi = x2.astype(BF16)
    lo = (x2 - hi.astype(F32)).astype(BF16)
    ss = (jnp.dot(hi, block_ones, preferred_element_type=F32)
          + jnp.dot(lo, block_ones, preferred_element_type=F32))
    return x * lax.rsqrt(ss * (1.0 / HEAD_DIM) + EPS) * gain


def _rope(x, cos, sin_signed):
    n = x.shape[1]
    lane = lax.broadcasted_iota(I32, x.shape, 1)
    first_half = (lane & (HEAD_DIM // 2)) == 0
    partner = jnp.where(first_half, pltpu.roll(x, n - HEAD_DIM // 2, 1),
                        pltpu.roll(x, HEAD_DIM // 2, 1))
    return x * cos + partner * sin_signed


def _sigmoid(x):
    return 1.0 / (1.0 + jnp.exp(-x))


def _silu(x):
    return x * _sigmoid(x)


def _rope_tables(pos, width, rotary_width=None):
    half = HEAD_DIM // 2
    inv = ROPE_THETA ** (-jnp.arange(half, dtype=F32) / half)
    ang = pos.astype(F32)[:, None] * inv[None, :]
    cos = jnp.cos(ang)
    sin = jnp.sin(ang)
    cos64 = jnp.concatenate([cos, cos], axis=-1)
    sin64 = jnp.concatenate([-sin, sin], axis=-1)
    rotary_width = width if rotary_width is None else rotary_width
    reps = rotary_width // HEAD_DIM
    cos_t = jnp.tile(cos64, (1, reps))
    sin_t = jnp.tile(sin64, (1, reps))
    if rotary_width < width:
        pad = width - rotary_width
        cos_t = jnp.concatenate([cos_t, jnp.ones((pos.shape[0], pad), F32)], axis=-1)
        sin_t = jnp.concatenate([sin_t, jnp.zeros((pos.shape[0], pad), F32)], axis=-1)
    return cos_t, sin_t


def _block_ones(width):
    head = jnp.arange(width, dtype=I32) // HEAD_DIM
    return (head[:, None] == head[None, :]).astype(BF16)


def _float_key(x):
    b = lax.bitcast_convert_type(x, I32)
    return jnp.where(b >= 0, b, b ^ jnp.int32(0x7FFFFFFF))


def _select_kernel(qi_ref, ki_ref, wi_ref, tri_ref, bias_ref, key_ref, top_ref, *, tq, tk, nkb,
                   causal, q_pos0, n_valid, n_sel, rows):
    qb = pl.program_id(1)
    n_chunks = tk // LANES
    n_sub = tq // rows
    half = rows // 2
    if causal:
        n_act = jnp.minimum(nkb, ((qb + 1) * tq + tk - 1) // tk)
    else:
        n_act = nkb
    q_pos = q_pos0 + qb * tq + lax.broadcasted_iota(I32, (tq, tk), 0)
    q_chunk = lax.shift_right_logical(q_pos, CHUNK_SHIFT)
    w = (wi_ref[0] * (IDX_HEADS ** -0.5)) * (IDX_DIM ** -0.5)
    qi = qi_ref[0]

    def score_block(kb, carry):
        start = pl.multiple_of(kb * tk, tk)
        ki = ki_ref[0, pl.ds(start, tk), :]
        s = jnp.zeros((tq, tk), F32)
        for h in range(IDX_HEADS):
            r = lax.dot_general(qi[:, h * IDX_DIM:(h + 1) * IDX_DIM], ki,
                                (((1,), (1,)), ((), ())), preferred_element_type=F32)
            s = s + jnp.maximum(r, 0.0) * w[:, h:h + 1]
        k_pos = start + lax.broadcasted_iota(I32, (tq, tk), 1)
        admissible = (lax.shift_right_logical(k_pos, CHUNK_SHIFT) <= q_chunk) & (k_pos < n_valid)
        key = _float_key(jnp.where(admissible, s, -jnp.inf))
        top = lax.shift_right_logical(key ^ jnp.int32(INT_MIN), 32 - TOP_BITS)
        for sub in range(n_sub):
            key_ref[sub, kb] = key[sub * rows:(sub + 1) * rows]
            lo_rows = top[sub * rows:sub * rows + half]
            hi_rows = top[sub * rows + half:(sub + 1) * rows]
            top_ref[sub, kb] = lo_rows | jnp.left_shift(hi_rows, HALF_WORD_BITS)
        return carry

    lax.fori_loop(0, n_act, score_block, 0)
    for sub in range(n_sub):
        key_ref[sub, n_act] = jnp.full((rows, tk), INT_MIN, I32)
        top_ref[sub, n_act] = jnp.zeros((half, tk), I32)
    packed = dict(n_sub=n_sub, n_act=n_act, rows=rows, tk=tk)
    zero = jnp.zeros((rows, LANES), F32)
    top_digit = _search_packed_digits(top_ref, [zero] * n_sub, n_sel=n_sel, **packed)
    at_most = _count_packed_le(top_ref, top_digit, **packed)
    above = [_keys_counted(n_act, tk) - c for c in at_most]

    def bucket_block(kb, carry):
        for sub in range(n_sub):
            u = key_ref[sub, kb] ^ jnp.int32(INT_MIN)
            top = jnp.concatenate([top_digit[sub]] * n_chunks, axis=1)
            mid = lax.shift_right_logical(u, 32 - 2 * TOP_BITS) & ((1 << TOP_BITS) - 1)
            mid = jnp.where(lax.shift_right_logical(u, 32 - TOP_BITS) == top, mid, 0)
            top_ref[sub, kb] = mid[:half] | jnp.left_shift(mid[half:], HALF_WORD_BITS)
        return carry

    lax.fori_loop(0, n_act, bucket_block, 0)
    mid_digit = _search_packed_digits(top_ref, above, n_sel=n_sel, **packed)
    for sub in range(n_sub):
        prefix = (jnp.left_shift(top_digit[sub], 32 - TOP_BITS)
                  | jnp.left_shift(mid_digit[sub], 32 - 2 * TOP_BITS))
        _select_rows(key_ref.at[sub], prefix, tri_ref,
                     bias_ref.at[0, :, sub * rows:(sub + 1) * rows, :],
                     n_act=n_act, rows=rows, tk=tk, n_sel=n_sel)

    def fill_block(kb, carry):
        bias_ref[0, kb] = jnp.full((tq, tk), MASKED_BIAS, bias_ref.dtype)
        return carry

    lax.fori_loop(n_act, nkb, fill_block, 0)


def _keys_counted(n_act, tk):
    return (tk * 2) * ((n_act + 1) // 2)


def _count_packed_le(dig_ref, limits, *, n_sub, n_act, rows, tk):
    half = rows // 2
    n_chunks = tk // LANES
    ones_mat = jnp.ones((LANES, LANES), BF16)
    pair_ones = 1 | (1 << HALF_WORD_BITS)
    half_mask = (1 << HALF_WORD_BITS) - 1
    cws = []
    for limit in limits:
        guarded = limit | (1 << TOP_BITS)
        cws.append(guarded[:half] | jnp.left_shift(guarded[half:], HALF_WORD_BITS))

    def pair(p, accs):
        accs = list(accs)
        for kb in (2 * p, 2 * p + 1):
            for c in range(n_chunks):
                for sub in range(n_sub):
                    d = cws[sub] - dig_ref[sub, kb, :, c * LANES:(c + 1) * LANES]
                    accs[sub] = accs[sub] + (lax.shift_right_logical(d, TOP_BITS) & pair_ones)
        return tuple(accs)

    accs = lax.fori_loop(0, (n_act + 1) // 2, pair,
                         tuple(jnp.zeros((half, LANES), I32) for _ in range(n_sub)))
    counts = []
    for acc in accs:
        le = jnp.concatenate([acc & half_mask, lax.shift_right_logical(acc, HALF_WORD_BITS)], axis=0)
        counts.append(jnp.dot(le.astype(F32).astype(BF16), ones_mat, preferred_element_type=F32))
    return counts


def _search_packed_digits(dig_ref, bases, *, n_sel, n_sub, n_act, rows, tk):
    n_counted = _keys_counted(n_act, tk)

    def step(i, prefixes):
        cands = [p | jnp.left_shift(jnp.int32(1), TOP_BITS - 1 - i) for p in prefixes]
        n_le = _count_packed_le(dig_ref, [c - 1 for c in cands], n_sub=n_sub, n_act=n_act,
                                rows=rows, tk=tk)
        return tuple(jnp.where(bases[s] + (n_counted - n_le[s]) >= n_sel, cands[s], prefixes[s])
                     for s in range(n_sub))

    return lax.fori_loop(0, TOP_BITS, step,
                         tuple(jnp.zeros((rows, LANES), I32) for _ in range(n_sub)))


def _select_rows(key_ref, packed_prefix, tri_ref, out_ref, *, n_act, rows, tk, n_sel):
    tq = rows
    n_chunks = tk // LANES

    ones_mat = jnp.ones((LANES, LANES), BF16)

    def count(hit_fn):
        def pair(p, acc):
            for kb in (2 * p, 2 * p + 1):
                for c in range(n_chunks):
                    acc = acc + jnp.where(hit_fn(key_ref[kb, :, c * LANES:(c + 1) * LANES]), 1, 0)
            return acc
        acc = lax.fori_loop(0, (n_act + 1) // 2, pair, jnp.zeros((tq, LANES), I32))
        return jnp.dot(acc.astype(F32).astype(BF16), ones_mat, preferred_element_type=F32)

    def bit_step(i, prefix):
        cand = prefix | jnp.left_shift(jnp.int32(1), 31 - i)
        cand_signed = cand ^ jnp.int32(INT_MIN)
        cnt = count(lambda k: k >= cand_signed)
        return jnp.where(cnt >= n_sel, cand, prefix)

    prefix = lax.fori_loop(2 * TOP_BITS, 32, bit_step, packed_prefix)
    tau_rep = prefix ^ jnp.int32(INT_MIN)
    n_gt = count(lambda k: k > tau_rep)
    n_eq = count(lambda k: k == tau_rep)
    need_rep = n_sel - n_gt

    er = min(rows, EMIT_ROWS)
    for g in range(rows // er):
        rs = slice(g * er, (g + 1) * er)
        tau = tau_rep[rs, :1]
        need = need_rep[rs, :1]
        all_ties_taken = jnp.max(jnp.where(need_rep[rs] >= n_eq[rs], 0.0, 1.0)) == 0.0

        def emit_all_ties(kb, carry, rs=rs, tau=tau):
            key = key_ref[kb, rs, :]
            sel = (key >= tau) & (key > KEY_NEG_INF)
            out_ref[kb, rs, :] = jnp.where(sel, 0.0, MASKED_BIAS).astype(out_ref.dtype)
            return carry

        def emit_ranked(kb, seen, rs=rs, tau=tau, need=need):
            for c in range(n_chunks):
                cs = slice(c * LANES, (c + 1) * LANES)
                key = key_ref[kb, rs, cs]
                eq = key == tau
                rank = seen + jnp.dot(jnp.where(eq, 1.0, 0.0).astype(BF16), tri_ref[...],
                                      preferred_element_type=F32)
                sel = ((key > tau) | (eq & (rank <= need))) & (key > KEY_NEG_INF)
                out_ref[kb, rs, cs] = jnp.where(sel, 0.0, MASKED_BIAS).astype(out_ref.dtype)
                seen = rank[:, LANES - 1:LANES]
            return seen

        @pl.when(all_ties_taken)
        def _fast(emit_all_ties=emit_all_ties):
            lax.fori_loop(0, n_act, emit_all_ties, 0)

        @pl.when(jnp.logical_not(all_ties_taken))
        def _ranked(emit_ranked=emit_ranked):
            lax.fori_loop(0, n_act, emit_ranked, jnp.zeros((er, 1), F32))


def select_bias(qi, ki, wi, *, tq, tk, causal, q_pos0, n_valid, n_sel):
    b, t_q, _ = qi.shape
    n_k = ki.shape[1]
    nkb = n_k // tk
    idx = jnp.arange(LANES, dtype=I32)
    tri = (idx[:, None] <= idx[None, :]).astype(BF16)
    rows = min(tq, SEARCH_ROWS)
    kernel = functools.partial(_select_kernel, tq=tq, tk=tk, nkb=nkb, causal=causal,
                               q_pos0=q_pos0, n_valid=n_valid, n_sel=n_sel, rows=rows)
    return pl.pallas_call(
        kernel,
        grid=(b, t_q // tq),
        in_specs=[
            pl.BlockSpec((1, tq, qi.shape[2]), lambda bi, qb: (bi, qb, 0)),
            pl.BlockSpec((1, n_k, ki.shape[2]), lambda bi, qb: (bi, 0, 0)),
            pl.BlockSpec((1, tq, wi.shape[2]), lambda bi, qb: (bi, qb, 0)),
            pl.BlockSpec((LANES, LANES), lambda bi, qb: (0, 0)),
        ],
        out_specs=pl.BlockSpec((1, nkb, tq, tk), lambda bi, qb: (bi, 0, qb, 0)),
        out_shape=jax.ShapeDtypeStruct((b, nkb, t_q, tk), BF16),
        scratch_shapes=[pltpu.VMEM((tq // rows, nkb + 1, rows, tk), I32),
                        pltpu.VMEM((tq // rows, nkb + 1, rows // 2, tk), I32)],
        compiler_params=_params(("parallel", "parallel")),
        name="select_bias",
    )(qi, ki, wi, tri)


def _attn_kernel(qb_ref, kb_ref, last_ref, q_ref, kt_ref, v_ref, bias_ref, o_ref, qs_ref, m_ref,
                 acc_ref, *, tq, tk):
    del qb_ref
    step = pl.program_id(1)
    rows = Q_PER_KV * tq

    @pl.when(kb_ref[step] == 0)
    def _init():
        m_ref[...] = jnp.full(m_ref.shape, MASKED_BIAS, F32)
        acc_ref[...] = jnp.zeros(acc_ref.shape, F32)
        for h in range(N_HEADS):
            g, r = divmod(h, Q_PER_KV)
            qh = q_ref[0, :, h * HEAD_DIM:(h + 1) * HEAD_DIM].astype(F32) * (HEAD_DIM ** -0.5)
            qs_ref[g, r * tq:(r + 1) * tq, :] = qh.astype(qs_ref.dtype)

    bias = jnp.concatenate([bias_ref[0, j] for j in range(bias_ref.shape[1])],
                           axis=1).astype(F32)
    sub_rows = ATTN_HEADS_PER_DOT * tq
    for g in range(N_KV_HEADS):
        for sub in range(Q_PER_KV // ATTN_HEADS_PER_DOT):
            rs = slice(sub * sub_rows, (sub + 1) * sub_rows)
            s = jnp.dot(qs_ref[g, rs, :], kt_ref[0, g], preferred_element_type=F32)
            s = (s * LOG2_E).reshape(ATTN_HEADS_PER_DOT, tq, tk) + bias[None]
            s = s.reshape(sub_rows, tk)
            m_prev = m_ref[g, rs, :]
            m_blk = jnp.max(s, axis=1, keepdims=True)
            m_new = jnp.maximum(m_prev, jnp.broadcast_to(m_blk, m_prev.shape))
            alpha = jnp.exp2(m_prev - m_new)
            p = jnp.concatenate(
                [jnp.exp2(s[:, c * LANES:(c + 1) * LANES] - m_new) for c in range(tk // LANES)],
                axis=1).astype(BF16)
            acc_ref[g, rs, :] = alpha * acc_ref[g, rs, :] + jnp.dot(
                p, v_ref[0, g], preferred_element_type=F32)
            m_ref[g, rs, :] = m_new

    @pl.when(last_ref[step] == 1)
    def _finish():
        for h in range(N_HEADS):
            g, r = divmod(h, Q_PER_KV)
            a = acc_ref[g, r * tq:(r + 1) * tq, :]
            out = a[:, :HEAD_DIM] / a[:, HEAD_DIM:HEAD_DIM + 1]
            o_ref[0, :, h * HEAD_DIM:(h + 1) * HEAD_DIM] = out.astype(o_ref.dtype)


def masked_attention(q, k, v, bias, *, tq, kspan, causal):
    b, t_q, d_q = q.shape
    n_k = k.shape[1]
    tk = kspan * bias.shape[3]
    nkb = n_k // tk
    kt = jnp.transpose(k.reshape(b, n_k, N_KV_HEADS, HEAD_DIM), (0, 2, 3, 1))
    v4 = jnp.transpose(v.reshape(b, n_k, N_KV_HEADS, HEAD_DIM), (0, 2, 1, 3))
    v_ext = jnp.concatenate(
        [v4, jnp.ones((b, N_KV_HEADS, n_k, 1), BF16),
         jnp.zeros((b, N_KV_HEADS, n_k, LANES - HEAD_DIM - 1), BF16)], axis=-1)

    pairs = []
    for qb in range(t_q // tq):
        last = min(nkb - 1, ((qb + 1) * tq - 1) // tk) if causal else nkb - 1
        pairs += [(qb, kb, int(kb == last)) for kb in range(last + 1)]
    qb_ids, kb_ids, last_flags = (jnp.asarray(col, I32) for col in zip(*pairs))

    kernel = functools.partial(_attn_kernel, tq=tq, tk=tk)
    rows = Q_PER_KV * tq
    return pl.pallas_call(
        kernel,
        grid_spec=pltpu.PrefetchScalarGridSpec(
            num_scalar_prefetch=3,
            grid=(b, len(pairs)),
            in_specs=[
                pl.BlockSpec((1, tq, d_q), lambda bi, s, qbs, kbs, lf: (bi, qbs[s], 0)),
                pl.BlockSpec((1, N_KV_HEADS, HEAD_DIM, tk),
                             lambda bi, s, qbs, kbs, lf: (bi, 0, 0, kbs[s])),
                pl.BlockSpec((1, N_KV_HEADS, tk, LANES),
                             lambda bi, s, qbs, kbs, lf: (bi, 0, kbs[s], 0)),
                pl.BlockSpec((1, kspan, tq, bias.shape[3]),
                             lambda bi, s, qbs, kbs, lf: (bi, kbs[s], qbs[s], 0)),
            ],
            out_specs=pl.BlockSpec((1, tq, d_q), lambda bi, s, qbs, kbs, lf: (bi, qbs[s], 0)),
            scratch_shapes=[pltpu.VMEM((N_KV_HEADS, rows, HEAD_DIM), BF16),
                            pltpu.VMEM((N_KV_HEADS, rows, LANES), F32),
                            pltpu.VMEM((N_KV_HEADS, rows, LANES), F32)]),
        out_shape=jax.ShapeDtypeStruct((b, t_q, d_q), BF16),
        compiler_params=_params(("parallel", "arbitrary")),
        name="masked_attention",
    )(qb_ids, kb_ids, last_flags, q, kt, v_ext, bias)


def _conv_kernel(cur_ref, prev_ref, ctx_ref, w_ref, b_ref, g_ref, beta_ref, o_ref, buf_ref,
                 conv_ref, *, tm, cw):
    @pl.when(pl.program_id(1) == 0)
    def _first():
        buf_ref[0:CONV_HALO, :] = ctx_ref[0]

    @pl.when(pl.program_id(1) > 0)
    def _later():
        buf_ref[0:CONV_HALO, :] = prev_ref[0]

    buf_ref[CONV_HALO:CONV_HALO + tm, :] = cur_ref[0]
    lead = CONV_HALO - (CONV_WIDTH - 1)
    c_total = buf_ref.shape[1]
    rw = min(tm, 128)
    for c in range(c_total // cw):
        cs = slice(c * cw, (c + 1) * cw)
        for r in range(tm // rw):
            acc = jnp.zeros((rw, cw), F32) + b_ref[:, cs]
            for j in range(CONV_WIDTH):
                r0 = lead + j + r * rw
                acc = acc + w_ref[j:j + 1, cs] * buf_ref[r0:r0 + rw, cs]
            conv_ref[r * rw:(r + 1) * rw, cs] = acc
    x = conv_ref[...]
    mu = jnp.mean(x, axis=-1, keepdims=True)
    xc = x - mu
    y = xc * lax.rsqrt(jnp.mean(xc * xc, axis=-1, keepdims=True) + EPS)
    y = y * g_ref[...] + beta_ref[...]
    o_ref[0] = _silu(y).astype(o_ref.dtype)


def conv_ln_silu(y, ctx, w_dw, b_dw, ln_g, ln_b, *, tm):
    b, l, c = y.shape
    kernel = functools.partial(_conv_kernel, tm=tm, cw=LANES)
    vec = lambda a: a.reshape(1, c).astype(F32)
    halo_blocks = tm // CONV_HALO
    return pl.pallas_call(
        kernel,
        grid=(b, l // tm),
        in_specs=[
            pl.BlockSpec((1, tm, c), lambda bi, i: (bi, i, 0)),
            pl.BlockSpec((1, CONV_HALO, c),
                         lambda bi, i: (bi, jnp.maximum(i * halo_blocks - 1, 0), 0)),
            pl.BlockSpec((1, CONV_HALO, c), lambda bi, i: (bi, 0, 0)),
            pl.BlockSpec((CONV_WIDTH, c), lambda bi, i: (0, 0)),
            pl.BlockSpec((1, c), lambda bi, i: (0, 0)),
            pl.BlockSpec((1, c), lambda bi, i: (0, 0)),
            pl.BlockSpec((1, c), lambda bi, i: (0, 0)),
        ],
        out_specs=pl.BlockSpec((1, tm, c), lambda bi, i: (bi, i, 0)),
        out_shape=jax.ShapeDtypeStruct((b, l, c), BF16),
        scratch_shapes=[pltpu.VMEM((tm + CONV_HALO, c), F32), pltpu.VMEM((tm, c), F32)],
        compiler_params=_params(("parallel", "parallel")),
        name="conv_ln_silu",
    )(y, y, ctx, w_dw.astype(F32), vec(b_dw), vec(ln_g), vec(ln_b))


ROUTE_IDX1, ROUTE_IDX2, ROUTE_GATE1, ROUTE_GATE2, ROUTE_RANK1, ROUTE_RANK2 = range(6)


def _router_kernel(h_ref, g_ref, wr_ref, lower_ref, u_ref, route_ref, counts_ref, seen_ref, *,
                   n_experts):
    @pl.when(pl.program_id(0) == 0)
    def _init():
        seen_ref[...] = jnp.zeros(seen_ref.shape, F32)

    x = h_ref[...]
    u = x * lax.rsqrt(jnp.mean(x * x, axis=-1, keepdims=True) + EPS) * g_ref[...]
    u_ref[...] = u
    logits = jnp.dot(u.astype(BF16), wr_ref[...], preferred_element_type=F32)
    lane = lax.broadcasted_iota(I32, logits.shape, 1).astype(F32)
    logits = jnp.where(lane < n_experts, logits, -jnp.inf)
    top1 = jnp.max(logits, axis=1, keepdims=True)
    idx1 = jnp.min(jnp.where(logits == top1, lane, float(LANES)), axis=1, keepdims=True)
    rest = jnp.where(lane == idx1, -jnp.inf, logits)
    top2 = jnp.max(rest, axis=1, keepdims=True)
    idx2 = jnp.min(jnp.where(rest == top2, lane, float(LANES)), axis=1, keepdims=True)
    e2 = jnp.exp(top2 - top1)
    denom = 1.0 + e2

    hot1 = jnp.where(lane == idx1, 1.0, 0.0)
    hot2 = jnp.where(lane == idx2, 1.0, 0.0)
    hot = hot1 + hot2
    before = seen_ref[...] + jnp.dot(lower_ref[...], hot.astype(BF16), preferred_element_type=F32)
    rank1 = jnp.sum(hot1 * before, axis=1, keepdims=True)
    rank2 = jnp.sum(hot2 * before, axis=1, keepdims=True)
    seen_ref[...] += jnp.sum(hot, axis=0, keepdims=True)
    counts_ref[...] = seen_ref[...]

    route = jnp.zeros(logits.shape, F32)
    for slot, val in ((ROUTE_IDX1, idx1), (ROUTE_IDX2, idx2), (ROUTE_GATE1, 1.0 / denom),
                      (ROUTE_GATE2, e2 / denom), (ROUTE_RANK1, rank1), (ROUTE_RANK2, rank2)):
        route = jnp.where(lane == slot, val, route)
    route_ref[...] = route


def moe_router(h, gain, w_router_padded, n_experts, *, tm):
    m, d = h.shape
    row = jnp.arange(tm, dtype=I32)
    lower = (row[None, :] < row[:, None]).astype(BF16)
    kernel = functools.partial(_router_kernel, n_experts=n_experts)
    return pl.pallas_call(
        kernel,
        grid=(m // tm,),
        in_specs=[pl.BlockSpec((tm, d), lambda i: (i, 0)),
                  pl.BlockSpec((1, d), lambda i: (0, 0)),
                  pl.BlockSpec((d, LANES), lambda i: (0, 0)),
                  pl.BlockSpec((tm, tm), lambda i: (0, 0))],
        out_specs=[pl.BlockSpec((tm, d), lambda i: (i, 0)),
                   pl.BlockSpec((tm, LANES), lambda i: (i, 0)),
                   pl.BlockSpec((1, LANES), lambda i: (0, 0))],
        out_shape=[jax.ShapeDtypeStruct((m, d), F32), jax.ShapeDtypeStruct((m, LANES), F32),
                   jax.ShapeDtypeStruct((1, LANES), F32)],
        scratch_shapes=[pltpu.VMEM((1, LANES), F32)],
        compiler_params=_params(("arbitrary",)),
        name="moe_router",
    )(h, gain.reshape(1, d).astype(F32), w_router_padded, lower)


def _scatter_rows_kernel(d1_ref, d2_ref, u_ref, init_ref, xs_ref, sem):
    del init_ref
    tm = u_ref.shape[0]

    def row_copy(j, dest):
        return pltpu.make_async_copy(u_ref.at[pl.ds(j, 1)], xs_ref.at[pl.ds(dest, 1)], sem)

    def issue(j, carry):
        row_copy(j, d1_ref[0, 0, j]).start(priority=0)
        row_copy(j, d2_ref[0, 0, j]).start(priority=1)
        return carry

    lax.fori_loop(0, tm, issue, 0, unroll=DMA_LOOP_UNROLL)

    def drain(j, carry):
        row_copy(j, 0).wait()
        row_copy(j, 0).wait()
        return carry

    lax.fori_loop(0, tm, drain, 0, unroll=DMA_LOOP_UNROLL)


def scatter_rows(u, dest1, dest2, n_rows, *, tm):
    m, d = u.shape
    smem_idx = lambda a: a.reshape(m // tm, 1, tm)
    idx_spec = pl.BlockSpec((1, 1, tm), lambda i: (i, 0, 0), memory_space=pltpu.SMEM)
    return pl.pallas_call(
        _scatter_rows_kernel,
        grid=(m // tm,),
        in_specs=[idx_spec, idx_spec,
                  pl.BlockSpec((tm, d), lambda i: (i, 0)),
                  pl.BlockSpec(memory_space=pl.ANY)],
        out_specs=pl.BlockSpec(memory_space=pl.ANY),
        out_shape=jax.ShapeDtypeStruct((n_rows, d), u.dtype),
        scratch_shapes=[pltpu.SemaphoreType.DMA],
        input_output_aliases={3: 0},
        compiler_params=_params(("arbitrary",)),
        name="moe_scatter_rows",
    )(smem_idx(dest1), smem_idx(dest2), u, jnp.zeros((n_rows, d), u.dtype))


def _combine_kernel(d1_ref, d2_ref, rows_ref, route_ref, h_ref, o_ref, buf_ref, sem):
    tm = h_ref.shape[0]

    def row_copy(slot, j, src):
        return pltpu.make_async_copy(rows_ref.at[pl.ds(src, 1)], buf_ref.at[slot, pl.ds(j, 1)], sem)

    def issue(j, carry):
        row_copy(0, j, d1_ref[0, 0, j]).start(priority=0)
        row_copy(1, j, d2_ref[0, 0, j]).start(priority=1)
        return carry

    lax.fori_loop(0, tm, issue, 0, unroll=DMA_LOOP_UNROLL)

    def drain(j, carry):
        row_copy(0, j, 0).wait()
        row_copy(1, j, 0).wait()
        return carry

    lax.fori_loop(0, tm, drain, 0, unroll=DMA_LOOP_UNROLL)
    route = route_ref[...]
    g1 = route[:, ROUTE_GATE1:ROUTE_GATE1 + 1]
    g2 = route[:, ROUTE_GATE2:ROUTE_GATE2 + 1]
    o_ref[...] = h_ref[...] + (g1 * buf_ref[0] + g2 * buf_ref[1])


def combine_rows(rows, dest1, dest2, route, h, *, tm):
    m, d = h.shape
    smem_idx = lambda a: a.reshape(m // tm, 1, tm)
    idx_spec = pl.BlockSpec((1, 1, tm), lambda i: (i, 0, 0), memory_space=pltpu.SMEM)
    return pl.pallas_call(
        _combine_kernel,
        grid=(m // tm,),
        in_specs=[idx_spec, idx_spec,
                  pl.BlockSpec(memory_space=pl.ANY),
                  pl.BlockSpec((tm, LANES), lambda i: (i, 0)),
                  pl.BlockSpec((tm, d), lambda i: (i, 0))],
        out_specs=pl.BlockSpec((tm, d), lambda i: (i, 0)),
        out_shape=jax.ShapeDtypeStruct((m, d), F32),
        scratch_shapes=[pltpu.VMEM((2, tm, d), F32), pltpu.SemaphoreType.DMA],
        compiler_params=_params(("arbitrary",)),
        name="moe_combine_rows",
    )(smem_idx(dest1), smem_idx(dest2), rows, route, h)


def _expert_up_kernel(blk_expert_ref, n_used_ref, xs_ref, wg_ref, wu_ref, o_ref, stage_ref):
    del blk_expert_ref
    i = pl.program_id(0)

    @pl.when(pl.program_id(1) == 0)
    def _stage():
        stage_ref[...] = xs_ref[...].astype(BF16)

    @pl.when(i < n_used_ref[0])
    def _compute():
        u = stage_ref[...]
        a = jnp.dot(u, wg_ref[0], preferred_element_type=F32)
        b = jnp.dot(u, wu_ref[0], preferred_element_type=F32)
        o_ref[...] = (_silu(a) * b).astype(o_ref.dtype)

    @pl.when(i >= n_used_ref[0])
    def _skip():
        o_ref[...] = jnp.zeros(o_ref.shape, o_ref.dtype)


def expert_up(xs, w_gate, w_up, blk_expert, n_used, *, blk, tn):
    n_rows, d = xs.shape
    f = w_gate.shape[2]
    return pl.pallas_call(
        _expert_up_kernel,
        grid_spec=pltpu.PrefetchScalarGridSpec(
            num_scalar_prefetch=2,
            grid=(n_rows // blk, f // tn),
            in_specs=[pl.BlockSpec((blk, d), lambda i, j, be, nu: (i, 0)),
                      pl.BlockSpec((1, d, tn), lambda i, j, be, nu: (be[i], 0, j)),
                      pl.BlockSpec((1, d, tn), lambda i, j, be, nu: (be[i], 0, j))],
            out_specs=pl.BlockSpec((blk, tn), lambda i, j, be, nu: (i, j)),
            scratch_shapes=[pltpu.VMEM((blk, d), BF16)]),
        out_shape=jax.ShapeDtypeStruct((n_rows, f), BF16),
        compiler_params=_params(("parallel", "arbitrary")),
        name="expert_up",
    )(blk_expert, n_used, xs, w_gate, w_up)


def _expert_down_kernel(blk_expert_ref, n_used_ref, hb_ref, wd_ref, o_ref):
    del blk_expert_ref
    i = pl.program_id(0)

    @pl.when(i < n_used_ref[0])
    def _compute():
        o_ref[...] = jnp.dot(hb_ref[...], wd_ref[0], preferred_element_type=F32)

    @pl.when(i >= n_used_ref[0])
    def _skip():
        o_ref[...] = jnp.zeros(o_ref.shape, o_ref.dtype)


def expert_down(hb, w_down, blk_expert, n_used, *, blk, tn):
    n_rows, f = hb.shape
    d = w_down.shape[2]
    return pl.pallas_call(
        _expert_down_kernel,
        grid_spec=pltpu.PrefetchScalarGridSpec(
            num_scalar_prefetch=2,
            grid=(n_rows // blk, d // tn),
            in_specs=[pl.BlockSpec((blk, f), lambda i, j, be, nu: (i, 0)),
                      pl.BlockSpec((1, f, tn), lambda i, j, be, nu: (be[i], 0, j))],
            out_specs=pl.BlockSpec((blk, tn), lambda i, j, be, nu: (i, j))),
        out_shape=jax.ShapeDtypeStruct((n_rows, d), F32),
        compiler_params=_params(("parallel", "parallel")),
        name="expert_down",
    )(blk_expert, n_used, hb, w_down)


def moe_swiglu(h, gain, w_router, w_gate, w_up, w_down):
    m, d = h.shape
    n_experts = w_router.shape[1]
    blk = MOE_ROW_BLOCK if m >= 8 * MOE_ROW_BLOCK else LANES
    tm = _pick_tile(m, 256)
    w_router_padded = jnp.concatenate([w_router, jnp.zeros((d, LANES - n_experts), BF16)], axis=1)
    u, route, counts = moe_router(h, gain, w_router_padded, n_experts, tm=_pick_tile(m, 512))

    counts = counts[0, :n_experts].astype(I32)
    padded = (counts + blk - 1) // blk * blk
    pad_end = jnp.cumsum(padded)
    pad_start = pad_end - padded
    idx1 = route[:, ROUTE_IDX1].astype(I32)
    idx2 = route[:, ROUTE_IDX2].astype(I32)
    dest1 = pad_start[idx1] + route[:, ROUTE_RANK1].astype(I32)
    dest2 = pad_start[idx2] + route[:, ROUTE_RANK2].astype(I32)
    n_blocks = -(-(TOP_K_EXPERTS * m + n_experts * (blk - 1)) // blk)
    blk_start = jnp.arange(n_blocks, dtype=I32) * blk
    blk_expert = jnp.minimum(jnp.searchsorted(pad_end, blk_start, side="right"),
                             n_experts - 1).astype(I32)
    n_used = (pad_end[-1:] // blk).astype(I32)

    xs = scatter_rows(u, dest1, dest2, n_blocks * blk, tm=tm)
    hb = expert_up(xs, w_gate, w_up, blk_expert, n_used, blk=blk, tn=_pick_tile(w_gate.shape[2], 1792))
    rows = expert_down(hb, w_down, blk_expert, n_used, blk=blk, tn=_pick_tile(d, 1024))
    return combine_rows(rows, dest1, dest2, route, h, tm=tm)


def _split_w_in(w_in_layer, d_model):
    attn_dim = N_HEADS * HEAD_DIM
    kv_dim = N_KV_HEADS * HEAD_DIM
    sizes = (attn_dim, kv_dim, kv_dim, IDX_HEADS * IDX_DIM, IDX_DIM, IDX_HEADS, d_model, d_model,
             d_model, d_model)
    parts, off = [], 0
    for s in sizes:
        parts.append(w_in_layer[:, off:off + s])
        off += s
    assert off == w_in_layer.shape[1]
    return parts


def _layer_stack(x, p, weights, cache, *, past):
    bsz, length, d_model = x.shape
    m = bsz * length
    kv_dim = N_KV_HEADS * HEAD_DIM
    pos = past + jnp.arange(length, dtype=I32)
    row_pos = jnp.tile(pos, bsz)
    depth = weights["w_in"].shape[0]

    cos_q, sin_q = _rope_tables(row_pos, 512)
    cos_k, sin_k = _rope_tables(row_pos, kv_dim)
    cos_i, sin_i = _rope_tables(row_pos, LANES, rotary_width=IDX_DIM)
    ones_q = _block_ones(512)
    ones_k = _block_ones(kv_dim)

    n_keys = past + length
    n_sel = min(TOPK_MAX, n_keys // 4)
    if past == 0:
        tq_sel, tk, tq_attn = min(256, length), min(512, length), min(512, length)
        n_keys_pad = n_keys
        kspan = 2 if n_keys_pad % (2 * tk) == 0 else 1
    else:
        tq_sel = tq_attn = length
        n_keys_pad = -(-n_keys // LANES) * LANES
        tk = _pick_tile(n_keys_pad, 512)
        kspan = n_keys_pad // tk

    h = x.reshape(m, d_model)
    ks, vs, kis, convs = [], [], [], []
    for i in range(depth):
        wq, wk, wv, wqi, wki, wwi, wglu_a, wglu_g, wga, wgb = _split_w_in(weights["w_in"][i], d_model)
        w_kiwi = jnp.concatenate(
            [wki, wwi, jnp.zeros((d_model, LANES - IDX_DIM - IDX_HEADS), BF16)], axis=1)
        g_mix = weights["g_mix"][i]

        qn_gain = jnp.tile(weights["q_norm_g"][i].astype(F32), 512 // HEAD_DIM).reshape(1, 512)
        kn_gain = jnp.tile(weights["k_norm_g"][i].astype(F32), kv_dim // HEAD_DIM).reshape(1, kv_dim)

        def norm_rope(accs, tiled, rows, consts):
            y = _head_rms_norm(accs[0], consts[0], consts[1])
            y = _rope(y, rows[0], rows[1])
            return (y, y)

        u = rms_norm_rows(h, g_mix)
        (q_bf,) = fused_mm("proj_q", [u], [wq], [0],
                           lambda a, t, r, c: (_rope(_head_rms_norm(a[0], c[0], c[1]), r[0], r[1]),),
                           [BF16], rows=[cos_q, sin_q], consts=[ones_q, qn_gain], tn=512)
        k_f32, k_bf = fused_mm("proj_k", [u], [wk], [0], norm_rope, [F32, BF16],
                               rows=[cos_k, sin_k], consts=[ones_k, kn_gain])
        v_f32, v_bf = fused_mm("proj_v", [u], [wv], [0], lambda a, t, r, c: (a[0], a[0]),
                               [F32, BF16])
        (qi_bf,) = fused_mm("proj_qi", [u], [wqi], [0],
                            lambda a, t, r, c: (_rope(a[0], r[0], r[1]),),
                            [BF16], rows=[cos_q, sin_q], tn=512)
        (kiwi,) = fused_mm("proj_kiwi", [u], [w_kiwi], [0],
                           lambda a, t, r, c: (_rope(a[0], r[0], r[1]),),
                           [F32], rows=[cos_i, sin_i])
        (glu,) = fused_mm("proj_glu", [u], [wglu_a, wglu_g], [0, 0],
                          lambda a, t, r, c: (a[0] * _sigmoid(a[1]),), [F32])
        gate_a, gate_b = fused_mm("proj_gates", [u], [wga, wgb], [0, 0],
                                  lambda a, t, r, c: (_sigmoid(a[0]), _sigmoid(a[1])),
                                  [F32, F32])

        ki_f32 = kiwi[:, :IDX_DIM]
        wi = kiwi[:, IDX_DIM:IDX_DIM + IDX_HEADS]

        k3 = k_bf.reshape(bsz, length, kv_dim)
        v3 = v_bf.reshape(bsz, length, kv_dim)
        ki3 = ki_f32.astype(BF16).reshape(bsz, length, IDX_DIM)
        if past:
            pad = n_keys_pad - n_keys
            k3 = jnp.concatenate([cache["k"][i].reshape(bsz, past, kv_dim).astype(BF16), k3,
                                  jnp.zeros((bsz, pad, kv_dim), BF16)], axis=1)
            v3 = jnp.concatenate([cache["v"][i].reshape(bsz, past, kv_dim).astype(BF16), v3,
                                  jnp.zeros((bsz, pad, kv_dim), BF16)], axis=1)
            ki3 = jnp.concatenate([cache["kidx"][i].astype(BF16), ki3,
                                   jnp.zeros((bsz, pad, IDX_DIM), BF16)], axis=1)
        bias = select_bias(qi_bf.reshape(bsz, length, -1), ki3, wi.reshape(bsz, length, IDX_HEADS),
                           tq=tq_sel, tk=tk, causal=(past == 0), q_pos0=past, n_valid=n_keys,
                           n_sel=n_sel)
        attn = masked_attention(q_bf.reshape(bsz, length, -1), k3, v3, bias, tq=tq_attn,
                                kspan=kspan, causal=(past == 0))
        attn = attn.reshape(m, -1)

        glu3 = glu.reshape(bsz, length, d_model)
        if past:
            ctx = cache["conv"][i]
        else:
            ctx = jnp.zeros((bsz, CONV_WIDTH - 1, d_model), F32)
        lead = jnp.zeros((bsz, CONV_HALO - (CONV_WIDTH - 1), d_model), F32)
        ctx32 = jnp.concatenate([lead, ctx], axis=1)
        l_pad = -(-length // CONV_HALO) * CONV_HALO
        if l_pad == length:
            y_conv = glu3
        else:
            y_conv = jnp.concatenate(
                [glu3, jnp.zeros((bsz, l_pad - length, d_model), F32)], axis=1)
        if length >= CONV_WIDTH - 1:
            conv_state = glu3[:, length - (CONV_WIDTH - 1):]
        else:
            conv_state = jnp.concatenate([ctx, glu3], axis=1)[:, -(CONV_WIDTH - 1):]
        z = conv_ln_silu(y_conv, ctx32, weights["w_dw"][i], weights["b_dw"][i],
                         weights["conv_ln_g"][i], weights["conv_ln_b"][i], tm=min(256, l_pad))
        z = z[:, :length].reshape(m, d_model)

        (merged,) = fused_mm("mix", [attn, z], [weights["w_o_attn"][i], weights["w_o_conv"][i]],
                             [0, 1], lambda a, t, r, c: (t[0] * a[0] + t[1] * a[1],), [BF16],
                             tiled=[gate_a, gate_b])
        (h,) = fused_mm("out_proj", [merged], [weights["w_out"][i]], [0],
                        lambda a, t, r, c: (t[0] + a[0],), [F32], tiled=[h])

        j = i // 2
        if i % 2 == 0:
            (hb,) = fused_mm("ffn_up", [h], [weights["ffn_w_gate"][j], weights["ffn_w_up"][j]],
                             [0, 0], lambda a, t, r, c: (_silu(a[0]) * a[1],), [BF16],
                             gains=[weights["g_ffn"][i]], tn=1408)
            (h,) = fused_mm("ffn_down", [hb], [weights["ffn_w_down"][j]], [0],
                            lambda a, t, r, c: (t[0] + a[0],), [F32], tiled=[h])
        else:
            h = moe_swiglu(h, weights["g_ffn"][i], weights["moe_router"][j],
                           weights["moe_w_gate"][j], weights["moe_w_up"][j],
                           weights["moe_w_down"][j])

        (h,) = fused_mm("ple", [h, p[i].reshape(m, -1)],
                        [weights["w_ple_gate"][i], weights["w_ple_proj"][i]],
                        [0, 1], lambda a, t, r, c: (t[0] + _sigmoid(a[0]) * a[1],), [F32],
                        gains=[weights["g_ple"][i], None], tiled=[h])

        ks.append(k_f32.reshape(bsz, length, N_KV_HEADS, HEAD_DIM))
        vs.append(v_f32.reshape(bsz, length, N_KV_HEADS, HEAD_DIM))
        kis.append(ki_f32.reshape(bsz, length, IDX_DIM))
        convs.append(conv_state)
    return (h.reshape(bsz, length, d_model), jnp.stack(ks), jnp.stack(vs), jnp.stack(kis),
            jnp.stack(convs))


def kernel(x_prompt, x_sample, cache_k, cache_v, cache_kidx, state_conv, p_prompt, p_sample, w_in, q_norm_g, k_norm_g, w_dw, b_dw, conv_ln_g, conv_ln_b, w_o_attn, w_o_conv, w_out, g_mix, g_ffn, ffn_w_gate, ffn_w_up, ffn_w_down, moe_router, moe_w_gate, moe_w_up, moe_w_down, g_ple, w_ple_gate, w_ple_proj):
    bf = lambda a: a.astype(BF16)
    weights = dict(
        w_in=bf(w_in), q_norm_g=q_norm_g, k_norm_g=k_norm_g, w_dw=w_dw, b_dw=b_dw,
        conv_ln_g=conv_ln_g, conv_ln_b=conv_ln_b, w_o_attn=bf(w_o_attn), w_o_conv=bf(w_o_conv),
        w_out=bf(w_out), g_mix=g_mix, g_ffn=g_ffn, ffn_w_gate=bf(ffn_w_gate),
        ffn_w_up=bf(ffn_w_up), ffn_w_down=bf(ffn_w_down), moe_router=bf(moe_router),
        moe_w_gate=bf(moe_w_gate), moe_w_up=bf(moe_w_up), moe_w_down=bf(moe_w_down), g_ple=g_ple,
        w_ple_gate=bf(w_ple_gate), w_ple_proj=bf(w_ple_proj))
    cache = dict(k=cache_k, v=cache_v, kidx=cache_kidx, conv=state_conv)
    past_len = cache_k.shape[2]
    y_p, k_p, v_p, ki_p, conv_p = _layer_stack(x_prompt, p_prompt, weights, None, past=0)
    y_s, k_s, v_s, ki_s, conv_s = _layer_stack(x_sample, p_sample, weights, cache, past=past_len)
    return (y_p, y_s, k_p, v_p, ki_p, conv_p, k_s, v_s, ki_s, conv_s)
```

```python
import functools

import jax
import jax.numpy as jnp
from jax import lax
from jax.experimental import pallas as pl
from jax.experimental.pallas import tpu as pltpu

F32 = jnp.float32
BF16 = jnp.bfloat16
I32 = jnp.int32

CHUNK = 64
CHUNK_SHIFT = CHUNK.bit_length() - 1
assert 1 << CHUNK_SHIFT == CHUNK
N_HEADS = 16
HEAD_DIM = 64
N_KV_HEADS = 4
Q_PER_KV = N_HEADS // N_KV_HEADS
IDX_HEADS = 8
IDX_DIM = 64
TOPK_MAX = 256
ROPE_THETA = 10000.0
CONV_WIDTH = 31
TOP_K_EXPERTS = 2
EPS = 1e-6

LANES = 128
CONV_HALO = 32
MASKED_BIAS = -1e30
LOG2_E = 1.4426950408889634
INT_MIN = -2 ** 31
KEY_NEG_INF = -2139095041
SEARCH_ROWS = 128
HALF_WORD_BITS = 16
TOP_BITS = HALF_WORD_BITS - 1
DMA_LOOP_UNROLL = 8
ATTN_HEADS_PER_DOT = Q_PER_KV
MOE_ROW_BLOCK = 512
VMEM_LIMIT = 52 * 1024 * 1024


def _params(semantics):
    return pltpu.CompilerParams(dimension_semantics=semantics, vmem_limit_bytes=VMEM_LIMIT)


def _pick_tile(n, pref):
    if n <= pref:
        return n
    best = None
    for t in range(LANES, pref + 1, LANES):
        if n % t == 0:
            best = t
    assert best is not None, (n, pref)
    return best


def _rms_norm_kernel(x_ref, g_ref, o_ref):
    x = x_ref[...]
    y = x * lax.rsqrt(jnp.mean(x * x, axis=-1, keepdims=True) + EPS) * g_ref[...]
    o_ref[...] = y.astype(o_ref.dtype)


def rms_norm_rows(x, gain, *, tm=512):
    m, d = x.shape
    tm = _pick_tile(m, tm) if m % LANES == 0 else m
    return pl.pallas_call(
        _rms_norm_kernel,
        grid=(m // tm,),
        in_specs=[pl.BlockSpec((tm, d), lambda i: (i, 0)), pl.BlockSpec((1, d), lambda i: (0, 0))],
        out_specs=pl.BlockSpec((tm, d), lambda i: (i, 0)),
        out_shape=jax.ShapeDtypeStruct((m, d), BF16),
        compiler_params=_params(("parallel",)),
        name="rms_norm_rows",
    )(x, gain.reshape(1, d).astype(F32))


def _fused_mm_kernel(*refs, n_lhs, has_gain, staged, pair_lhs, n_tiled, n_rows, n_consts, n_out,
                     epilogue):
    it = iter(refs)
    lhs_refs = [next(it) for _ in range(n_lhs)]
    gain_refs = [next(it) if has_gain[a] else None for a in range(n_lhs)]
    rhs_refs = [next(it) for _ in range(len(pair_lhs))]
    tiled_refs = [next(it) for _ in range(n_tiled)]
    row_refs = [next(it) for _ in range(n_rows)]
    const_refs = [next(it) for _ in range(n_consts)]
    out_refs = [next(it) for _ in range(n_out)]
    operand_refs = [next(it) if staged[a] else lhs_refs[a] for a in range(n_lhs)]

    if any(staged):
        @pl.when(pl.program_id(1) == 0)
        def _stage():
            for a in range(n_lhs):
                if not staged[a]:
                    continue
                x = lhs_refs[a][...].astype(F32)
                if has_gain[a]:
                    x = x * lax.rsqrt(jnp.mean(x * x, axis=-1, keepdims=True) + EPS)
                    x = x * gain_refs[a][...]
                operand_refs[a][...] = x.astype(BF16)

    accs = [jnp.dot(operand_refs[pair_lhs[r]][...], rhs_refs[r][...], preferred_element_type=F32)
            for r in range(len(pair_lhs))]
    outs = epilogue(accs, [t[...] for t in tiled_refs], [t[...] for t in row_refs],
                    [t[...] for t in const_refs])
    for o_ref, o in zip(out_refs, outs):
        o_ref[...] = o.astype(o_ref.dtype)


def fused_mm(name, lhs, rhs, pair_lhs, epilogue, out_dtypes, *, gains=None, tiled=(), rows=(),
             consts=(), tm=1024, tn=1024):
    m = lhs[0].shape[0]
    n = rhs[0].shape[1]
    tm = _pick_tile(m, tm) if m % LANES == 0 else m
    tn = _pick_tile(n, tn)
    n_lhs = len(lhs)
    gains = list(gains) if gains is not None else [None] * n_lhs
    has_gain = tuple(g is not None for g in gains)
    staged = tuple(has_gain[a] or lhs[a].dtype != BF16 for a in range(n_lhs))

    in_specs, args = [], []
    for x in lhs:
        in_specs.append(pl.BlockSpec((tm, x.shape[1]), lambda i, j: (i, 0)))
        args.append(x)
    for g in gains:
        if g is not None:
            g2 = g.reshape(1, -1).astype(F32)
            in_specs.append(pl.BlockSpec(g2.shape, lambda i, j: (0, 0)))
            args.append(g2)
    for w in rhs:
        in_specs.append(pl.BlockSpec((w.shape[0], tn), lambda i, j: (0, j)))
        args.append(w)
    for t in tiled:
        in_specs.append(pl.BlockSpec((tm, tn), lambda i, j: (i, j)))
        args.append(t)
    for t in rows:
        in_specs.append(pl.BlockSpec((tm, t.shape[1]), lambda i, j: (i, 0)))
        args.append(t)
    for c in consts:
        in_specs.append(pl.BlockSpec(c.shape, lambda i, j: (0, 0)))
        args.append(c)

    kernel = functools.partial(
        _fused_mm_kernel, n_lhs=n_lhs, has_gain=has_gain, staged=staged, pair_lhs=tuple(pair_lhs),
        n_tiled=len(tiled), n_rows=len(rows), n_consts=len(consts), n_out=len(out_dtypes),
        epilogue=epilogue)
    outs = pl.pallas_call(
        kernel,
        grid=(m // tm, n // tn),
        in_specs=in_specs,
        out_specs=[pl.BlockSpec((tm, tn), lambda i, j: (i, j)) for _ in out_dtypes],
        out_shape=[jax.ShapeDtypeStruct((m, n), dt) for dt in out_dtypes],
        scratch_shapes=[pltpu.VMEM((tm, x.shape[1]), BF16) for a, x in enumerate(lhs) if staged[a]],
        compiler_params=_params(("parallel", "arbitrary")),
        name=name,
    )(*args)
    return outs


def _head_rms_norm(x, block_ones, gain):
    x2 = x * x
    hi = x2.astype(BF16)
    lo = (x2 - hi.astype(F32)).astype(BF16)
    ss = (jnp.dot(hi, block_ones, preferred_element_type=F32)
          + jnp.dot(lo, block_ones, preferred_element_type=F32))
    return x * lax.rsqrt(ss * (1.0 / HEAD_DIM) + EPS) * gain


def _rope(x, cos, sin_signed):
    n = x.shape[1]
    lane = lax.broadcasted_iota(I32, x.shape, 1)
    first_half = (lane & (HEAD_DIM // 2)) == 0
    partner = jnp.where(first_half, pltpu.roll(x, n - HEAD_DIM // 2, 1),
                        pltpu.roll(x, HEAD_DIM // 2, 1))
    return x * cos + partner * sin_signed


def _sigmoid(x):
    return 1.0 / (1.0 + jnp.exp(-x))


def _silu(x):
    return x * _sigmoid(x)


def _rope_tables(pos, width, rotary_width=None):
    half = HEAD_DIM // 2
    inv = ROPE_THETA ** (-jnp.arange(half, dtype=F32) / half)
    ang = pos.astype(F32)[:, None] * inv[None, :]
    cos = jnp.cos(ang)
    sin = jnp.sin(ang)
    cos64 = jnp.concatenate([cos, cos], axis=-1)
    sin64 = jnp.concatenate([-sin, sin], axis=-1)
    rotary_width = width if rotary_width is None else rotary_width
    reps = rotary_width // HEAD_DIM
    cos_t = jnp.tile(cos64, (1, reps))
    sin_t = jnp.tile(sin64, (1, reps))
    if rotary_width < width:
        pad = width - rotary_width
        cos_t = jnp.concatenate([cos_t, jnp.ones((pos.shape[0], pad), F32)], axis=-1)
        sin_t = jnp.concatenate([sin_t, jnp.zeros((pos.shape[0], pad), F32)], axis=-1)
    return cos_t, sin_t


def _block_ones(width):
    head = jnp.arange(width, dtype=I32) // HEAD_DIM
    return (head[:, None] == head[None, :]).astype(BF16)


def _float_key(x):
    b = lax.bitcast_convert_type(x, I32)
    return jnp.where(b >= 0, b, b ^ jnp.int32(0x7FFFFFFF))


def _select_kernel(qi_ref, ki_ref, wi_ref, tri_ref, bias_ref, key_ref, top_ref, *, tq, tk, nkb,
                   causal, q_pos0, n_valid, n_sel, rows, n_groups):
    qb = pl.program_id(1)
    n_chunks = tk // LANES
    n_sub = tq // rows
    half = rows // 2
    gq = tq // n_groups
    if causal:
        n_act = jnp.minimum(nkb, ((qb + 1) * tq + tk - 1) // tk)
    else:
        n_act = nkb
    row = lax.broadcasted_iota(I32, (tq, tk), 0)
    q_pos = q_pos0 + qb * tq + (row if n_groups == 1 else row % gq)
    q_chunk = lax.shift_right_logical(q_pos, CHUNK_SHIFT)
    w = (wi_ref[0] * (IDX_HEADS ** -0.5)) * (IDX_DIM ** -0.5)
    qi = qi_ref[0]

    def score_block(kb, carry):
        start = pl.multiple_of(kb * tk, tk)
        parts = []
        for g in range(n_groups):
            gs = slice(g * gq, (g + 1) * gq)
            ki = ki_ref[g, pl.ds(start, tk), :]
            sg = jnp.zeros((gq, tk), F32)
            for h in range(IDX_HEADS):
                r = lax.dot_general(qi[gs, h * IDX_DIM:(h + 1) * IDX_DIM], ki,
                                    (((1,), (1,)), ((), ())), preferred_element_type=F32)
                sg = sg + jnp.maximum(r, 0.0) * w[gs, h:h + 1]
            parts.append(sg)
        s = parts[0] if n_groups == 1 else jnp.concatenate(parts, axis=0)
        k_pos = start + lax.broadcasted_iota(I32, (tq, tk), 1)
        admissible = (lax.shift_right_logical(k_pos, CHUNK_SHIFT) <= q_chunk) & (k_pos < n_valid)
        key = _float_key(jnp.where(admissible, s, -jnp.inf))
        top = lax.shift_right_logical(key ^ jnp.int32(INT_MIN), 32 - TOP_BITS)
        for sub in range(n_sub):
            key_ref[sub, kb] = key[sub * rows:(sub + 1) * rows]
            lo_rows = top[sub * rows:sub * rows + half]
            hi_rows = top[sub * rows + half:(sub + 1) * rows]
            top_ref[sub, kb] = lo_rows | jnp.left_shift(hi_rows, HALF_WORD_BITS)
        return carry

    lax.fori_loop(0, n_act, score_block, 0)
    for sub in range(n_sub):
        key_ref[sub, n_act] = jnp.full((rows, tk), INT_MIN, I32)
        top_ref[sub, n_act] = jnp.zeros((half, tk), I32)
    packed = dict(n_sub=n_sub, n_act=n_act, rows=rows, tk=tk)
    zero = jnp.zeros((rows, LANES), F32)
    top_digit = _search_packed_digits(top_ref, [zero] * n_sub, n_sel=n_sel, **packed)
    at_most = _count_packed_le(top_ref, top_digit, **packed)
    above = [_keys_counted(n_act, tk) - c for c in at_most]

    def bucket_block(kb, carry):
        for sub in range(n_sub):
            u = key_ref[sub, kb] ^ jnp.int32(INT_MIN)
            top = jnp.concatenate([top_digit[sub]] * n_chunks, axis=1)
            mid = lax.shift_right_logical(u, 32 - 2 * TOP_BITS) & ((1 << TOP_BITS) - 1)
            mid = jnp.where(lax.shift_right_logical(u, 32 - TOP_BITS) == top, mid, 0)
            top_ref[sub, kb] = mid[:half] | jnp.left_shift(mid[half:], HALF_WORD_BITS)
        return carry

    lax.fori_loop(0, n_act, bucket_block, 0)
    mid_digit = _search_packed_digits(top_ref, above, n_sel=n_sel, **packed)
    for sub in range(n_sub):
        prefix = (jnp.left_shift(top_digit[sub], 32 - TOP_BITS)
                  | jnp.left_shift(mid_digit[sub], 32 - 2 * TOP_BITS))
        _select_rows(key_ref.at[sub], prefix, tri_ref,
                     bias_ref.at[0, :, sub * rows:(sub + 1) * rows, :],
                     n_act=n_act, rows=rows, tk=tk, n_sel=n_sel)

    def fill_block(kb, carry):
        bias_ref[0, kb] = jnp.full((tq, tk), MASKED_BIAS, bias_ref.dtype)
        return carry

    lax.fori_loop(n_act, nkb, fill_block, 0)


def _keys_counted(n_act, tk):
    return (tk * 2) * ((n_act + 1) // 2)


def _count_packed_le(dig_ref, limits, *, n_sub, n_act, rows, tk):
    half = rows // 2
    n_chunks = tk // LANES
    ones_mat = jnp.ones((LANES, LANES), BF16)
    pair_ones = 1 | (1 << HALF_WORD_BITS)
    half_mask = (1 << HALF_WORD_BITS) - 1
    cws = []
    for limit in limits:
        guarded = limit | (1 << TOP_BITS)
        cws.append(guarded[:half] | jnp.left_shift(guarded[half:], HALF_WORD_BITS))

    def pair(p, accs):
        accs = list(accs)
        for kb in (2 * p, 2 * p + 1):
            for c in range(n_chunks):
                for sub in range(n_sub):
                    d = cws[sub] - dig_ref[sub, kb, :, c * LANES:(c + 1) * LANES]
                    accs[sub] = accs[sub] + (lax.shift_right_logical(d, TOP_BITS) & pair_ones)
        return tuple(accs)

    accs = lax.fori_loop(0, (n_act + 1) // 2, pair,
                         tuple(jnp.zeros((half, LANES), I32) for _ in range(n_sub)))
    counts = []
    for acc in accs:
        le = jnp.concatenate([acc & half_mask, lax.shift_right_logical(acc, HALF_WORD_BITS)], axis=0)
        counts.append(jnp.dot(le.astype(F32).astype(BF16), ones_mat, preferred_element_type=F32))
    return counts


def _search_packed_digits(dig_ref, bases, *, n_sel, n_sub, n_act, rows, tk):
    n_counted = _keys_counted(n_act, tk)

    def step(i, prefixes):
        cands = [p | jnp.left_shift(jnp.int32(1), TOP_BITS - 1 - i) for p in prefixes]
        n_le = _count_packed_le(dig_ref, [c - 1 for c in cands], n_sub=n_sub, n_act=n_act,
                                rows=rows, tk=tk)
        return tuple(jnp.where(bases[s] + (n_counted - n_le[s]) >= n_sel, cands[s], prefixes[s])
                     for s in range(n_sub))

    return lax.fori_loop(0, TOP_BITS, step,
                         tuple(jnp.zeros((rows, LANES), I32) for _ in range(n_sub)))


def _select_rows(key_ref, packed_prefix, tri_ref, out_ref, *, n_act, rows, tk, n_sel):
    tq = rows
    n_chunks = tk // LANES

    ones_mat = jnp.ones((LANES, LANES), BF16)

    def count(hit_fn):
        def pair(p, acc):
            for kb in (2 * p, 2 * p + 1):
                for c in range(n_chunks):
                    acc = acc + jnp.where(hit_fn(key_ref[kb, :, c * LANES:(c + 1) * LANES]), 1, 0)
            return acc
        acc = lax.fori_loop(0, (n_act + 1) // 2, pair, jnp.zeros((tq, LANES), I32))
        return jnp.dot(acc.astype(F32).astype(BF16), ones_mat, preferred_element_type=F32)

    def bit_step(i, prefix):
        cand = prefix | jnp.left_shift(jnp.int32(1), 31 - i)
        cand_signed = cand ^ jnp.int32(INT_MIN)
        cnt = count(lambda k: k >= cand_signed)
        return jnp.where(cnt >= n_sel, cand, prefix)

    prefix = lax.fori_loop(2 * TOP_BITS, 32, bit_step, packed_prefix)
    tau_rep = prefix ^ jnp.int32(INT_MIN)
    tau = tau_rep[:, :1]
    n_ge = count(lambda k: k >= tau_rep)
    all_ties_taken = jnp.max(n_ge) <= n_sel

    def emit_all_ties(kb, carry):
        key = key_ref[kb]
        sel = (key >= tau) & (key > KEY_NEG_INF)
        out_ref[kb] = jnp.where(sel, 0.0, MASKED_BIAS).astype(out_ref.dtype)
        return carry

    @pl.when(all_ties_taken)
    def _fast():
        lax.fori_loop(0, n_act, emit_all_ties, 0)

    @pl.when(jnp.logical_not(all_ties_taken))
    def _ranked():
        need = (n_sel - count(lambda k: k > tau_rep))[:, :1]

        def emit_ranked(kb, seen):
            for c in range(n_chunks):
                cs = slice(c * LANES, (c + 1) * LANES)
                key = key_ref[kb, :, cs]
                eq = key == tau
                rank = seen + jnp.dot(jnp.where(eq, 1.0, 0.0).astype(BF16), tri_ref[...],
                                      preferred_element_type=F32)
                sel = ((key > tau) | (eq & (rank <= need))) & (key > KEY_NEG_INF)
                out_ref[kb, :, cs] = jnp.where(sel, 0.0, MASKED_BIAS).astype(out_ref.dtype)
                seen = rank[:, LANES - 1:LANES]
            return seen

        lax.fori_loop(0, n_act, emit_ranked, jnp.zeros((tq, 1), F32))


def select_bias(qi, ki, wi, *, tq, tk, causal, q_pos0, n_valid, n_sel):
    b, t_q, _ = qi.shape
    n_k = ki.shape[1]
    nkb = n_k // tk
    n_groups = max(1, tq // t_q)
    assert b % n_groups == 0 and (n_groups == 1 or not causal)
    tiles = b // n_groups
    t_rows = n_groups * t_q
    qi = qi.reshape(tiles, t_rows, qi.shape[2])
    wi = wi.reshape(tiles, t_rows, wi.shape[2])
    idx = jnp.arange(LANES, dtype=I32)
    tri = (idx[:, None] <= idx[None, :]).astype(BF16)
    rows = min(tq, SEARCH_ROWS)
    kernel = functools.partial(_select_kernel, tq=tq, tk=tk, nkb=nkb, causal=causal,
                               q_pos0=q_pos0, n_valid=n_valid, n_sel=n_sel, rows=rows,
                               n_groups=n_groups)
    bias = pl.pallas_call(
        kernel,
        grid=(tiles, t_rows // tq),
        in_specs=[
            pl.BlockSpec((1, tq, qi.shape[2]), lambda bi, qb: (bi, qb, 0)),
            pl.BlockSpec((n_groups, n_k, ki.shape[2]), lambda bi, qb: (bi, 0, 0)),
            pl.BlockSpec((1, tq, wi.shape[2]), lambda bi, qb: (bi, qb, 0)),
            pl.BlockSpec((LANES, LANES), lambda bi, qb: (0, 0)),
        ],
        out_specs=pl.BlockSpec((1, nkb, tq, tk), lambda bi, qb: (bi, 0, qb, 0)),
        out_shape=jax.ShapeDtypeStruct((tiles, nkb, t_rows, tk), BF16),
        scratch_shapes=[pltpu.VMEM((tq // rows, nkb + 1, rows, tk), I32),
                        pltpu.VMEM((tq // rows, nkb + 1, rows // 2, tk), I32)],
        compiler_params=_params(("parallel", "parallel")),
        name="select_bias",
    )(qi, ki, wi, tri)
    if n_groups == 1:
        return bias
    bias = bias.reshape(tiles, nkb, n_groups, t_q, tk)
    return jnp.transpose(bias, (0, 2, 1, 3, 4)).reshape(b, nkb, t_q, tk)


def _attn_kernel(qb_ref, kb_ref, last_ref, q_ref, kt_ref, v_ref, bias_ref, o_ref, qs_ref, m_ref,
                 acc_ref, *, tq, tk):
    del qb_ref
    step = pl.program_id(1)
    rows = Q_PER_KV * tq

    @pl.when(kb_ref[step] == 0)
    def _init():
        m_ref[...] = jnp.full(m_ref.shape, MASKED_BIAS, F32)
        acc_ref[...] = jnp.zeros(acc_ref.shape, F32)
        for h in range(N_HEADS):
            g, r = divmod(h, Q_PER_KV)
            qh = q_ref[0, :, h * HEAD_DIM:(h + 1) * HEAD_DIM].astype(F32) * (HEAD_DIM ** -0.5)
            qs_ref[g, r * tq:(r + 1) * tq, :] = qh.astype(qs_ref.dtype)

    bias = jnp.concatenate([bias_ref[0, j] for j in range(bias_ref.shape[1])],
                           axis=1).astype(F32)
    sub_rows = ATTN_HEADS_PER_DOT * tq
    for g in range(N_KV_HEADS):
        for sub in range(Q_PER_KV // ATTN_HEADS_PER_DOT):
            rs = slice(sub * sub_rows, (sub + 1) * sub_rows)
            s = jnp.dot(qs_ref[g, rs, :], kt_ref[0, g], preferred_element_type=F32)
            s = (s * LOG2_E).reshape(ATTN_HEADS_PER_DOT, tq, tk) + bias[None]
            s = s.reshape(sub_rows, tk)
            m_prev = m_ref[g, rs, :]
            m_blk = jnp.max(s, axis=1, keepdims=True)
            m_new = jnp.maximum(m_prev, jnp.broadcast_to(m_blk, m_prev.shape))
            alpha = jnp.exp2(m_prev - m_new)
            p = jnp.concatenate(
                [jnp.exp2(s[:, c * LANES:(c + 1) * LANES] - m_new) for c in range(tk // LANES)],
                axis=1).astype(BF16)
            acc_ref[g, rs, :] = alpha * acc_ref[g, rs, :] + jnp.dot(
                p, v_ref[0, g], preferred_element_type=F32)
            m_ref[g, rs, :] = m_new

    @pl.when(last_ref[step] == 1)
    def _finish():
        for h in range(N_HEADS):
            g, r = divmod(h, Q_PER_KV)
            a = acc_ref[g, r * tq:(r + 1) * tq, :]
            out = a[:, :HEAD_DIM] / a[:, HEAD_DIM:HEAD_DIM + 1]
            o_ref[0, :, h * HEAD_DIM:(h + 1) * HEAD_DIM] = out.astype(o_ref.dtype)


def masked_attention(q, k, v, bias, *, tq, kspan, causal):
    b, t_q, d_q = q.shape
    n_k = k.shape[1]
    tk = kspan * bias.shape[3]
    nkb = n_k // tk
    kt = jnp.transpose(k.reshape(b, n_k, N_KV_HEADS, HEAD_DIM), (0, 2, 3, 1))
    v4 = jnp.transpose(v.reshape(b, n_k, N_KV_HEADS, HEAD_DIM), (0, 2, 1, 3))
    v_ext = jnp.concatenate(
        [v4, jnp.ones((b, N_KV_HEADS, n_k, 1), BF16),
         jnp.zeros((b, N_KV_HEADS, n_k, LANES - HEAD_DIM - 1), BF16)], axis=-1)

    pairs = []
    for qb in range(t_q // tq):
        last = min(nkb - 1, ((qb + 1) * tq - 1) // tk) if causal else nkb - 1
        pairs += [(qb, kb, int(kb == last)) for kb in range(last + 1)]
    qb_ids, kb_ids, last_flags = (jnp.asarray(col, I32) for col in zip(*pairs))

    kernel = functools.partial(_attn_kernel, tq=tq, tk=tk)
    rows = Q_PER_KV * tq
    return pl.pallas_call(
        kernel,
        grid_spec=pltpu.PrefetchScalarGridSpec(
            num_scalar_prefetch=3,
            grid=(b, len(pairs)),
            in_specs=[
                pl.BlockSpec((1, tq, d_q), lambda bi, s, qbs, kbs, lf: (bi, qbs[s], 0)),
                pl.BlockSpec((1, N_KV_HEADS, HEAD_DIM, tk),
                             lambda bi, s, qbs, kbs, lf: (bi, 0, 0, kbs[s])),
                pl.BlockSpec((1, N_KV_HEADS, tk, LANES),
                             lambda bi, s, qbs, kbs, lf: (bi, 0, kbs[s], 0)),
                pl.BlockSpec((1, kspan, tq, bias.shape[3]),
                             lambda bi, s, qbs, kbs, lf: (bi, kbs[s], qbs[s], 0)),
            ],
            out_specs=pl.BlockSpec((1, tq, d_q), lambda bi, s, qbs, kbs, lf: (bi, qbs[s], 0)),
            scratch_shapes=[pltpu.VMEM((N_KV_HEADS, rows, HEAD_DIM), BF16),
                            pltpu.VMEM((N_KV_HEADS, rows, LANES), F32),
                            pltpu.VMEM((N_KV_HEADS, rows, LANES), F32)]),
        out_shape=jax.ShapeDtypeStruct((b, t_q, d_q), BF16),
        compiler_params=_params(("parallel", "arbitrary")),
        name="masked_attention",
    )(qb_ids, kb_ids, last_flags, q, kt, v_ext, bias)


def _conv_kernel(cur_ref, prev_ref, ctx_ref, w_ref, b_ref, g_ref, beta_ref, o_ref, buf_ref,
                 conv_ref, *, tm, cw):
    @pl.when(pl.program_id(1) == 0)
    def _first():
        buf_ref[0:CONV_HALO, :] = ctx_ref[0]

    @pl.when(pl.program_id(1) > 0)
    def _later():
        buf_ref[0:CONV_HALO, :] = prev_ref[0]

    buf_ref[CONV_HALO:CONV_HALO + tm, :] = cur_ref[0]
    lead = CONV_HALO - (CONV_WIDTH - 1)
    c_total = buf_ref.shape[1]
    rw = min(tm, 128)
    for c in range(c_total // cw):
        cs = slice(c * cw, (c + 1) * cw)
        for r in range(tm // rw):
            acc = jnp.zeros((rw, cw), F32) + b_ref[:, cs]
            for j in range(CONV_WIDTH):
                r0 = lead + j + r * rw
                acc = acc + w_ref[j:j + 1, cs] * buf_ref[r0:r0 + rw, cs]
            conv_ref[r * rw:(r + 1) * rw, cs] = acc
    x = conv_ref[...]
    mu = jnp.mean(x, axis=-1, keepdims=True)
    xc = x - mu
    y = xc * lax.rsqrt(jnp.mean(xc * xc, axis=-1, keepdims=True) + EPS)
    y = y * g_ref[...] + beta_ref[...]
    o_ref[0] = _silu(y).astype(o_ref.dtype)


def conv_ln_silu(y, ctx, w_dw, b_dw, ln_g, ln_b, *, tm):
    b, l, c = y.shape
    kernel = functools.partial(_conv_kernel, tm=tm, cw=LANES)
    vec = lambda a: a.reshape(1, c).astype(F32)
    halo_blocks = tm // CONV_HALO
    return pl.pallas_call(
        kernel,
        grid=(b, l // tm),
        in_specs=[
            pl.BlockSpec((1, tm, c), lambda bi, i: (bi, i, 0)),
            pl.BlockSpec((1, CONV_HALO, c),
                         lambda bi, i: (bi, jnp.maximum(i * halo_blocks - 1, 0), 0)),
            pl.BlockSpec((1, CONV_HALO, c), lambda bi, i: (bi, 0, 0)),
            pl.BlockSpec((CONV_WIDTH, c), lambda bi, i: (0, 0)),
            pl.BlockSpec((1, c), lambda bi, i: (0, 0)),
            pl.BlockSpec((1, c), lambda bi, i: (0, 0)),
            pl.BlockSpec((1, c), lambda bi, i: (0, 0)),
        ],
        out_specs=pl.BlockSpec((1, tm, c), lambda bi, i: (bi, i, 0)),
        out_shape=jax.ShapeDtypeStruct((b, l, c), BF16),
        scratch_shapes=[pltpu.VMEM((tm + CONV_HALO, c), F32), pltpu.VMEM((tm, c), F32)],
        compiler_params=_params(("parallel", "parallel")),
        name="conv_ln_silu",
    )(y, y, ctx, w_dw.astype(F32), vec(b_dw), vec(ln_g), vec(ln_b))


ROUTE_IDX1, ROUTE_IDX2, ROUTE_GATE1, ROUTE_GATE2, ROUTE_RANK1, ROUTE_RANK2 = range(6)


def _router_kernel(h_ref, g_ref, wr_ref, lower_ref, u_ref, route_ref, counts_ref, seen_ref, *,
                   n_experts):
    @pl.when(pl.program_id(0) == 0)
    def _init():
        seen_ref[...] = jnp.zeros(seen_ref.shape, F32)

    x = h_ref[...]
    u = x * lax.rsqrt(jnp.mean(x * x, axis=-1, keepdims=True) + EPS) * g_ref[...]
    u_ref[...] = u
    logits = jnp.dot(u.astype(BF16), wr_ref[...], preferred_element_type=F32)
    lane = lax.broadcasted_iota(I32, logits.shape, 1).astype(F32)
    logits = jnp.where(lane < n_experts, logits, -jnp.inf)
    top1 = jnp.max(logits, axis=1, keepdims=True)
    idx1 = jnp.min(jnp.where(logits == top1, lane, float(LANES)), axis=1, keepdims=True)
    rest = jnp.where(lane == idx1, -jnp.inf, logits)
    top2 = jnp.max(rest, axis=1, keepdims=True)
    idx2 = jnp.min(jnp.where(rest == top2, lane, float(LANES)), axis=1, keepdims=True)
    e2 = jnp.exp(top2 - top1)
    denom = 1.0 + e2

    hot1 = jnp.where(lane == idx1, 1.0, 0.0)
    hot2 = jnp.where(lane == idx2, 1.0, 0.0)
    hot = hot1 + hot2
    before = seen_ref[...] + jnp.dot(lower_ref[...], hot.astype(BF16), preferred_element_type=F32)
    rank1 = jnp.sum(hot1 * before, axis=1, keepdims=True)
    rank2 = jnp.sum(hot2 * before, axis=1, keepdims=True)
    seen_ref[...] += jnp.sum(hot, axis=0, keepdims=True)
    counts_ref[...] = seen_ref[...]

    route = jnp.zeros(logits.shape, F32)
    for slot, val in ((ROUTE_IDX1, idx1), (ROUTE_IDX2, idx2), (ROUTE_GATE1, 1.0 / denom),
                      (ROUTE_GATE2, e2 / denom), (ROUTE_RANK1, rank1), (ROUTE_RANK2, rank2)):
        route = jnp.where(lane == slot, val, route)
    route_ref[...] = route


def moe_router(h, gain, w_router_padded, n_experts, *, tm):
    m, d = h.shape
    row = jnp.arange(tm, dtype=I32)
    lower = (row[None, :] < row[:, None]).astype(BF16)
    kernel = functools.partial(_router_kernel, n_experts=n_experts)
    return pl.pallas_call(
        kernel,
        grid=(m // tm,),
        in_specs=[pl.BlockSpec((tm, d), lambda i: (i, 0)),
                  pl.BlockSpec((1, d), lambda i: (0, 0)),
                  pl.BlockSpec((d, LANES), lambda i: (0, 0)),
                  pl.BlockSpec((tm, tm), lambda i: (0, 0))],
        out_specs=[pl.BlockSpec((tm, d), lambda i: (i, 0)),
                   pl.BlockSpec((tm, LANES), lambda i: (i, 0)),
                   pl.BlockSpec((1, LANES), lambda i: (0, 0))],
        out_shape=[jax.ShapeDtypeStruct((m, d), F32), jax.ShapeDtypeStruct((m, LANES), F32),
                   jax.ShapeDtypeStruct((1, LANES), F32)],
        scratch_shapes=[pltpu.VMEM((1, LANES), F32)],
        compiler_params=_params(("arbitrary",)),
        name="moe_router",
    )(h, gain.reshape(1, d).astype(F32), w_router_padded, lower)


def _scatter_rows_kernel(d1_ref, d2_ref, u_ref, init_ref, xs_ref, sem):
    del init_ref
    tm = u_ref.shape[0]

    def row_copy(j, dest):
        return pltpu.make_async_copy(u_ref.at[pl.ds(j, 1)], xs_ref.at[pl.ds(dest, 1)], sem)

    def issue(j, carry):
        row_copy(j, d1_ref[0, 0, j]).start(priority=0)
        row_copy(j, d2_ref[0, 0, j]).start(priority=1)
        return carry

    lax.fori_loop(0, tm, issue, 0, unroll=DMA_LOOP_UNROLL)

    def drain(j, carry):
        row_copy(j, 0).wait()
        row_copy(j, 0).wait()
        return carry

    lax.fori_loop(0, tm, drain, 0, unroll=DMA_LOOP_UNROLL)


def scatter_rows(u, dest1, dest2, n_rows, *, tm):
    m, d = u.shape
    smem_idx = lambda a: a.reshape(m // tm, 1, tm)
    idx_spec = pl.BlockSpec((1, 1, tm), lambda i: (i, 0, 0), memory_space=pltpu.SMEM)
    return pl.pallas_call(
        _scatter_rows_kernel,
        grid=(m // tm,),
        in_specs=[idx_spec, idx_spec,
                  pl.BlockSpec((tm, d), lambda i: (i, 0)),
                  pl.BlockSpec(memory_space=pl.ANY)],
        out_specs=pl.BlockSpec(memory_space=pl.ANY),
        out_shape=jax.ShapeDtypeStruct((n_rows, d), u.dtype),
        scratch_shapes=[pltpu.SemaphoreType.DMA],
        input_output_aliases={3: 0},
        compiler_params=_params(("arbitrary",)),
        name="moe_scatter_rows",
    )(smem_idx(dest1), smem_idx(dest2), u, jnp.zeros((n_rows, d), u.dtype))


def _combine_kernel(d1_ref, d2_ref, rows_ref, route_ref, h_ref, o_ref, buf_ref, sem):
    tm = h_ref.shape[0]

    def row_copy(slot, j, src):
        return pltpu.make_async_copy(rows_ref.at[pl.ds(src, 1)], buf_ref.at[slot, pl.ds(j, 1)], sem)

    def issue(j, carry):
        row_copy(0, j, d1_ref[0, 0, j]).start(priority=0)
        row_copy(1, j, d2_ref[0, 0, j]).start(priority=1)
        return carry

    lax.fori_loop(0, tm, issue, 0, unroll=DMA_LOOP_UNROLL)

    def drain(j, carry):
        row_copy(0, j, 0).wait()
        row_copy(1, j, 0).wait()
        return carry

    lax.fori_loop(0, tm, drain, 0, unroll=DMA_LOOP_UNROLL)
    route = route_ref[...]
    g1 = route[:, ROUTE_GATE1:ROUTE_GATE1 + 1]
    g2 = route[:, ROUTE_GATE2:ROUTE_GATE2 + 1]
    o_ref[...] = h_ref[...] + (g1 * buf_ref[0] + g2 * buf_ref[1])


def combine_rows(rows, dest1, dest2, route, h, *, tm):
    m, d = h.shape
    smem_idx = lambda a: a.reshape(m // tm, 1, tm)
    idx_spec = pl.BlockSpec((1, 1, tm), lambda i: (i, 0, 0), memory_space=pltpu.SMEM)
    return pl.pallas_call(
        _combine_kernel,
        grid=(m // tm,),
        in_specs=[idx_spec, idx_spec,
                  pl.BlockSpec(memory_space=pl.ANY),
                  pl.BlockSpec((tm, LANES), lambda i: (i, 0)),
                  pl.BlockSpec((tm, d), lambda i: (i, 0))],
        out_specs=pl.BlockSpec((tm, d), lambda i: (i, 0)),
        out_shape=jax.ShapeDtypeStruct((m, d), F32),
        scratch_shapes=[pltpu.VMEM((2, tm, d), F32), pltpu.SemaphoreType.DMA],
        compiler_params=_params(("arbitrary",)),
        name="moe_combine_rows",
    )(smem_idx(dest1), smem_idx(dest2), rows, route, h)


def _expert_up_kernel(blk_expert_ref, n_used_ref, xs_ref, wg_ref, wu_ref, o_ref, stage_ref):
    del blk_expert_ref
    i = pl.program_id(0)

    @pl.when(pl.program_id(1) == 0)
    def _stage():
        stage_ref[...] = xs_ref[...].astype(BF16)

    @pl.when(i < n_used_ref[0])
    def _compute():
        u = stage_ref[...]
        a = jnp.dot(u, wg_ref[0], preferred_element_type=F32)
        b = jnp.dot(u, wu_ref[0], preferred_element_type=F32)
        o_ref[...] = (_silu(a) * b).astype(o_ref.dtype)

    @pl.when(i >= n_used_ref[0])
    def _skip():
        o_ref[...] = jnp.zeros(o_ref.shape, o_ref.dtype)


def expert_up(xs, w_gate, w_up, blk_expert, n_used, *, blk, tn):
    n_rows, d = xs.shape
    f = w_gate.shape[2]
    return pl.pallas_call(
        _expert_up_kernel,
        grid_spec=pltpu.PrefetchScalarGridSpec(
            num_scalar_prefetch=2,
            grid=(n_rows // blk, f // tn),
            in_specs=[pl.BlockSpec((blk, d), lambda i, j, be, nu: (i, 0)),
                      pl.BlockSpec((1, d, tn), lambda i, j, be, nu: (be[i], 0, j)),
                      pl.BlockSpec((1, d, tn), lambda i, j, be, nu: (be[i], 0, j))],
            out_specs=pl.BlockSpec((blk, tn), lambda i, j, be, nu: (i, j)),
            scratch_shapes=[pltpu.VMEM((blk, d), BF16)]),
        out_shape=jax.ShapeDtypeStruct((n_rows, f), BF16),
        compiler_params=_params(("parallel", "arbitrary")),
        name="expert_up",
    )(blk_expert, n_used, xs, w_gate, w_up)


def _expert_down_kernel(blk_expert_ref, n_used_ref, hb_ref, wd_ref, o_ref):
    del blk_expert_ref
    i = pl.program_id(0)

    @pl.when(i < n_used_ref[0])
    def _compute():
        o_ref[...] = jnp.dot(hb_ref[...], wd_ref[0], preferred_element_type=F32)

    @pl.when(i >= n_used_ref[0])
    def _skip():
        o_ref[...] = jnp.zeros(o_ref.shape, o_ref.dtype)


def expert_down(hb, w_down, blk_expert, n_used, *, blk, tn):
    n_rows, f = hb.shape
    d = w_down.shape[2]
    return pl.pallas_call(
        _expert_down_kernel,
        grid_spec=pltpu.PrefetchScalarGridSpec(
            num_scalar_prefetch=2,
            grid=(n_rows // blk, d // tn),
            in_specs=[pl.BlockSpec((blk, f), lambda i, j, be, nu: (i, 0)),
                      pl.BlockSpec((1, f, tn), lambda i, j, be, nu: (be[i], 0, j))],
            out_specs=pl.BlockSpec((blk, tn), lambda i, j, be, nu: (i, j))),
        out_shape=jax.ShapeDtypeStruct((n_rows, d), F32),
        compiler_params=_params(("parallel", "parallel")),
        name="expert_down",
    )(blk_expert, n_used, hb, w_down)


def moe_swiglu(h, gain, w_router, w_gate, w_up, w_down):
    m, d = h.shape
    n_experts = w_router.shape[1]
    blk = MOE_ROW_BLOCK if m >= 8 * MOE_ROW_BLOCK else LANES
    tm = _pick_tile(m, 256)
    w_router_padded = jnp.concatenate([w_router, jnp.zeros((d, LANES - n_experts), BF16)], axis=1)
    u, route, counts = moe_router(h, gain, w_router_padded, n_experts, tm=_pick_tile(m, 512))

    counts = counts[0, :n_experts].astype(I32)
    padded = (counts + blk - 1) // blk * blk
    pad_end = jnp.cumsum(padded)
    pad_start = pad_end - padded
    idx1 = route[:, ROUTE_IDX1].astype(I32)
    idx2 = route[:, ROUTE_IDX2].astype(I32)
    dest1 = pad_start[idx1] + route[:, ROUTE_RANK1].astype(I32)
    dest2 = pad_start[idx2] + route[:, ROUTE_RANK2].astype(I32)
    n_blocks = -(-(TOP_K_EXPERTS * m + n_experts * (blk - 1)) // blk)
    blk_start = jnp.arange(n_blocks, dtype=I32) * blk
    blk_expert = jnp.minimum(jnp.searchsorted(pad_end, blk_start, side="right"),
                             n_experts - 1).astype(I32)
    n_used = (pad_end[-1:] // blk).astype(I32)

    xs = scatter_rows(u, dest1, dest2, n_blocks * blk, tm=tm)
    hb = expert_up(xs, w_gate, w_up, blk_expert, n_used, blk=blk, tn=_pick_tile(w_gate.shape[2], 1792))
    rows = expert_down(hb, w_down, blk_expert, n_used, blk=blk, tn=_pick_tile(d, 1024))
    return combine_rows(rows, dest1, dest2, route, h, tm=tm)


def _split_w_in(w_in_layer, d_model):
    attn_dim = N_HEADS * HEAD_DIM
    kv_dim = N_KV_HEADS * HEAD_DIM
    sizes = (attn_dim, kv_dim, kv_dim, IDX_HEADS * IDX_DIM, IDX_DIM, IDX_HEADS, d_model, d_model,
             d_model, d_model)
    parts, off = [], 0
    for s in sizes:
        parts.append(w_in_layer[:, off:off + s])
        off += s
    assert off == w_in_layer.shape[1]
    return parts


def _layer_stack(x, p, weights, cache, *, past):
    bsz, length, d_model = x.shape
    m = bsz * length
    kv_dim = N_KV_HEADS * HEAD_DIM
    pos = past + jnp.arange(length, dtype=I32)
    row_pos = jnp.tile(pos, bsz)
    depth = weights["w_in"].shape[0]

    cos_q, sin_q = _rope_tables(row_pos, 512)
    cos_k, sin_k = _rope_tables(row_pos, kv_dim)
    cos_i, sin_i = _rope_tables(row_pos, LANES, rotary_width=IDX_DIM)
    ones_q = _block_ones(512)
    ones_k = _block_ones(kv_dim)

    n_keys = past + length
    n_sel = min(TOPK_MAX, n_keys // 4)
    if past == 0:
        tq_sel, tk, tq_attn = min(256, length), min(512, length), min(512, length)
        n_keys_pad = n_keys
        kspan = 2 if n_keys_pad % (2 * tk) == 0 else 1
    else:
        tq_attn = length
        batch_per_tile = min(bsz, max(1, 256 // length))
        while bsz % batch_per_tile:
            batch_per_tile -= 1
        tq_sel = batch_per_tile * length
        n_keys_pad = -(-n_keys // LANES) * LANES
        tk = _pick_tile(n_keys_pad, 512)
        kspan = n_keys_pad // tk

    h = x.reshape(m, d_model)
    ks, vs, kis, convs = [], [], [], []
    for i in range(depth):
        wq, wk, wv, wqi, wki, wwi, wglu_a, wglu_g, wga, wgb = _split_w_in(weights["w_in"][i], d_model)
        w_kiwi = jnp.concatenate(
            [wki, wwi, jnp.zeros((d_model, LANES - IDX_DIM - IDX_HEADS), BF16)], axis=1)
        g_mix = weights["g_mix"][i]

        qn_gain = jnp.tile(weights["q_norm_g"][i].astype(F32), 512 // HEAD_DIM).reshape(1, 512)
        kn_gain = jnp.tile(weights["k_norm_g"][i].astype(F32), kv_dim // HEAD_DIM).reshape(1, kv_dim)

        def norm_rope(accs, tiled, rows, consts):
            y = _head_rms_norm(accs[0], consts[0], consts[1])
            y = _rope(y, rows[0], rows[1])
            return (y, y)

        u = rms_norm_rows(h, g_mix)
        (q_bf,) = fused_mm("proj_q", [u], [wq], [0],
                           lambda a, t, r, c: (_rope(_head_rms_norm(a[0], c[0], c[1]), r[0], r[1]),),
                           [BF16], rows=[cos_q, sin_q], consts=[ones_q, qn_gain], tn=512)
        k_f32, k_bf = fused_mm("proj_k", [u], [wk], [0], norm_rope, [F32, BF16],
                               rows=[cos_k, sin_k], consts=[ones_k, kn_gain])
        v_f32, v_bf = fused_mm("proj_v", [u], [wv], [0], lambda a, t, r, c: (a[0], a[0]),
                               [F32, BF16])
        (qi_bf,) = fused_mm("proj_qi", [u], [wqi], [0],
                            lambda a, t, r, c: (_rope(a[0], r[0], r[1]),),
                            [BF16], rows=[cos_q, sin_q], tn=512)
        (kiwi,) = fused_mm("proj_kiwi", [u], [w_kiwi], [0],
                           lambda a, t, r, c: (_rope(a[0], r[0], r[1]),),
                           [F32], rows=[cos_i, sin_i])
        (glu,) = fused_mm("proj_glu", [u], [wglu_a, wglu_g], [0, 0],
                          lambda a, t, r, c: (a[0] * _sigmoid(a[1]),), [F32])
        gate_a, gate_b = fused_mm("proj_gates", [u], [wga, wgb], [0, 0],
                                  lambda a, t, r, c: (_sigmoid(a[0]), _sigmoid(a[1])),
                                  [F32, F32])

        ki_f32 = kiwi[:, :IDX_DIM]
        wi = kiwi[:, IDX_DIM:IDX_DIM + IDX_HEADS]

        k3 = k_bf.reshape(bsz, length, kv_dim)
        v3 = v_bf.reshape(bsz, length, kv_dim)
        ki3 = ki_f32.astype(BF16).reshape(bsz, length, IDX_DIM)
        if past:
            pad = n_keys_pad - n_keys
            k3 = jnp.concatenate([cache["k"][i].reshape(bsz, past, kv_dim).astype(BF16), k3,
                                  jnp.zeros((bsz, pad, kv_dim), BF16)], axis=1)
            v3 = jnp.concatenate([cache["v"][i].reshape(bsz, past, kv_dim).astype(BF16), v3,
                                  jnp.zeros((bsz, pad, kv_dim), BF16)], axis=1)
            ki3 = jnp.concatenate([cache["kidx"][i].astype(BF16), ki3,
                                   jnp.zeros((bsz, pad, IDX_DIM), BF16)], axis=1)
        bias = select_bias(qi_bf.reshape(bsz, length, -1), ki3, wi.reshape(bsz, length, IDX_HEADS),
                           tq=tq_sel, tk=tk, causal=(past == 0), q_pos0=past, n_valid=n_keys,
                           n_sel=n_sel)
        attn = masked_attention(q_bf.reshape(bsz, length, -1), k3, v3, bias, tq=tq_attn,
                                kspan=kspan, causal=(past == 0))
        attn = attn.reshape(m, -1)

        glu3 = glu.reshape(bsz, length, d_model)
        if past:
            ctx = cache["conv"][i]
        else:
            ctx = jnp.zeros((bsz, CONV_WIDTH - 1, d_model), F32)
        lead = jnp.zeros((bsz, CONV_HALO - (CONV_WIDTH - 1), d_model), F32)
        ctx32 = jnp.concatenate([lead, ctx], axis=1)
        l_pad = -(-length // CONV_HALO) * CONV_HALO
        if l_pad == length:
            y_conv = glu3
        else:
            y_conv = jnp.concatenate(
                [glu3, jnp.zeros((bsz, l_pad - length, d_model), F32)], axis=1)
        if length >= CONV_WIDTH - 1:
            conv_state = glu3[:, length - (CONV_WIDTH - 1):]
        else:
            conv_state = jnp.concatenate([ctx, glu3], axis=1)[:, -(CONV_WIDTH - 1):]
        z = conv_ln_silu(y_conv, ctx32, weights["w_dw"][i], weights["b_dw"][i],
                         weights["conv_ln_g"][i], weights["conv_ln_b"][i], tm=min(256, l_pad))
        z = z[:, :length].reshape(m, d_model)

        (merged,) = fused_mm("mix", [attn, z], [weights["w_o_attn"][i], weights["w_o_conv"][i]],
                             [0, 1], lambda a, t, r, c: (t[0] * a[0] + t[1] * a[1],), [BF16],
                             tiled=[gate_a, gate_b])
        (h,) = fused_mm("out_proj", [merged], [weights["w_out"][i]], [0],
                        lambda a, t, r, c: (t[0] + a[0],), [F32], tiled=[h])

        j = i // 2
        if i % 2 == 0:
            (hb,) = fused_mm("ffn_up", [h], [weights["ffn_w_gate"][j], weights["ffn_w_up"][j]],
                             [0, 0], lambda a, t, r, c: (_silu(a[0]) * a[1],), [BF16],
                             gains=[weights["g_ffn"][i]], tn=1408)
            (h,) = fused_mm("ffn_down", [hb], [weights["ffn_w_down"][j]], [0],
                            lambda a, t, r, c: (t[0] + a[0],), [F32], tiled=[h])
        else:
            h = moe_swiglu(h, weights["g_ffn"][i], weights["moe_router"][j],
                           weights["moe_w_gate"][j], weights["moe_w_up"][j],
                           weights["moe_w_down"][j])

        (h,) = fused_mm("ple", [h, p[i].reshape(m, -1)],
                        [weights["w_ple_gate"][i], weights["w_ple_proj"][i]],
                        [0, 1], lambda a, t, r, c: (t[0] + _sigmoid(a[0]) * a[1],), [F32],
                        gains=[weights["g_ple"][i], None], tiled=[h])

        ks.append(k_f32.reshape(bsz, length, N_KV_HEADS, HEAD_DIM))
        vs.append(v_f32.reshape(bsz, length, N_KV_HEADS, HEAD_DIM))
        kis.append(ki_f32.reshape(bsz, length, IDX_DIM))
        convs.append(conv_state)
    return (h.reshape(bsz, length, d_model), jnp.stack(ks), jnp.stack(vs), jnp.stack(kis),
            jnp.stack(convs))


def kernel(x_prompt, x_sample, cache_k, cache_v, cache_kidx, state_conv, p_prompt, p_sample, w_in, q_norm_g, k_norm_g, w_dw, b_dw, conv_ln_g, conv_ln_b, w_o_attn, w_o_conv, w_out, g_mix, g_ffn, ffn_w_gate, ffn_w_up, ffn_w_down, moe_router, moe_w_gate, moe_w_up, moe_w_down, g_ple, w_ple_gate, w_ple_proj):
    bf = lambda a: a.astype(BF16)
    weights = dict(
        w_in=bf(w_in), q_norm_g=q_norm_g, k_norm_g=k_norm_g, w_dw=w_dw, b_dw=b_dw,
        conv_ln_g=conv_ln_g, conv_ln_b=conv_ln_b, w_o_attn=bf(w_o_attn), w_o_conv=bf(w_o_conv),
        w_out=bf(w_out), g_mix=g_mix, g_ffn=g_ffn, ffn_w_gate=bf(ffn_w_gate),
        ffn_w_up=bf(ffn_w_up), ffn_w_down=bf(ffn_w_down), moe_router=bf(moe_router),
        moe_w_gate=bf(moe_w_gate), moe_w_up=bf(moe_w_up), moe_w_down=bf(moe_w_down), g_ple=g_ple,
        w_ple_gate=bf(w_ple_gate), w_ple_proj=bf(w_ple_proj))
    cache = dict(k=cache_k, v=cache_v, kidx=cache_kidx, conv=state_conv)
    past_len = cache_k.shape[2]
    y_p, k_p, v_p, ki_p, conv_p = _layer_stack(x_prompt, p_prompt, weights, None, past=0)
    y_s, k_s, v_s, ki_s, conv_s = _layer_stack(x_sample, p_sample, weights, cache, past=past_len)
    return (y_p, y_s, k_p, v_p, ki_p, conv_p, k_s, v_s, ki_s, conv_s)
```
